```python
import jax, jax.numpy as jnp
from jax import lax
import numpy as np

D_MODEL = 2048
BATCH = 2
SEQ = 16384
DEPTH = 1

F32 = jnp.float32
GRID_W = 64
CTX_LEN = 256
EPS = 1e-6
HEAD_DIM = 128
ATTN_HEADS = 8
ATTN_KV_HEADS = 2
ATTN_GROUP = ATTN_HEADS // ATTN_KV_HEADS
WINDOW = 128
BAND_BLOCK = 128
ROPE_THETA = 10000.0
HGRN_HEADS = 8
HGRN_DK = 128
HGRN_DV = 128
HGRN_CHUNK = 64
ATTN_WIDTH = ATTN_HEADS * HEAD_DIM
KV_WIDTH = ATTN_KV_HEADS * HEAD_DIM
HGRN_KW = HGRN_HEADS * HGRN_DK
HGRN_WIDTH = HGRN_HEADS * HGRN_DV
D_MIX = ATTN_WIDTH + HGRN_WIDTH
COL_CTX = 2 * KV_WIDTH + 2 * HGRN_KW + HGRN_WIDTH
N_IN_COLS = COL_CTX + ATTN_WIDTH + HGRN_KW + HGRN_WIDTH
PEER_HEADS = 8
PEER_NKEYS = 128
PEER_EXPERTS = PEER_NKEYS * PEER_NKEYS
PEER_DKEY = 256
PEER_TOPK = 16
PEER_TOKEN_BLOCK = 128

kernel_name = "hymba_hgrn2_swa_peer_dit_block"


def rms_norm(x, gain):
    xf = x.astype(F32)
    y = xf * lax.rsqrt(jnp.mean(xf * xf, axis=-1, keepdims=True) + EPS)
    return (y * gain.astype(F32)).astype(x.dtype)


def modulate(h, shift, scale):
    return h * (1 + scale) + shift


def to_heads(t, n_heads):
    return t.reshape(t.shape[0], t.shape[1], n_heads, t.shape[-1] // n_heads)


def split_projection(p, n_parts):
    sizes = (KV_WIDTH, KV_WIDTH, HGRN_KW, HGRN_KW, HGRN_WIDTH, ATTN_WIDTH, HGRN_KW, HGRN_WIDTH)[:n_parts]
    points = [int(s) for s in np.cumsum(sizes)[:-1]]
    return jnp.split(p, points, axis=-1)


def axial_rope_tables(n_tokens):
    rows = n_tokens // GRID_W
    row_pos = jnp.repeat(jnp.arange(rows, dtype=F32), GRID_W)
    col_pos = jnp.tile(jnp.arange(GRID_W, dtype=F32), rows)
    half = HEAD_DIM // 2
    inv_freq = jnp.power(ROPE_THETA, -jnp.arange(0, half, 2, dtype=F32) / half)
    ang_r = row_pos[:, None] * inv_freq
    ang_c = col_pos[:, None] * inv_freq
    return (jnp.cos(ang_r)[:, None], jnp.sin(ang_r)[:, None],
            jnp.cos(ang_c)[:, None], jnp.sin(ang_c)[:, None])


def rotate(x, cos, sin):
    x1, x2 = jnp.split(x, 2, axis=-1)
    return jnp.concatenate([x1 * cos - x2 * sin, x1 * sin + x2 * cos], axis=-1)


def apply_axial_rope(x, rope):
    cos_r, sin_r, cos_c, sin_c = rope
    xr, xc = jnp.split(x.astype(F32), 2, axis=-1)
    return jnp.concatenate([rotate(xr, cos_r, sin_r), rotate(xc, cos_c, sin_c)], axis=-1).astype(x.dtype)


def banded_attention_with_context(q, k, v, k_ctx, v_ctx, sink):
    B, T = q.shape[:2]
    nb = T // BAND_BLOCK
    scale = HEAD_DIM ** -0.5
    qb = q.reshape(B, nb, BAND_BLOCK, ATTN_KV_HEADS, ATTN_GROUP, HEAD_DIM)
    pad = ((0, 0), (BAND_BLOCK, BAND_BLOCK), (0, 0), (0, 0))

    def band(t):
        t = jnp.pad(t, pad).reshape(B, nb + 2, BAND_BLOCK, ATTN_KV_HEADS, HEAD_DIM)
        return jnp.concatenate([t[:, :-2], t[:, 1:-1], t[:, 2:]], axis=2)

    kb, vb = band(k), band(v)
    qi = jnp.arange(BAND_BLOCK)[:, None]
    ki = jnp.arange(3 * BAND_BLOCK)[None, :]
    rel = ki - BAND_BLOCK - qi
    key_pos = jnp.arange(nb)[:, None, None] * BAND_BLOCK + ki[None] - BAND_BLOCK
    valid = (jnp.abs(rel) <= WINDOW)[None] & (key_pos >= 0) & (key_pos < T)

    s_band = jnp.einsum('bnqhgd,bnkhd->bhgnqk', qb, kb, preferred_element_type=F32) * scale
    s_band = jnp.where(valid, s_band, -jnp.inf)
    s_ctx = jnp.einsum('bnqhgd,blhd->bhgnql', qb, k_ctx, preferred_element_type=F32) * scale
    sink_l = sink.astype(F32).reshape(ATTN_KV_HEADS, ATTN_GROUP)[None, :, :, None, None]
    m = jnp.maximum(jnp.maximum(s_band.max(-1), s_ctx.max(-1)), sink_l)
    p_band = jnp.exp(s_band - m[..., None])
    p_ctx = jnp.exp(s_ctx - m[..., None])
    denom = p_band.sum(-1) + p_ctx.sum(-1) + jnp.exp(sink_l - m)
    o = (jnp.einsum('bhgnqk,bnkhd->bnqhgd', p_band.astype(v.dtype), vb, preferred_element_type=F32)
         + jnp.einsum('bhgnql,blhd->bnqhgd', p_ctx.astype(v.dtype), v_ctx, preferred_element_type=F32))
    o = o / jnp.transpose(denom, (0, 3, 4, 1, 2))[..., None]
    return o.reshape(B, T, ATTN_WIDTH).astype(q.dtype)


def context_attention(q, k, v, sink):
    B, L = q.shape[:2]
    qg = q.reshape(B, L, ATTN_KV_HEADS, ATTN_GROUP, HEAD_DIM)
    s = jnp.einsum('blhgd,bmhd->bhglm', qg, k, preferred_element_type=F32) * (HEAD_DIM ** -0.5)
    sink_b = jnp.broadcast_to(sink.astype(F32).reshape(ATTN_KV_HEADS, ATTN_GROUP)[None, :, :, None, None],
                              s.shape[:-1] + (1,))
    p = jax.nn.softmax(jnp.concatenate([s, sink_b], axis=-1), axis=-1)[..., :-1]
    o = jnp.einsum('bhglm,bmhd->blhgd', p.astype(v.dtype), v, preferred_element_type=F32)
    return o.reshape(B, L, ATTN_WIDTH).astype(q.dtype)


def hgrn_gates(f_logit, lower_bound):
    f = lower_bound + (1 - lower_bound) * jax.nn.sigmoid(f_logit.astype(F32))
    return jnp.log(f), 1 - f


def scan_layout(t, n_heads):
    return to_heads(t, n_heads).transpose(0, 2, 1, 3)


def hgrn_chunk_scan(k, v, log_f, s0, q):
    B, H, T, _ = k.shape
    nc = T // HGRN_CHUNK
    with_out = q is not None

    def to_chunks(t):
        return jnp.moveaxis(t.astype(F32).reshape(B, H, nc, HGRN_CHUNK, t.shape[-1]), 2, 0)

    xs = (to_chunks(k), to_chunks(v), to_chunks(log_f), to_chunks(q)) if with_out else \
         (to_chunks(k), to_chunks(v), to_chunks(log_f))
    incl = jnp.tril(jnp.ones((HGRN_CHUNK, HGRN_CHUNK), dtype=bool))[:, :, None]

    def step(state, xc):
        kc, vc, lfc = xc[0], xc[1], xc[2]
        a = jnp.cumsum(lfc, axis=2)
        a_end = a[:, :, -1]
        new_state = (jnp.exp(a_end)[..., None] * state
                     + jnp.einsum('bhsk,bhsv->bhkv', kc * jnp.exp(a_end[:, :, None] - a), vc))
        if not with_out:
            return new_state, None
        qc = xc[3]
        inter = jnp.einsum('bhtk,bhkv->bhtv', qc * jnp.exp(a), state)
        decay = jnp.exp(jnp.where(incl, a[:, :, :, None, :] - a[:, :, None, :, :], -jnp.inf))
        scores = jnp.einsum('bhtk,bhtsk,bhsk->bhts', qc, decay, kc)
        return new_state, inter + jnp.einsum('bhts,bhsv->bhtv', scores, vc)

    state, ys = lax.scan(step, s0.astype(F32), xs)
    if not with_out:
        return state, None
    return state, jnp.moveaxis(ys, 0, 2).reshape(B, H, T, ys.shape[-1])


def hgrn_direction(k, v, log_f, s0, q, reverse):
    if reverse:
        k, v, log_f = jnp.flip(k, 2), jnp.flip(v, 2), jnp.flip(log_f, 2)
        q = None if q is None else jnp.flip(q, 2)
    state, o = hgrn_chunk_scan(k, v, log_f, s0, q)
    if o is not None and reverse:
        o = jnp.flip(o, 2)
    return state, o


def hgrn_output(o, g, gain):
    o = rms_norm(o.transpose(0, 2, 1, 3), gain)
    y = o * jax.nn.silu(to_heads(g, HGRN_HEADS).astype(F32))
    return y.reshape(y.shape[0], y.shape[1], HGRN_WIDTH).astype(g.dtype)


def mixing_sublayer(hx, hc, w_in, q_gain, k_gain, sink, lb_fwd, lb_bwd, o_gain, w_out, rope, with_ctx_out):
    B = hx.shape[0]
    k_x, v_x, ff_x, fb_x, i_x, qa_x, qh_x, g_x = split_projection(hx @ w_in, 8)
    ctx_cols = N_IN_COLS if with_ctx_out else COL_CTX
    parts_c = split_projection(hc @ w_in[:, :ctx_cols], 8 if with_ctx_out else 5)
    k_c, v_c, ff_c, fb_c, i_c = parts_c[0], parts_c[1], parts_c[2], parts_c[3], parts_c[4]

    q_xh = apply_axial_rope(rms_norm(to_heads(qa_x, ATTN_HEADS), q_gain), rope)
    k_xh = apply_axial_rope(rms_norm(to_heads(k_x, ATTN_KV_HEADS), k_gain), rope)
    k_ch = rms_norm(to_heads(k_c, ATTN_KV_HEADS), k_gain)
    v_xh, v_ch = to_heads(v_x, ATTN_KV_HEADS), to_heads(v_c, ATTN_KV_HEADS)
    attn_x = banded_attention_with_context(q_xh, k_xh, v_xh, k_ch, v_ch, sink)

    lf_fx, kf_x = hgrn_gates(ff_x, lb_fwd)
    lf_bx, kb_x = hgrn_gates(fb_x, lb_bwd)
    lf_fc, kf_c = hgrn_gates(ff_c, lb_fwd)
    lf_bc, kb_c = hgrn_gates(fb_c, lb_bwd)
    s0 = jnp.zeros((B, HGRN_HEADS, HGRN_DK, HGRN_DV), F32)
    q_c = scan_layout(parts_c[6], HGRN_HEADS) if with_ctx_out else None
    i_cs = scan_layout(i_c, HGRN_HEADS)
    sf_c, of_c = hgrn_direction(scan_layout(kf_c, HGRN_HEADS), i_cs, scan_layout(lf_fc, HGRN_HEADS), s0, q_c, False)
    sb_c, ob_c = hgrn_direction(scan_layout(kb_c, HGRN_HEADS), i_cs, scan_layout(lf_bc, HGRN_HEADS), s0, q_c, True)
    q_x = scan_layout(qh_x, HGRN_HEADS)
    i_xs = scan_layout(i_x, HGRN_HEADS)
    _, of_x = hgrn_direction(scan_layout(kf_x, HGRN_HEADS), i_xs, scan_layout(lf_fx, HGRN_HEADS), sf_c, q_x, False)
    _, ob_x = hgrn_direction(scan_layout(kb_x, HGRN_HEADS), i_xs, scan_layout(lf_bx, HGRN_HEADS), sb_c, q_x, True)
    hgrn_x = hgrn_output(of_x + ob_x, g_x, o_gain)

    mix_x = jnp.concatenate([attn_x, hgrn_x], axis=-1) @ w_out
    if not with_ctx_out:
        return mix_x, None
    q_ch = rms_norm(to_heads(parts_c[5], ATTN_HEADS), q_gain)
    attn_c = context_attention(q_ch, k_ch, v_ch, sink)
    hgrn_c = hgrn_output(of_c + ob_c, parts_c[7], o_gain)
    mix_c = jnp.concatenate([attn_c, hgrn_c], axis=-1) @ w_out
    return mix_x, mix_c


def peer_ffn(h, w_q, sub_keys, u, v):
    B, T, D = h.shape
    n = B * T
    hf = h.reshape(n, D)
    q = (hf @ w_q).reshape(n, PEER_HEADS, 2, PEER_DKEY // 2)
    s = jnp.einsum('nhpd,hpkd->nhpk', q, sub_keys, preferred_element_type=F32)
    s_top, i_top = lax.top_k(s, PEER_TOPK)
    cand = s_top[:, :, 0, :, None] + s_top[:, :, 1, None, :]
    cand_idx = i_top[:, :, 0, :, None] * PEER_NKEYS + i_top[:, :, 1, None, :]
    best, pos = lax.top_k(cand.reshape(n, PEER_HEADS, PEER_TOPK * PEER_TOPK), PEER_TOPK)
    idx = jnp.take_along_axis(cand_idx.reshape(n, PEER_HEADS, PEER_TOPK * PEER_TOPK), pos, axis=-1)
    gate = jax.nn.softmax(best, axis=-1)
    nblk = n // PEER_TOKEN_BLOCK

    def block(args):
        hb, ib, gb = args
        a = jnp.einsum('thkd,td->thk', u[ib], hb, preferred_element_type=F32)
        w = (gb * jax.nn.gelu(a, approximate=False)).astype(hb.dtype)
        return jnp.einsum('thk,thkd->td', w, v[ib])

    out = lax.map(block, (hf.reshape(nblk, PEER_TOKEN_BLOCK, D),
                          idx.reshape(nblk, PEER_TOKEN_BLOCK, PEER_HEADS, PEER_TOPK),
                          gate.reshape(nblk, PEER_TOKEN_BLOCK, PEER_HEADS, PEER_TOPK)))
    return out.reshape(B, T, D)


def setup_inputs(seed: int = 0) -> dict:
    key = jax.random.key(seed)
    ks = jax.random.split(key, 20)

    def nrm(k, shape, scale):
        return jax.random.normal(k, shape, F32) * scale

    return {
        "x": nrm(ks[0], (BATCH, SEQ, D_MODEL), 1.0),
        "c": nrm(ks[1], (BATCH, D_MODEL), 1.0),
        "ctx": nrm(ks[2], (BATCH, CTX_LEN, D_MODEL), 1.0),
        "c_ctx": nrm(ks[3], (D_MODEL,), 1.0),
        "w_ada": nrm(ks[4], (DEPTH, D_MODEL, 6 * D_MODEL), 0.02),
        "b_ada": nrm(ks[5], (DEPTH, 6 * D_MODEL), 0.02),
        "norm_mix": 1.0 + nrm(ks[6], (DEPTH, D_MODEL), 0.05),
        "norm_ffn": 1.0 + nrm(ks[7], (DEPTH, D_MODEL), 0.05),
        "w_in": nrm(ks[8], (DEPTH, D_MODEL, N_IN_COLS), D_MODEL ** -0.5),
        "q_norm": 1.0 + nrm(ks[9], (DEPTH, HEAD_DIM), 0.05),
        "k_norm": 1.0 + nrm(ks[10], (DEPTH, HEAD_DIM), 0.05),
        "attn_sink": nrm(ks[11], (DEPTH, ATTN_HEADS), 0.5),
        "hgrn_lb_logits": nrm(ks[12], (2, DEPTH + 1, HGRN_KW), 0.5),
        "hgrn_norm": 1.0 + nrm(ks[13], (DEPTH, HGRN_DV), 0.05),
        "w_out": nrm(ks[14], (DEPTH, D_MIX, D_MODEL), D_MIX ** -0.5),
        "peer_w_q": nrm(ks[15], (DEPTH, D_MODEL, PEER_HEADS * PEER_DKEY), D_MODEL ** -0.5),
        "peer_sub_keys": nrm(ks[16], (DEPTH, PEER_HEADS, 2, PEER_NKEYS, PEER_DKEY // 2), (PEER_DKEY // 2) ** -0.5),
        "peer_u": nrm(ks[17], (DEPTH, PEER_EXPERTS, D_MODEL), D_MODEL ** -0.5),
        "peer_v": nrm(ks[18], (DEPTH, PEER_EXPERTS, D_MODEL), 0.5),
    }


def reference(x, c, ctx, c_ctx, w_ada, b_ada, norm_mix, norm_ffn, w_in, q_norm, k_norm, attn_sink,
              hgrn_lb_logits, hgrn_norm, w_out, peer_w_q, peer_sub_keys, peer_u, peer_v):
    rope = axial_rope_tables(x.shape[1])
    lower_bounds = jnp.cumsum(jax.nn.softmax(hgrn_lb_logits.astype(F32), axis=1), axis=1)
    silu_c = jax.nn.silu(c)
    silu_cc = jax.nn.silu(c_ctx)
    for layer in range(DEPTH):
        last = layer == DEPTH - 1
        mod_x = (silu_c @ w_ada[layer] + b_ada[layer])[:, None, :]
        sh1, sc1, g1, sh2, sc2, g2 = jnp.split(mod_x, 6, axis=-1)
        n_ctx_mod = 2 if last else 6
        mod_c = silu_cc @ w_ada[layer][:, :n_ctx_mod * D_MODEL] + b_ada[layer][:n_ctx_mod * D_MODEL]
        mc = jnp.split(mod_c, n_ctx_mod, axis=-1)

        hx = modulate(rms_norm(x, norm_mix[layer]), sh1, sc1)
        hc = modulate(rms_norm(ctx, norm_mix[layer]), mc[0], mc[1])
        mix_x, mix_c = mixing_sublayer(hx, hc, w_in[layer], q_norm[layer], k_norm[layer], attn_sink[layer],
                                       lower_bounds[0, layer], lower_bounds[1, layer], hgrn_norm[layer],
                                       w_out[layer], rope, not last)
        x = x + g1 * mix_x
        hx2 = modulate(rms_norm(x, norm_ffn[layer]), sh2, sc2)
        x = x + g2 * peer_ffn(hx2, peer_w_q[layer], peer_sub_keys[layer], peer_u[layer], peer_v[layer])
        if not last:
            ctx = ctx + mc[2] * mix_c
            hc2 = modulate(rms_norm(ctx, norm_ffn[layer]), mc[3], mc[4])
            ctx = ctx + mc[5] * peer_ffn(hc2, peer_w_q[layer], peer_sub_keys[layer], peer_u[layer], peer_v[layer])
    return x
```

```python
import functools

import jax
import jax.numpy as jnp
from jax import lax
from jax.experimental import pallas as pl
from jax.experimental.pallas import tpu as pltpu
from jax.experimental.pallas import tpu_sc as plsc

F32 = jnp.float32
BF16 = jnp.bfloat16
I32 = jnp.int32

D_MODEL = 2048
GRID_W = 64
EPS = 1e-6
HEAD_DIM = 128
ATTN_HEADS = 8
ATTN_KV_HEADS = 2
ATTN_GROUP = ATTN_HEADS // ATTN_KV_HEADS
BAND_BLOCK = 128
ROPE_THETA = 10000.0
HGRN_HEADS = 8
HGRN_D = 128
ATTN_WIDTH = ATTN_HEADS * HEAD_DIM
KV_WIDTH = ATTN_KV_HEADS * HEAD_DIM
HGRN_WIDTH = HGRN_HEADS * HGRN_D
COL_K = 0
COL_V = COL_K + KV_WIDTH
COL_FF = COL_V + KV_WIDTH
COL_FB = COL_FF + HGRN_WIDTH
COL_I = COL_FB + HGRN_WIDTH
COL_Q = COL_I + HGRN_WIDTH
COL_QH = COL_Q + ATTN_WIDTH
COL_G = COL_QH + HGRN_WIDTH
N_IN_COLS = COL_G + HGRN_WIDTH
PEER_HEADS = 8
PEER_NKEYS = 128
PEER_DHALF = 128
PEER_TOPK = 16
PEER_SLOTS = PEER_HEADS * PEER_TOPK

LANES = 128
SUBLANES = 8
VMEM_LIMIT_BYTES = 56 * 1024 * 1024

ROW_TILE = 256
HGRN_CHUNK = 128
HGRN_STEP = 512
NEG_INF = float("-inf")


def _cparams(*sem):
    return pltpu.CompilerParams(dimension_semantics=sem, vmem_limit_bytes=VMEM_LIMIT_BYTES)


def _resident(shape):
    nd = len(shape)
    return pl.BlockSpec(shape, lambda *_: (0,) * nd, pipeline_mode=pl.Buffered(1))


def _ada_kernel(c_ref, w_ref, b_ref, o_ref):
    c = c_ref[...]
    s = (c * jax.nn.sigmoid(c)).astype(BF16)
    o_ref[...] = jnp.dot(s, w_ref[...].astype(BF16), preferred_element_type=F32) + b_ref[...]


def _ada(cvecs, w, b):
    n = w.shape[1]
    tn = 1024
    return pl.pallas_call(
        _ada_kernel,
        grid=(n // tn,),
        in_specs=[pl.BlockSpec((SUBLANES, D_MODEL), lambda j: (0, 0)),
                  pl.BlockSpec((D_MODEL, tn), lambda j: (0, j)),
                  pl.BlockSpec((1, tn), lambda j: (0, j))],
        out_specs=pl.BlockSpec((SUBLANES, tn), lambda j: (0, j)),
        out_shape=jax.ShapeDtypeStruct((SUBLANES, n), F32),
        compiler_params=_cparams("arbitrary"),
        name="ada",
    )(cvecs, w, b.reshape(1, n))


def _rms(x, gain):
    return x * lax.rsqrt(jnp.mean(x * x, axis=-1, keepdims=True) + EPS) * gain


def _rope(x, cos, sin_a, sin_b):
    q = HEAD_DIM // 4
    return x * cos + pltpu.roll(x, HEAD_DIM - q, 1) * sin_a + pltpu.roll(x, q, 1) * sin_b


def _inproj_kernel(x_ref, sh_ref, sc_ref, gain_ref, w_ref, cos_ref, sa_ref, sb_ref, qg_ref, kg_ref,
                   k_ref, v_ref, ff_ref, fb_ref, i_ref, q_ref, qh_ref, g_ref):
    x = x_ref[0]
    h = _rms(x, gain_ref[...]) * (1.0 + sc_ref[0]) + sh_ref[0]
    hb = h.astype(BF16)

    def seg(lo, width):
        return jnp.dot(hb, w_ref[:, lo:lo + width], preferred_element_type=F32)

    cos, sa, sb = cos_ref[...], sa_ref[...], sb_ref[...]

    def normed_heads(p, gain, n_heads, out_ref):
        for hd in range(n_heads):
            ph = p[:, hd * HEAD_DIM:(hd + 1) * HEAD_DIM]
            out_ref[0, :, hd * HEAD_DIM:(hd + 1) * HEAD_DIM] = _rope(_rms(ph, gain), cos, sa, sb).astype(BF16)

    normed_heads(seg(COL_K, KV_WIDTH), kg_ref[...], ATTN_KV_HEADS, k_ref)
    v_ref[0] = seg(COL_V, KV_WIDTH).astype(BF16)
    ff_ref[0] = seg(COL_FF, HGRN_WIDTH)
    fb_ref[0] = seg(COL_FB, HGRN_WIDTH)
    i_ref[0] = seg(COL_I, HGRN_WIDTH).astype(BF16)
    normed_heads(seg(COL_Q, ATTN_WIDTH), qg_ref[...], ATTN_HEADS, q_ref)
    qh_ref[0] = seg(COL_QH, HGRN_WIDTH).astype(BF16)
    g_ref[0] = seg(COL_G, HGRN_WIDTH).astype(BF16)


def _inproj(x, shift, scale, gain, w_bf16, rope, q_gain, k_gain):
    B, T, _ = x.shape
    tm = min(ROW_TILE, T)
    row = lambda w: pl.BlockSpec((1, tm, w), lambda b, t: (b, t, 0))
    vec = pl.BlockSpec((1, 1, D_MODEL), lambda b, t: (b, 0, 0))
    tab = pl.BlockSpec((tm, HEAD_DIM), lambda b, t: (t, 0))
    out_w = [(KV_WIDTH, BF16), (KV_WIDTH, BF16), (HGRN_WIDTH, F32), (HGRN_WIDTH, F32), (HGRN_WIDTH, BF16),
             (ATTN_WIDTH, BF16), (HGRN_WIDTH, BF16), (HGRN_WIDTH, BF16)]
    return pl.pallas_call(
        _inproj_kernel,
        grid=(B, T // tm),
        in_specs=[row(D_MODEL), vec, vec, _resident((1, D_MODEL)), _resident((D_MODEL, N_IN_COLS)),
                  tab, tab, tab, _resident((1, HEAD_DIM)), _resident((1, HEAD_DIM))],
        out_specs=[row(w) for w, _ in out_w],
        out_shape=[jax.ShapeDtypeStruct((B, T, w), dt) for w, dt in out_w],
        compiler_params=_cparams("parallel", "parallel"),
        name="inproj",
    )(x, shift, scale, gain.reshape(1, D_MODEL), w_bf16, *rope,
      q_gain.reshape(1, HEAD_DIM), k_gain.reshape(1, HEAD_DIM))


def _rope_tables(T):
    rows = T // GRID_W
    row_pos = jnp.repeat(jnp.arange(rows, dtype=F32), GRID_W)
    col_pos = jnp.tile(jnp.arange(GRID_W, dtype=F32), rows)
    half = HEAD_DIM // 2
    inv_freq = jnp.power(ROPE_THETA, -jnp.arange(0, half, 2, dtype=F32) / half)
    ang_r = row_pos[:, None] * inv_freq
    ang_c = col_pos[:, None] * inv_freq
    cr, sr, cc, sc = jnp.cos(ang_r), jnp.sin(ang_r), jnp.cos(ang_c), jnp.sin(ang_c)
    z = jnp.zeros_like(sr)
    return (jnp.concatenate([cr, cr, cc, cc], -1),
            jnp.concatenate([-sr, z, -sc, z], -1),
            jnp.concatenate([z, sr, z, sc], -1))


def _identity_rope(T):
    return (jnp.ones((T, HEAD_DIM), F32), jnp.zeros((T, HEAD_DIM), F32), jnp.zeros((T, HEAD_DIM), F32))


_NT = (((1,), (1,)), ((), ()))
_TN = (((0,), (0,)), ((), ()))


def _attn_kernel(sink_ref, q_ref, kp_ref, kc_ref, kn_ref, vp_ref, vc_ref, vn_ref, kx_ref, vx_ref, o_ref):
    n = pl.program_id(1)
    nb = pl.num_programs(1)
    blk = BAND_BLOCK
    rows = ATTN_GROUP * blk
    qi = lax.broadcasted_iota(I32, (rows, blk), 0) & (blk - 1)
    kj = lax.broadcasted_iota(I32, (rows, blk), 1)
    prev_ok = kj >= qi + jnp.where(n > 0, 0, blk)
    next_ok = kj <= qi - jnp.where(n < nb - 1, 0, blk)
    scale = HEAD_DIM ** -0.5
    for h in range(ATTN_KV_HEADS):
        cs = slice(h * HEAD_DIM, (h + 1) * HEAD_DIM)
        heads = [h * ATTN_GROUP + g for g in range(ATTN_GROUP)]
        q4 = jnp.concatenate([q_ref[0, :, hd * HEAD_DIM:(hd + 1) * HEAD_DIM] for hd in heads], axis=0)

        def scores(k_ref):
            return lax.dot_general(q4, k_ref[0, :, cs], _NT, preferred_element_type=F32) * scale

        s_p = jnp.where(prev_ok, scores(kp_ref), NEG_INF)
        s_c = scores(kc_ref)
        s_n = jnp.where(next_ok, scores(kn_ref), NEG_INF)
        s_x = scores(kx_ref)
        sink = jnp.concatenate([jnp.full((blk, 1), sink_ref[hd], F32) for hd in heads], axis=0)
        m = jnp.maximum(jnp.maximum(jnp.max(s_p, -1, keepdims=True), jnp.max(s_c, -1, keepdims=True)),
                        jnp.maximum(jnp.max(s_n, -1, keepdims=True), jnp.max(s_x, -1, keepdims=True)))
        m = jnp.maximum(m, sink)
        p_p, p_c, p_n, p_x = jnp.exp(s_p - m), jnp.exp(s_c - m), jnp.exp(s_n - m), jnp.exp(s_x - m)
        denom = (jnp.sum(p_p, -1, keepdims=True) + jnp.sum(p_c, -1, keepdims=True)
                 + jnp.sum(p_n, -1, keepdims=True) + jnp.sum(p_x, -1, keepdims=True) + jnp.exp(sink - m))

        def pv(p, v_ref):
            return jnp.dot(p.astype(BF16), v_ref[0, :, cs], preferred_element_type=F32)

        o = (pv(p_p, vp_ref) + pv(p_c, vc_ref) + pv(p_n, vn_ref) + pv(p_x, vx_ref)) / denom
        for g, hd in enumerate(heads):
            o_ref[0, :, hd * HEAD_DIM:(hd + 1) * HEAD_DIM] = o[g * blk:(g + 1) * blk].astype(BF16)


def _attention(q, k, v, k_ctx, v_ctx, sink):
    B, T, _ = q.shape
    L = k_ctx.shape[1]
    nb = T // BAND_BLOCK
    kv = lambda f: pl.BlockSpec((1, BAND_BLOCK, KV_WIDTH), lambda b, n: (b, f(n), 0))
    prev, cur, nxt = (lambda n: jnp.maximum(n - 1, 0)), (lambda n: n), (lambda n: jnp.minimum(n + 1, nb - 1))
    ctx = pl.BlockSpec((1, L, KV_WIDTH), lambda b, n: (b, 0, 0))
    return pl.pallas_call(
        _attn_kernel,
        grid=(B, nb),
        in_specs=[pl.BlockSpec(memory_space=pltpu.SMEM),
                  pl.BlockSpec((1, BAND_BLOCK, ATTN_WIDTH), lambda b, n: (b, n, 0)),
                  kv(prev), kv(cur), kv(nxt), kv(prev), kv(cur), kv(nxt), ctx, ctx],
        out_specs=pl.BlockSpec((1, BAND_BLOCK, ATTN_WIDTH), lambda b, n: (b, n, 0)),
        out_shape=jax.ShapeDtypeStruct((B, T, ATTN_WIDTH), BF16),
        compiler_params=_cparams("parallel", "parallel"),
        name="attn",
    )(sink, q, k, k, k, v, v, v, k_ctx, v_ctx)


_DIAG = SUBLANES
_LEVELS = (64, 32, 16, 8)


def _hgrn_chunk(logit, v, q, lb, st_ref, reverse):
    C = HGRN_CHUNK
    f = lb + (1.0 - lb) * jax.nn.sigmoid(logit)
    lf = jnp.log(f)
    kk = 1.0 - f
    qf = q.astype(F32)
    r = lax.broadcasted_iota(I32, (C, C), 0)
    c = lax.broadcasted_iota(I32, (C, C), 1)
    incl = (r <= c) if reverse else (r >= c)
    a = jnp.dot(incl.astype(F32), lf, precision=lax.Precision.HIGHEST, preferred_element_type=F32)
    a_end = a[0:1] if reverse else a[C - 1:C]

    st = st_ref[...]
    inter = lax.dot_general((qf * jnp.exp(a)).astype(BF16), st.astype(BF16), _NT, preferred_element_type=F32)
    kd = (kk * jnp.exp(a_end - a)).astype(BF16)
    st_ref[...] = st * jnp.exp(a_end) + lax.dot_general(v, kd, _TN, preferred_element_type=F32)

    later = (r < c) if reverse else (r > c)
    att = jnp.zeros((C, C), F32)
    for m in _LEVELS:
        a3 = a.reshape(C // (2 * m), 2 * m, HGRN_D)
        edge = a3[:, m:m + 1, :] if reverse else a3[:, m - 1:m, :]
        e = jnp.exp(-jnp.abs(a3 - edge)).reshape(C, HGRN_D)
        p = lax.dot_general((qf * e).astype(BF16), (kk * e).astype(BF16), _NT, preferred_element_type=F32)
        pair = ((r ^ c) >> (m.bit_length() - 1)) == 1
        att = jnp.where(pair & later, p, att)

    lane = lax.broadcasted_iota(I32, (_DIAG, C), 1)
    sub = lax.broadcasted_iota(I32, (_DIAG, C), 0)
    blocks = []
    for j in range(C // _DIAG):
        rows = slice(j * _DIAG, (j + 1) * _DIAG)
        a_j, q_j, k_j = a[rows], qf[rows], kk[rows]
        blk = jnp.zeros((_DIAG, C), F32)
        for s in range(_DIAG):
            e = jnp.exp(jnp.minimum(a_j - a_j[s:s + 1], 0.0))
            col = jnp.sum(q_j * e * k_j[s:s + 1], axis=-1, keepdims=True)
            ok = (sub <= s) if reverse else (sub >= s)
            blk = jnp.where((lane == j * _DIAG + s) & ok, col, blk)
        blocks.append(blk)
    att = att + jnp.concatenate(blocks, axis=0)
    return inter + jnp.dot(att.astype(BF16), v, preferred_element_type=F32)


def _hgrn_kernel(ff_ref, vf_ref, qf_ref, fb_ref, vb_ref, qb_ref, lbf_ref, lbb_ref, s0f_ref, s0b_ref,
                 of_ref, ob_ref, sf_ref, sb_ref, stf, stb):
    step = pl.program_id(2)
    nsub = ff_ref.shape[1] // HGRN_CHUNK

    @pl.when(step == 0)
    def _():
        stf[...] = s0f_ref[0, 0]
        stb[...] = s0b_ref[0, 0]

    def body(j, carry):
        fo = pl.multiple_of(j * HGRN_CHUNK, HGRN_CHUNK)
        rows = pl.ds(fo, HGRN_CHUNK)
        of_ref[0, rows, :] = _hgrn_chunk(ff_ref[0, rows, :], vf_ref[0, rows, :], qf_ref[0, rows, :],
                                         lbf_ref[...], stf, False)
        bo = pl.multiple_of((nsub - 1 - j) * HGRN_CHUNK, HGRN_CHUNK)
        rows = pl.ds(bo, HGRN_CHUNK)
        ob_ref[0, rows, :] = _hgrn_chunk(fb_ref[0, rows, :], vb_ref[0, rows, :], qb_ref[0, rows, :],
                                         lbb_ref[...], stb, True)
        return carry

    lax.fori_loop(0, nsub, body, 0)

    @pl.when(step == pl.num_programs(2) - 1)
    def _():
        sf_ref[0, 0] = stf[...]
        sb_ref[0, 0] = stb[...]


def _hgrn(ff, fb, val, q, lb_f, lb_b, s0f, s0b):
    B, T, _ = ff.shape
    ts = min(HGRN_STEP, T)
    ns = T // ts
    fwd = pl.BlockSpec((1, ts, HGRN_D), lambda b, h, s: (b, s, h))
    bwd = pl.BlockSpec((1, ts, HGRN_D), lambda b, h, s: (b, ns - 1 - s, h))
    lbs = pl.BlockSpec((1, HGRN_D), lambda b, h, s: (0, h))
    st = pl.BlockSpec((1, 1, HGRN_D, HGRN_D), lambda b, h, s: (b, h, 0, 0))
    o_sds = jax.ShapeDtypeStruct((B, T, HGRN_WIDTH), F32)
    s_sds = jax.ShapeDtypeStruct((B, HGRN_HEADS, HGRN_D, HGRN_D), F32)
    return pl.pallas_call(
        _hgrn_kernel,
        grid=(B, HGRN_HEADS, ns),
        in_specs=[fwd, fwd, fwd, bwd, bwd, bwd, lbs, lbs, st, st],
        out_specs=[fwd, bwd, st, st],
        out_shape=[o_sds, o_sds, s_sds, s_sds],
        scratch_shapes=[pltpu.VMEM((HGRN_D, HGRN_D), F32), pltpu.VMEM((HGRN_D, HGRN_D), F32)],
        compiler_params=_cparams("parallel", "parallel", "arbitrary"),
        name="hgrn",
    )(ff, val, q, fb, val, q, lb_f, lb_b, s0f, s0b)


def _outproj_kernel(attn_ref, of_ref, ob_ref, g_ref, x_ref, g1_ref, sh2_ref, sc2_ref, og_ref, nf_ref,
                    wo_ref, wq_ref, sk_ref, x1_ref, h2_ref, s_ref):
    o = of_ref[0] + ob_ref[0]
    og = og_ref[...]
    parts = []
    for hd in range(HGRN_HEADS):
        cs = slice(hd * HGRN_D, (hd + 1) * HGRN_D)
        gh = g_ref[0, :, cs].astype(F32)
        parts.append((_rms(o[:, cs], og) * (gh * jax.nn.sigmoid(gh))).astype(BF16))
    hg = jnp.concatenate(parts, axis=-1)
    mix = (jnp.dot(attn_ref[0], wo_ref[:ATTN_WIDTH], preferred_element_type=F32)
           + jnp.dot(hg, wo_ref[ATTN_WIDTH:], preferred_element_type=F32))
    x1 = x_ref[0] + g1_ref[0] * mix
    x1_ref[0] = x1
    h2 = _rms(x1, nf_ref[...]) * (1.0 + sc2_ref[0]) + sh2_ref[0]
    h2_ref[0] = h2
    pq = jnp.dot(h2.astype(BF16), wq_ref[...], preferred_element_type=F32)
    for hp in range(2 * PEER_HEADS):
        cs = slice(hp * PEER_DHALF, (hp + 1) * PEER_DHALF)
        s_ref[0, :, cs] = lax.dot_general(pq[:, cs].astype(BF16), sk_ref[hp], _NT, preferred_element_type=F32)


def _outproj(attn, of, ob, g, x, g1, sh2, sc2, o_gain, norm_ffn, wo_bf16, wq_bf16, sk_bf16):
    B, T, _ = x.shape
    tm = min(ROW_TILE, T)
    row = lambda w: pl.BlockSpec((1, tm, w), lambda b, t: (b, t, 0))
    vec = pl.BlockSpec((1, 1, D_MODEL), lambda b, t: (b, 0, 0))
    n_sk = 2 * PEER_HEADS
    sds = lambda w: jax.ShapeDtypeStruct((B, T, w), F32)
    return pl.pallas_call(
        _outproj_kernel,
        grid=(B, T // tm),
        in_specs=[row(ATTN_WIDTH), row(HGRN_WIDTH), row(HGRN_WIDTH), row(HGRN_WIDTH), row(D_MODEL),
                  vec, vec, vec, _resident((1, HGRN_D)), _resident((1, D_MODEL)),
                  _resident((ATTN_WIDTH + HGRN_WIDTH, D_MODEL)), _resident((D_MODEL, n_sk * PEER_DHALF)),
                  _resident((n_sk, PEER_NKEYS, PEER_DHALF))],
        out_specs=[row(D_MODEL), row(D_MODEL), row(n_sk * PEER_NKEYS)],
        out_shape=[sds(D_MODEL), sds(D_MODEL), sds(n_sk * PEER_NKEYS)],
        compiler_params=_cparams("parallel", "parallel"),
        name="outproj",
    )(attn, of, ob, g, x, g1, sh2, sc2, o_gain.reshape(1, HGRN_D), norm_ffn.reshape(1, D_MODEL),
      wo_bf16, wq_bf16, sk_bf16)


TOPK_ROWS = 128


def _topk_kernel(s_ref, idx_ref, gate_ref):
    tm = s_ref.shape[0]
    lane = lax.broadcasted_iota(I32, (tm, PEER_NKEYS), 1)
    lane_f = lane.astype(F32)
    lane2 = lax.broadcasted_iota(I32, (tm, PEER_TOPK * PEER_TOPK), 1)
    big = float(PEER_NKEYS * PEER_NKEYS)

    def head(h, carry):
        cand, cidx = None, None
        for p in range(2):
            s = s_ref[:, pl.ds(pl.multiple_of((2 * h + p) * PEER_NKEYS, PEER_NKEYS), PEER_NKEYS)]
            cv = jnp.zeros((tm, PEER_TOPK * PEER_TOPK), F32)
            ci = jnp.zeros((tm, PEER_TOPK * PEER_TOPK), F32)
            for r in range(PEER_TOPK):
                m = jnp.max(s, axis=-1, keepdims=True)
                i = jnp.min(jnp.where(s == m, lane_f, float(PEER_NKEYS)), axis=-1, keepdims=True)
                s = jnp.where(lane_f == i, NEG_INF, s)
                sel = ((lane2 >> 4) == r) if p == 0 else ((lane2 & (PEER_TOPK - 1)) == r)
                cv = jnp.where(sel, m, cv)
                ci = jnp.where(sel, i * float(PEER_NKEYS) if p == 0 else i, ci)
            cand = cv if cand is None else cand + cv
            cidx = ci if cidx is None else cidx + ci
        best = jnp.zeros((tm, PEER_NKEYS), F32)
        eid = jnp.zeros((tm, PEER_NKEYS), F32)
        m0 = None
        for r in range(PEER_TOPK):
            m = jnp.max(cand, axis=-1, keepdims=True)
            e = jnp.min(jnp.where(cand == m, cidx, big), axis=-1, keepdims=True)
            cand = jnp.where(cidx == e, NEG_INF, cand)
            sel = lane == h * PEER_TOPK + r
            best = jnp.where(sel, m, best)
            eid = jnp.where(sel, e, eid)
            m0 = m if m0 is None else m0
        inhead = (lane >> 4) == h
        ex = jnp.where(inhead, jnp.exp(best - m0), 0.0)
        gate = ex / jnp.sum(ex, axis=-1, keepdims=True)
        idx_ref[...] = jnp.where(inhead, eid.astype(I32), idx_ref[...])
        gate_ref[...] = jnp.where(inhead, gate, gate_ref[...])
        return carry

    idx_ref[...] = jnp.zeros_like(idx_ref)
    gate_ref[...] = jnp.zeros_like(gate_ref)
    lax.fori_loop(0, PEER_HEADS, head, 0)


def _topk(s):
    n = s.shape[0]
    tm = min(TOPK_ROWS, n)
    return pl.pallas_call(
        _topk_kernel,
        grid=(n // tm,),
        in_specs=[pl.BlockSpec((tm, s.shape[1]), lambda i: (i, 0))],
        out_specs=[pl.BlockSpec((tm, PEER_SLOTS), lambda i: (i, 0)), pl.BlockSpec((tm, PEER_SLOTS), lambda i: (i, 0))],
        out_shape=[jax.ShapeDtypeStruct((n, PEER_SLOTS), I32), jax.ShapeDtypeStruct((n, PEER_SLOTS), F32)],
        compiler_params=_cparams("parallel"),
        name="topk",
    )(s)


SC_CORES = 2
SC_SUBCORES = 16
SC_LANES = 16
SC_WORKERS = SC_CORES * SC_SUBCORES
PEER_GROUP = 8
PEER_ITEMS = PEER_GROUP * PEER_HEADS
SC_CHUNK = SC_LANES * SC_LANES


_SC_PARAMS = pltpu.CompilerParams(needs_layout_passes=False)


def _sc_mesh():
    return plsc.VectorSubcoreMesh(core_axis_name="c", subcore_axis_name="s")


def _sc_worker():
    return lax.axis_index("s") * SC_CORES + lax.axis_index("c")


def _sc_item_pipeline(table_hbm, idx_v, rows0, rows1, sem0, sem1, compute):
    def gather(j, rows, sem):
        return pltpu.make_async_copy(table_hbm.at[idx_v.at[j]], rows, sem)

    gather(0, rows0, sem0).start()

    @pl.loop(0, PEER_ITEMS // 2)
    def _(i):
        j = 2 * i
        gather(j + 1, rows1, sem1).start()
        gather(j, rows0, sem0).wait()
        compute(j, rows0)

        @pl.when(i < PEER_ITEMS // 2 - 1)
        def _():
            gather(j + 2, rows0, sem0).start()

        gather(j + 1, rows1, sem1).wait()
        compute(j + 1, rows1)


def _peer_dot_kernel(x_hbm, idx_hbm, u_hbm, a_hbm, x_v, idx_v, rows0, rows1, a_v, sem0, sem1):
    tok_per_w = x_hbm.shape[0] // SC_WORKERS
    wid = _sc_worker()
    lane = lax.iota(I32, SC_LANES)

    def compute(j, rows):
        t = j // PEER_HEADS

        def chunk(c, acc):
            base = c * SC_CHUNK
            xs = [x_v[t, pl.ds(base + k * SC_LANES, SC_LANES)] for k in range(SC_LANES)]
            new = []
            for r in range(PEER_TOPK):
                a_r = acc[r]
                for k in range(SC_LANES):
                    a_r = a_r + rows[r, pl.ds(base + k * SC_LANES, SC_LANES)] * xs[k]
                new.append(a_r)
            return tuple(new)

        zero = jnp.zeros((SC_LANES,), F32)
        acc = lax.fori_loop(0, D_MODEL // SC_CHUNK, chunk, (zero,) * PEER_TOPK)
        out = zero
        for r in range(PEER_TOPK):
            out = jnp.where(lane == r, jnp.sum(acc[r]), out)
        a_v[j, :] = out

    @pl.loop(0, tok_per_w // PEER_GROUP)
    def _(g):
        tok0 = wid * tok_per_w + g * PEER_GROUP
        item0 = tok0 * PEER_HEADS
        pltpu.sync_copy(x_hbm.at[pl.ds(tok0, PEER_GROUP)], x_v)
        pltpu.sync_copy(idx_hbm.at[pl.ds(item0, PEER_ITEMS)], idx_v)
        _sc_item_pipeline(u_hbm, idx_v, rows0, rows1, sem0, sem1, compute)
        pltpu.sync_copy(a_v, a_hbm.at[pl.ds(item0, PEER_ITEMS)])


def _peer_sum_kernel(w_hbm, idx_hbm, v_hbm, o_hbm, w_v, idx_v, rows0, rows1, o_v, sem0, sem1):
    tok_per_w = o_hbm.shape[0] // SC_WORKERS
    wid = _sc_worker()
    zero = jnp.zeros((SC_LANES,), F32)

    def compute(j, rows):
        t = j // PEER_HEADS
        jv = jnp.full((SC_LANES,), j, I32)
        ws = [plsc.load_gather(w_v, [jv, jnp.full((SC_LANES,), r, I32)]) for r in range(PEER_TOPK)]

        @pl.loop(0, D_MODEL // SC_CHUNK)
        def _(c):
            base = c * SC_CHUNK
            accs = [zero] * SC_LANES
            for r in range(PEER_TOPK):
                for k in range(SC_LANES):
                    accs[k] = accs[k] + rows[r, pl.ds(base + k * SC_LANES, SC_LANES)] * ws[r]
            for k in range(SC_LANES):
                cols = pl.ds(base + k * SC_LANES, SC_LANES)
                o_v[t, cols] = o_v[t, cols] + accs[k]

    @pl.loop(0, tok_per_w // PEER_GROUP)
    def _(g):
        tok0 = wid * tok_per_w + g * PEER_GROUP
        item0 = tok0 * PEER_HEADS
        pltpu.sync_copy(w_hbm.at[pl.ds(item0, PEER_ITEMS)], w_v)
        pltpu.sync_copy(idx_hbm.at[pl.ds(item0, PEER_ITEMS)], idx_v)

        @pl.loop(0, PEER_GROUP)
        def _(t):
            @pl.loop(0, D_MODEL // SC_LANES)
            def _(k):
                o_v[t, pl.ds(k * SC_LANES, SC_LANES)] = zero

        _sc_item_pipeline(v_hbm, idx_v, rows0, rows1, sem0, sem1, compute)
        pltpu.sync_copy(o_v, o_hbm.at[pl.ds(tok0, PEER_GROUP)])


def _sc_scratch(first):
    return [first,
            pltpu.VMEM((PEER_ITEMS, PEER_TOPK), I32),
            pltpu.VMEM((PEER_TOPK, D_MODEL), F32),
            pltpu.VMEM((PEER_TOPK, D_MODEL), F32)]


def _peer_dot(h2, idx, u):
    n = h2.shape[0]
    assert n % (SC_WORKERS * PEER_GROUP) == 0
    call = pl.kernel(
        _peer_dot_kernel,
        out_type=jax.ShapeDtypeStruct((n * PEER_HEADS, PEER_TOPK), F32),
        mesh=_sc_mesh(),
        scratch_types=_sc_scratch(pltpu.VMEM((PEER_GROUP, D_MODEL), F32))
        + [pltpu.VMEM((PEER_ITEMS, PEER_TOPK), F32), pltpu.SemaphoreType.DMA, pltpu.SemaphoreType.DMA],
        compiler_params=_SC_PARAMS,
    )
    return call(h2, idx, u)


def _peer_sum(w, idx, v, n):
    assert n % (SC_WORKERS * PEER_GROUP) == 0
    call = pl.kernel(
        _peer_sum_kernel,
        out_type=jax.ShapeDtypeStruct((n, D_MODEL), F32),
        mesh=_sc_mesh(),
        scratch_types=_sc_scratch(pltpu.VMEM((PEER_ITEMS, PEER_TOPK), F32))
        + [pltpu.VMEM((PEER_GROUP, D_MODEL), F32), pltpu.SemaphoreType.DMA, pltpu.SemaphoreType.DMA],
        compiler_params=_SC_PARAMS,
    )
    return call(w, idx, v)


def _gelu_gate_kernel(a_ref, g_ref, w_ref):
    a = a_ref[...]
    w_ref[...] = g_ref[...] * (0.5 * a * (1.0 + lax.erf(a * (2.0 ** -0.5))))


def _gelu_gate(a, gate):
    n = a.shape[0]
    tm = min(2048, n)
    spec = pl.BlockSpec((tm, PEER_SLOTS), lambda i: (i, 0))
    return pl.pallas_call(
        _gelu_gate_kernel, grid=(n // tm,), in_specs=[spec, spec], out_specs=spec,
        out_shape=jax.ShapeDtypeStruct(a.shape, F32), compiler_params=_cparams("parallel"), name="gelu_gate",
    )(a, gate)


def _residual_kernel(x_ref, g_ref, p_ref, o_ref):
    o_ref[0] = x_ref[0] + g_ref[0] * p_ref[0]


def _residual(x1, g2, peer):
    B, T, _ = x1.shape
    tm = min(2 * ROW_TILE, T)
    row = pl.BlockSpec((1, tm, D_MODEL), lambda b, t: (b, t, 0))
    vec = pl.BlockSpec((1, 1, D_MODEL), lambda b, t: (b, 0, 0))
    return pl.pallas_call(
        _residual_kernel, grid=(B, T // tm), in_specs=[row, vec, row], out_specs=row,
        out_shape=jax.ShapeDtypeStruct(x1.shape, F32), compiler_params=_cparams("parallel", "parallel"),
        name="residual",
    )(x1, g2, peer)


def kernel(x, c, ctx, c_ctx, w_ada, b_ada, norm_mix, norm_ffn, w_in, q_norm, k_norm, attn_sink, hgrn_lb_logits,
           hgrn_norm, w_out, peer_w_q, peer_sub_keys, peer_u, peer_v):
    assert w_ada.shape[0] == 1, "single-layer block"
    B, T, D = x.shape
    L = ctx.shape[1]
    n = B * T

    cvecs = jnp.zeros((SUBLANES, D), F32).at[:B].set(c).at[B].set(c_ctx)
    mod = _ada(cvecs, w_ada[0], b_ada[0])
    part = lambda rows, i: rows[:, None, i * D:(i + 1) * D]
    mod_x = mod[:B]
    mod_c = jnp.broadcast_to(mod[B:B + 1], (B, 6 * D))
    sh1, sc1, g1, sh2, sc2, g2 = (part(mod_x, i) for i in range(6))

    lbs = jnp.cumsum(jax.nn.softmax(hgrn_lb_logits.astype(F32), axis=1), axis=1)
    lb_f, lb_b = lbs[0, 0].reshape(1, HGRN_WIDTH), lbs[1, 0].reshape(1, HGRN_WIDTH)

    w_in_b = w_in[0].astype(BF16)
    kx, vx, ffx, fbx, ix, qx, qhx, gx = _inproj(x, sh1, sc1, norm_mix[0], w_in_b, _rope_tables(T),
                                                q_norm[0], k_norm[0])
    kc, vc, ffc, fbc, ic, _, qhc, _ = _inproj(ctx, part(mod_c, 0), part(mod_c, 1), norm_mix[0], w_in_b,
                                              _identity_rope(L), q_norm[0], k_norm[0])

    attn = _attention(qx, kx, vx, kc, vc, attn_sink[0])

    s0 = jnp.zeros((B, HGRN_HEADS, HGRN_D, HGRN_D), F32)
    _, _, sfc, sbc = _hgrn(ffc, fbc, ic, qhc, lb_f, lb_b, s0, s0)
    of, ob, _, _ = _hgrn(ffx, fbx, ix, qhx, lb_f, lb_b, sfc, sbc)

    sk = peer_sub_keys[0].reshape(2 * PEER_HEADS, PEER_NKEYS, PEER_DHALF).astype(BF16)
    x1, h2, s = _outproj(attn, of, ob, gx, x, g1, sh2, sc2, hgrn_norm[0], norm_ffn[0],
                         w_out[0].astype(BF16), peer_w_q[0].astype(BF16), sk)

    idx, gate = _topk(s.reshape(n, 2 * PEER_HEADS * PEER_NKEYS))
    idx16 = idx.reshape(n * PEER_HEADS, PEER_TOPK)
    a = _peer_dot(h2.reshape(n, D), idx16, peer_u[0])
    w = _gelu_gate(a.reshape(n, PEER_SLOTS), gate)
    peer = _peer_sum(w.reshape(n * PEER_HEADS, PEER_TOPK), idx16, peer_v[0], n)
    return _residual(x1, g2, peer.reshape(B, T, D))
```

```python
import functools

import jax
import jax.numpy as jnp
from jax import lax
from jax.experimental import pallas as pl
from jax.experimental.pallas import tpu as pltpu
from jax.experimental.pallas import tpu_sc as plsc

F32 = jnp.float32
BF16 = jnp.bfloat16
I32 = jnp.int32

D_MODEL = 2048
GRID_W = 64
EPS = 1e-6
HEAD_DIM = 128
ATTN_HEADS = 8
ATTN_KV_HEADS = 2
ATTN_GROUP = ATTN_HEADS // ATTN_KV_HEADS
BAND_BLOCK = 128
ROPE_THETA = 10000.0
HGRN_HEADS = 8
HGRN_D = 128
ATTN_WIDTH = ATTN_HEADS * HEAD_DIM
KV_WIDTH = ATTN_KV_HEADS * HEAD_DIM
HGRN_WIDTH = HGRN_HEADS * HGRN_D
COL_K = 0
COL_V = COL_K + KV_WIDTH
COL_FF = COL_V + KV_WIDTH
COL_FB = COL_FF + HGRN_WIDTH
COL_I = COL_FB + HGRN_WIDTH
COL_Q = COL_I + HGRN_WIDTH
COL_QH = COL_Q + ATTN_WIDTH
COL_G = COL_QH + HGRN_WIDTH
N_IN_COLS = COL_G + HGRN_WIDTH
PEER_HEADS = 8
PEER_NKEYS = 128
PEER_DHALF = 128
PEER_TOPK = 16
PEER_SLOTS = PEER_HEADS * PEER_TOPK

LANES = 128
SUBLANES = 8
VMEM_LIMIT_BYTES = 56 * 1024 * 1024

ROW_TILE = 256
HGRN_CHUNK = 128
HGRN_STEP = 512
NEG_INF = float("-inf")


def _cparams(*sem):
    return pltpu.CompilerParams(dimension_semantics=sem, vmem_limit_bytes=VMEM_LIMIT_BYTES)


def _resident(shape):
    nd = len(shape)
    return pl.BlockSpec(shape, lambda *_: (0,) * nd, pipeline_mode=pl.Buffered(1))


def _ada_kernel(c_ref, w_ref, b_ref, o_ref):
    c = c_ref[...]
    s = (c * jax.nn.sigmoid(c)).astype(BF16)
    o_ref[...] = jnp.dot(s, w_ref[...].astype(BF16), preferred_element_type=F32) + b_ref[...]


def _ada(cvecs, w, b):
    n = w.shape[1]
    tn = 1024
    return pl.pallas_call(
        _ada_kernel,
        grid=(n // tn,),
        in_specs=[pl.BlockSpec((SUBLANES, D_MODEL), lambda j: (0, 0)),
                  pl.BlockSpec((D_MODEL, tn), lambda j: (0, j)),
                  pl.BlockSpec((1, tn), lambda j: (0, j))],
        out_specs=pl.BlockSpec((SUBLANES, tn), lambda j: (0, j)),
        out_shape=jax.ShapeDtypeStruct((SUBLANES, n), F32),
        compiler_params=_cparams("arbitrary"),
        name="ada",
    )(cvecs, w, b.reshape(1, n))


def _rms(x, gain):
    return x * lax.rsqrt(jnp.mean(x * x, axis=-1, keepdims=True) + EPS) * gain


def _rope(x, cos, sin_a, sin_b):
    q = HEAD_DIM // 4
    return x * cos + pltpu.roll(x, HEAD_DIM - q, 1) * sin_a + pltpu.roll(x, q, 1) * sin_b


def _inproj_kernel(x_ref, sh_ref, sc_ref, gain_ref, w_ref, cos_ref, sa_ref, sb_ref, qg_ref, kg_ref,
                   k_ref, v_ref, ff_ref, fb_ref, i_ref, q_ref, qh_ref, g_ref):
    x = x_ref[0]
    h = _rms(x, gain_ref[...]) * (1.0 + sc_ref[0]) + sh_ref[0]
    hb = h.astype(BF16)

    def seg(lo, width):
        return jnp.dot(hb, w_ref[:, lo:lo + width], preferred_element_type=F32)

    cos, sa, sb = cos_ref[...], sa_ref[...], sb_ref[...]

    def normed_heads(p, gain, n_heads, out_ref):
        for hd in range(n_heads):
            ph = p[:, hd * HEAD_DIM:(hd + 1) * HEAD_DIM]
            out_ref[0, :, hd * HEAD_DIM:(hd + 1) * HEAD_DIM] = _rope(_rms(ph, gain), cos, sa, sb).astype(BF16)

    normed_heads(seg(COL_K, KV_WIDTH), kg_ref[...], ATTN_KV_HEADS, k_ref)
    v_ref[0] = seg(COL_V, KV_WIDTH).astype(BF16)
    ff_ref[0] = seg(COL_FF, HGRN_WIDTH)
    fb_ref[0] = seg(COL_FB, HGRN_WIDTH)
    i_ref[0] = seg(COL_I, HGRN_WIDTH).astype(BF16)
    normed_heads(seg(COL_Q, ATTN_WIDTH), qg_ref[...], ATTN_HEADS, q_ref)
    qh_ref[0] = seg(COL_QH, HGRN_WIDTH).astype(BF16)
    g_ref[0] = seg(COL_G, HGRN_WIDTH).astype(BF16)


def _inproj(x, shift, scale, gain, w_bf16, rope, q_gain, k_gain):
    B, T, _ = x.shape
    tm = min(ROW_TILE, T)
    row = lambda w: pl.BlockSpec((1, tm, w), lambda b, t: (b, t, 0))
    vec = pl.BlockSpec((1, 1, D_MODEL), lambda b, t: (b, 0, 0))
    tab = pl.BlockSpec((tm, HEAD_DIM), lambda b, t: (t, 0))
    out_w = [(KV_WIDTH, BF16), (KV_WIDTH, BF16), (HGRN_WIDTH, F32), (HGRN_WIDTH, F32), (HGRN_WIDTH, BF16),
             (ATTN_WIDTH, BF16), (HGRN_WIDTH, BF16), (HGRN_WIDTH, BF16)]
    return pl.pallas_call(
        _inproj_kernel,
        grid=(B, T // tm),
        in_specs=[row(D_MODEL), vec, vec, _resident((1, D_MODEL)), _resident((D_MODEL, N_IN_COLS)),
                  tab, tab, tab, _resident((1, HEAD_DIM)), _resident((1, HEAD_DIM))],
        out_specs=[row(w) for w, _ in out_w],
        out_shape=[jax.ShapeDtypeStruct((B, T, w), dt) for w, dt in out_w],
        compiler_params=_cparams("parallel", "parallel"),
        name="inproj",
    )(x, shift, scale, gain.reshape(1, D_MODEL), w_bf16, *rope,
      q_gain.reshape(1, HEAD_DIM), k_gain.reshape(1, HEAD_DIM))


def _rope_tables(T):
    rows = T // GRID_W
    row_pos = jnp.repeat(jnp.arange(rows, dtype=F32), GRID_W)
    col_pos = jnp.tile(jnp.arange(GRID_W, dtype=F32), rows)
    half = HEAD_DIM // 2
    inv_freq = jnp.power(ROPE_THETA, -jnp.arange(0, half, 2, dtype=F32) / half)
    ang_r = row_pos[:, None] * inv_freq
    ang_c = col_pos[:, None] * inv_freq
    cr, sr, cc, sc = jnp.cos(ang_r), jnp.sin(ang_r), jnp.cos(ang_c), jnp.sin(ang_c)
    z = jnp.zeros_like(sr)
    return (jnp.concatenate([cr, cr, cc, cc], -1),
            jnp.concatenate([-sr, z, -sc, z], -1),
            jnp.concatenate([z, sr, z, sc], -1))


def _identity_rope(T):
    return (jnp.ones((T, HEAD_DIM), F32), jnp.zeros((T, HEAD_DIM), F32), jnp.zeros((T, HEAD_DIM), F32))


_NT = (((1,), (1,)), ((), ()))
_TN = (((0,), (0,)), ((), ()))


def _attn_kernel(sink_ref, q_ref, kp_ref, kc_ref, kn_ref, vp_ref, vc_ref, vn_ref, kx_ref, vx_ref, o_ref):
    n = pl.program_id(1)
    nb = pl.num_programs(1)
    blk = BAND_BLOCK
    rows = ATTN_GROUP * blk
    qi = lax.broadcasted_iota(I32, (rows, blk), 0) & (blk - 1)
    kj = lax.broadcasted_iota(I32, (rows, blk), 1)
    prev_ok = kj >= qi + jnp.where(n > 0, 0, blk)
    next_ok = kj <= qi - jnp.where(n < nb - 1, 0, blk)
    scale = HEAD_DIM ** -0.5
    for h in range(ATTN_KV_HEADS):
        cs = slice(h * HEAD_DIM, (h + 1) * HEAD_DIM)
        heads = [h * ATTN_GROUP + g for g in range(ATTN_GROUP)]
        q4 = jnp.concatenate([q_ref[0, :, hd * HEAD_DIM:(hd + 1) * HEAD_DIM] for hd in heads], axis=0)

        def scores(k_ref):
            return lax.dot_general(q4, k_ref[0, :, cs], _NT, preferred_element_type=F32) * scale

        s_p = jnp.where(prev_ok, scores(kp_ref), NEG_INF)
        s_c = scores(kc_ref)
        s_n = jnp.where(next_ok, scores(kn_ref), NEG_INF)
        s_x = scores(kx_ref)
        sink = jnp.concatenate([jnp.full((blk, 1), sink_ref[hd], F32) for hd in heads], axis=0)
        m = jnp.maximum(jnp.maximum(jnp.max(s_p, -1, keepdims=True), jnp.max(s_c, -1, keepdims=True)),
                        jnp.maximum(jnp.max(s_n, -1, keepdims=True), jnp.max(s_x, -1, keepdims=True)))
        m = jnp.maximum(m, sink)
        p_p, p_c, p_n, p_x = jnp.exp(s_p - m), jnp.exp(s_c - m), jnp.exp(s_n - m), jnp.exp(s_x - m)
        denom = (jnp.sum(p_p, -1, keepdims=True) + jnp.sum(p_c, -1, keepdims=True)
                 + jnp.sum(p_n, -1, keepdims=True) + jnp.sum(p_x, -1, keepdims=True) + jnp.exp(sink - m))

        def pv(p, v_ref):
            return jnp.dot(p.astype(BF16), v_ref[0, :, cs], preferred_element_type=F32)

        o = (pv(p_p, vp_ref) + pv(p_c, vc_ref) + pv(p_n, vn_ref) + pv(p_x, vx_ref)) / denom
        for g, hd in enumerate(heads):
            o_ref[0, :, hd * HEAD_DIM:(hd + 1) * HEAD_DIM] = o[g * blk:(g + 1) * blk].astype(BF16)


def _attention(q, k, v, k_ctx, v_ctx, sink):
    B, T, _ = q.shape
    L = k_ctx.shape[1]
    nb = T // BAND_BLOCK
    kv = lambda f: pl.BlockSpec((1, BAND_BLOCK, KV_WIDTH), lambda b, n: (b, f(n), 0))
    prev, cur, nxt = (lambda n: jnp.maximum(n - 1, 0)), (lambda n: n), (lambda n: jnp.minimum(n + 1, nb - 1))
    ctx = pl.BlockSpec((1, L, KV_WIDTH), lambda b, n: (b, 0, 0))
    return pl.pallas_call(
        _attn_kernel,
        grid=(B, nb),
        in_specs=[pl.BlockSpec(memory_space=pltpu.SMEM),
                  pl.BlockSpec((1, BAND_BLOCK, ATTN_WIDTH), lambda b, n: (b, n, 0)),
                  kv(prev), kv(cur), kv(nxt), kv(prev), kv(cur), kv(nxt), ctx, ctx],
        out_specs=pl.BlockSpec((1, BAND_BLOCK, ATTN_WIDTH), lambda b, n: (b, n, 0)),
        out_shape=jax.ShapeDtypeStruct((B, T, ATTN_WIDTH), BF16),
        compiler_params=_cparams("parallel", "parallel"),
        name="attn",
    )(sink, q, k, k, k, v, v, v, k_ctx, v_ctx)


_DIAG = SUBLANES
_LEVELS = (64, 32, 16, 8)


def _hgrn_chunk(logit, v, q, lb, st_ref, reverse):
    C = HGRN_CHUNK
    f = lb + (1.0 - lb) * jax.nn.sigmoid(logit)
    lf = jnp.log(f)
    kk = 1.0 - f
    qf = q.astype(F32)
    r = lax.broadcasted_iota(I32, (C, C), 0)
    c = lax.broadcasted_iota(I32, (C, C), 1)
    incl = (r <= c) if reverse else (r >= c)
    a = jnp.dot(incl.astype(F32), lf, precision=lax.Precision.HIGHEST, preferred_element_type=F32)
    a_end = a[0:1] if reverse else a[C - 1:C]

    st = st_ref[...]
    inter = lax.dot_general((qf * jnp.exp(a)).astype(BF16), st.astype(BF16), _NT, preferred_element_type=F32)
    kd = (kk * jnp.exp(a_end - a)).astype(BF16)
    st_ref[...] = st * jnp.exp(a_end) + lax.dot_general(v, kd, _TN, preferred_element_type=F32)

    later = (r < c) if reverse else (r > c)
    att = jnp.zeros((C, C), F32)
    for m in _LEVELS:
        a3 = a.reshape(C // (2 * m), 2 * m, HGRN_D)
        edge = a3[:, m:m + 1, :] if reverse else a3[:, m - 1:m, :]
        e = jnp.exp(-jnp.abs(a3 - edge)).reshape(C, HGRN_D)
        p = lax.dot_general((qf * e).astype(BF16), (kk * e).astype(BF16), _NT, preferred_element_type=F32)
        pair = ((r ^ c) >> (m.bit_length() - 1)) == 1
        att = jnp.where(pair & later, p, att)

    lane = lax.broadcasted_iota(I32, (_DIAG, C), 1)
    sub = lax.broadcasted_iota(I32, (_DIAG, C), 0)
    blocks = []
    for j in range(C // _DIAG):
        rows = slice(j * _DIAG, (j + 1) * _DIAG)
        a_j, q_j, k_j = a[rows], qf[rows], kk[rows]
        blk = jnp.zeros((_DIAG, C), F32)
        for s in range(_DIAG):
            e = jnp.exp(jnp.minimum(a_j - a_j[s:s + 1], 0.0))
            col = jnp.sum(q_j * e * k_j[s:s + 1], axis=-1, keepdims=True)
            ok = (sub <= s) if reverse else (sub >= s)
            blk = jnp.where((lane == j * _DIAG + s) & ok, col, blk)
        blocks.append(blk)
    att = att + jnp.concatenate(blocks, axis=0)
    return inter + jnp.dot(att.astype(BF16), v, preferred_element_type=F32)


def _hgrn_kernel(ff_ref, vf_ref, qf_ref, fb_ref, vb_ref, qb_ref, lbf_ref, lbb_ref, s0f_ref, s0b_ref,
                 of_ref, ob_ref, sf_ref, sb_ref, stf, stb):
    step = pl.program_id(2)
    nsub = ff_ref.shape[1] // HGRN_CHUNK

    @pl.when(step == 0)
    def _():
        stf[...] = s0f_ref[0, 0]
        stb[...] = s0b_ref[0, 0]

    def body(j, carry):
        fo = pl.multiple_of(j * HGRN_CHUNK, HGRN_CHUNK)
        rows = pl.ds(fo, HGRN_CHUNK)
        of_ref[0, rows, :] = _hgrn_chunk(ff_ref[0, rows, :], vf_ref[0, rows, :], qf_ref[0, rows, :],
                                         lbf_ref[...], stf, False)
        bo = pl.multiple_of((nsub - 1 - j) * HGRN_CHUNK, HGRN_CHUNK)
        rows = pl.ds(bo, HGRN_CHUNK)
        ob_ref[0, rows, :] = _hgrn_chunk(fb_ref[0, rows, :], vb_ref[0, rows, :], qb_ref[0, rows, :],
                                         lbb_ref[...], stb, True)
        return carry

    lax.fori_loop(0, nsub, body, 0)

    @pl.when(step == pl.num_programs(2) - 1)
    def _():
        sf_ref[0, 0] = stf[...]
        sb_ref[0, 0] = stb[...]


def _hgrn(ff, fb, val, q, lb_f, lb_b, s0f, s0b):
    B, T, _ = ff.shape
    ts = min(HGRN_STEP, T)
    ns = T // ts
    fwd = pl.BlockSpec((1, ts, HGRN_D), lambda b, h, s: (b, s, h))
    bwd = pl.BlockSpec((1, ts, HGRN_D), lambda b, h, s: (b, ns - 1 - s, h))
    lbs = pl.BlockSpec((1, HGRN_D), lambda b, h, s: (0, h))
    st = pl.BlockSpec((1, 1, HGRN_D, HGRN_D), lambda b, h, s: (b, h, 0, 0))
    o_sds = jax.ShapeDtypeStruct((B, T, HGRN_WIDTH), F32)
    s_sds = jax.ShapeDtypeStruct((B, HGRN_HEADS, HGRN_D, HGRN_D), F32)
    return pl.pallas_call(
        _hgrn_kernel,
        grid=(B, HGRN_HEADS, ns),
        in_specs=[fwd, fwd, fwd, bwd, bwd, bwd, lbs, lbs, st, st],
        out_specs=[fwd, bwd, st, st],
        out_shape=[o_sds, o_sds, s_sds, s_sds],
        scratch_shapes=[pltpu.VMEM((HGRN_D, HGRN_D), F32), pltpu.VMEM((HGRN_D, HGRN_D), F32)],
        compiler_params=_cparams("parallel", "parallel", "arbitrary"),
        name="hgrn",
    )(ff, val, q, fb, val, q, lb_f, lb_b, s0f, s0b)


def _outproj_kernel(attn_ref, of_ref, ob_ref, g_ref, x_ref, g1_ref, sh2_ref, sc2_ref, og_ref, nf_ref,
                    wo_ref, wq_ref, sk_ref, x1_ref, h2_ref, s_ref):
    o = of_ref[0] + ob_ref[0]
    og = og_ref[...]
    parts = []
    for hd in range(HGRN_HEADS):
        cs = slice(hd * HGRN_D, (hd + 1) * HGRN_D)
        gh = g_ref[0, :, cs].astype(F32)
        parts.append((_rms(o[:, cs], og) * (gh * jax.nn.sigmoid(gh))).astype(BF16))
    hg = jnp.concatenate(parts, axis=-1)
    mix = (jnp.dot(attn_ref[0], wo_ref[:ATTN_WIDTH], preferred_element_type=F32)
           + jnp.dot(hg, wo_ref[ATTN_WIDTH:], preferred_element_type=F32))
    x1 = x_ref[0] + g1_ref[0] * mix
    x1_ref[0] = x1
    h2 = _rms(x1, nf_ref[...]) * (1.0 + sc2_ref[0]) + sh2_ref[0]
    h2_ref[0] = h2
    pq = jnp.dot(h2.astype(BF16), wq_ref[...], preferred_element_type=F32)
    for hp in range(2 * PEER_HEADS):
        cs = slice(hp * PEER_DHALF, (hp + 1) * PEER_DHALF)
        s_ref[0, :, cs] = lax.dot_general(pq[:, cs].astype(BF16), sk_ref[hp], _NT, preferred_element_type=F32)


def _outproj(attn, of, ob, g, x, g1, sh2, sc2, o_gain, norm_ffn, wo_bf16, wq_bf16, sk_bf16):
    B, T, _ = x.shape
    tm = min(ROW_TILE, T)
    row = lambda w: pl.BlockSpec((1, tm, w), lambda b, t: (b, t, 0))
    vec = pl.BlockSpec((1, 1, D_MODEL), lambda b, t: (b, 0, 0))
    n_sk = 2 * PEER_HEADS
    sds = lambda w: jax.ShapeDtypeStruct((B, T, w), F32)
    return pl.pallas_call(
        _outproj_kernel,
        grid=(B, T // tm),
        in_specs=[row(ATTN_WIDTH), row(HGRN_WIDTH), row(HGRN_WIDTH), row(HGRN_WIDTH), row(D_MODEL),
                  vec, vec, vec, _resident((1, HGRN_D)), _resident((1, D_MODEL)),
                  _resident((ATTN_WIDTH + HGRN_WIDTH, D_MODEL)), _resident((D_MODEL, n_sk * PEER_DHALF)),
                  _resident((n_sk, PEER_NKEYS, PEER_DHALF))],
        out_specs=[row(D_MODEL), row(D_MODEL), row(n_sk * PEER_NKEYS)],
        out_shape=[sds(D_MODEL), sds(D_MODEL), sds(n_sk * PEER_NKEYS)],
        compiler_params=_cparams("parallel", "parallel"),
        name="outproj",
    )(attn, of, ob, g, x, g1, sh2, sc2, o_gain.reshape(1, HGRN_D), norm_ffn.reshape(1, D_MODEL),
      wo_bf16, wq_bf16, sk_bf16)


TOPK_ROWS = 128


def _topk_kernel(s_ref, idx_ref, gate_ref):
    tm = s_ref.shape[0]
    lane = lax.broadcasted_iota(I32, (tm, PEER_NKEYS), 1)
    lane_f = lane.astype(F32)
    lane2 = lax.broadcasted_iota(I32, (tm, PEER_TOPK * PEER_TOPK), 1)
    big = float(PEER_NKEYS * PEER_NKEYS)

    def head(h, carry):
        cand, cidx = None, None
        for p in range(2):
            s = s_ref[:, pl.ds(pl.multiple_of((2 * h + p) * PEER_NKEYS, PEER_NKEYS), PEER_NKEYS)]
            cv = jnp.zeros((tm, PEER_TOPK * PEER_TOPK), F32)
            ci = jnp.zeros((tm, PEER_TOPK * PEER_TOPK), F32)
            for r in range(PEER_TOPK):
                m = jnp.max(s, axis=-1, keepdims=True)
                i = jnp.min(jnp.where(s == m, lane_f, float(PEER_NKEYS)), axis=-1, keepdims=True)
                s = jnp.where(lane_f == i, NEG_INF, s)
                sel = ((lane2 >> 4) == r) if p == 0 else ((lane2 & (PEER_TOPK - 1)) == r)
                cv = jnp.where(sel, m, cv)
                ci = jnp.where(sel, i * float(PEER_NKEYS) if p == 0 else i, ci)
            cand = cv if cand is None else cand + cv
            cidx = ci if cidx is None else cidx + ci
        best = jnp.zeros((tm, PEER_NKEYS), F32)
        eid = jnp.zeros((tm, PEER_NKEYS), F32)
        m0 = None
        for r in range(PEER_TOPK):
            m = jnp.max(cand, axis=-1, keepdims=True)
            e = jnp.min(jnp.where(cand == m, cidx, big), axis=-1, keepdims=True)
            cand = jnp.where(cidx == e, NEG_INF, cand)
            sel = lane == h * PEER_TOPK + r
            best = jnp.where(sel, m, best)
            eid = jnp.where(sel, e, eid)
            m0 = m if m0 is None else m0
        inhead = (lane >> 4) == h
        ex = jnp.where(inhead, jnp.exp(best - m0), 0.0)
        gate = ex / jnp.sum(ex, axis=-1, keepdims=True)
        idx_ref[...] = jnp.where(inhead, eid.astype(I32), idx_ref[...])
        gate_ref[...] = jnp.where(inhead, gate, gate_ref[...])
        return carry

    idx_ref[...] = jnp.zeros_like(idx_ref)
    gate_ref[...] = jnp.zeros_like(gate_ref)
    lax.fori_loop(0, PEER_HEADS, head, 0)


def _topk(s):
    n = s.shape[0]
    tm = min(TOPK_ROWS, n)
    return pl.pallas_call(
        _topk_kernel,
        grid=(n // tm,),
        in_specs=[pl.BlockSpec((tm, s.shape[1]), lambda i: (i, 0))],
        out_specs=[pl.BlockSpec((tm, PEER_SLOTS), lambda i: (i, 0)), pl.BlockSpec((tm, PEER_SLOTS), lambda i: (i, 0))],
        out_shape=[jax.ShapeDtypeStruct((n, PEER_SLOTS), I32), jax.ShapeDtypeStruct((n, PEER_SLOTS), F32)],
        compiler_params=_cparams("parallel"),
        name="topk",
    )(s)


SC_CORES = 2
SC_SUBCORES = 16
SC_LANES = 16
SC_WORKERS = SC_CORES * SC_SUBCORES
PEER_GROUP = 8
PEER_ITEMS = PEER_GROUP * PEER_HEADS
SC_UNROLL = 8


_SC_PARAMS = pltpu.CompilerParams(needs_layout_passes=False)


def _sc_mesh():
    return plsc.VectorSubcoreMesh(core_axis_name="c", subcore_axis_name="s")


def _sc_worker():
    return lax.axis_index("s") * SC_CORES + lax.axis_index("c")


def _sc_item_pipeline(table_hbm, idx_v, rows0, rows1, sem0, sem1, compute):
    def gather(j, rows, sem):
        return pltpu.make_async_copy(table_hbm.at[idx_v.at[j]], rows, sem)

    gather(0, rows0, sem0).start()

    @pl.loop(0, PEER_ITEMS // 2)
    def _(i):
        j = 2 * i
        gather(j + 1, rows1, sem1).start()
        gather(j, rows0, sem0).wait()
        compute(j, rows0)

        @pl.when(i < PEER_ITEMS // 2 - 1)
        def _():
            gather(j + 2, rows0, sem0).start()

        gather(j + 1, rows1, sem1).wait()
        compute(j + 1, rows1)


def _peer_dot_kernel(x_hbm, idx_hbm, u_hbm, a_hbm, x_v, idx_v, rows0, rows1, a_v, sem0, sem1):
    tok_per_w = x_hbm.shape[0] // SC_WORKERS
    wid = _sc_worker()
    lane = lax.iota(I32, SC_LANES)

    def compute(j, rows):
        t = j // PEER_HEADS

        zero = jnp.zeros((SC_LANES,), F32)

        @plsc.parallel_loop(0, D_MODEL // SC_LANES, unroll=SC_UNROLL, carry=(zero,) * PEER_TOPK)
        def acc(k, acc):
            cols = pl.ds(k * SC_LANES, SC_LANES)
            xk = x_v[t, cols]
            return tuple(acc[r] + rows[r, cols] * xk for r in range(PEER_TOPK))

        out = zero
        for r in range(PEER_TOPK):
            out = jnp.where(lane == r, jnp.sum(acc[r]), out)
        a_v[j, :] = out

    @pl.loop(0, tok_per_w // PEER_GROUP)
    def _(g):
        tok0 = wid * tok_per_w + g * PEER_GROUP
        item0 = tok0 * PEER_HEADS
        pltpu.sync_copy(x_hbm.at[pl.ds(tok0, PEER_GROUP)], x_v)
        pltpu.sync_copy(idx_hbm.at[pl.ds(item0, PEER_ITEMS)], idx_v)
        _sc_item_pipeline(u_hbm, idx_v, rows0, rows1, sem0, sem1, compute)
        pltpu.sync_copy(a_v, a_hbm.at[pl.ds(item0, PEER_ITEMS)])


def _peer_sum_kernel(w_hbm, idx_hbm, v_hbm, o_hbm, w_v, idx_v, rows0, rows1, o_v, sem0, sem1):
    tok_per_w = o_hbm.shape[0] // SC_WORKERS
    wid = _sc_worker()
    zero = jnp.zeros((SC_LANES,), F32)

    def compute(j, rows):
        t = j // PEER_HEADS
        jv = jnp.full((SC_LANES,), j, I32)
        ws = [plsc.load_gather(w_v, [jv, jnp.full((SC_LANES,), r, I32)]) for r in range(PEER_TOPK)]

        @plsc.parallel_loop(0, D_MODEL // SC_LANES, unroll=SC_UNROLL)
        def _(k):
            cols = pl.ds(k * SC_LANES, SC_LANES)
            terms = [rows[r, cols] * ws[r] for r in range(PEER_TOPK)]
            while len(terms) > 1:
                terms = [terms[i] + terms[i + 1] for i in range(0, len(terms), 2)]
            o_v[t, cols] = o_v[t, cols] + terms[0]

    @pl.loop(0, tok_per_w // PEER_GROUP)
    def _(g):
        tok0 = wid * tok_per_w + g * PEER_GROUP
        item0 = tok0 * PEER_HEADS
        pltpu.sync_copy(w_hbm.at[pl.ds(item0, PEER_ITEMS)], w_v)
        pltpu.sync_copy(idx_hbm.at[pl.ds(item0, PEER_ITEMS)], idx_v)

        @pl.loop(0, PEER_GROUP)
        def _(t):
            @pl.loop(0, D_MODEL // SC_LANES)
            def _(k):
                o_v[t, pl.ds(k * SC_LANES, SC_LANES)] = zero

        _sc_item_pipeline(v_hbm, idx_v, rows0, rows1, sem0, sem1, compute)
        pltpu.sync_copy(o_v, o_hbm.at[pl.ds(tok0, PEER_GROUP)])


def _sc_scratch(first):
    return [first,
            pltpu.VMEM((PEER_ITEMS, PEER_TOPK), I32),
            pltpu.VMEM((PEER_TOPK, D_MODEL), F32),
            pltpu.VMEM((PEER_TOPK, D_MODEL), F32)]


def _peer_dot(h2, idx, u):
    n = h2.shape[0]
    assert n % (SC_WORKERS * PEER_GROUP) == 0
    call = pl.kernel(
        _peer_dot_kernel,
        out_type=jax.ShapeDtypeStruct((n * PEER_HEADS, PEER_TOPK), F32),
        mesh=_sc_mesh(),
        scratch_types=_sc_scratch(pltpu.VMEM((PEER_GROUP, D_MODEL), F32))
        + [pltpu.VMEM((PEER_ITEMS, PEER_TOPK), F32), pltpu.SemaphoreType.DMA, pltpu.SemaphoreType.DMA],
        compiler_params=_SC_PARAMS,
    )
    return call(h2, idx, u)


def _peer_sum(w, idx, v, n):
    assert n % (SC_WORKERS * PEER_GROUP) == 0
    call = pl.kernel(
        _peer_sum_kernel,
        out_type=jax.ShapeDtypeStruct((n, D_MODEL), F32),
        mesh=_sc_mesh(),
        scratch_types=_sc_scratch(pltpu.VMEM((PEER_ITEMS, PEER_TOPK), F32))
        + [pltpu.VMEM((PEER_GROUP, D_MODEL), F32), pltpu.SemaphoreType.DMA, pltpu.SemaphoreType.DMA],
        compiler_params=_SC_PARAMS,
    )
    return call(w, idx, v)


def _gelu_gate_kernel(a_ref, g_ref, w_ref):
    a = a_ref[...]
    w_ref[...] = g_ref[...] * (0.5 * a * (1.0 + lax.erf(a * (2.0 ** -0.5))))


def _gelu_gate(a, gate):
    n = a.shape[0]
    tm = min(2048, n)
    spec = pl.BlockSpec((tm, PEER_SLOTS), lambda i: (i, 0))
    return pl.pallas_call(
        _gelu_gate_kernel, grid=(n // tm,), in_specs=[spec, spec], out_specs=spec,
        out_shape=jax.ShapeDtypeStruct(a.shape, F32), compiler_params=_cparams("parallel"), name="gelu_gate",
    )(a, gate)


def _residual_kernel(x_ref, g_ref, p_ref, o_ref):
    o_ref[0] = x_ref[0] + g_ref[0] * p_ref[0]


def _residual(x1, g2, peer):
    B, T, _ = x1.shape
    tm = min(2 * ROW_TILE, T)
    row = pl.BlockSpec((1, tm, D_MODEL), lambda b, t: (b, t, 0))
    vec = pl.BlockSpec((1, 1, D_MODEL), lambda b, t: (b, 0, 0))
    return pl.pallas_call(
        _residual_kernel, grid=(B, T // tm), in_specs=[row, vec, row], out_specs=row,
        out_shape=jax.ShapeDtypeStruct(x1.shape, F32), compiler_params=_cparams("parallel", "parallel"),
        name="residual",
    )(x1, g2, peer)


def kernel(x, c, ctx, c_ctx, w_ada, b_ada, norm_mix, norm_ffn, w_in, q_norm, k_norm, attn_sink, hgrn_lb_logits,
           hgrn_norm, w_out, peer_w_q, peer_sub_keys, peer_u, peer_v):
    assert w_ada.shape[0] == 1, "single-layer block"
    B, T, D = x.shape
    L = ctx.shape[1]
    n = B * T

    cvecs = jnp.zeros((SUBLANES, D), F32).at[:B].set(c).at[B].set(c_ctx)
    mod = _ada(cvecs, w_ada[0], b_ada[0])
    part = lambda rows, i: rows[:, None, i * D:(i + 1) * D]
    mod_x = mod[:B]
    mod_c = jnp.broadcast_to(mod[B:B + 1], (B, 6 * D))
    sh1, sc1, g1, sh2, sc2, g2 = (part(mod_x, i) for i in range(6))

    lbs = jnp.cumsum(jax.nn.softmax(hgrn_lb_logits.astype(F32), axis=1), axis=1)
    lb_f, lb_b = lbs[0, 0].reshape(1, HGRN_WIDTH), lbs[1, 0].reshape(1, HGRN_WIDTH)

    w_in_b = w_in[0].astype(BF16)
    kx, vx, ffx, fbx, ix, qx, qhx, gx = _inproj(x, sh1, sc1, norm_mix[0], w_in_b, _rope_tables(T),
                                                q_norm[0], k_norm[0])
    kc, vc, ffc, fbc, ic, _, qhc, _ = _inproj(ctx, part(mod_c, 0), part(mod_c, 1), norm_mix[0], w_in_b,
                                              _identity_rope(L), q_norm[0], k_norm[0])

    attn = _attention(qx, kx, vx, kc, vc, attn_sink[0])

    s0 = jnp.zeros((B, HGRN_HEADS, HGRN_D, HGRN_D), F32)
    _, _, sfc, sbc = _hgrn(ffc, fbc, ic, qhc, lb_f, lb_b, s0, s0)
    of, ob, _, _ = _hgrn(ffx, fbx, ix, qhx, lb_f, lb_b, sfc, sbc)

    sk = peer_sub_keys[0].reshape(2 * PEER_HEADS, PEER_NKEYS, PEER_DHALF).astype(BF16)
    x1, h2, s = _outproj(attn, of, ob, gx, x, g1, sh2, sc2, hgrn_norm[0], norm_ffn[0],
                         w_out[0].astype(BF16), peer_w_q[0].astype(BF16), sk)

    idx, gate = _topk(s.reshape(n, 2 * PEER_HEADS * PEER_NKEYS))
    idx16 = idx.reshape(n * PEER_HEADS, PEER_TOPK)
    a = _peer_dot(h2.reshape(n, D), idx16, peer_u[0])
    w = _gelu_gate(a.reshape(n, PEER_SLOTS), gate)
    peer = _peer_sum(w.reshape(n * PEER_HEADS, PEER_TOPK), idx16, peer_v[0], n)
    return _residual(x1, g2, peer.reshape(B, T, D))
```

```python
import functools

import jax
import jax.numpy as jnp
from jax import lax
from jax.experimental import pallas as pl
from jax.experimental.pallas import tpu as pltpu
from jax.experimental.pallas import tpu_sc as plsc

F32 = jnp.float32
BF16 = jnp.bfloat16
I32 = jnp.int32

D_MODEL = 2048
GRID_W = 64
EPS = 1e-6
HEAD_DIM = 128
ATTN_HEADS = 8
ATTN_KV_HEADS = 2
ATTN_GROUP = ATTN_HEADS // ATTN_KV_HEADS
BAND_BLOCK = 128
ROPE_THETA = 10000.0
HGRN_HEADS = 8
HGRN_D = 128
ATTN_WIDTH = ATTN_HEADS * HEAD_DIM
KV_WIDTH = ATTN_KV_HEADS * HEAD_DIM
HGRN_WIDTH = HGRN_HEADS * HGRN_D
COL_K = 0
COL_V = COL_K + KV_WIDTH
COL_FF = COL_V + KV_WIDTH
COL_FB = COL_FF + HGRN_WIDTH
COL_I = COL_FB + HGRN_WIDTH
COL_Q = COL_I + HGRN_WIDTH
COL_QH = COL_Q + ATTN_WIDTH
COL_G = COL_QH + HGRN_WIDTH
N_IN_COLS = COL_G + HGRN_WIDTH
PEER_HEADS = 8
PEER_NKEYS = 128
PEER_DHALF = 128
PEER_TOPK = 16
PEER_SLOTS = PEER_HEADS * PEER_TOPK

LANES = 128
SUBLANES = 8
VMEM_LIMIT_BYTES = 56 * 1024 * 1024

ROW_TILE = 256
HGRN_CHUNK = 128
HGRN_STEP = 512
NEG_INF = float("-inf")


def _cparams(*sem):
    return pltpu.CompilerParams(dimension_semantics=sem, vmem_limit_bytes=VMEM_LIMIT_BYTES)


def _resident(shape):
    nd = len(shape)
    return pl.BlockSpec(shape, lambda *_: (0,) * nd, pipeline_mode=pl.Buffered(1))


def _ada_kernel(c_ref, w_ref, b_ref, o_ref):
    c = c_ref[...]
    s = (c * jax.nn.sigmoid(c)).astype(BF16)
    o_ref[...] = jnp.dot(s, w_ref[...].astype(BF16), preferred_element_type=F32) + b_ref[...]


def _ada(cvecs, w, b):
    n = w.shape[1]
    tn = 1024
    return pl.pallas_call(
        _ada_kernel,
        grid=(n // tn,),
        in_specs=[pl.BlockSpec((SUBLANES, D_MODEL), lambda j: (0, 0)),
                  pl.BlockSpec((D_MODEL, tn), lambda j: (0, j)),
                  pl.BlockSpec((1, tn), lambda j: (0, j))],
        out_specs=pl.BlockSpec((SUBLANES, tn), lambda j: (0, j)),
        out_shape=jax.ShapeDtypeStruct((SUBLANES, n), F32),
        compiler_params=_cparams("arbitrary"),
        name="ada",
    )(cvecs, w, b.reshape(1, n))


def _rms(x, gain):
    return x * lax.rsqrt(jnp.mean(x * x, axis=-1, keepdims=True) + EPS) * gain


def _rope(x, cos, sin_a, sin_b):
    q = HEAD_DIM // 4
    return x * cos + pltpu.roll(x, HEAD_DIM - q, 1) * sin_a + pltpu.roll(x, q, 1) * sin_b


def _inproj_kernel(x_ref, sh_ref, sc_ref, gain_ref, w_ref, cos_ref, sa_ref, sb_ref, qg_ref, kg_ref,
                   k_ref, v_ref, ff_ref, fb_ref, i_ref, q_ref, qh_ref, g_ref):
    x = x_ref[0]
    h = _rms(x, gain_ref[...]) * (1.0 + sc_ref[0]) + sh_ref[0]
    hb = h.astype(BF16)

    def seg(lo, width):
        return jnp.dot(hb, w_ref[:, lo:lo + width], preferred_element_type=F32)

    cos, sa, sb = cos_ref[...], sa_ref[...], sb_ref[...]

    def normed_heads(p, gain, n_heads, out_ref):
        for hd in range(n_heads):
            ph = p[:, hd * HEAD_DIM:(hd + 1) * HEAD_DIM]
            out_ref[0, :, hd * HEAD_DIM:(hd + 1) * HEAD_DIM] = _rope(_rms(ph, gain), cos, sa, sb).astype(BF16)

    normed_heads(seg(COL_K, KV_WIDTH), kg_ref[...], ATTN_KV_HEADS, k_ref)
    v_ref[0] = seg(COL_V, KV_WIDTH).astype(BF16)
    ff_ref[0] = seg(COL_FF, HGRN_WIDTH)
    fb_ref[0] = seg(COL_FB, HGRN_WIDTH)
    i_ref[0] = seg(COL_I, HGRN_WIDTH).astype(BF16)
    normed_heads(seg(COL_Q, ATTN_WIDTH), qg_ref[...], ATTN_HEADS, q_ref)
    qh_ref[0] = seg(COL_QH, HGRN_WIDTH).astype(BF16)
    g_ref[0] = seg(COL_G, HGRN_WIDTH).astype(BF16)


def _inproj(x, shift, scale, gain, w_bf16, rope, q_gain, k_gain):
    B, T, _ = x.shape
    tm = min(ROW_TILE, T)
    row = lambda w: pl.BlockSpec((1, tm, w), lambda b, t: (b, t, 0))
    vec = pl.BlockSpec((1, 1, D_MODEL), lambda b, t: (b, 0, 0))
    tab = pl.BlockSpec((tm, HEAD_DIM), lambda b, t: (t, 0))
    out_w = [(KV_WIDTH, BF16), (KV_WIDTH, BF16), (HGRN_WIDTH, F32), (HGRN_WIDTH, F32), (HGRN_WIDTH, BF16),
             (ATTN_WIDTH, BF16), (HGRN_WIDTH, BF16), (HGRN_WIDTH, BF16)]
    return pl.pallas_call(
        _inproj_kernel,
        grid=(B, T // tm),
        in_specs=[row(D_MODEL), vec, vec, _resident((1, D_MODEL)), _resident((D_MODEL, N_IN_COLS)),
                  tab, tab, tab, _resident((1, HEAD_DIM)), _resident((1, HEAD_DIM))],
        out_specs=[row(w) for w, _ in out_w],
        out_shape=[jax.ShapeDtypeStruct((B, T, w), dt) for w, dt in out_w],
        compiler_params=_cparams("parallel", "parallel"),
        name="inproj",
    )(x, shift, scale, gain.reshape(1, D_MODEL), w_bf16, *rope,
      q_gain.reshape(1, HEAD_DIM), k_gain.reshape(1, HEAD_DIM))


def _rope_tables(T):
    rows = T // GRID_W
    row_pos = jnp.repeat(jnp.arange(rows, dtype=F32), GRID_W)
    col_pos = jnp.tile(jnp.arange(GRID_W, dtype=F32), rows)
    half = HEAD_DIM // 2
    inv_freq = jnp.power(ROPE_THETA, -jnp.arange(0, half, 2, dtype=F32) / half)
    ang_r = row_pos[:, None] * inv_freq
    ang_c = col_pos[:, None] * inv_freq
    cr, sr, cc, sc = jnp.cos(ang_r), jnp.sin(ang_r), jnp.cos(ang_c), jnp.sin(ang_c)
    z = jnp.zeros_like(sr)
    return (jnp.concatenate([cr, cr, cc, cc], -1),
            jnp.concatenate([-sr, z, -sc, z], -1),
            jnp.concatenate([z, sr, z, sc], -1))


def _identity_rope(T):
    return (jnp.ones((T, HEAD_DIM), F32), jnp.zeros((T, HEAD_DIM), F32), jnp.zeros((T, HEAD_DIM), F32))


_NT = (((1,), (1,)), ((), ()))
_TN = (((0,), (0,)), ((), ()))


def _attn_kernel(sink_ref, q_ref, kp_ref, kc_ref, kn_ref, vp_ref, vc_ref, vn_ref, kx_ref, vx_ref, o_ref):
    n = pl.program_id(1)
    nb = pl.num_programs(1)
    blk = BAND_BLOCK
    rows = ATTN_GROUP * blk
    qi = lax.broadcasted_iota(I32, (rows, blk), 0) & (blk - 1)
    kj = lax.broadcasted_iota(I32, (rows, blk), 1)
    prev_ok = kj >= qi + jnp.where(n > 0, 0, blk)
    next_ok = kj <= qi - jnp.where(n < nb - 1, 0, blk)
    scale = HEAD_DIM ** -0.5
    for h in range(ATTN_KV_HEADS):
        cs = slice(h * HEAD_DIM, (h + 1) * HEAD_DIM)
        heads = [h * ATTN_GROUP + g for g in range(ATTN_GROUP)]
        q4 = jnp.concatenate([q_ref[0, :, hd * HEAD_DIM:(hd + 1) * HEAD_DIM] for hd in heads], axis=0)

        def scores(k_ref):
            return lax.dot_general(q4, k_ref[0, :, cs], _NT, preferred_element_type=F32) * scale

        s_p = jnp.where(prev_ok, scores(kp_ref), NEG_INF)
        s_c = scores(kc_ref)
        s_n = jnp.where(next_ok, scores(kn_ref), NEG_INF)
        s_x = scores(kx_ref)
        sink = jnp.concatenate([jnp.full((blk, 1), sink_ref[hd], F32) for hd in heads], axis=0)
        m = jnp.maximum(jnp.maximum(jnp.max(s_p, -1, keepdims=True), jnp.max(s_c, -1, keepdims=True)),
                        jnp.maximum(jnp.max(s_n, -1, keepdims=True), jnp.max(s_x, -1, keepdims=True)))
        m = jnp.maximum(m, sink)
        p_p, p_c, p_n, p_x = jnp.exp(s_p - m), jnp.exp(s_c - m), jnp.exp(s_n - m), jnp.exp(s_x - m)
        denom = (jnp.sum(p_p, -1, keepdims=True) + jnp.sum(p_c, -1, keepdims=True)
                 + jnp.sum(p_n, -1, keepdims=True) + jnp.sum(p_x, -1, keepdims=True) + jnp.exp(sink - m))

        def pv(p, v_ref):
            return jnp.dot(p.astype(BF16), v_ref[0, :, cs], preferred_element_type=F32)

        o = (pv(p_p, vp_ref) + pv(p_c, vc_ref) + pv(p_n, vn_ref) + pv(p_x, vx_ref)) / denom
        for g, hd in enumerate(heads):
            o_ref[0, :, hd * HEAD_DIM:(hd + 1) * HEAD_DIM] = o[g * blk:(g + 1) * blk].astype(BF16)


def _attention(q, k, v, k_ctx, v_ctx, sink):
    B, T, _ = q.shape
    L = k_ctx.shape[1]
    nb = T // BAND_BLOCK
    kv = lambda f: pl.BlockSpec((1, BAND_BLOCK, KV_WIDTH), lambda b, n: (b, f(n), 0))
    prev, cur, nxt = (lambda n: jnp.maximum(n - 1, 0)), (lambda n: n), (lambda n: jnp.minimum(n + 1, nb - 1))
    ctx = pl.BlockSpec((1, L, KV_WIDTH), lambda b, n: (b, 0, 0))
    return pl.pallas_call(
        _attn_kernel,
        grid=(B, nb),
        in_specs=[pl.BlockSpec(memory_space=pltpu.SMEM),
                  pl.BlockSpec((1, BAND_BLOCK, ATTN_WIDTH), lambda b, n: (b, n, 0)),
                  kv(prev), kv(cur), kv(nxt), kv(prev), kv(cur), kv(nxt), ctx, ctx],
        out_specs=pl.BlockSpec((1, BAND_BLOCK, ATTN_WIDTH), lambda b, n: (b, n, 0)),
        out_shape=jax.ShapeDtypeStruct((B, T, ATTN_WIDTH), BF16),
        compiler_params=_cparams("parallel", "parallel"),
        name="attn",
    )(sink, q, k, k, k, v, v, v, k_ctx, v_ctx)


_DIAG = SUBLANES
_LEVELS = (64, 32, 16, 8)


def _hgrn_chunk(logit, v, q, lb, st_ref, reverse):
    C = HGRN_CHUNK
    f = lb + (1.0 - lb) * jax.nn.sigmoid(logit)
    lf = jnp.log(f)
    kk = 1.0 - f
    qf = q.astype(F32)
    r = lax.broadcasted_iota(I32, (C, C), 0)
    c = lax.broadcasted_iota(I32, (C, C), 1)
    incl = (r <= c) if reverse else (r >= c)
    a = jnp.dot(incl.astype(F32), lf, precision=lax.Precision.HIGHEST, preferred_element_type=F32)
    a_end = a[0:1] if reverse else a[C - 1:C]

    st = st_ref[...]
    inter = lax.dot_general((qf * jnp.exp(a)).astype(BF16), st.astype(BF16), _NT, preferred_element_type=F32)
    kd = (kk * jnp.exp(a_end - a)).astype(BF16)
    st_ref[...] = st * jnp.exp(a_end) + lax.dot_general(v, kd, _TN, preferred_element_type=F32)

    later = (r < c) if reverse else (r > c)
    att = jnp.zeros((C, C), F32)
    for m in _LEVELS:
        a3 = a.reshape(C // (2 * m), 2 * m, HGRN_D)
        edge = a3[:, m:m + 1, :] if reverse else a3[:, m - 1:m, :]
        e = jnp.exp(-jnp.abs(a3 - edge)).reshape(C, HGRN_D)
        p = lax.dot_general((qf * e).astype(BF16), (kk * e).astype(BF16), _NT, preferred_element_type=F32)
        pair = ((r ^ c) >> (m.bit_length() - 1)) == 1
        att = jnp.where(pair & later, p, att)

    lane = lax.broadcasted_iota(I32, (_DIAG, C), 1)
    sub = lax.broadcasted_iota(I32, (_DIAG, C), 0)
    blocks = []
    for j in range(C // _DIAG):
        rows = slice(j * _DIAG, (j + 1) * _DIAG)
        a_j, q_j, k_j = a[rows], qf[rows], kk[rows]
        blk = jnp.zeros((_DIAG, C), F32)
        for s in range(_DIAG):
            e = jnp.exp(jnp.minimum(a_j - a_j[s:s + 1], 0.0))
            col = jnp.sum(q_j * e * k_j[s:s + 1], axis=-1, keepdims=True)
            ok = (sub <= s) if reverse else (sub >= s)
            blk = jnp.where((lane == j * _DIAG + s) & ok, col, blk)
        blocks.append(blk)
    att = att + jnp.concatenate(blocks, axis=0)
    return inter + jnp.dot(att.astype(BF16), v, preferred_element_type=F32)


def _hgrn_kernel(ff_ref, vf_ref, qf_ref, fb_ref, vb_ref, qb_ref, lbf_ref, lbb_ref, s0f_ref, s0b_ref,
                 of_ref, ob_ref, sf_ref, sb_ref, stf, stb):
    step = pl.program_id(2)
    nsub = ff_ref.shape[1] // HGRN_CHUNK

    @pl.when(step == 0)
    def _():
        stf[...] = s0f_ref[0, 0]
        stb[...] = s0b_ref[0, 0]

    def body(j, carry):
        fo = pl.multiple_of(j * HGRN_CHUNK, HGRN_CHUNK)
        rows = pl.ds(fo, HGRN_CHUNK)
        of_ref[0, rows, :] = _hgrn_chunk(ff_ref[0, rows, :], vf_ref[0, rows, :], qf_ref[0, rows, :],
                                         lbf_ref[...], stf, False)
        bo = pl.multiple_of((nsub - 1 - j) * HGRN_CHUNK, HGRN_CHUNK)
        rows = pl.ds(bo, HGRN_CHUNK)
        ob_ref[0, rows, :] = _hgrn_chunk(fb_ref[0, rows, :], vb_ref[0, rows, :], qb_ref[0, rows, :],
                                         lbb_ref[...], stb, True)
        return carry

    lax.fori_loop(0, nsub, body, 0)

    @pl.when(step == pl.num_programs(2) - 1)
    def _():
        sf_ref[0, 0] = stf[...]
        sb_ref[0, 0] = stb[...]


def _hgrn(ff, fb, val, q, lb_f, lb_b, s0f, s0b):
    B, T, _ = ff.shape
    ts = min(HGRN_STEP, T)
    ns = T // ts
    fwd = pl.BlockSpec((1, ts, HGRN_D), lambda b, h, s: (b, s, h))
    bwd = pl.BlockSpec((1, ts, HGRN_D), lambda b, h, s: (b, ns - 1 - s, h))
    lbs = pl.BlockSpec((1, HGRN_D), lambda b, h, s: (0, h))
    st = pl.BlockSpec((1, 1, HGRN_D, HGRN_D), lambda b, h, s: (b, h, 0, 0))
    o_sds = jax.ShapeDtypeStruct((B, T, HGRN_WIDTH), F32)
    s_sds = jax.ShapeDtypeStruct((B, HGRN_HEADS, HGRN_D, HGRN_D), F32)
    return pl.pallas_call(
        _hgrn_kernel,
        grid=(B, HGRN_HEADS, ns),
        in_specs=[fwd, fwd, fwd, bwd, bwd, bwd, lbs, lbs, st, st],
        out_specs=[fwd, bwd, st, st],
        out_shape=[o_sds, o_sds, s_sds, s_sds],
        scratch_shapes=[pltpu.VMEM((HGRN_D, HGRN_D), F32), pltpu.VMEM((HGRN_D, HGRN_D), F32)],
        compiler_params=_cparams("parallel", "parallel", "arbitrary"),
        name="hgrn",
    )(ff, val, q, fb, val, q, lb_f, lb_b, s0f, s0b)


def _outproj_kernel(attn_ref, of_ref, ob_ref, g_ref, x_ref, g1_ref, sh2_ref, sc2_ref, og_ref, nf_ref,
                    wo_ref, wq_ref, sk_ref, x1_ref, h2_ref, s_ref):
    o = of_ref[0] + ob_ref[0]
    og = og_ref[...]
    parts = []
    for hd in range(HGRN_HEADS):
        cs = slice(hd * HGRN_D, (hd + 1) * HGRN_D)
        gh = g_ref[0, :, cs].astype(F32)
        parts.append((_rms(o[:, cs], og) * (gh * jax.nn.sigmoid(gh))).astype(BF16))
    hg = jnp.concatenate(parts, axis=-1)
    mix = (jnp.dot(attn_ref[0], wo_ref[:ATTN_WIDTH], preferred_element_type=F32)
           + jnp.dot(hg, wo_ref[ATTN_WIDTH:], preferred_element_type=F32))
    x1 = x_ref[0] + g1_ref[0] * mix
    x1_ref[0] = x1
    h2 = _rms(x1, nf_ref[...]) * (1.0 + sc2_ref[0]) + sh2_ref[0]
    h2_ref[0] = h2
    pq = jnp.dot(h2.astype(BF16), wq_ref[...], preferred_element_type=F32)
    for hp in range(2 * PEER_HEADS):
        cs = slice(hp * PEER_DHALF, (hp + 1) * PEER_DHALF)
        for jb in range(pq.shape[0] // LANES):
            rows = slice(jb * LANES, (jb + 1) * LANES)
            s_ref[hp, 0, jb * PEER_NKEYS:(jb + 1) * PEER_NKEYS, :] = lax.dot_general(
                sk_ref[hp], pq[rows, cs].astype(BF16), _NT, preferred_element_type=F32)


def _outproj(attn, of, ob, g, x, g1, sh2, sc2, o_gain, norm_ffn, wo_bf16, wq_bf16, sk_bf16):
    B, T, _ = x.shape
    tm = min(ROW_TILE, T)
    nt = T // tm
    per = TOPK_TOKENS // tm
    assert (B * T) % TOPK_TOKENS == 0 and TOPK_TOKENS % tm == 0 and tm % LANES == 0
    row = lambda w: pl.BlockSpec((1, tm, w), lambda b, t: (b, t, 0))
    vec = pl.BlockSpec((1, 1, D_MODEL), lambda b, t: (b, 0, 0))
    n_sk = 2 * PEER_HEADS
    sds = lambda w: jax.ShapeDtypeStruct((B, T, w), F32)
    s_rows = tm // LANES * PEER_NKEYS
    s_spec = pl.BlockSpec((n_sk, 1, s_rows, LANES), lambda b, t: (0, (b * nt + t) // per, (b * nt + t) % per, 0))
    s_sds = jax.ShapeDtypeStruct((n_sk, B * T // TOPK_TOKENS, TOPK_TOKENS // LANES * PEER_NKEYS, LANES), F32)
    return pl.pallas_call(
        _outproj_kernel,
        grid=(B, T // tm),
        in_specs=[row(ATTN_WIDTH), row(HGRN_WIDTH), row(HGRN_WIDTH), row(HGRN_WIDTH), row(D_MODEL),
                  vec, vec, vec, _resident((1, HGRN_D)), _resident((1, D_MODEL)),
                  _resident((ATTN_WIDTH + HGRN_WIDTH, D_MODEL)), _resident((D_MODEL, n_sk * PEER_DHALF)),
                  _resident((n_sk, PEER_NKEYS, PEER_DHALF))],
        out_specs=[row(D_MODEL), row(D_MODEL), s_spec],
        out_shape=[sds(D_MODEL), sds(D_MODEL), s_sds],
        compiler_params=_cparams("parallel", "parallel"),
        name="outproj",
    )(attn, of, ob, g, x, g1, sh2, sc2, o_gain.reshape(1, HGRN_D), norm_ffn.reshape(1, D_MODEL),
      wo_bf16, wq_bf16, sk_bf16)


TOPK_TOKENS = SUBLANES * LANES
_CAND_PAIRS = [(a, b) for a in range(PEER_TOPK) for b in range(PEER_TOPK) if (a + 1) * (b + 1) <= PEER_TOPK]


def _first_argmax(values, ids, n_chains):
    per = -(-len(values) // n_chains)
    parts = []
    for lo in range(0, len(values), per):
        m, i = values[lo], ids[lo]
        if not isinstance(i, jax.Array):
            i = jnp.full(m.shape, i, F32)
        for v, vid in zip(values[lo + 1:lo + per], ids[lo + 1:lo + per]):
            c = v > m
            m = jnp.where(c, v, m)
            i = jnp.where(c, vid, i)
        parts.append((m, i))
    m, i = parts[0]
    for pm, pi in parts[1:]:
        c = pm > m
        m = jnp.where(c, pm, m)
        i = jnp.where(c, pi, i)
    return m, i


def _topk_kernel(s_ref, idx_ref, gate_ref, wk, tv, ti, cv, ci, bv):
    shape = (SUBLANES, LANES)
    none = jnp.full(shape, -1.0, F32)

    def head(h, carry):
        for p in range(2):
            hp = 2 * h + p
            for k in range(PEER_NKEYS):
                wk[k] = s_ref[hp, 0, pl.ds(k, SUBLANES, stride=PEER_NKEYS), :]

            def extract(r, prev, p=p):
                vals = []
                for k in range(PEER_NKEYS):
                    s = jnp.where(prev == float(k), NEG_INF, wk[k])
                    wk[k] = s
                    vals.append(s)
                m, i = _first_argmax(vals, [float(k) for k in range(PEER_NKEYS)], 4)
                tv[p, r] = m
                ti[p, r] = i
                return i

            lax.fori_loop(0, PEER_TOPK, extract, none)

        for c, (a, b) in enumerate(_CAND_PAIRS):
            cv[c] = tv[0, a] + tv[1, b]
            ci[c] = ti[0, a] * float(PEER_NKEYS) + ti[1, b]

        def pick(r, prev):
            vals, ids = [], []
            for c in range(len(_CAND_PAIRS)):
                cid = ci[c]
                s = jnp.where(cid == prev, NEG_INF, cv[c])
                cv[c] = s
                vals.append(s)
                ids.append(cid)
            m, i = _first_argmax(vals, ids, 2)
            bv[r] = m
            idx_ref[h * PEER_TOPK + r, 0] = i.astype(I32)
            return i

        lax.fori_loop(0, PEER_TOPK, pick, none)

        es = [jnp.exp(bv[r] - bv[0]) for r in range(PEER_TOPK)]
        tot = es[0]
        for e in es[1:]:
            tot = tot + e
        for r in range(PEER_TOPK):
            gate_ref[h * PEER_TOPK + r, 0] = es[r] / tot
        return carry

    lax.fori_loop(0, PEER_HEADS, head, 0)


def _topk(s):
    nt = s.shape[1]
    vreg = (SUBLANES, LANES)
    out_spec = pl.BlockSpec((PEER_SLOTS, 1) + vreg, lambda i: (0, i, 0, 0))
    return pl.pallas_call(
        _topk_kernel,
        grid=(nt,),
        in_specs=[pl.BlockSpec((s.shape[0], 1) + s.shape[2:], lambda i: (0, i, 0, 0))],
        out_specs=[out_spec, out_spec],
        out_shape=[jax.ShapeDtypeStruct((PEER_SLOTS, nt) + vreg, I32),
                   jax.ShapeDtypeStruct((PEER_SLOTS, nt) + vreg, F32)],
        scratch_shapes=[pltpu.VMEM((PEER_NKEYS,) + vreg, F32),
                        pltpu.VMEM((2, PEER_TOPK) + vreg, F32), pltpu.VMEM((2, PEER_TOPK) + vreg, F32),
                        pltpu.VMEM((len(_CAND_PAIRS),) + vreg, F32), pltpu.VMEM((len(_CAND_PAIRS),) + vreg, F32),
                        pltpu.VMEM((PEER_TOPK,) + vreg, F32)],
        compiler_params=_cparams("parallel"),
        name="topk",
    )(s)


SC_CORES = 2
SC_SUBCORES = 16
SC_LANES = 16
SC_WORKERS = SC_CORES * SC_SUBCORES
PEER_GROUP = 8
PEER_ITEMS = PEER_GROUP * PEER_HEADS
SC_UNROLL = 8


_SC_PARAMS = pltpu.CompilerParams(needs_layout_passes=False)


def _sc_mesh():
    return plsc.VectorSubcoreMesh(core_axis_name="c", subcore_axis_name="s")


def _sc_worker():
    return lax.axis_index("s") * SC_CORES + lax.axis_index("c")


def _sc_item_pipeline(table_hbm, idx_v, rows0, rows1, sem0, sem1, compute):
    def gather(j, rows, sem):
        return pltpu.make_async_copy(table_hbm.at[idx_v.at[j]], rows, sem)

    gather(0, rows0, sem0).start()

    @pl.loop(0, PEER_ITEMS // 2)
    def _(i):
        j = 2 * i
        gather(j + 1, rows1, sem1).start()
        gather(j, rows0, sem0).wait()
        compute(j, rows0)

        @pl.when(i < PEER_ITEMS // 2 - 1)
        def _():
            gather(j + 2, rows0, sem0).start()

        gather(j + 1, rows1, sem1).wait()
        compute(j + 1, rows1)


def _peer_dot_kernel(x_hbm, idx_hbm, u_hbm, a_hbm, x_v, idx_v, rows0, rows1, a_v, sem0, sem1):
    tok_per_w = x_hbm.shape[0] // SC_WORKERS
    wid = _sc_worker()
    lane = lax.iota(I32, SC_LANES)

    def compute(j, rows):
        t = j // PEER_HEADS

        zero = jnp.zeros((SC_LANES,), F32)

        @plsc.parallel_loop(0, D_MODEL // SC_LANES, unroll=SC_UNROLL, carry=(zero,) * PEER_TOPK)
        def acc(k, acc):
            cols = pl.ds(k * SC_LANES, SC_LANES)
            xk = x_v[t, cols]
            return tuple(acc[r] + rows[r, cols] * xk for r in range(PEER_TOPK))

        out = zero
        for r in range(PEER_TOPK):
            out = jnp.where(lane == r, jnp.sum(acc[r]), out)
        a_v[j, :] = out

    @pl.loop(0, tok_per_w // PEER_GROUP)
    def _(g):
        tok0 = wid * tok_per_w + g * PEER_GROUP
        item0 = tok0 * PEER_HEADS
        pltpu.sync_copy(x_hbm.at[pl.ds(tok0, PEER_GROUP)], x_v)
        pltpu.sync_copy(idx_hbm.at[pl.ds(item0, PEER_ITEMS)], idx_v)
        _sc_item_pipeline(u_hbm, idx_v, rows0, rows1, sem0, sem1, compute)
        pltpu.sync_copy(a_v, a_hbm.at[pl.ds(item0, PEER_ITEMS)])


def _peer_sum_kernel(w_hbm, idx_hbm, v_hbm, o_hbm, w_v, idx_v, rows0, rows1, o_v, sem0, sem1):
    tok_per_w = o_hbm.shape[0] // SC_WORKERS
    wid = _sc_worker()
    zero = jnp.zeros((SC_LANES,), F32)

    def compute(j, rows):
        t = j // PEER_HEADS
        jv = jnp.full((SC_LANES,), j, I32)
        ws = [plsc.load_gather(w_v, [jv, jnp.full((SC_LANES,), r, I32)]) for r in range(PEER_TOPK)]

        @plsc.parallel_loop(0, D_MODEL // SC_LANES, unroll=SC_UNROLL)
        def _(k):
            cols = pl.ds(k * SC_LANES, SC_LANES)
            terms = [rows[r, cols] * ws[r] for r in range(PEER_TOPK)]
            while len(terms) > 1:
                terms = [terms[i] + terms[i + 1] for i in range(0, len(terms), 2)]
            o_v[t, cols] = o_v[t, cols] + terms[0]

    @pl.loop(0, tok_per_w // PEER_GROUP)
    def _(g):
        tok0 = wid * tok_per_w + g * PEER_GROUP
        item0 = tok0 * PEER_HEADS
        pltpu.sync_copy(w_hbm.at[pl.ds(item0, PEER_ITEMS)], w_v)
        pltpu.sync_copy(idx_hbm.at[pl.ds(item0, PEER_ITEMS)], idx_v)

        @pl.loop(0, PEER_GROUP)
        def _(t):
            @pl.loop(0, D_MODEL // SC_LANES)
            def _(k):
                o_v[t, pl.ds(k * SC_LANES, SC_LANES)] = zero

        _sc_item_pipeline(v_hbm, idx_v, rows0, rows1, sem0, sem1, compute)
        pltpu.sync_copy(o_v, o_hbm.at[pl.ds(tok0, PEER_GROUP)])


def _sc_scratch(first):
    return [first,
            pltpu.VMEM((PEER_ITEMS, PEER_TOPK), I32),
            pltpu.VMEM((PEER_TOPK, D_MODEL), F32),
            pltpu.VMEM((PEER_TOPK, D_MODEL), F32)]


def _peer_dot(h2, idx, u):
    n = h2.shape[0]
    assert n % (SC_WORKERS * PEER_GROUP) == 0
    call = pl.kernel(
        _peer_dot_kernel,
        out_type=jax.ShapeDtypeStruct((n * PEER_HEADS, PEER_TOPK), F32),
        mesh=_sc_mesh(),
        scratch_types=_sc_scratch(pltpu.VMEM((PEER_GROUP, D_MODEL), F32))
        + [pltpu.VMEM((PEER_ITEMS, PEER_TOPK), F32), pltpu.SemaphoreType.DMA, pltpu.SemaphoreType.DMA],
        compiler_params=_SC_PARAMS,
    )
    return call(h2, idx, u)


def _peer_sum(w, idx, v, n):
    assert n % (SC_WORKERS * PEER_GROUP) == 0
    call = pl.kernel(
        _peer_sum_kernel,
        out_type=jax.ShapeDtypeStruct((n, D_MODEL), F32),
        mesh=_sc_mesh(),
        scratch_types=_sc_scratch(pltpu.VMEM((PEER_ITEMS, PEER_TOPK), F32))
        + [pltpu.VMEM((PEER_GROUP, D_MODEL), F32), pltpu.SemaphoreType.DMA, pltpu.SemaphoreType.DMA],
        compiler_params=_SC_PARAMS,
    )
    return call(w, idx, v)


def _gelu_gate_kernel(a_ref, g_ref, w_ref):
    a = a_ref[...]
    w_ref[...] = g_ref[...] * (0.5 * a * (1.0 + lax.erf(a * (2.0 ** -0.5))))


def _gelu_gate(a, gate):
    n = a.shape[0]
    tm = min(2048, n)
    spec = pl.BlockSpec((tm, PEER_SLOTS), lambda i: (i, 0))
    return pl.pallas_call(
        _gelu_gate_kernel, grid=(n // tm,), in_specs=[spec, spec], out_specs=spec,
        out_shape=jax.ShapeDtypeStruct(a.shape, F32), compiler_params=_cparams("parallel"), name="gelu_gate",
    )(a, gate)


def _residual_kernel(x_ref, g_ref, p_ref, o_ref):
    o_ref[0] = x_ref[0] + g_ref[0] * p_ref[0]


def _residual(x1, g2, peer):
    B, T, _ = x1.shape
    tm = min(2 * ROW_TILE, T)
    row = pl.BlockSpec((1, tm, D_MODEL), lambda b, t: (b, t, 0))
    vec = pl.BlockSpec((1, 1, D_MODEL), lambda b, t: (b, 0, 0))
    return pl.pallas_call(
        _residual_kernel, grid=(B, T // tm), in_specs=[row, vec, row], out_specs=row,
        out_shape=jax.ShapeDtypeStruct(x1.shape, F32), compiler_params=_cparams("parallel", "parallel"),
        name="residual",
    )(x1, g2, peer)


def kernel(x, c, ctx, c_ctx, w_ada, b_ada, norm_mix, norm_ffn, w_in, q_norm, k_norm, attn_sink, hgrn_lb_logits,
           hgrn_norm, w_out, peer_w_q, peer_sub_keys, peer_u, peer_v):
    assert w_ada.shape[0] == 1, "single-layer block"
    B, T, D = x.shape
    L = ctx.shape[1]
    n = B * T

    cvecs = jnp.zeros((SUBLANES, D), F32).at[:B].set(c).at[B].set(c_ctx)
    mod = _ada(cvecs, w_ada[0], b_ada[0])
    part = lambda rows, i: rows[:, None, i * D:(i + 1) * D]
    mod_x = mod[:B]
    mod_c = jnp.broadcast_to(mod[B:B + 1], (B, 6 * D))
    sh1, sc1, g1, sh2, sc2, g2 = (part(mod_x, i) for i in range(6))

    lbs = jnp.cumsum(jax.nn.softmax(hgrn_lb_logits.astype(F32), axis=1), axis=1)
    lb_f, lb_b = lbs[0, 0].reshape(1, HGRN_WIDTH), lbs[1, 0].reshape(1, HGRN_WIDTH)

    w_in_b = w_in[0].astype(BF16)
    kx, vx, ffx, fbx, ix, qx, qhx, gx = _inproj(x, sh1, sc1, norm_mix[0], w_in_b, _rope_tables(T),
                                                q_norm[0], k_norm[0])
    kc, vc, ffc, fbc, ic, _, qhc, _ = _inproj(ctx, part(mod_c, 0), part(mod_c, 1), norm_mix[0], w_in_b,
                                              _identity_rope(L), q_norm[0], k_norm[0])

    attn = _attention(qx, kx, vx, kc, vc, attn_sink[0])

    s0 = jnp.zeros((B, HGRN_HEADS, HGRN_D, HGRN_D), F32)
    _, _, sfc, sbc = _hgrn(ffc, fbc, ic, qhc, lb_f, lb_b, s0, s0)
    of, ob, _, _ = _hgrn(ffx, fbx, ix, qhx, lb_f, lb_b, sfc, sbc)

    sk = peer_sub_keys[0].reshape(2 * PEER_HEADS, PEER_NKEYS, PEER_DHALF).astype(BF16)
    x1, h2, s = _outproj(attn, of, ob, gx, x, g1, sh2, sc2, hgrn_norm[0], norm_ffn[0],
                         w_out[0].astype(BF16), peer_w_q[0].astype(BF16), sk)

    idx_t, gate_t = _topk(s)
    idx16 = idx_t.reshape(PEER_SLOTS, n).T.reshape(n * PEER_HEADS, PEER_TOPK)
    gate = gate_t.reshape(PEER_SLOTS, n).T
    a = _peer_dot(h2.reshape(n, D), idx16, peer_u[0])
    w = _gelu_gate(a.reshape(n, PEER_SLOTS), gate)
    peer = _peer_sum(w.reshape(n * PEER_HEADS, PEER_TOPK), idx16, peer_v[0], n)
    return _residual(x1, g2, peer.reshape(B, T, D))
```

```python
import functools

import jax
import jax.numpy as jnp
from jax import lax
from jax.experimental import pallas as pl
from jax.experimental.pallas import tpu as pltpu
from jax.experimental.pallas import tpu_sc as plsc

F32 = jnp.float32
BF16 = jnp.bfloat16
I32 = jnp.int32

D_MODEL = 2048
GRID_W = 64
EPS = 1e-6
HEAD_DIM = 128
ATTN_HEADS = 8
ATTN_KV_HEADS = 2
ATTN_GROUP = ATTN_HEADS // ATTN_KV_HEADS
BAND_BLOCK = 128
ROPE_THETA = 10000.0
HGRN_HEADS = 8
HGRN_D = 128
ATTN_WIDTH = ATTN_HEADS * HEAD_DIM
KV_WIDTH = ATTN_KV_HEADS * HEAD_DIM
HGRN_WIDTH = HGRN_HEADS * HGRN_D
COL_K = 0
COL_V = COL_K + KV_WIDTH
COL_FF = COL_V + KV_WIDTH
COL_FB = COL_FF + HGRN_WIDTH
COL_I = COL_FB + HGRN_WIDTH
COL_Q = COL_I + HGRN_WIDTH
COL_QH = COL_Q + ATTN_WIDTH
COL_G = COL_QH + HGRN_WIDTH
N_IN_COLS = COL_G + HGRN_WIDTH
PEER_HEADS = 8
PEER_NKEYS = 128
PEER_DHALF = 128
PEER_TOPK = 16
PEER_SLOTS = PEER_HEADS * PEER_TOPK

LANES = 128
SUBLANES = 8
VMEM_LIMIT_BYTES = 56 * 1024 * 1024

ROW_TILE = 256
HGRN_CHUNK = 128
HGRN_STEP = 512
NEG_INF = float("-inf")


def _cparams(*sem):
    return pltpu.CompilerParams(dimension_semantics=sem, vmem_limit_bytes=VMEM_LIMIT_BYTES)


def _resident(shape):
    nd = len(shape)
    return pl.BlockSpec(shape, lambda *_: (0,) * nd, pipeline_mode=pl.Buffered(1))


def _ada_kernel(c_ref, w_ref, b_ref, o_ref):
    c = c_ref[...]
    s = (c * jax.nn.sigmoid(c)).astype(BF16)
    o_ref[...] = jnp.dot(s, w_ref[...].astype(BF16), preferred_element_type=F32) + b_ref[...]


def _ada(cvecs, w, b):
    n = w.shape[1]
    tn = 1024
    return pl.pallas_call(
        _ada_kernel,
        grid=(n // tn,),
        in_specs=[pl.BlockSpec((SUBLANES, D_MODEL), lambda j: (0, 0)),
                  pl.BlockSpec((D_MODEL, tn), lambda j: (0, j)),
                  pl.BlockSpec((1, tn), lambda j: (0, j))],
        out_specs=pl.BlockSpec((SUBLANES, tn), lambda j: (0, j)),
        out_shape=jax.ShapeDtypeStruct((SUBLANES, n), F32),
        compiler_params=_cparams("arbitrary"),
        name="ada",
    )(cvecs, w, b.reshape(1, n))


def _rms(x, gain):
    return x * lax.rsqrt(jnp.mean(x * x, axis=-1, keepdims=True) + EPS) * gain


def _rope(x, cos, sin_a, sin_b):
    q = HEAD_DIM // 4
    return x * cos + pltpu.roll(x, HEAD_DIM - q, 1) * sin_a + pltpu.roll(x, q, 1) * sin_b


def _inproj_kernel(x_ref, sh_ref, sc_ref, gain_ref, w_ref, cos_ref, sa_ref, sb_ref, qg_ref, kg_ref,
                   k_ref, v_ref, ff_ref, fb_ref, i_ref, q_ref, qh_ref, g_ref):
    x = x_ref[0]
    h = _rms(x, gain_ref[...]) * (1.0 + sc_ref[0]) + sh_ref[0]
    hb = h.astype(BF16)

    def seg(lo, width):
        return jnp.dot(hb, w_ref[:, lo:lo + width], preferred_element_type=F32)

    cos, sa, sb = cos_ref[...], sa_ref[...], sb_ref[...]

    def normed_heads(p, gain, n_heads, out_ref):
        for hd in range(n_heads):
            ph = p[:, hd * HEAD_DIM:(hd + 1) * HEAD_DIM]
            out_ref[0, :, hd * HEAD_DIM:(hd + 1) * HEAD_DIM] = _rope(_rms(ph, gain), cos, sa, sb).astype(BF16)

    normed_heads(seg(COL_K, KV_WIDTH), kg_ref[...], ATTN_KV_HEADS, k_ref)
    v_ref[0] = seg(COL_V, KV_WIDTH).astype(BF16)
    ff_ref[0] = seg(COL_FF, HGRN_WIDTH)
    fb_ref[0] = seg(COL_FB, HGRN_WIDTH)
    i_ref[0] = seg(COL_I, HGRN_WIDTH).astype(BF16)
    normed_heads(seg(COL_Q, ATTN_WIDTH), qg_ref[...], ATTN_HEADS, q_ref)
    qh_ref[0] = seg(COL_QH, HGRN_WIDTH).astype(BF16)
    g_ref[0] = seg(COL_G, HGRN_WIDTH).astype(BF16)


def _inproj(x, shift, scale, gain, w_bf16, rope, q_gain, k_gain):
    B, T, _ = x.shape
    tm = min(ROW_TILE, T)
    row = lambda w: pl.BlockSpec((1, tm, w), lambda b, t: (b, t, 0))
    vec = pl.BlockSpec((1, 1, D_MODEL), lambda b, t: (b, 0, 0))
    tab = pl.BlockSpec((tm, HEAD_DIM), lambda b, t: (t, 0))
    out_w = [(KV_WIDTH, BF16), (KV_WIDTH, BF16), (HGRN_WIDTH, F32), (HGRN_WIDTH, F32), (HGRN_WIDTH, BF16),
             (ATTN_WIDTH, BF16), (HGRN_WIDTH, BF16), (HGRN_WIDTH, BF16)]
    return pl.pallas_call(
        _inproj_kernel,
        grid=(B, T // tm),
        in_specs=[row(D_MODEL), vec, vec, _resident((1, D_MODEL)), _resident((D_MODEL, N_IN_COLS)),
                  tab, tab, tab, _resident((1, HEAD_DIM)), _resident((1, HEAD_DIM))],
        out_specs=[row(w) for w, _ in out_w],
        out_shape=[jax.ShapeDtypeStruct((B, T, w), dt) for w, dt in out_w],
        compiler_params=_cparams("parallel", "parallel"),
        name="inproj",
    )(x, shift, scale, gain.reshape(1, D_MODEL), w_bf16, *rope,
      q_gain.reshape(1, HEAD_DIM), k_gain.reshape(1, HEAD_DIM))


def _rope_tables(T):
    rows = T // GRID_W
    row_pos = jnp.repeat(jnp.arange(rows, dtype=F32), GRID_W)
    col_pos = jnp.tile(jnp.arange(GRID_W, dtype=F32), rows)
    half = HEAD_DIM // 2
    inv_freq = jnp.power(ROPE_THETA, -jnp.arange(0, half, 2, dtype=F32) / half)
    ang_r = row_pos[:, None] * inv_freq
    ang_c = col_pos[:, None] * inv_freq
    cr, sr, cc, sc = jnp.cos(ang_r), jnp.sin(ang_r), jnp.cos(ang_c), jnp.sin(ang_c)
    z = jnp.zeros_like(sr)
    return (jnp.concatenate([cr, cr, cc, cc], -1),
            jnp.concatenate([-sr, z, -sc, z], -1),
            jnp.concatenate([z, sr, z, sc], -1))


def _identity_rope(T):
    return (jnp.ones((T, HEAD_DIM), F32), jnp.zeros((T, HEAD_DIM), F32), jnp.zeros((T, HEAD_DIM), F32))


_NT = (((1,), (1,)), ((), ()))
_TN = (((0,), (0,)), ((), ()))


def _attn_kernel(sink_ref, q_ref, kp_ref, kc_ref, kn_ref, vp_ref, vc_ref, vn_ref, kx_ref, vx_ref, o_ref):
    n = pl.program_id(1)
    nb = pl.num_programs(1)
    blk = BAND_BLOCK
    rows = ATTN_GROUP * blk
    qi = lax.broadcasted_iota(I32, (rows, blk), 0) & (blk - 1)
    kj = lax.broadcasted_iota(I32, (rows, blk), 1)
    prev_ok = kj >= qi + jnp.where(n > 0, 0, blk)
    next_ok = kj <= qi - jnp.where(n < nb - 1, 0, blk)
    scale = HEAD_DIM ** -0.5
    for h in range(ATTN_KV_HEADS):
        cs = slice(h * HEAD_DIM, (h + 1) * HEAD_DIM)
        heads = [h * ATTN_GROUP + g for g in range(ATTN_GROUP)]
        q4 = jnp.concatenate([q_ref[0, :, hd * HEAD_DIM:(hd + 1) * HEAD_DIM] for hd in heads], axis=0)

        def scores(k_ref):
            return lax.dot_general(q4, k_ref[0, :, cs], _NT, preferred_element_type=F32) * scale

        s_p = jnp.where(prev_ok, scores(kp_ref), NEG_INF)
        s_c = scores(kc_ref)
        s_n = jnp.where(next_ok, scores(kn_ref), NEG_INF)
        s_x = scores(kx_ref)
        sink = jnp.concatenate([jnp.full((blk, 1), sink_ref[hd], F32) for hd in heads], axis=0)
        m = jnp.maximum(jnp.maximum(jnp.max(s_p, -1, keepdims=True), jnp.max(s_c, -1, keepdims=True)),
                        jnp.maximum(jnp.max(s_n, -1, keepdims=True), jnp.max(s_x, -1, keepdims=True)))
        m = jnp.maximum(m, sink)
        p_p, p_c, p_n, p_x = jnp.exp(s_p - m), jnp.exp(s_c - m), jnp.exp(s_n - m), jnp.exp(s_x - m)
        denom = (jnp.sum(p_p, -1, keepdims=True) + jnp.sum(p_c, -1, keepdims=True)
                 + jnp.sum(p_n, -1, keepdims=True) + jnp.sum(p_x, -1, keepdims=True) + jnp.exp(sink - m))

        def pv(p, v_ref):
            return jnp.dot(p.astype(BF16), v_ref[0, :, cs], preferred_element_type=F32)

        o = (pv(p_p, vp_ref) + pv(p_c, vc_ref) + pv(p_n, vn_ref) + pv(p_x, vx_ref)) / denom
        for g, hd in enumerate(heads):
            o_ref[0, :, hd * HEAD_DIM:(hd + 1) * HEAD_DIM] = o[g * blk:(g + 1) * blk].astype(BF16)


def _attention(q, k, v, k_ctx, v_ctx, sink):
    B, T, _ = q.shape
    L = k_ctx.shape[1]
    nb = T // BAND_BLOCK
    kv = lambda f: pl.BlockSpec((1, BAND_BLOCK, KV_WIDTH), lambda b, n: (b, f(n), 0))
    prev, cur, nxt = (lambda n: jnp.maximum(n - 1, 0)), (lambda n: n), (lambda n: jnp.minimum(n + 1, nb - 1))
    ctx = pl.BlockSpec((1, L, KV_WIDTH), lambda b, n: (b, 0, 0))
    return pl.pallas_call(
        _attn_kernel,
        grid=(B, nb),
        in_specs=[pl.BlockSpec(memory_space=pltpu.SMEM),
                  pl.BlockSpec((1, BAND_BLOCK, ATTN_WIDTH), lambda b, n: (b, n, 0)),
                  kv(prev), kv(cur), kv(nxt), kv(prev), kv(cur), kv(nxt), ctx, ctx],
        out_specs=pl.BlockSpec((1, BAND_BLOCK, ATTN_WIDTH), lambda b, n: (b, n, 0)),
        out_shape=jax.ShapeDtypeStruct((B, T, ATTN_WIDTH), BF16),
        compiler_params=_cparams("parallel", "parallel"),
        name="attn",
    )(sink, q, k, k, k, v, v, v, k_ctx, v_ctx)


_DIAG = SUBLANES
_LEVELS = (64, 32, 16, 8)


def _hgrn_chunk(logit, v, q, lb, st_ref, reverse):
    C = HGRN_CHUNK
    f = lb + (1.0 - lb) * jax.nn.sigmoid(logit)
    lf = jnp.log(f)
    kk = 1.0 - f
    qf = q.astype(F32)
    r = lax.broadcasted_iota(I32, (C, C), 0)
    c = lax.broadcasted_iota(I32, (C, C), 1)
    incl = (r <= c) if reverse else (r >= c)
    a = jnp.dot(incl.astype(F32), lf, precision=lax.Precision.HIGHEST, preferred_element_type=F32)
    a_end = a[0:1] if reverse else a[C - 1:C]

    st = st_ref[...]
    inter = lax.dot_general((qf * jnp.exp(a)).astype(BF16), st.astype(BF16), _NT, preferred_element_type=F32)
    kd = (kk * jnp.exp(a_end - a)).astype(BF16)
    st_ref[...] = st * jnp.exp(a_end) + lax.dot_general(v, kd, _TN, preferred_element_type=F32)

    later = (r < c) if reverse else (r > c)
    att = jnp.zeros((C, C), F32)
    for m in _LEVELS:
        a3 = a.reshape(C // (2 * m), 2 * m, HGRN_D)
        edge = a3[:, m:m + 1, :] if reverse else a3[:, m - 1:m, :]
        e = jnp.exp(-jnp.abs(a3 - edge)).reshape(C, HGRN_D)
        p = lax.dot_general((qf * e).astype(BF16), (kk * e).astype(BF16), _NT, preferred_element_type=F32)
        pair = ((r ^ c) >> (m.bit_length() - 1)) == 1
        att = jnp.where(pair & later, p, att)

    lane = lax.broadcasted_iota(I32, (_DIAG, C), 1)
    sub = lax.broadcasted_iota(I32, (_DIAG, C), 0)
    blocks = []
    for j in range(C // _DIAG):
        rows = slice(j * _DIAG, (j + 1) * _DIAG)
        a_j, q_j, k_j = a[rows], qf[rows], kk[rows]
        blk = jnp.zeros((_DIAG, C), F32)
        for s in range(_DIAG):
            e = jnp.exp(jnp.minimum(a_j - a_j[s:s + 1], 0.0))
            col = jnp.sum(q_j * e * k_j[s:s + 1], axis=-1, keepdims=True)
            ok = (sub <= s) if reverse else (sub >= s)
            blk = jnp.where((lane == j * _DIAG + s) & ok, col, blk)
        blocks.append(blk)
    att = att + jnp.concatenate(blocks, axis=0)
    return inter + jnp.dot(att.astype(BF16), v, preferred_element_type=F32)


def _hgrn_kernel(ff_ref, vf_ref, qf_ref, fb_ref, vb_ref, qb_ref, lbf_ref, lbb_ref, s0f_ref, s0b_ref,
                 of_ref, ob_ref, sf_ref, sb_ref, stf, stb):
    step = pl.program_id(2)
    nsub = ff_ref.shape[1] // HGRN_CHUNK

    @pl.when(step == 0)
    def _():
        stf[...] = s0f_ref[0, 0]
        stb[...] = s0b_ref[0, 0]

    def body(j, carry):
        fo = pl.multiple_of(j * HGRN_CHUNK, HGRN_CHUNK)
        rows = pl.ds(fo, HGRN_CHUNK)
        of_ref[0, rows, :] = _hgrn_chunk(ff_ref[0, rows, :], vf_ref[0, rows, :], qf_ref[0, rows, :],
                                         lbf_ref[...], stf, False)
        bo = pl.multiple_of((nsub - 1 - j) * HGRN_CHUNK, HGRN_CHUNK)
        rows = pl.ds(bo, HGRN_CHUNK)
        ob_ref[0, rows, :] = _hgrn_chunk(fb_ref[0, rows, :], vb_ref[0, rows, :], qb_ref[0, rows, :],
                                         lbb_ref[...], stb, True)
        return carry

    lax.fori_loop(0, nsub, body, 0)

    @pl.when(step == pl.num_programs(2) - 1)
    def _():
        sf_ref[0, 0] = stf[...]
        sb_ref[0, 0] = stb[...]


def _hgrn(ff, fb, val, q, lb_f, lb_b, s0f, s0b):
    B, T, _ = ff.shape
    ts = min(HGRN_STEP, T)
    ns = T // ts
    fwd = pl.BlockSpec((1, ts, HGRN_D), lambda b, h, s: (b, s, h))
    bwd = pl.BlockSpec((1, ts, HGRN_D), lambda b, h, s: (b, ns - 1 - s, h))
    lbs = pl.BlockSpec((1, HGRN_D), lambda b, h, s: (0, h))
    st = pl.BlockSpec((1, 1, HGRN_D, HGRN_D), lambda b, h, s: (b, h, 0, 0))
    o_sds = jax.ShapeDtypeStruct((B, T, HGRN_WIDTH), F32)
    s_sds = jax.ShapeDtypeStruct((B, HGRN_HEADS, HGRN_D, HGRN_D), F32)
    return pl.pallas_call(
        _hgrn_kernel,
        grid=(B, HGRN_HEADS, ns),
        in_specs=[fwd, fwd, fwd, bwd, bwd, bwd, lbs, lbs, st, st],
        out_specs=[fwd, bwd, st, st],
        out_shape=[o_sds, o_sds, s_sds, s_sds],
        scratch_shapes=[pltpu.VMEM((HGRN_D, HGRN_D), F32), pltpu.VMEM((HGRN_D, HGRN_D), F32)],
        compiler_params=_cparams("parallel", "parallel", "arbitrary"),
        name="hgrn",
    )(ff, val, q, fb, val, q, lb_f, lb_b, s0f, s0b)


def _outproj_kernel(attn_ref, of_ref, ob_ref, g_ref, x_ref, g1_ref, sh2_ref, sc2_ref, og_ref, nf_ref,
                    wo_ref, wq_ref, sk_ref, x1_ref, h2_ref, s_ref):
    o = of_ref[0] + ob_ref[0]
    og = og_ref[...]
    parts = []
    for hd in range(HGRN_HEADS):
        cs = slice(hd * HGRN_D, (hd + 1) * HGRN_D)
        gh = g_ref[0, :, cs].astype(F32)
        parts.append((_rms(o[:, cs], og) * (gh * jax.nn.sigmoid(gh))).astype(BF16))
    hg = jnp.concatenate(parts, axis=-1)
    mix = (jnp.dot(attn_ref[0], wo_ref[:ATTN_WIDTH], preferred_element_type=F32)
           + jnp.dot(hg, wo_ref[ATTN_WIDTH:], preferred_element_type=F32))
    x1 = x_ref[0] + g1_ref[0] * mix
    x1_ref[0] = x1
    h2 = _rms(x1, nf_ref[...]) * (1.0 + sc2_ref[0]) + sh2_ref[0]
    h2_ref[0] = h2
    pq = jnp.dot(h2.astype(BF16), wq_ref[...], preferred_element_type=F32)
    for hp in range(2 * PEER_HEADS):
        cs = slice(hp * PEER_DHALF, (hp + 1) * PEER_DHALF)
        for jb in range(pq.shape[0] // LANES):
            rows = slice(jb * LANES, (jb + 1) * LANES)
            s_ref[hp, 0, jb * PEER_NKEYS:(jb + 1) * PEER_NKEYS, :] = lax.dot_general(
                sk_ref[hp], pq[rows, cs].astype(BF16), _NT, preferred_element_type=F32)


def _outproj(attn, of, ob, g, x, g1, sh2, sc2, o_gain, norm_ffn, wo_bf16, wq_bf16, sk_bf16):
    B, T, _ = x.shape
    tm = min(ROW_TILE, T)
    nt = T // tm
    per = TOPK_TOKENS // tm
    assert (B * T) % TOPK_TOKENS == 0 and TOPK_TOKENS % tm == 0 and tm % LANES == 0
    row = lambda w: pl.BlockSpec((1, tm, w), lambda b, t: (b, t, 0))
    vec = pl.BlockSpec((1, 1, D_MODEL), lambda b, t: (b, 0, 0))
    n_sk = 2 * PEER_HEADS
    sds = lambda w: jax.ShapeDtypeStruct((B, T, w), F32)
    s_rows = tm // LANES * PEER_NKEYS
    s_spec = pl.BlockSpec((n_sk, 1, s_rows, LANES), lambda b, t: (0, (b * nt + t) // per, (b * nt + t) % per, 0))
    s_sds = jax.ShapeDtypeStruct((n_sk, B * T // TOPK_TOKENS, TOPK_TOKENS // LANES * PEER_NKEYS, LANES), F32)
    return pl.pallas_call(
        _outproj_kernel,
        grid=(B, T // tm),
        in_specs=[row(ATTN_WIDTH), row(HGRN_WIDTH), row(HGRN_WIDTH), row(HGRN_WIDTH), row(D_MODEL),
                  vec, vec, vec, _resident((1, HGRN_D)), _resident((1, D_MODEL)),
                  _resident((ATTN_WIDTH + HGRN_WIDTH, D_MODEL)), _resident((D_MODEL, n_sk * PEER_DHALF)),
                  _resident((n_sk, PEER_NKEYS, PEER_DHALF))],
        out_specs=[row(D_MODEL), row(D_MODEL), s_spec],
        out_shape=[sds(D_MODEL), sds(D_MODEL), s_sds],
        compiler_params=_cparams("parallel", "parallel"),
        name="outproj",
    )(attn, of, ob, g, x, g1, sh2, sc2, o_gain.reshape(1, HGRN_D), norm_ffn.reshape(1, D_MODEL),
      wo_bf16, wq_bf16, sk_bf16)


TOPK_TOKENS = SUBLANES * LANES
_CAND_PAIRS = [(a, b) for a in range(PEER_TOPK) for b in range(PEER_TOPK) if (a + 1) * (b + 1) <= PEER_TOPK]


def _first_argmax(values, ids, n_chains):
    per = -(-len(values) // n_chains)
    parts = []
    for lo in range(0, len(values), per):
        m, i = values[lo], ids[lo]
        if not isinstance(i, jax.Array):
            i = jnp.full(m.shape, i, F32)
        for v, vid in zip(values[lo + 1:lo + per], ids[lo + 1:lo + per]):
            c = v > m
            m = jnp.where(c, v, m)
            i = jnp.where(c, vid, i)
        parts.append((m, i))
    m, i = parts[0]
    for pm, pi in parts[1:]:
        c = pm > m
        m = jnp.where(c, pm, m)
        i = jnp.where(c, pi, i)
    return m, i


def _topk_kernel(s_ref, idx_ref, gate_ref, wk, tv, ti, cv, ci, bv):
    shape = (SUBLANES, LANES)
    none = jnp.full(shape, -1.0, F32)

    def head(h, carry):
        for p in range(2):
            hp = 2 * h + p
            for k in range(PEER_NKEYS):
                wk[k] = s_ref[hp, 0, pl.ds(k, SUBLANES, stride=PEER_NKEYS), :]

            def extract(r, prev, p=p):
                vals = []
                for k in range(PEER_NKEYS):
                    s = jnp.where(prev == float(k), NEG_INF, wk[k])
                    wk[k] = s
                    vals.append(s)
                m, i = _first_argmax(vals, [float(k) for k in range(PEER_NKEYS)], 4)
                tv[p, r] = m
                ti[p, r] = i
                return i

            lax.fori_loop(0, PEER_TOPK, extract, none)

        for c, (a, b) in enumerate(_CAND_PAIRS):
            cv[c] = tv[0, a] + tv[1, b]
            ci[c] = ti[0, a] * float(PEER_NKEYS) + ti[1, b]

        def pick(r, prev):
            vals, ids = [], []
            for c in range(len(_CAND_PAIRS)):
                cid = ci[c]
                s = jnp.where(cid == prev, NEG_INF, cv[c])
                cv[c] = s
                vals.append(s)
                ids.append(cid)
            m, i = _first_argmax(vals, ids, 2)
            bv[r] = m
            idx_ref[h * PEER_TOPK + r, 0] = i.astype(I32)
            return i

        lax.fori_loop(0, PEER_TOPK, pick, none)

        es = [jnp.exp(bv[r] - bv[0]) for r in range(PEER_TOPK)]
        tot = es[0]
        for e in es[1:]:
            tot = tot + e
        for r in range(PEER_TOPK):
            gate_ref[h * PEER_TOPK + r, 0] = es[r] / tot
        return carry

    lax.fori_loop(0, PEER_HEADS, head, 0)


def _topk(s):
    nt = s.shape[1]
    vreg = (SUBLANES, LANES)
    out_spec = pl.BlockSpec((PEER_SLOTS, 1) + vreg, lambda i: (0, i, 0, 0))
    return pl.pallas_call(
        _topk_kernel,
        grid=(nt,),
        in_specs=[pl.BlockSpec((s.shape[0], 1) + s.shape[2:], lambda i: (0, i, 0, 0))],
        out_specs=[out_spec, out_spec],
        out_shape=[jax.ShapeDtypeStruct((PEER_SLOTS, nt) + vreg, I32),
                   jax.ShapeDtypeStruct((PEER_SLOTS, nt) + vreg, F32)],
        scratch_shapes=[pltpu.VMEM((PEER_NKEYS,) + vreg, F32),
                        pltpu.VMEM((2, PEER_TOPK) + vreg, F32), pltpu.VMEM((2, PEER_TOPK) + vreg, F32),
                        pltpu.VMEM((len(_CAND_PAIRS),) + vreg, F32), pltpu.VMEM((len(_CAND_PAIRS),) + vreg, F32),
                        pltpu.VMEM((PEER_TOPK,) + vreg, F32)],
        compiler_params=_cparams("parallel"),
        name="topk",
    )(s)


SC_CORES = 2
SC_SUBCORES = 16
SC_LANES = 16
SC_WORKERS = SC_CORES * SC_SUBCORES
PEER_GROUP = 8
PEER_ITEMS = PEER_GROUP * PEER_HEADS
SC_UNROLL = 8


_SC_PARAMS = pltpu.CompilerParams(needs_layout_passes=False)


def _sc_mesh():
    return plsc.VectorSubcoreMesh(core_axis_name="c", subcore_axis_name="s")


def _sc_worker():
    return lax.axis_index("s") * SC_CORES + lax.axis_index("c")


def _sc_item_pipeline(table_hbm, idx_v, rows0, rows1, sem0, sem1, compute):
    def gather(j, rows, sem):
        return pltpu.make_async_copy(table_hbm.at[idx_v.at[j]], rows, sem)

    gather(0, rows0, sem0).start()

    @pl.loop(0, PEER_ITEMS // 2)
    def _(i):
        j = 2 * i
        gather(j + 1, rows1, sem1).start()
        gather(j, rows0, sem0).wait()
        compute(j, rows0)

        @pl.when(i < PEER_ITEMS // 2 - 1)
        def _():
            gather(j + 2, rows0, sem0).start()

        gather(j + 1, rows1, sem1).wait()
        compute(j + 1, rows1)


def _peer_dot_kernel(x_hbm, idx_hbm, u_hbm, a_hbm, x_v, idx_v, rows0, rows1, a_v, sem0, sem1):
    tok_per_w = x_hbm.shape[0] // SC_WORKERS
    wid = _sc_worker()
    lane = lax.iota(I32, SC_LANES)

    def compute(j, rows):
        t = j // PEER_HEADS

        zero = jnp.zeros((SC_LANES,), F32)

        def load(k):
            cols = pl.ds(k * SC_LANES, SC_LANES)
            return (x_v[t, cols],) + tuple(rows[r, cols] for r in range(PEER_TOPK))

        def fma(acc, vals):
            return tuple(acc[r] + vals[1 + r] * vals[0] for r in range(PEER_TOPK))

        @plsc.parallel_loop(1, D_MODEL // SC_LANES, unroll=SC_UNROLL, carry=((zero,) * PEER_TOPK, load(0)))
        def state(k, state):
            acc, vals = state
            nxt = load(k)
            return fma(acc, vals), nxt

        acc = fma(*state)
        out = zero
        for r in range(PEER_TOPK):
            out = jnp.where(lane == r, jnp.sum(acc[r]), out)
        a_v[j, :] = out

    @pl.loop(0, tok_per_w // PEER_GROUP)
    def _(g):
        tok0 = wid * tok_per_w + g * PEER_GROUP
        item0 = tok0 * PEER_HEADS
        pltpu.sync_copy(x_hbm.at[pl.ds(tok0, PEER_GROUP)], x_v)
        pltpu.sync_copy(idx_hbm.at[pl.ds(item0, PEER_ITEMS)], idx_v)
        _sc_item_pipeline(u_hbm, idx_v, rows0, rows1, sem0, sem1, compute)
        pltpu.sync_copy(a_v, a_hbm.at[pl.ds(item0, PEER_ITEMS)])


def _peer_sum_kernel(w_hbm, idx_hbm, v_hbm, o_hbm, w_v, idx_v, rows0, rows1, o_v, sem0, sem1):
    tok_per_w = o_hbm.shape[0] // SC_WORKERS
    wid = _sc_worker()
    zero = jnp.zeros((SC_LANES,), F32)

    def compute(j, rows):
        t = j // PEER_HEADS
        jv = jnp.full((SC_LANES,), j, I32)
        ws = [plsc.load_gather(w_v, [jv, jnp.full((SC_LANES,), r, I32)]) for r in range(PEER_TOPK)]

        def load(k):
            cols = pl.ds(k * SC_LANES, SC_LANES)
            return (o_v[t, cols],) + tuple(rows[r, cols] for r in range(PEER_TOPK))

        def finish(k, vals):
            terms = [vals[1 + r] * ws[r] for r in range(PEER_TOPK)]
            while len(terms) > 1:
                terms = [terms[i] + terms[i + 1] for i in range(0, len(terms), 2)]
            o_v[t, pl.ds(k * SC_LANES, SC_LANES)] = vals[0] + terms[0]

        @plsc.parallel_loop(1, D_MODEL // SC_LANES, unroll=SC_UNROLL, carry=load(0))
        def vals(k, vals):
            nxt = load(k)
            finish(k - 1, vals)
            return nxt

        finish(D_MODEL // SC_LANES - 1, vals)

    @pl.loop(0, tok_per_w // PEER_GROUP)
    def _(g):
        tok0 = wid * tok_per_w + g * PEER_GROUP
        item0 = tok0 * PEER_HEADS
        pltpu.sync_copy(w_hbm.at[pl.ds(item0, PEER_ITEMS)], w_v)
        pltpu.sync_copy(idx_hbm.at[pl.ds(item0, PEER_ITEMS)], idx_v)

        @pl.loop(0, PEER_GROUP)
        def _(t):
            @pl.loop(0, D_MODEL // SC_LANES)
            def _(k):
                o_v[t, pl.ds(k * SC_LANES, SC_LANES)] = zero

        _sc_item_pipeline(v_hbm, idx_v, rows0, rows1, sem0, sem1, compute)
        pltpu.sync_copy(o_v, o_hbm.at[pl.ds(tok0, PEER_GROUP)])


def _sc_scratch(first):
    return [first,
            pltpu.VMEM((PEER_ITEMS, PEER_TOPK), I32),
            pltpu.VMEM((PEER_TOPK, D_MODEL), F32),
            pltpu.VMEM((PEER_TOPK, D_MODEL), F32)]


def _peer_dot(h2, idx, u):
    n = h2.shape[0]
    assert n % (SC_WORKERS * PEER_GROUP) == 0
    call = pl.kernel(
        _peer_dot_kernel,
        out_type=jax.ShapeDtypeStruct((n * PEER_HEADS, PEER_TOPK), F32),
        mesh=_sc_mesh(),
        scratch_types=_sc_scratch(pltpu.VMEM((PEER_GROUP, D_MODEL), F32))
        + [pltpu.VMEM((PEER_ITEMS, PEER_TOPK), F32), pltpu.SemaphoreType.DMA, pltpu.SemaphoreType.DMA],
        compiler_params=_SC_PARAMS,
    )
    return call(h2, idx, u)


def _peer_sum(w, idx, v, n):
    assert n % (SC_WORKERS * PEER_GROUP) == 0
    call = pl.kernel(
        _peer_sum_kernel,
        out_type=jax.ShapeDtypeStruct((n, D_MODEL), F32),
        mesh=_sc_mesh(),
        scratch_types=_sc_scratch(pltpu.VMEM((PEER_ITEMS, PEER_TOPK), F32))
        + [pltpu.VMEM((PEER_GROUP, D_MODEL), F32), pltpu.SemaphoreType.DMA, pltpu.SemaphoreType.DMA],
        compiler_params=_SC_PARAMS,
    )
    return call(w, idx, v)


def _gelu_gate_kernel(a_ref, g_ref, w_ref):
    a = a_ref[...]
    w_ref[...] = g_ref[...] * (0.5 * a * (1.0 + lax.erf(a * (2.0 ** -0.5))))


def _gelu_gate(a, gate):
    n = a.shape[0]
    tm = min(2048, n)
    spec = pl.BlockSpec((tm, PEER_SLOTS), lambda i: (i, 0))
    return pl.pallas_call(
        _gelu_gate_kernel, grid=(n // tm,), in_specs=[spec, spec], out_specs=spec,
        out_shape=jax.ShapeDtypeStruct(a.shape, F32), compiler_params=_cparams("parallel"), name="gelu_gate",
    )(a, gate)


def _residual_kernel(x_ref, g_ref, p_ref, o_ref):
    o_ref[0] = x_ref[0] + g_ref[0] * p_ref[0]


def _residual(x1, g2, peer):
    B, T, _ = x1.shape
    tm = min(2 * ROW_TILE, T)
    row = pl.BlockSpec((1, tm, D_MODEL), lambda b, t: (b, t, 0))
    vec = pl.BlockSpec((1, 1, D_MODEL), lambda b, t: (b, 0, 0))
    return pl.pallas_call(
        _residual_kernel, grid=(B, T // tm), in_specs=[row, vec, row], out_specs=row,
        out_shape=jax.ShapeDtypeStruct(x1.shape, F32), compiler_params=_cparams("parallel", "parallel"),
        name="residual",
    )(x1, g2, peer)


def kernel(x, c, ctx, c_ctx, w_ada, b_ada, norm_mix, norm_ffn, w_in, q_norm, k_norm, attn_sink, hgrn_lb_logits,
           hgrn_norm, w_out, peer_w_q, peer_sub_keys, peer_u, peer_v):
    assert w_ada.shape[0] == 1, "single-layer block"
    B, T, D = x.shape
    L = ctx.shape[1]
    n = B * T

    cvecs = jnp.zeros((SUBLANES, D), F32).at[:B].set(c).at[B].set(c_ctx)
    mod = _ada(cvecs, w_ada[0], b_ada[0])
    part = lambda rows, i: rows[:, None, i * D:(i + 1) * D]
    mod_x = mod[:B]
    mod_c = jnp.broadcast_to(mod[B:B + 1], (B, 6 * D))
    sh1, sc1, g1, sh2, sc2, g2 = (part(mod_x, i) for i in range(6))

    lbs = jnp.cumsum(jax.nn.softmax(hgrn_lb_logits.astype(F32), axis=1), axis=1)
    lb_f, lb_b = lbs[0, 0].reshape(1, HGRN_WIDTH), lbs[1, 0].reshape(1, HGRN_WIDTH)

    w_in_b = w_in[0].astype(BF16)
    kx, vx, ffx, fbx, ix, qx, qhx, gx = _inproj(x, sh1, sc1, norm_mix[0], w_in_b, _rope_tables(T),
                                                q_norm[0], k_norm[0])
    kc, vc, ffc, fbc, ic, _, qhc, _ = _inproj(ctx, part(mod_c, 0), part(mod_c, 1), norm_mix[0], w_in_b,
                                              _identity_rope(L), q_norm[0], k_norm[0])

    attn = _attention(qx, kx, vx, kc, vc, attn_sink[0])

    s0 = jnp.zeros((B, HGRN_HEADS, HGRN_D, HGRN_D), F32)
    _, _, sfc, sbc = _hgrn(ffc, fbc, ic, qhc, lb_f, lb_b, s0, s0)
    of, ob, _, _ = _hgrn(ffx, fbx, ix, qhx, lb_f, lb_b, sfc, sbc)

    sk = peer_sub_keys[0].reshape(2 * PEER_HEADS, PEER_NKEYS, PEER_DHALF).astype(BF16)
    x1, h2, s = _outproj(attn, of, ob, gx, x, g1, sh2, sc2, hgrn_norm[0], norm_ffn[0],
                         w_out[0].astype(BF16), peer_w_q[0].astype(BF16), sk)

    idx_t, gate_t = _topk(s)
    idx16 = idx_t.reshape(PEER_SLOTS, n).T.reshape(n * PEER_HEADS, PEER_TOPK)
    gate = gate_t.reshape(PEER_SLOTS, n).T
    a = _peer_dot(h2.reshape(n, D), idx16, peer_u[0])
    w = _gelu_gate(a.reshape(n, PEER_SLOTS), gate)
    peer = _peer_sum(w.reshape(n * PEER_HEADS, PEER_TOPK), idx16, peer_v[0], n)
    return _residual(x1, g2, peer.reshape(B, T, D))
```

```python
import functools

import jax
import jax.numpy as jnp
from jax import lax
from jax.experimental import pallas as pl
from jax.experimental.pallas import tpu as pltpu
from jax.experimental.pallas import tpu_sc as plsc

F32 = jnp.float32
BF16 = jnp.bfloat16
I32 = jnp.int32

D_MODEL = 2048
GRID_W = 64
EPS = 1e-6
HEAD_DIM = 128
ATTN_HEADS = 8
ATTN_KV_HEADS = 2
ATTN_GROUP = ATTN_HEADS // ATTN_KV_HEADS
BAND_BLOCK = 128
ROPE_THETA = 10000.0
HGRN_HEADS = 8
HGRN_D = 128
ATTN_WIDTH = ATTN_HEADS * HEAD_DIM
KV_WIDTH = ATTN_KV_HEADS * HEAD_DIM
HGRN_WIDTH = HGRN_HEADS * HGRN_D
COL_K = 0
COL_V = COL_K + KV_WIDTH
COL_FF = COL_V + KV_WIDTH
COL_FB = COL_FF + HGRN_WIDTH
COL_I = COL_FB + HGRN_WIDTH
COL_Q = COL_I + HGRN_WIDTH
COL_QH = COL_Q + ATTN_WIDTH
COL_G = COL_QH + HGRN_WIDTH
N_IN_COLS = COL_G + HGRN_WIDTH
PEER_HEADS = 8
PEER_NKEYS = 128
PEER_DHALF = 128
PEER_TOPK = 16
PEER_SLOTS = PEER_HEADS * PEER_TOPK

LANES = 128
SUBLANES = 8
VMEM_LIMIT_BYTES = 56 * 1024 * 1024

ROW_TILE = 256
HGRN_CHUNK = 128
HGRN_STEP = 512
NEG_INF = float("-inf")


def _cparams(*sem):
    return pltpu.CompilerParams(dimension_semantics=sem, vmem_limit_bytes=VMEM_LIMIT_BYTES)


def _resident(shape):
    nd = len(shape)
    return pl.BlockSpec(shape, lambda *_: (0,) * nd, pipeline_mode=pl.Buffered(1))


def _ada_kernel(c_ref, w_ref, b_ref, o_ref):
    c = c_ref[...]
    s = (c * jax.nn.sigmoid(c)).astype(BF16)
    o_ref[...] = jnp.dot(s, w_ref[...].astype(BF16), preferred_element_type=F32) + b_ref[...]


def _ada(cvecs, w, b):
    n = w.shape[1]
    tn = 1024
    return pl.pallas_call(
        _ada_kernel,
        grid=(n // tn,),
        in_specs=[pl.BlockSpec((SUBLANES, D_MODEL), lambda j: (0, 0)),
                  pl.BlockSpec((D_MODEL, tn), lambda j: (0, j)),
                  pl.BlockSpec((1, tn), lambda j: (0, j))],
        out_specs=pl.BlockSpec((SUBLANES, tn), lambda j: (0, j)),
        out_shape=jax.ShapeDtypeStruct((SUBLANES, n), F32),
        compiler_params=_cparams("arbitrary"),
        name="ada",
    )(cvecs, w, b.reshape(1, n))


def _rms(x, gain):
    return x * lax.rsqrt(jnp.mean(x * x, axis=-1, keepdims=True) + EPS) * gain


def _rope(x, cos, sin_a, sin_b):
    q = HEAD_DIM // 4
    return x * cos + pltpu.roll(x, HEAD_DIM - q, 1) * sin_a + pltpu.roll(x, q, 1) * sin_b


def _inproj_kernel(x_ref, sh_ref, sc_ref, gain_ref, w_ref, cos_ref, sa_ref, sb_ref, qg_ref, kg_ref,
                   k_ref, v_ref, ff_ref, fb_ref, i_ref, q_ref, qh_ref, g_ref):
    x = x_ref[0]
    h = _rms(x, gain_ref[...]) * (1.0 + sc_ref[0]) + sh_ref[0]
    hb = h.astype(BF16)

    def seg(lo, width):
        return jnp.dot(hb, w_ref[:, lo:lo + width], preferred_element_type=F32)

    cos, sa, sb = cos_ref[...], sa_ref[...], sb_ref[...]

    def normed_heads(p, gain, n_heads, out_ref):
        for hd in range(n_heads):
            ph = p[:, hd * HEAD_DIM:(hd + 1) * HEAD_DIM]
            out_ref[0, :, hd * HEAD_DIM:(hd + 1) * HEAD_DIM] = _rope(_rms(ph, gain), cos, sa, sb).astype(BF16)

    normed_heads(seg(COL_K, KV_WIDTH), kg_ref[...], ATTN_KV_HEADS, k_ref)
    v_ref[0] = seg(COL_V, KV_WIDTH).astype(BF16)
    ff_ref[0] = seg(COL_FF, HGRN_WIDTH)
    fb_ref[0] = seg(COL_FB, HGRN_WIDTH)
    i_ref[0] = seg(COL_I, HGRN_WIDTH).astype(BF16)
    normed_heads(seg(COL_Q, ATTN_WIDTH), qg_ref[...], ATTN_HEADS, q_ref)
    qh_ref[0] = seg(COL_QH, HGRN_WIDTH).astype(BF16)
    g_ref[0] = seg(COL_G, HGRN_WIDTH).astype(BF16)


def _inproj(x, shift, scale, gain, w_bf16, rope, q_gain, k_gain):
    B, T, _ = x.shape
    tm = min(ROW_TILE, T)
    row = lambda w: pl.BlockSpec((1, tm, w), lambda b, t: (b, t, 0))
    vec = pl.BlockSpec((1, 1, D_MODEL), lambda b, t: (b, 0, 0))
    tab = pl.BlockSpec((tm, HEAD_DIM), lambda b, t: (t, 0))
    out_w = [(KV_WIDTH, BF16), (KV_WIDTH, BF16), (HGRN_WIDTH, F32), (HGRN_WIDTH, F32), (HGRN_WIDTH, BF16),
             (ATTN_WIDTH, BF16), (HGRN_WIDTH, BF16), (HGRN_WIDTH, BF16)]
    return pl.pallas_call(
        _inproj_kernel,
        grid=(B, T // tm),
        in_specs=[row(D_MODEL), vec, vec, _resident((1, D_MODEL)), _resident((D_MODEL, N_IN_COLS)),
                  tab, tab, tab, _resident((1, HEAD_DIM)), _resident((1, HEAD_DIM))],
        out_specs=[row(w) for w, _ in out_w],
        out_shape=[jax.ShapeDtypeStruct((B, T, w), dt) for w, dt in out_w],
        compiler_params=_cparams("parallel", "parallel"),
        name="inproj",
    )(x, shift, scale, gain.reshape(1, D_MODEL), w_bf16, *rope,
      q_gain.reshape(1, HEAD_DIM), k_gain.reshape(1, HEAD_DIM))


def _rope_tables(T):
    rows = T // GRID_W
    row_pos = jnp.repeat(jnp.arange(rows, dtype=F32), GRID_W)
    col_pos = jnp.tile(jnp.arange(GRID_W, dtype=F32), rows)
    half = HEAD_DIM // 2
    inv_freq = jnp.power(ROPE_THETA, -jnp.arange(0, half, 2, dtype=F32) / half)
    ang_r = row_pos[:, None] * inv_freq
    ang_c = col_pos[:, None] * inv_freq
    cr, sr, cc, sc = jnp.cos(ang_r), jnp.sin(ang_r), jnp.cos(ang_c), jnp.sin(ang_c)
    z = jnp.zeros_like(sr)
    return (jnp.concatenate([cr, cr, cc, cc], -1),
            jnp.concatenate([-sr, z, -sc, z], -1),
            jnp.concatenate([z, sr, z, sc], -1))


def _identity_rope(T):
    return (jnp.ones((T, HEAD_DIM), F32), jnp.zeros((T, HEAD_DIM), F32), jnp.zeros((T, HEAD_DIM), F32))


_NT = (((1,), (1,)), ((), ()))
_TN = (((0,), (0,)), ((), ()))


def _attn_kernel(sink_ref, q_ref, kp_ref, kc_ref, kn_ref, vp_ref, vc_ref, vn_ref, kx_ref, vx_ref, o_ref):
    n = pl.program_id(1)
    nb = pl.num_programs(1)
    blk = BAND_BLOCK
    rows = ATTN_GROUP * blk
    qi = lax.broadcasted_iota(I32, (rows, blk), 0) & (blk - 1)
    kj = lax.broadcasted_iota(I32, (rows, blk), 1)
    prev_ok = kj >= qi + jnp.where(n > 0, 0, blk)
    next_ok = kj <= qi - jnp.where(n < nb - 1, 0, blk)
    scale = HEAD_DIM ** -0.5
    for h in range(ATTN_KV_HEADS):
        cs = slice(h * HEAD_DIM, (h + 1) * HEAD_DIM)
        heads = [h * ATTN_GROUP + g for g in range(ATTN_GROUP)]
        q4 = jnp.concatenate([q_ref[0, :, hd * HEAD_DIM:(hd + 1) * HEAD_DIM] for hd in heads], axis=0)

        def scores(k_ref):
            return lax.dot_general(q4, k_ref[0, :, cs], _NT, preferred_element_type=F32) * scale

        s_p = jnp.where(prev_ok, scores(kp_ref), NEG_INF)
        s_c = scores(kc_ref)
        s_n = jnp.where(next_ok, scores(kn_ref), NEG_INF)
        s_x = scores(kx_ref)
        sink = jnp.concatenate([jnp.full((blk, 1), sink_ref[hd], F32) for hd in heads], axis=0)
        m = jnp.maximum(jnp.maximum(jnp.max(s_p, -1, keepdims=True), jnp.max(s_c, -1, keepdims=True)),
                        jnp.maximum(jnp.max(s_n, -1, keepdims=True), jnp.max(s_x, -1, keepdims=True)))
        m = jnp.maximum(m, sink)
        p_p, p_c, p_n, p_x = jnp.exp(s_p - m), jnp.exp(s_c - m), jnp.exp(s_n - m), jnp.exp(s_x - m)
        denom = (jnp.sum(p_p, -1, keepdims=True) + jnp.sum(p_c, -1, keepdims=True)
                 + jnp.sum(p_n, -1, keepdims=True) + jnp.sum(p_x, -1, keepdims=True) + jnp.exp(sink - m))

        def pv(p, v_ref):
            return jnp.dot(p.astype(BF16), v_ref[0, :, cs], preferred_element_type=F32)

        o = (pv(p_p, vp_ref) + pv(p_c, vc_ref) + pv(p_n, vn_ref) + pv(p_x, vx_ref)) / denom
        for g, hd in enumerate(heads):
            o_ref[0, :, hd * HEAD_DIM:(hd + 1) * HEAD_DIM] = o[g * blk:(g + 1) * blk].astype(BF16)


def _attention(q, k, v, k_ctx, v_ctx, sink):
    B, T, _ = q.shape
    L = k_ctx.shape[1]
    nb = T // BAND_BLOCK
    kv = lambda f: pl.BlockSpec((1, BAND_BLOCK, KV_WIDTH), lambda b, n: (b, f(n), 0))
    prev, cur, nxt = (lambda n: jnp.maximum(n - 1, 0)), (lambda n: n), (lambda n: jnp.minimum(n + 1, nb - 1))
    ctx = pl.BlockSpec((1, L, KV_WIDTH), lambda b, n: (b, 0, 0))
    return pl.pallas_call(
        _attn_kernel,
        grid=(B, nb),
        in_specs=[pl.BlockSpec(memory_space=pltpu.SMEM),
                  pl.BlockSpec((1, BAND_BLOCK, ATTN_WIDTH), lambda b, n: (b, n, 0)),
                  kv(prev), kv(cur), kv(nxt), kv(prev), kv(cur), kv(nxt), ctx, ctx],
        out_specs=pl.BlockSpec((1, BAND_BLOCK, ATTN_WIDTH), lambda b, n: (b, n, 0)),
        out_shape=jax.ShapeDtypeStruct((B, T, ATTN_WIDTH), BF16),
        compiler_params=_cparams("parallel", "parallel"),
        name="attn",
    )(sink, q, k, k, k, v, v, v, k_ctx, v_ctx)


_DIAG = SUBLANES
_LEVELS = (64, 32, 16, 8)


def _hgrn_chunk(logit, v, q, lb, st_ref, reverse):
    C = HGRN_CHUNK
    f = lb + (1.0 - lb) * jax.nn.sigmoid(logit)
    lf = jnp.log(f)
    kk = 1.0 - f
    qf = q.astype(F32)
    r = lax.broadcasted_iota(I32, (C, C), 0)
    c = lax.broadcasted_iota(I32, (C, C), 1)
    incl = (r <= c) if reverse else (r >= c)
    a = jnp.dot(incl.astype(F32), lf, precision=lax.Precision.HIGHEST, preferred_element_type=F32)
    a_end = a[0:1] if reverse else a[C - 1:C]

    st = st_ref[...]
    inter = lax.dot_general((qf * jnp.exp(a)).astype(BF16), st.astype(BF16), _NT, preferred_element_type=F32)
    kd = (kk * jnp.exp(a_end - a)).astype(BF16)
    st_ref[...] = st * jnp.exp(a_end) + lax.dot_general(v, kd, _TN, preferred_element_type=F32)

    later = (r < c) if reverse else (r > c)
    att = jnp.zeros((C, C), F32)
    for m in _LEVELS:
        a3 = a.reshape(C // (2 * m), 2 * m, HGRN_D)
        edge = a3[:, m:m + 1, :] if reverse else a3[:, m - 1:m, :]
        e = jnp.exp(-jnp.abs(a3 - edge)).reshape(C, HGRN_D)
        p = lax.dot_general((qf * e).astype(BF16), (kk * e).astype(BF16), _NT, preferred_element_type=F32)
        pair = ((r ^ c) >> (m.bit_length() - 1)) == 1
        att = jnp.where(pair & later, p, att)

    lane = lax.broadcasted_iota(I32, (_DIAG, C), 1)
    sub = lax.broadcasted_iota(I32, (_DIAG, C), 0)
    blocks = []
    for j in range(C // _DIAG):
        rows = slice(j * _DIAG, (j + 1) * _DIAG)
        a_j, q_j, k_j = a[rows], qf[rows], kk[rows]
        blk = jnp.zeros((_DIAG, C), F32)
        for s in range(_DIAG):
            e = jnp.exp(jnp.minimum(a_j - a_j[s:s + 1], 0.0))
            col = jnp.sum(q_j * e * k_j[s:s + 1], axis=-1, keepdims=True)
            ok = (sub <= s) if reverse else (sub >= s)
            blk = jnp.where((lane == j * _DIAG + s) & ok, col, blk)
        blocks.append(blk)
    att = att + jnp.concatenate(blocks, axis=0)
    return inter + jnp.dot(att.astype(BF16), v, preferred_element_type=F32)


def _hgrn_kernel(ff_ref, vf_ref, qf_ref, fb_ref, vb_ref, qb_ref, lbf_ref, lbb_ref, s0f_ref, s0b_ref,
                 of_ref, ob_ref, sf_ref, sb_ref, stf, stb):
    step = pl.program_id(2)
    nsub = ff_ref.shape[1] // HGRN_CHUNK

    @pl.when(step == 0)
    def _():
        stf[...] = s0f_ref[0, 0]
        stb[...] = s0b_ref[0, 0]

    def body(j, carry):
        fo = pl.multiple_of(j * HGRN_CHUNK, HGRN_CHUNK)
        rows = pl.ds(fo, HGRN_CHUNK)
        of_ref[0, rows, :] = _hgrn_chunk(ff_ref[0, rows, :], vf_ref[0, rows, :], qf_ref[0, rows, :],
                                         lbf_ref[...], stf, False)
        bo = pl.multiple_of((nsub - 1 - j) * HGRN_CHUNK, HGRN_CHUNK)
        rows = pl.ds(bo, HGRN_CHUNK)
        ob_ref[0, rows, :] = _hgrn_chunk(fb_ref[0, rows, :], vb_ref[0, rows, :], qb_ref[0, rows, :],
                                         lbb_ref[...], stb, True)
        return carry

    lax.fori_loop(0, nsub, body, 0)

    @pl.when(step == pl.num_programs(2) - 1)
    def _():
        sf_ref[0, 0] = stf[...]
        sb_ref[0, 0] = stb[...]


def _hgrn(ff, fb, val, q, lb_f, lb_b, s0f, s0b):
    B, T, _ = ff.shape
    ts = min(HGRN_STEP, T)
    ns = T // ts
    fwd = pl.BlockSpec((1, ts, HGRN_D), lambda b, h, s: (b, s, h))
    bwd = pl.BlockSpec((1, ts, HGRN_D), lambda b, h, s: (b, ns - 1 - s, h))
    lbs = pl.BlockSpec((1, HGRN_D), lambda b, h, s: (0, h))
    st = pl.BlockSpec((1, 1, HGRN_D, HGRN_D), lambda b, h, s: (b, h, 0, 0))
    o_sds = jax.ShapeDtypeStruct((B, T, HGRN_WIDTH), F32)
    s_sds = jax.ShapeDtypeStruct((B, HGRN_HEADS, HGRN_D, HGRN_D), F32)
    return pl.pallas_call(
        _hgrn_kernel,
        grid=(B, HGRN_HEADS, ns),
        in_specs=[fwd, fwd, fwd, bwd, bwd, bwd, lbs, lbs, st, st],
        out_specs=[fwd, bwd, st, st],
        out_shape=[o_sds, o_sds, s_sds, s_sds],
        scratch_shapes=[pltpu.VMEM((HGRN_D, HGRN_D), F32), pltpu.VMEM((HGRN_D, HGRN_D), F32)],
        compiler_params=_cparams("parallel", "parallel", "arbitrary"),
        name="hgrn",
    )(ff, val, q, fb, val, q, lb_f, lb_b, s0f, s0b)


def _outproj_kernel(attn_ref, of_ref, ob_ref, g_ref, x_ref, g1_ref, sh2_ref, sc2_ref, og_ref, nf_ref,
                    wo_ref, wq_ref, sk_ref, x1_ref, h2_ref, s_ref):
    o = of_ref[0] + ob_ref[0]
    og = og_ref[...]
    parts = []
    for hd in range(HGRN_HEADS):
        cs = slice(hd * HGRN_D, (hd + 1) * HGRN_D)
        gh = g_ref[0, :, cs].astype(F32)
        parts.append((_rms(o[:, cs], og) * (gh * jax.nn.sigmoid(gh))).astype(BF16))
    hg = jnp.concatenate(parts, axis=-1)
    mix = (jnp.dot(attn_ref[0], wo_ref[:ATTN_WIDTH], preferred_element_type=F32)
           + jnp.dot(hg, wo_ref[ATTN_WIDTH:], preferred_element_type=F32))
    x1 = x_ref[0] + g1_ref[0] * mix
    x1_ref[0] = x1
    h2 = _rms(x1, nf_ref[...]) * (1.0 + sc2_ref[0]) + sh2_ref[0]
    h2_ref[0] = h2
    pq = jnp.dot(h2.astype(BF16), wq_ref[...], preferred_element_type=F32)
    for hp in range(2 * PEER_HEADS):
        cs = slice(hp * PEER_DHALF, (hp + 1) * PEER_DHALF)
        for jb in range(pq.shape[0] // LANES):
            rows = slice(jb * LANES, (jb + 1) * LANES)
            s_ref[hp, 0, jb * PEER_NKEYS:(jb + 1) * PEER_NKEYS, :] = lax.dot_general(
                sk_ref[hp], pq[rows, cs].astype(BF16), _NT, preferred_element_type=F32)


def _outproj(attn, of, ob, g, x, g1, sh2, sc2, o_gain, norm_ffn, wo_bf16, wq_bf16, sk_bf16):
    B, T, _ = x.shape
    tm = min(ROW_TILE, T)
    nt = T // tm
    per = TOPK_TOKENS // tm
    assert (B * T) % TOPK_TOKENS == 0 and TOPK_TOKENS % tm == 0 and tm % LANES == 0
    row = lambda w: pl.BlockSpec((1, tm, w), lambda b, t: (b, t, 0))
    vec = pl.BlockSpec((1, 1, D_MODEL), lambda b, t: (b, 0, 0))
    n_sk = 2 * PEER_HEADS
    sds = lambda w: jax.ShapeDtypeStruct((B, T, w), F32)
    s_rows = tm // LANES * PEER_NKEYS
    s_spec = pl.BlockSpec((n_sk, 1, s_rows, LANES), lambda b, t: (0, (b * nt + t) // per, (b * nt + t) % per, 0))
    s_sds = jax.ShapeDtypeStruct((n_sk, B * T // TOPK_TOKENS, TOPK_TOKENS // LANES * PEER_NKEYS, LANES), F32)
    return pl.pallas_call(
        _outproj_kernel,
        grid=(B, T // tm),
        in_specs=[row(ATTN_WIDTH), row(HGRN_WIDTH), row(HGRN_WIDTH), row(HGRN_WIDTH), row(D_MODEL),
                  vec, vec, vec, _resident((1, HGRN_D)), _resident((1, D_MODEL)),
                  _resident((ATTN_WIDTH + HGRN_WIDTH, D_MODEL)), _resident((D_MODEL, n_sk * PEER_DHALF)),
                  _resident((n_sk, PEER_NKEYS, PEER_DHALF))],
        out_specs=[row(D_MODEL), row(D_MODEL), s_spec],
        out_shape=[sds(D_MODEL), sds(D_MODEL), s_sds],
        compiler_params=_cparams("parallel", "parallel"),
        name="outproj",
    )(attn, of, ob, g, x, g1, sh2, sc2, o_gain.reshape(1, HGRN_D), norm_ffn.reshape(1, D_MODEL),
      wo_bf16, wq_bf16, sk_bf16)


TOPK_TOKENS = SUBLANES * LANES
_CAND_PAIRS = [(a, b) for a in range(PEER_TOPK) for b in range(PEER_TOPK) if (a + 1) * (b + 1) <= PEER_TOPK]


def _first_argmax(values, ids, n_chains):
    per = -(-len(values) // n_chains)
    parts = []
    for lo in range(0, len(values), per):
        m, i = values[lo], ids[lo]
        if not isinstance(i, jax.Array):
            i = jnp.full(m.shape, i, F32)
        for v, vid in zip(values[lo + 1:lo + per], ids[lo + 1:lo + per]):
            c = v > m
            m = jnp.where(c, v, m)
            i = jnp.where(c, vid, i)
        parts.append((m, i))
    m, i = parts[0]
    for pm, pi in parts[1:]:
        c = pm > m
        m = jnp.where(c, pm, m)
        i = jnp.where(c, pi, i)
    return m, i


def _topk_kernel(s_ref, idx_ref, gate_ref, wk, tv, ti, cv, ci, bv):
    shape = (SUBLANES, LANES)
    none = jnp.full(shape, -1.0, F32)

    def head(h, carry):
        for p in range(2):
            hp = 2 * h + p
            for k in range(PEER_NKEYS):
                wk[k] = s_ref[hp, 0, pl.ds(k, SUBLANES, stride=PEER_NKEYS), :]

            def extract(r, prev, p=p):
                vals = []
                for k in range(PEER_NKEYS):
                    s = jnp.where(prev == float(k), NEG_INF, wk[k])
                    wk[k] = s
                    vals.append(s)
                m, i = _first_argmax(vals, [float(k) for k in range(PEER_NKEYS)], 4)
                tv[p, r] = m
                ti[p, r] = i
                return i

            lax.fori_loop(0, PEER_TOPK, extract, none)

        for c, (a, b) in enumerate(_CAND_PAIRS):
            cv[c] = tv[0, a] + tv[1, b]
            ci[c] = ti[0, a] * float(PEER_NKEYS) + ti[1, b]

        def pick(r, prev):
            vals, ids = [], []
            for c in range(len(_CAND_PAIRS)):
                cid = ci[c]
                s = jnp.where(cid == prev, NEG_INF, cv[c])
                cv[c] = s
                vals.append(s)
                ids.append(cid)
            m, i = _first_argmax(vals, ids, 2)
            bv[r] = m
            idx_ref[h * PEER_TOPK + r, 0] = i.astype(I32)
            return i

        lax.fori_loop(0, PEER_TOPK, pick, none)

        es = [jnp.exp(bv[r] - bv[0]) for r in range(PEER_TOPK)]
        tot = es[0]
        for e in es[1:]:
            tot = tot + e
        for r in range(PEER_TOPK):
            gate_ref[h * PEER_TOPK + r, 0] = es[r] / tot
        return carry

    lax.fori_loop(0, PEER_HEADS, head, 0)


def _topk(s):
    nt = s.shape[1]
    vreg = (SUBLANES, LANES)
    out_spec = pl.BlockSpec((PEER_SLOTS, 1) + vreg, lambda i: (0, i, 0, 0))
    return pl.pallas_call(
        _topk_kernel,
        grid=(nt,),
        in_specs=[pl.BlockSpec((s.shape[0], 1) + s.shape[2:], lambda i: (0, i, 0, 0))],
        out_specs=[out_spec, out_spec],
        out_shape=[jax.ShapeDtypeStruct((PEER_SLOTS, nt) + vreg, I32),
                   jax.ShapeDtypeStruct((PEER_SLOTS, nt) + vreg, F32)],
        scratch_shapes=[pltpu.VMEM((PEER_NKEYS,) + vreg, F32),
                        pltpu.VMEM((2, PEER_TOPK) + vreg, F32), pltpu.VMEM((2, PEER_TOPK) + vreg, F32),
                        pltpu.VMEM((len(_CAND_PAIRS),) + vreg, F32), pltpu.VMEM((len(_CAND_PAIRS),) + vreg, F32),
                        pltpu.VMEM((PEER_TOPK,) + vreg, F32)],
        compiler_params=_cparams("parallel"),
        name="topk",
    )(s)


SC_CORES = 2
SC_SUBCORES = 16
SC_LANES = 16
SC_WORKERS = SC_CORES * SC_SUBCORES
PEER_GROUP = 8
PEER_ITEMS = PEER_GROUP * PEER_HEADS
SC_UNROLL = 8
EXPERT_SLAB = (D_MODEL // LANES, LANES)


_SC_PARAMS = pltpu.CompilerParams(needs_layout_passes=False)


def _sc_mesh():
    return plsc.VectorSubcoreMesh(core_axis_name="c", subcore_axis_name="s")


def _sc_worker():
    return lax.axis_index("s") * SC_CORES + lax.axis_index("c")


def _expert_slabs(table):
    return table.reshape((table.shape[0],) + EXPERT_SLAB)


def _row_vectors(rows, k):
    per = LANES // SC_LANES
    sub, cols = k // per, pl.ds((k % per) * SC_LANES, SC_LANES)
    return tuple(rows[r, sub, cols] for r in range(PEER_TOPK))


def _sc_item_pipeline(table_hbm, idx_v, rows0, rows1, sem0, sem1, compute):
    def gather(j, rows, sem):
        return pltpu.make_async_copy(table_hbm.at[idx_v.at[j]], rows, sem)

    gather(0, rows0, sem0).start()

    @pl.loop(0, PEER_ITEMS // 2)
    def _(i):
        j = 2 * i
        gather(j + 1, rows1, sem1).start()
        gather(j, rows0, sem0).wait()
        compute(j, rows0)

        @pl.when(i < PEER_ITEMS // 2 - 1)
        def _():
            gather(j + 2, rows0, sem0).start()

        gather(j + 1, rows1, sem1).wait()
        compute(j + 1, rows1)


def _peer_dot_kernel(x_hbm, idx_hbm, u_hbm, a_hbm, x_v, idx_v, rows0, rows1, a_v, sem0, sem1):
    tok_per_w = x_hbm.shape[0] // SC_WORKERS
    wid = _sc_worker()
    lane = lax.iota(I32, SC_LANES)

    def compute(j, rows):
        t = j // PEER_HEADS

        zero = jnp.zeros((SC_LANES,), F32)

        def load(k):
            cols = pl.ds(k * SC_LANES, SC_LANES)
            return (x_v[t, cols],) + _row_vectors(rows, k)

        def fma(acc, vals):
            return tuple(acc[r] + vals[1 + r] * vals[0] for r in range(PEER_TOPK))

        @plsc.parallel_loop(1, D_MODEL // SC_LANES, unroll=SC_UNROLL, carry=((zero,) * PEER_TOPK, load(0)))
        def state(k, state):
            acc, vals = state
            nxt = load(k)
            return fma(acc, vals), nxt

        acc = fma(*state)
        out = zero
        for r in range(PEER_TOPK):
            out = jnp.where(lane == r, jnp.sum(acc[r]), out)
        a_v[j, :] = out

    @pl.loop(0, tok_per_w // PEER_GROUP)
    def _(g):
        tok0 = wid * tok_per_w + g * PEER_GROUP
        item0 = tok0 * PEER_HEADS
        pltpu.sync_copy(x_hbm.at[pl.ds(tok0, PEER_GROUP)], x_v)
        pltpu.sync_copy(idx_hbm.at[pl.ds(item0, PEER_ITEMS)], idx_v)
        _sc_item_pipeline(u_hbm, idx_v, rows0, rows1, sem0, sem1, compute)
        pltpu.sync_copy(a_v, a_hbm.at[pl.ds(item0, PEER_ITEMS)])


def _peer_sum_kernel(w_hbm, idx_hbm, v_hbm, o_hbm, w_v, idx_v, rows0, rows1, o_v, sem0, sem1):
    tok_per_w = o_hbm.shape[0] // SC_WORKERS
    wid = _sc_worker()
    zero = jnp.zeros((SC_LANES,), F32)

    def compute(j, rows):
        t = j // PEER_HEADS
        jv = jnp.full((SC_LANES,), j, I32)
        ws = [plsc.load_gather(w_v, [jv, jnp.full((SC_LANES,), r, I32)]) for r in range(PEER_TOPK)]

        def load(k):
            cols = pl.ds(k * SC_LANES, SC_LANES)
            return (o_v[t, cols],) + _row_vectors(rows, k)

        def finish(k, vals):
            terms = [vals[1 + r] * ws[r] for r in range(PEER_TOPK)]
            while len(terms) > 1:
                terms = [terms[i] + terms[i + 1] for i in range(0, len(terms), 2)]
            o_v[t, pl.ds(k * SC_LANES, SC_LANES)] = vals[0] + terms[0]

        @plsc.parallel_loop(1, D_MODEL // SC_LANES, unroll=SC_UNROLL, carry=load(0))
        def vals(k, vals):
            nxt = load(k)
            finish(k - 1, vals)
            return nxt

        finish(D_MODEL // SC_LANES - 1, vals)

    @pl.loop(0, tok_per_w // PEER_GROUP)
    def _(g):
        tok0 = wid * tok_per_w + g * PEER_GROUP
        item0 = tok0 * PEER_HEADS
        pltpu.sync_copy(w_hbm.at[pl.ds(item0, PEER_ITEMS)], w_v)
        pltpu.sync_copy(idx_hbm.at[pl.ds(item0, PEER_ITEMS)], idx_v)

        @pl.loop(0, PEER_GROUP)
        def _(t):
            @pl.loop(0, D_MODEL // SC_LANES)
            def _(k):
                o_v[t, pl.ds(k * SC_LANES, SC_LANES)] = zero

        _sc_item_pipeline(v_hbm, idx_v, rows0, rows1, sem0, sem1, compute)
        pltpu.sync_copy(o_v, o_hbm.at[pl.ds(tok0, PEER_GROUP)])


def _sc_scratch(first):
    return [first,
            pltpu.VMEM((PEER_ITEMS, PEER_TOPK), I32),
            pltpu.VMEM((PEER_TOPK,) + EXPERT_SLAB, F32),
            pltpu.VMEM((PEER_TOPK,) + EXPERT_SLAB, F32)]


def _peer_dot(h2, idx, u):
    n = h2.shape[0]
    assert n % (SC_WORKERS * PEER_GROUP) == 0
    call = pl.kernel(
        _peer_dot_kernel,
        out_type=jax.ShapeDtypeStruct((n * PEER_HEADS, PEER_TOPK), F32),
        mesh=_sc_mesh(),
        scratch_types=_sc_scratch(pltpu.VMEM((PEER_GROUP, D_MODEL), F32))
        + [pltpu.VMEM((PEER_ITEMS, PEER_TOPK), F32), pltpu.SemaphoreType.DMA, pltpu.SemaphoreType.DMA],
        compiler_params=_SC_PARAMS,
    )
    return call(h2, idx, u)


def _peer_sum(w, idx, v, n):
    assert n % (SC_WORKERS * PEER_GROUP) == 0
    call = pl.kernel(
        _peer_sum_kernel,
        out_type=jax.ShapeDtypeStruct((n, D_MODEL), F32),
        mesh=_sc_mesh(),
        scratch_types=_sc_scratch(pltpu.VMEM((PEER_ITEMS, PEER_TOPK), F32))
        + [pltpu.VMEM((PEER_GROUP, D_MODEL), F32), pltpu.SemaphoreType.DMA, pltpu.SemaphoreType.DMA],
        compiler_params=_SC_PARAMS,
    )
    return call(w, idx, v)


def _gelu_gate_kernel(a_ref, g_ref, w_ref):
    a = a_ref[...]
    w_ref[...] = g_ref[...] * (0.5 * a * (1.0 + lax.erf(a * (2.0 ** -0.5))))


def _gelu_gate(a, gate):
    n = a.shape[0]
    tm = min(2048, n)
    spec = pl.BlockSpec((tm, PEER_SLOTS), lambda i: (i, 0))
    return pl.pallas_call(
        _gelu_gate_kernel, grid=(n // tm,), in_specs=[spec, spec], out_specs=spec,
        out_shape=jax.ShapeDtypeStruct(a.shape, F32), compiler_params=_cparams("parallel"), name="gelu_gate",
    )(a, gate)


def _residual_kernel(x_ref, g_ref, p_ref, o_ref):
    o_ref[0] = x_ref[0] + g_ref[0] * p_ref[0]


def _residual(x1, g2, peer):
    B, T, _ = x1.shape
    tm = min(2 * ROW_TILE, T)
    row = pl.BlockSpec((1, tm, D_MODEL), lambda b, t: (b, t, 0))
    vec = pl.BlockSpec((1, 1, D_MODEL), lambda b, t: (b, 0, 0))
    return pl.pallas_call(
        _residual_kernel, grid=(B, T // tm), in_specs=[row, vec, row], out_specs=row,
        out_shape=jax.ShapeDtypeStruct(x1.shape, F32), compiler_params=_cparams("parallel", "parallel"),
        name="residual",
    )(x1, g2, peer)


def kernel(x, c, ctx, c_ctx, w_ada, b_ada, norm_mix, norm_ffn, w_in, q_norm, k_norm, attn_sink, hgrn_lb_logits,
           hgrn_norm, w_out, peer_w_q, peer_sub_keys, peer_u, peer_v):
    assert w_ada.shape[0] == 1, "single-layer block"
    B, T, D = x.shape
    L = ctx.shape[1]
    n = B * T

    cvecs = jnp.zeros((SUBLANES, D), F32).at[:B].set(c).at[B].set(c_ctx)
    mod = _ada(cvecs, w_ada[0], b_ada[0])
    part = lambda rows, i: rows[:, None, i * D:(i + 1) * D]
    mod_x = mod[:B]
    mod_c = jnp.broadcast_to(mod[B:B + 1], (B, 6 * D))
    sh1, sc1, g1, sh2, sc2, g2 = (part(mod_x, i) for i in range(6))

    lbs = jnp.cumsum(jax.nn.softmax(hgrn_lb_logits.astype(F32), axis=1), axis=1)
    lb_f, lb_b = lbs[0, 0].reshape(1, HGRN_WIDTH), lbs[1, 0].reshape(1, HGRN_WIDTH)

    w_in_b = w_in[0].astype(BF16)
    kx, vx, ffx, fbx, ix, qx, qhx, gx = _inproj(x, sh1, sc1, norm_mix[0], w_in_b, _rope_tables(T),
                                                q_norm[0], k_norm[0])
    kc, vc, ffc, fbc, ic, _, qhc, _ = _inproj(ctx, part(mod_c, 0), part(mod_c, 1), norm_mix[0], w_in_b,
                                              _identity_rope(L), q_norm[0], k_norm[0])

    attn = _attention(qx, kx, vx, kc, vc, attn_sink[0])

    s0 = jnp.zeros((B, HGRN_HEADS, HGRN_D, HGRN_D), F32)
    _, _, sfc, sbc = _hgrn(ffc, fbc, ic, qhc, lb_f, lb_b, s0, s0)
    of, ob, _, _ = _hgrn(ffx, fbx, ix, qhx, lb_f, lb_b, sfc, sbc)

    sk = peer_sub_keys[0].reshape(2 * PEER_HEADS, PEER_NKEYS, PEER_DHALF).astype(BF16)
    x1, h2, s = _outproj(attn, of, ob, gx, x, g1, sh2, sc2, hgrn_norm[0], norm_ffn[0],
                         w_out[0].astype(BF16), peer_w_q[0].astype(BF16), sk)

    idx_t, gate_t = _topk(s)
    idx16 = idx_t.reshape(PEER_SLOTS, n).T.reshape(n * PEER_HEADS, PEER_TOPK)
    gate = gate_t.reshape(PEER_SLOTS, n).T
    a = _peer_dot(h2.reshape(n, D), idx16, _expert_slabs(peer_u[0]))
    w = _gelu_gate(a.reshape(n, PEER_SLOTS), gate)
    peer = _peer_sum(w.reshape(n * PEER_HEADS, PEER_TOPK), idx16, _expert_slabs(peer_v[0]), n)
    return _residual(x1, g2, peer.reshape(B, T, D))
```

```python
import functools

import jax
import jax.numpy as jnp
from jax import lax
from jax.experimental import pallas as pl
from jax.experimental.pallas import tpu as pltpu
from jax.experimental.pallas import tpu_sc as plsc

F32 = jnp.float32
BF16 = jnp.bfloat16
I32 = jnp.int32

D_MODEL = 2048
GRID_W = 64
EPS = 1e-6
HEAD_DIM = 128
ATTN_HEADS = 8
ATTN_KV_HEADS = 2
ATTN_GROUP = ATTN_HEADS // ATTN_KV_HEADS
BAND_BLOCK = 128
ROPE_THETA = 10000.0
HGRN_HEADS = 8
HGRN_D = 128
ATTN_WIDTH = ATTN_HEADS * HEAD_DIM
KV_WIDTH = ATTN_KV_HEADS * HEAD_DIM
HGRN_WIDTH = HGRN_HEADS * HGRN_D
COL_K = 0
COL_V = COL_K + KV_WIDTH
COL_FF = COL_V + KV_WIDTH
COL_FB = COL_FF + HGRN_WIDTH
COL_I = COL_FB + HGRN_WIDTH
COL_Q = COL_I + HGRN_WIDTH
COL_QH = COL_Q + ATTN_WIDTH
COL_G = COL_QH + HGRN_WIDTH
N_IN_COLS = COL_G + HGRN_WIDTH
PEER_HEADS = 8
PEER_NKEYS = 128
PEER_DHALF = 128
PEER_TOPK = 16
PEER_SLOTS = PEER_HEADS * PEER_TOPK

LANES = 128
SUBLANES = 8
VMEM_LIMIT_BYTES = 56 * 1024 * 1024

ROW_TILE = 256
HGRN_CHUNK = 128
HGRN_STEP = 512
NEG_INF = float("-inf")


def _cparams(*sem):
    return pltpu.CompilerParams(dimension_semantics=sem, vmem_limit_bytes=VMEM_LIMIT_BYTES)


def _resident(shape):
    nd = len(shape)
    return pl.BlockSpec(shape, lambda *_: (0,) * nd, pipeline_mode=pl.Buffered(1))


def _ada_kernel(c_ref, w_ref, b_ref, o_ref):
    c = c_ref[...]
    s = (c * jax.nn.sigmoid(c)).astype(BF16)
    o_ref[...] = jnp.dot(s, w_ref[...].astype(BF16), preferred_element_type=F32) + b_ref[...]


def _ada(cvecs, w, b):
    n = w.shape[1]
    tn = 1024
    return pl.pallas_call(
        _ada_kernel,
        grid=(n // tn,),
        in_specs=[pl.BlockSpec((SUBLANES, D_MODEL), lambda j: (0, 0)),
                  pl.BlockSpec((D_MODEL, tn), lambda j: (0, j)),
                  pl.BlockSpec((1, tn), lambda j: (0, j))],
        out_specs=pl.BlockSpec((SUBLANES, tn), lambda j: (0, j)),
        out_shape=jax.ShapeDtypeStruct((SUBLANES, n), F32),
        compiler_params=_cparams("arbitrary"),
        name="ada",
    )(cvecs, w, b.reshape(1, n))


def _rms(x, gain):
    return x * lax.rsqrt(jnp.mean(x * x, axis=-1, keepdims=True) + EPS) * gain


def _rope(x, cos, sin_a, sin_b):
    q = HEAD_DIM // 4
    return x * cos + pltpu.roll(x, HEAD_DIM - q, 1) * sin_a + pltpu.roll(x, q, 1) * sin_b


def _inproj_kernel(x_ref, sh_ref, sc_ref, gain_ref, w_ref, cos_ref, sa_ref, sb_ref, qg_ref, kg_ref,
                   k_ref, v_ref, ff_ref, fb_ref, i_ref, q_ref, qh_ref, g_ref):
    x = x_ref[0]
    h = _rms(x, gain_ref[...]) * (1.0 + sc_ref[0]) + sh_ref[0]
    hb = h.astype(BF16)

    def seg(lo, width):
        return jnp.dot(hb, w_ref[:, lo:lo + width], preferred_element_type=F32)

    cos, sa, sb = cos_ref[...], sa_ref[...], sb_ref[...]

    def normed_heads(p, gain, n_heads, out_ref):
        for hd in range(n_heads):
            ph = p[:, hd * HEAD_DIM:(hd + 1) * HEAD_DIM]
            out_ref[0, :, hd * HEAD_DIM:(hd + 1) * HEAD_DIM] = _rope(_rms(ph, gain), cos, sa, sb).astype(BF16)

    normed_heads(seg(COL_K, KV_WIDTH), kg_ref[...], ATTN_KV_HEADS, k_ref)
    v_ref[0] = seg(COL_V, KV_WIDTH).astype(BF16)
    ff_ref[0] = seg(COL_FF, HGRN_WIDTH)
    fb_ref[0] = seg(COL_FB, HGRN_WIDTH)
    i_ref[0] = seg(COL_I, HGRN_WIDTH).astype(BF16)
    normed_heads(seg(COL_Q, ATTN_WIDTH), qg_ref[...], ATTN_HEADS, q_ref)
    qh_ref[0] = seg(COL_QH, HGRN_WIDTH).astype(BF16)
    g_ref[0] = seg(COL_G, HGRN_WIDTH).astype(BF16)


def _inproj(x, shift, scale, gain, w_bf16, rope, q_gain, k_gain):
    B, T, _ = x.shape
    tm = min(ROW_TILE, T)
    row = lambda w: pl.BlockSpec((1, tm, w), lambda b, t: (b, t, 0))
    vec = pl.BlockSpec((1, 1, D_MODEL), lambda b, t: (b, 0, 0))
    tab = pl.BlockSpec((tm, HEAD_DIM), lambda b, t: (t, 0))
    out_w = [(KV_WIDTH, BF16), (KV_WIDTH, BF16), (HGRN_WIDTH, F32), (HGRN_WIDTH, F32), (HGRN_WIDTH, BF16),
             (ATTN_WIDTH, BF16), (HGRN_WIDTH, BF16), (HGRN_WIDTH, BF16)]
    return pl.pallas_call(
        _inproj_kernel,
        grid=(B, T // tm),
        in_specs=[row(D_MODEL), vec, vec, _resident((1, D_MODEL)), _resident((D_MODEL, N_IN_COLS)),
                  tab, tab, tab, _resident((1, HEAD_DIM)), _resident((1, HEAD_DIM))],
        out_specs=[row(w) for w, _ in out_w],
        out_shape=[jax.ShapeDtypeStruct((B, T, w), dt) for w, dt in out_w],
        compiler_params=_cparams("parallel", "parallel"),
        name="inproj",
    )(x, shift, scale, gain.reshape(1, D_MODEL), w_bf16, *rope,
      q_gain.reshape(1, HEAD_DIM), k_gain.reshape(1, HEAD_DIM))


def _rope_tables(T):
    rows = T // GRID_W
    row_pos = jnp.repeat(jnp.arange(rows, dtype=F32), GRID_W)
    col_pos = jnp.tile(jnp.arange(GRID_W, dtype=F32), rows)
    half = HEAD_DIM // 2
    inv_freq = jnp.power(ROPE_THETA, -jnp.arange(0, half, 2, dtype=F32) / half)
    ang_r = row_pos[:, None] * inv_freq
    ang_c = col_pos[:, None] * inv_freq
    cr, sr, cc, sc = jnp.cos(ang_r), jnp.sin(ang_r), jnp.cos(ang_c), jnp.sin(ang_c)
    z = jnp.zeros_like(sr)
    return (jnp.concatenate([cr, cr, cc, cc], -1),
            jnp.concatenate([-sr, z, -sc, z], -1),
            jnp.concatenate([z, sr, z, sc], -1))


def _identity_rope(T):
    return (jnp.ones((T, HEAD_DIM), F32), jnp.zeros((T, HEAD_DIM), F32), jnp.zeros((T, HEAD_DIM), F32))


_NT = (((1,), (1,)), ((), ()))
_TN = (((0,), (0,)), ((), ()))


def _attn_kernel(sink_ref, q_ref, kp_ref, kc_ref, kn_ref, vp_ref, vc_ref, vn_ref, kx_ref, vx_ref, o_ref):
    n = pl.program_id(1)
    nb = pl.num_programs(1)
    blk = BAND_BLOCK
    rows = ATTN_GROUP * blk
    qi = lax.broadcasted_iota(I32, (rows, blk), 0) & (blk - 1)
    kj = lax.broadcasted_iota(I32, (rows, blk), 1)
    prev_ok = kj >= qi + jnp.where(n > 0, 0, blk)
    next_ok = kj <= qi - jnp.where(n < nb - 1, 0, blk)
    scale = HEAD_DIM ** -0.5
    for h in range(ATTN_KV_HEADS):
        cs = slice(h * HEAD_DIM, (h + 1) * HEAD_DIM)
        heads = [h * ATTN_GROUP + g for g in range(ATTN_GROUP)]
        q4 = jnp.concatenate([q_ref[0, :, hd * HEAD_DIM:(hd + 1) * HEAD_DIM] for hd in heads], axis=0)

        def scores(k_ref):
            return lax.dot_general(q4, k_ref[0, :, cs], _NT, preferred_element_type=F32) * scale

        s_p = jnp.where(prev_ok, scores(kp_ref), NEG_INF)
        s_c = scores(kc_ref)
        s_n = jnp.where(next_ok, scores(kn_ref), NEG_INF)
        s_x = scores(kx_ref)
        sink = jnp.concatenate([jnp.full((blk, 1), sink_ref[hd], F32) for hd in heads], axis=0)
        m = jnp.maximum(jnp.maximum(jnp.max(s_p, -1, keepdims=True), jnp.max(s_c, -1, keepdims=True)),
                        jnp.maximum(jnp.max(s_n, -1, keepdims=True), jnp.max(s_x, -1, keepdims=True)))
        m = jnp.maximum(m, sink)
        p_p, p_c, p_n, p_x = jnp.exp(s_p - m), jnp.exp(s_c - m), jnp.exp(s_n - m), jnp.exp(s_x - m)
        denom = (jnp.sum(p_p, -1, keepdims=True) + jnp.sum(p_c, -1, keepdims=True)
                 + jnp.sum(p_n, -1, keepdims=True) + jnp.sum(p_x, -1, keepdims=True) + jnp.exp(sink - m))

        def pv(p, v_ref):
            return jnp.dot(p.astype(BF16), v_ref[0, :, cs], preferred_element_type=F32)

        o = (pv(p_p, vp_ref) + pv(p_c, vc_ref) + pv(p_n, vn_ref) + pv(p_x, vx_ref)) / denom
        for g, hd in enumerate(heads):
            o_ref[0, :, hd * HEAD_DIM:(hd + 1) * HEAD_DIM] = o[g * blk:(g + 1) * blk].astype(BF16)


def _attention(q, k, v, k_ctx, v_ctx, sink):
    B, T, _ = q.shape
    L = k_ctx.shape[1]
    nb = T // BAND_BLOCK
    kv = lambda f: pl.BlockSpec((1, BAND_BLOCK, KV_WIDTH), lambda b, n: (b, f(n), 0))
    prev, cur, nxt = (lambda n: jnp.maximum(n - 1, 0)), (lambda n: n), (lambda n: jnp.minimum(n + 1, nb - 1))
    ctx = pl.BlockSpec((1, L, KV_WIDTH), lambda b, n: (b, 0, 0))
    return pl.pallas_call(
        _attn_kernel,
        grid=(B, nb),
        in_specs=[pl.BlockSpec(memory_space=pltpu.SMEM),
                  pl.BlockSpec((1, BAND_BLOCK, ATTN_WIDTH), lambda b, n: (b, n, 0)),
                  kv(prev), kv(cur), kv(nxt), kv(prev), kv(cur), kv(nxt), ctx, ctx],
        out_specs=pl.BlockSpec((1, BAND_BLOCK, ATTN_WIDTH), lambda b, n: (b, n, 0)),
        out_shape=jax.ShapeDtypeStruct((B, T, ATTN_WIDTH), BF16),
        compiler_params=_cparams("parallel", "parallel"),
        name="attn",
    )(sink, q, k, k, k, v, v, v, k_ctx, v_ctx)


_DIAG = SUBLANES
_LEVELS = (64, 32, 16, 8)


def _hgrn_chunk(logit, v, q, lb, st_ref, reverse):
    C = HGRN_CHUNK
    f = lb + (1.0 - lb) * jax.nn.sigmoid(logit)
    lf = jnp.log(f)
    kk = 1.0 - f
    qf = q.astype(F32)
    r = lax.broadcasted_iota(I32, (C, C), 0)
    c = lax.broadcasted_iota(I32, (C, C), 1)
    incl = (r <= c) if reverse else (r >= c)
    a = jnp.dot(incl.astype(F32), lf, precision=lax.Precision.HIGHEST, preferred_element_type=F32)
    a_end = a[0:1] if reverse else a[C - 1:C]

    st = st_ref[...]
    inter = lax.dot_general((qf * jnp.exp(a)).astype(BF16), st.astype(BF16), _NT, preferred_element_type=F32)
    kd = (kk * jnp.exp(a_end - a)).astype(BF16)
    st_ref[...] = st * jnp.exp(a_end) + lax.dot_general(v, kd, _TN, preferred_element_type=F32)

    later = (r < c) if reverse else (r > c)
    att = jnp.zeros((C, C), F32)
    for m in _LEVELS:
        a3 = a.reshape(C // (2 * m), 2 * m, HGRN_D)
        edge = a3[:, m:m + 1, :] if reverse else a3[:, m - 1:m, :]
        e = jnp.exp(-jnp.abs(a3 - edge)).reshape(C, HGRN_D)
        p = lax.dot_general((qf * e).astype(BF16), (kk * e).astype(BF16), _NT, preferred_element_type=F32)
        pair = ((r ^ c) >> (m.bit_length() - 1)) == 1
        att = jnp.where(pair & later, p, att)

    lane = lax.broadcasted_iota(I32, (_DIAG, C), 1)
    sub = lax.broadcasted_iota(I32, (_DIAG, C), 0)
    blocks = []
    for j in range(C // _DIAG):
        rows = slice(j * _DIAG, (j + 1) * _DIAG)
        a_j, q_j, k_j = a[rows], qf[rows], kk[rows]
        blk = jnp.zeros((_DIAG, C), F32)
        for s in range(_DIAG):
            e = jnp.exp(jnp.minimum(a_j - a_j[s:s + 1], 0.0))
            col = jnp.sum(q_j * e * k_j[s:s + 1], axis=-1, keepdims=True)
            ok = (sub <= s) if reverse else (sub >= s)
            blk = jnp.where((lane == j * _DIAG + s) & ok, col, blk)
        blocks.append(blk)
    att = att + jnp.concatenate(blocks, axis=0)
    return inter + jnp.dot(att.astype(BF16), v, preferred_element_type=F32)


def _hgrn_kernel(ff_ref, vf_ref, qf_ref, fb_ref, vb_ref, qb_ref, lbf_ref, lbb_ref, s0f_ref, s0b_ref,
                 of_ref, ob_ref, sf_ref, sb_ref, stf, stb):
    step = pl.program_id(2)
    nsub = ff_ref.shape[1] // HGRN_CHUNK

    @pl.when(step == 0)
    def _():
        stf[...] = s0f_ref[0, 0]
        stb[...] = s0b_ref[0, 0]

    def body(j, carry):
        fo = pl.multiple_of(j * HGRN_CHUNK, HGRN_CHUNK)
        rows = pl.ds(fo, HGRN_CHUNK)
        of_ref[0, rows, :] = _hgrn_chunk(ff_ref[0, rows, :], vf_ref[0, rows, :], qf_ref[0, rows, :],
                                         lbf_ref[...], stf, False)
        bo = pl.multiple_of((nsub - 1 - j) * HGRN_CHUNK, HGRN_CHUNK)
        rows = pl.ds(bo, HGRN_CHUNK)
        ob_ref[0, rows, :] = _hgrn_chunk(fb_ref[0, rows, :], vb_ref[0, rows, :], qb_ref[0, rows, :],
                                         lbb_ref[...], stb, True)
        return carry

    lax.fori_loop(0, nsub, body, 0)

    @pl.when(step == pl.num_programs(2) - 1)
    def _():
        sf_ref[0, 0] = stf[...]
        sb_ref[0, 0] = stb[...]


def _hgrn(ff, fb, val, q, lb_f, lb_b, s0f, s0b):
    B, T, _ = ff.shape
    ts = min(HGRN_STEP, T)
    ns = T // ts
    fwd = pl.BlockSpec((1, ts, HGRN_D), lambda b, h, s: (b, s, h))
    bwd = pl.BlockSpec((1, ts, HGRN_D), lambda b, h, s: (b, ns - 1 - s, h))
    lbs = pl.BlockSpec((1, HGRN_D), lambda b, h, s: (0, h))
    st = pl.BlockSpec((1, 1, HGRN_D, HGRN_D), lambda b, h, s: (b, h, 0, 0))
    o_sds = jax.ShapeDtypeStruct((B, T, HGRN_WIDTH), F32)
    s_sds = jax.ShapeDtypeStruct((B, HGRN_HEADS, HGRN_D, HGRN_D), F32)
    return pl.pallas_call(
        _hgrn_kernel,
        grid=(B, HGRN_HEADS, ns),
        in_specs=[fwd, fwd, fwd, bwd, bwd, bwd, lbs, lbs, st, st],
        out_specs=[fwd, bwd, st, st],
        out_shape=[o_sds, o_sds, s_sds, s_sds],
        scratch_shapes=[pltpu.VMEM((HGRN_D, HGRN_D), F32), pltpu.VMEM((HGRN_D, HGRN_D), F32)],
        compiler_params=_cparams("parallel", "parallel", "arbitrary"),
        name="hgrn",
    )(ff, val, q, fb, val, q, lb_f, lb_b, s0f, s0b)


def _outproj_kernel(attn_ref, of_ref, ob_ref, g_ref, x_ref, g1_ref, sh2_ref, sc2_ref, og_ref, nf_ref,
                    wo_ref, wq_ref, sk_ref, x1_ref, h2_ref, s_ref):
    o = of_ref[0] + ob_ref[0]
    og = og_ref[...]
    parts = []
    for hd in range(HGRN_HEADS):
        cs = slice(hd * HGRN_D, (hd + 1) * HGRN_D)
        gh = g_ref[0, :, cs].astype(F32)
        parts.append((_rms(o[:, cs], og) * (gh * jax.nn.sigmoid(gh))).astype(BF16))
    hg = jnp.concatenate(parts, axis=-1)
    mix = (jnp.dot(attn_ref[0], wo_ref[:ATTN_WIDTH], preferred_element_type=F32)
           + jnp.dot(hg, wo_ref[ATTN_WIDTH:], preferred_element_type=F32))
    x1 = x_ref[0] + g1_ref[0] * mix
    x1_ref[0] = x1
    h2 = _rms(x1, nf_ref[...]) * (1.0 + sc2_ref[0]) + sh2_ref[0]
    h2_ref[0] = h2
    pq = jnp.dot(h2.astype(BF16), wq_ref[...], preferred_element_type=F32)
    for hp in range(2 * PEER_HEADS):
        cs = slice(hp * PEER_DHALF, (hp + 1) * PEER_DHALF)
        for jb in range(pq.shape[0] // LANES):
            rows = slice(jb * LANES, (jb + 1) * LANES)
            s_ref[hp, 0, jb * PEER_NKEYS:(jb + 1) * PEER_NKEYS, :] = lax.dot_general(
                sk_ref[hp], pq[rows, cs].astype(BF16), _NT, preferred_element_type=F32)


def _outproj(attn, of, ob, g, x, g1, sh2, sc2, o_gain, norm_ffn, wo_bf16, wq_bf16, sk_bf16):
    B, T, _ = x.shape
    tm = min(ROW_TILE, T)
    nt = T // tm
    per = TOPK_TOKENS // tm
    assert (B * T) % TOPK_TOKENS == 0 and TOPK_TOKENS % tm == 0 and tm % LANES == 0
    row = lambda w: pl.BlockSpec((1, tm, w), lambda b, t: (b, t, 0))
    vec = pl.BlockSpec((1, 1, D_MODEL), lambda b, t: (b, 0, 0))
    n_sk = 2 * PEER_HEADS
    sds = lambda w: jax.ShapeDtypeStruct((B, T, w), F32)
    s_rows = tm // LANES * PEER_NKEYS
    s_spec = pl.BlockSpec((n_sk, 1, s_rows, LANES), lambda b, t: (0, (b * nt + t) // per, (b * nt + t) % per, 0))
    s_sds = jax.ShapeDtypeStruct((n_sk, B * T // TOPK_TOKENS, TOPK_TOKENS // LANES * PEER_NKEYS, LANES), F32)
    return pl.pallas_call(
        _outproj_kernel,
        grid=(B, T // tm),
        in_specs=[row(ATTN_WIDTH), row(HGRN_WIDTH), row(HGRN_WIDTH), row(HGRN_WIDTH), row(D_MODEL),
                  vec, vec, vec, _resident((1, HGRN_D)), _resident((1, D_MODEL)),
                  _resident((ATTN_WIDTH + HGRN_WIDTH, D_MODEL)), _resident((D_MODEL, n_sk * PEER_DHALF)),
                  _resident((n_sk, PEER_NKEYS, PEER_DHALF))],
        out_specs=[row(D_MODEL), row(D_MODEL), s_spec],
        out_shape=[sds(D_MODEL), sds(D_MODEL), s_sds],
        compiler_params=_cparams("parallel", "parallel"),
        name="outproj",
    )(attn, of, ob, g, x, g1, sh2, sc2, o_gain.reshape(1, HGRN_D), norm_ffn.reshape(1, D_MODEL),
      wo_bf16, wq_bf16, sk_bf16)


TOPK_TOKENS = SUBLANES * LANES
_CAND_PAIRS = [(a, b) for a in range(PEER_TOPK) for b in range(PEER_TOPK) if (a + 1) * (b + 1) <= PEER_TOPK]


def _first_argmax(values, ids, n_chains):
    per = -(-len(values) // n_chains)
    parts = []
    for lo in range(0, len(values), per):
        m, i = values[lo], ids[lo]
        if not isinstance(i, jax.Array):
            i = jnp.full(m.shape, i, F32)
        for v, vid in zip(values[lo + 1:lo + per], ids[lo + 1:lo + per]):
            c = v > m
            m = jnp.where(c, v, m)
            i = jnp.where(c, vid, i)
        parts.append((m, i))
    m, i = parts[0]
    for pm, pi in parts[1:]:
        c = pm > m
        m = jnp.where(c, pm, m)
        i = jnp.where(c, pi, i)
    return m, i


def _topk_kernel(s_ref, idx_ref, gate_ref, wk, tv, ti, cv, ci, bv):
    shape = (SUBLANES, LANES)
    none = jnp.full(shape, -1.0, F32)

    def head(h, carry):
        for p in range(2):
            hp = 2 * h + p
            for k in range(PEER_NKEYS):
                wk[k] = s_ref[hp, 0, pl.ds(k, SUBLANES, stride=PEER_NKEYS), :]

            def extract(r, prev, p=p):
                vals = []
                for k in range(PEER_NKEYS):
                    s = jnp.where(prev == float(k), NEG_INF, wk[k])
                    wk[k] = s
                    vals.append(s)
                m, i = _first_argmax(vals, [float(k) for k in range(PEER_NKEYS)], 4)
                tv[p, r] = m
                ti[p, r] = i
                return i

            lax.fori_loop(0, PEER_TOPK, extract, none)

        for c, (a, b) in enumerate(_CAND_PAIRS):
            cv[c] = tv[0, a] + tv[1, b]
            ci[c] = ti[0, a] * float(PEER_NKEYS) + ti[1, b]

        def pick(r, prev):
            vals, ids = [], []
            for c in range(len(_CAND_PAIRS)):
                cid = ci[c]
                s = jnp.where(cid == prev, NEG_INF, cv[c])
                cv[c] = s
                vals.append(s)
                ids.append(cid)
            m, i = _first_argmax(vals, ids, 2)
            bv[r] = m
            idx_ref[h * PEER_TOPK + r, 0] = i.astype(I32)
            return i

        lax.fori_loop(0, PEER_TOPK, pick, none)

        es = [jnp.exp(bv[r] - bv[0]) for r in range(PEER_TOPK)]
        tot = es[0]
        for e in es[1:]:
            tot = tot + e
        for r in range(PEER_TOPK):
            gate_ref[h * PEER_TOPK + r, 0] = es[r] / tot
        return carry

    lax.fori_loop(0, PEER_HEADS, head, 0)


def _topk(s):
    nt = s.shape[1]
    vreg = (SUBLANES, LANES)
    out_spec = pl.BlockSpec((PEER_SLOTS, 1) + vreg, lambda i: (0, i, 0, 0))
    return pl.pallas_call(
        _topk_kernel,
        grid=(nt,),
        in_specs=[pl.BlockSpec((s.shape[0], 1) + s.shape[2:], lambda i: (0, i, 0, 0))],
        out_specs=[out_spec, out_spec],
        out_shape=[jax.ShapeDtypeStruct((PEER_SLOTS, nt) + vreg, I32),
                   jax.ShapeDtypeStruct((PEER_SLOTS, nt) + vreg, F32)],
        scratch_shapes=[pltpu.VMEM((PEER_NKEYS,) + vreg, F32),
                        pltpu.VMEM((2, PEER_TOPK) + vreg, F32), pltpu.VMEM((2, PEER_TOPK) + vreg, F32),
                        pltpu.VMEM((len(_CAND_PAIRS),) + vreg, F32), pltpu.VMEM((len(_CAND_PAIRS),) + vreg, F32),
                        pltpu.VMEM((PEER_TOPK,) + vreg, F32)],
        compiler_params=_cparams("parallel"),
        name="topk",
    )(s)


SC_CORES = 2
SC_SUBCORES = 16
SC_LANES = 16
SC_WORKERS = SC_CORES * SC_SUBCORES
PEER_GROUP = 8
PEER_ITEMS = PEER_GROUP * PEER_HEADS
SC_UNROLL = 8
PACKED_WORDS = D_MODEL // 2
EXPERT_SLAB = (PACKED_WORDS // LANES, LANES)


_SC_PARAMS = pltpu.CompilerParams(needs_layout_passes=False)


def _sc_mesh():
    return plsc.VectorSubcoreMesh(core_axis_name="c", subcore_axis_name="s")


def _sc_worker():
    return lax.axis_index("s") * SC_CORES + lax.axis_index("c")


def _pack_kernel(t_ref, o_ref):
    x = t_ref[...]
    lo = pltpu.bitcast(x[:, :PACKED_WORDS].astype(BF16).astype(F32), jnp.uint32)
    hi = pltpu.bitcast(x[:, PACKED_WORDS:].astype(BF16).astype(F32), jnp.uint32)
    word = pltpu.bitcast((hi & jnp.uint32(0xFFFF0000)) | (lo >> 16), I32)
    for s in range(EXPERT_SLAB[0]):
        o_ref[:, s, :] = word[:, s * LANES:(s + 1) * LANES]


def _pack_table(table):
    e = table.shape[0]
    tr = 256
    return pl.pallas_call(
        _pack_kernel,
        grid=(e // tr,),
        in_specs=[pl.BlockSpec((tr, D_MODEL), lambda i: (i, 0))],
        out_specs=pl.BlockSpec((tr,) + EXPERT_SLAB, lambda i: (i, 0, 0)),
        out_shape=jax.ShapeDtypeStruct((e,) + EXPERT_SLAB, I32),
        compiler_params=_cparams("parallel"),
        name="pack_table",
    )(table)


def _row_words(rows, k):
    per = LANES // SC_LANES
    sub, cols = k // per, pl.ds((k % per) * SC_LANES, SC_LANES)
    return tuple(rows[r, sub, cols] for r in range(PEER_TOPK))


def _unpack(word):
    return lax.bitcast_convert_type(word << 16, F32), lax.bitcast_convert_type(word & -65536, F32)


def _half_cols(k):
    return pl.ds(k * SC_LANES, SC_LANES), pl.ds(PACKED_WORDS + k * SC_LANES, SC_LANES)


def _sc_item_pipeline(table_hbm, idx_v, rows0, rows1, sem0, sem1, compute):
    def gather(j, rows, sem):
        return pltpu.make_async_copy(table_hbm.at[idx_v.at[j]], rows, sem)

    gather(0, rows0, sem0).start()

    @pl.loop(0, PEER_ITEMS // 2)
    def _(i):
        j = 2 * i
        gather(j + 1, rows1, sem1).start()
        gather(j, rows0, sem0).wait()
        compute(j, rows0)

        @pl.when(i < PEER_ITEMS // 2 - 1)
        def _():
            gather(j + 2, rows0, sem0).start()

        gather(j + 1, rows1, sem1).wait()
        compute(j + 1, rows1)


def _peer_dot_kernel(x_hbm, idx_hbm, u_hbm, a_hbm, x_v, idx_v, rows0, rows1, a_v, sem0, sem1):
    tok_per_w = x_hbm.shape[0] // SC_WORKERS
    wid = _sc_worker()
    lane = lax.iota(I32, SC_LANES)

    def compute(j, rows):
        t = j // PEER_HEADS

        zero = jnp.zeros((SC_LANES,), F32)

        def load(k):
            lo, hi = _half_cols(k)
            return (x_v[t, lo], x_v[t, hi]) + _row_words(rows, k)

        def fma(acc, vals):
            out = []
            for r in range(PEER_TOPK):
                lo, hi = _unpack(vals[2 + r])
                out.append(acc[r] + lo * vals[0] + hi * vals[1])
            return tuple(out)

        @plsc.parallel_loop(1, PACKED_WORDS // SC_LANES, unroll=SC_UNROLL, carry=((zero,) * PEER_TOPK, load(0)))
        def state(k, state):
            acc, vals = state
            nxt = load(k)
            return fma(acc, vals), nxt

        acc = fma(*state)
        out = zero
        for r in range(PEER_TOPK):
            out = jnp.where(lane == r, jnp.sum(acc[r]), out)
        a_v[j, :] = out

    @pl.loop(0, tok_per_w // PEER_GROUP)
    def _(g):
        tok0 = wid * tok_per_w + g * PEER_GROUP
        item0 = tok0 * PEER_HEADS
        pltpu.sync_copy(x_hbm.at[pl.ds(tok0, PEER_GROUP)], x_v)
        pltpu.sync_copy(idx_hbm.at[pl.ds(item0, PEER_ITEMS)], idx_v)
        _sc_item_pipeline(u_hbm, idx_v, rows0, rows1, sem0, sem1, compute)
        pltpu.sync_copy(a_v, a_hbm.at[pl.ds(item0, PEER_ITEMS)])


def _peer_sum_kernel(w_hbm, idx_hbm, v_hbm, o_hbm, w_v, idx_v, rows0, rows1, o_v, sem0, sem1):
    tok_per_w = o_hbm.shape[0] // SC_WORKERS
    wid = _sc_worker()
    zero = jnp.zeros((SC_LANES,), F32)

    def compute(j, rows):
        t = j // PEER_HEADS
        jv = jnp.full((SC_LANES,), j, I32)
        ws = [plsc.load_gather(w_v, [jv, jnp.full((SC_LANES,), r, I32)]) for r in range(PEER_TOPK)]

        def load(k):
            lo, hi = _half_cols(k)
            return (o_v[t, lo], o_v[t, hi]) + _row_words(rows, k)

        def finish(k, vals):
            acc_lo, acc_hi = vals[0], vals[1]
            for r in range(PEER_TOPK):
                lo, hi = _unpack(vals[2 + r])
                acc_lo = acc_lo + lo * ws[r]
                acc_hi = acc_hi + hi * ws[r]
            lo, hi = _half_cols(k)
            o_v[t, lo] = acc_lo
            o_v[t, hi] = acc_hi

        @plsc.parallel_loop(1, PACKED_WORDS // SC_LANES, unroll=SC_UNROLL, carry=load(0))
        def vals(k, vals):
            nxt = load(k)
            finish(k - 1, vals)
            return nxt

        finish(PACKED_WORDS // SC_LANES - 1, vals)

    @pl.loop(0, tok_per_w // PEER_GROUP)
    def _(g):
        tok0 = wid * tok_per_w + g * PEER_GROUP
        item0 = tok0 * PEER_HEADS
        pltpu.sync_copy(w_hbm.at[pl.ds(item0, PEER_ITEMS)], w_v)
        pltpu.sync_copy(idx_hbm.at[pl.ds(item0, PEER_ITEMS)], idx_v)

        @pl.loop(0, PEER_GROUP)
        def _(t):
            @pl.loop(0, D_MODEL // SC_LANES)
            def _(k):
                o_v[t, pl.ds(k * SC_LANES, SC_LANES)] = zero

        _sc_item_pipeline(v_hbm, idx_v, rows0, rows1, sem0, sem1, compute)
        pltpu.sync_copy(o_v, o_hbm.at[pl.ds(tok0, PEER_GROUP)])


def _sc_scratch(first):
    return [first,
            pltpu.VMEM((PEER_ITEMS, PEER_TOPK), I32),
            pltpu.VMEM((PEER_TOPK,) + EXPERT_SLAB, I32),
            pltpu.VMEM((PEER_TOPK,) + EXPERT_SLAB, I32)]


def _peer_dot(h2, idx, u):
    n = h2.shape[0]
    assert n % (SC_WORKERS * PEER_GROUP) == 0
    call = pl.kernel(
        _peer_dot_kernel,
        out_type=jax.ShapeDtypeStruct((n * PEER_HEADS, PEER_TOPK), F32),
        mesh=_sc_mesh(),
        scratch_types=_sc_scratch(pltpu.VMEM((PEER_GROUP, D_MODEL), F32))
        + [pltpu.VMEM((PEER_ITEMS, PEER_TOPK), F32), pltpu.SemaphoreType.DMA, pltpu.SemaphoreType.DMA],
        compiler_params=_SC_PARAMS,
    )
    return call(h2, idx, u)


def _peer_sum(w, idx, v, n):
    assert n % (SC_WORKERS * PEER_GROUP) == 0
    call = pl.kernel(
        _peer_sum_kernel,
        out_type=jax.ShapeDtypeStruct((n, D_MODEL), F32),
        mesh=_sc_mesh(),
        scratch_types=_sc_scratch(pltpu.VMEM((PEER_ITEMS, PEER_TOPK), F32))
        + [pltpu.VMEM((PEER_GROUP, D_MODEL), F32), pltpu.SemaphoreType.DMA, pltpu.SemaphoreType.DMA],
        compiler_params=_SC_PARAMS,
    )
    return call(w, idx, v)


def _gelu_gate_kernel(a_ref, g_ref, w_ref):
    a = a_ref[...]
    w_ref[...] = g_ref[...] * (0.5 * a * (1.0 + lax.erf(a * (2.0 ** -0.5))))


def _gelu_gate(a, gate):
    n = a.shape[0]
    tm = min(2048, n)
    spec = pl.BlockSpec((tm, PEER_SLOTS), lambda i: (i, 0))
    return pl.pallas_call(
        _gelu_gate_kernel, grid=(n // tm,), in_specs=[spec, spec], out_specs=spec,
        out_shape=jax.ShapeDtypeStruct(a.shape, F32), compiler_params=_cparams("parallel"), name="gelu_gate",
    )(a, gate)


def _residual_kernel(x_ref, g_ref, p_ref, o_ref):
    o_ref[0] = x_ref[0] + g_ref[0] * p_ref[0]


def _residual(x1, g2, peer):
    B, T, _ = x1.shape
    tm = min(2 * ROW_TILE, T)
    row = pl.BlockSpec((1, tm, D_MODEL), lambda b, t: (b, t, 0))
    vec = pl.BlockSpec((1, 1, D_MODEL), lambda b, t: (b, 0, 0))
    return pl.pallas_call(
        _residual_kernel, grid=(B, T // tm), in_specs=[row, vec, row], out_specs=row,
        out_shape=jax.ShapeDtypeStruct(x1.shape, F32), compiler_params=_cparams("parallel", "parallel"),
        name="residual",
    )(x1, g2, peer)


def kernel(x, c, ctx, c_ctx, w_ada, b_ada, norm_mix, norm_ffn, w_in, q_norm, k_norm, attn_sink, hgrn_lb_logits,
           hgrn_norm, w_out, peer_w_q, peer_sub_keys, peer_u, peer_v):
    assert w_ada.shape[0] == 1, "single-layer block"
    B, T, D = x.shape
    L = ctx.shape[1]
    n = B * T

    cvecs = jnp.zeros((SUBLANES, D), F32).at[:B].set(c).at[B].set(c_ctx)
    mod = _ada(cvecs, w_ada[0], b_ada[0])
    part = lambda rows, i: rows[:, None, i * D:(i + 1) * D]
    mod_x = mod[:B]
    mod_c = jnp.broadcast_to(mod[B:B + 1], (B, 6 * D))
    sh1, sc1, g1, sh2, sc2, g2 = (part(mod_x, i) for i in range(6))

    lbs = jnp.cumsum(jax.nn.softmax(hgrn_lb_logits.astype(F32), axis=1), axis=1)
    lb_f, lb_b = lbs[0, 0].reshape(1, HGRN_WIDTH), lbs[1, 0].reshape(1, HGRN_WIDTH)

    w_in_b = w_in[0].astype(BF16)
    kx, vx, ffx, fbx, ix, qx, qhx, gx = _inproj(x, sh1, sc1, norm_mix[0], w_in_b, _rope_tables(T),
                                                q_norm[0], k_norm[0])
    kc, vc, ffc, fbc, ic, _, qhc, _ = _inproj(ctx, part(mod_c, 0), part(mod_c, 1), norm_mix[0], w_in_b,
                                              _identity_rope(L), q_norm[0], k_norm[0])

    attn = _attention(qx, kx, vx, kc, vc, attn_sink[0])

    s0 = jnp.zeros((B, HGRN_HEADS, HGRN_D, HGRN_D), F32)
    _, _, sfc, sbc = _hgrn(ffc, fbc, ic, qhc, lb_f, lb_b, s0, s0)
    of, ob, _, _ = _hgrn(ffx, fbx, ix, qhx, lb_f, lb_b, sfc, sbc)

    sk = peer_sub_keys[0].reshape(2 * PEER_HEADS, PEER_NKEYS, PEER_DHALF).astype(BF16)
    x1, h2, s = _outproj(attn, of, ob, gx, x, g1, sh2, sc2, hgrn_norm[0], norm_ffn[0],
                         w_out[0].astype(BF16), peer_w_q[0].astype(BF16), sk)

    idx_t, gate_t = _topk(s)
    idx16 = idx_t.reshape(PEER_SLOTS, n).T.reshape(n * PEER_HEADS, PEER_TOPK)
    gate = gate_t.reshape(PEER_SLOTS, n).T
    a = _peer_dot(h2.reshape(n, D), idx16, _pack_table(peer_u[0]))
    w = _gelu_gate(a.reshape(n, PEER_SLOTS), gate)
    peer = _peer_sum(w.reshape(n * PEER_HEADS, PEER_TOPK), idx16, _pack_table(peer_v[0]), n)
    return _residual(x1, g2, peer.reshape(B, T, D))
```

```python
import functools

import jax
import jax.numpy as jnp
from jax import lax
from jax.experimental import pallas as pl
from jax.experimental.pallas import tpu as pltpu
from jax.experimental.pallas import tpu_sc as plsc

F32 = jnp.float32
BF16 = jnp.bfloat16
I32 = jnp.int32

D_MODEL = 2048
GRID_W = 64
EPS = 1e-6
HEAD_DIM = 128
ATTN_HEADS = 8
ATTN_KV_HEADS = 2
ATTN_GROUP = ATTN_HEADS // ATTN_KV_HEADS
BAND_BLOCK = 128
ROPE_THETA = 10000.0
HGRN_HEADS = 8
HGRN_D = 128
ATTN_WIDTH = ATTN_HEADS * HEAD_DIM
KV_WIDTH = ATTN_KV_HEADS * HEAD_DIM
HGRN_WIDTH = HGRN_HEADS * HGRN_D
COL_K = 0
COL_V = COL_K + KV_WIDTH
COL_FF = COL_V + KV_WIDTH
COL_FB = COL_FF + HGRN_WIDTH
COL_I = COL_FB + HGRN_WIDTH
COL_Q = COL_I + HGRN_WIDTH
COL_QH = COL_Q + ATTN_WIDTH
COL_G = COL_QH + HGRN_WIDTH
N_IN_COLS = COL_G + HGRN_WIDTH
PEER_HEADS = 8
PEER_NKEYS = 128
PEER_DHALF = 128
PEER_TOPK = 16
PEER_SLOTS = PEER_HEADS * PEER_TOPK

LANES = 128
SUBLANES = 8
VMEM_LIMIT_BYTES = 56 * 1024 * 1024

ROW_TILE = 256
HGRN_CHUNK = 128
HGRN_STEP = 512
NEG_INF = float("-inf")


def _cparams(*sem):
    return pltpu.CompilerParams(dimension_semantics=sem, vmem_limit_bytes=VMEM_LIMIT_BYTES)


def _resident(shape):
    nd = len(shape)
    return pl.BlockSpec(shape, lambda *_: (0,) * nd, pipeline_mode=pl.Buffered(1))


def _ada_kernel(c_ref, w_ref, b_ref, o_ref):
    c = c_ref[...]
    s = (c * jax.nn.sigmoid(c)).astype(BF16)
    o_ref[...] = jnp.dot(s, w_ref[...].astype(BF16), preferred_element_type=F32) + b_ref[...]


def _ada(cvecs, w, b):
    n = w.shape[1]
    tn = 1024
    return pl.pallas_call(
        _ada_kernel,
        grid=(n // tn,),
        in_specs=[pl.BlockSpec((SUBLANES, D_MODEL), lambda j: (0, 0)),
                  pl.BlockSpec((D_MODEL, tn), lambda j: (0, j)),
                  pl.BlockSpec((1, tn), lambda j: (0, j))],
        out_specs=pl.BlockSpec((SUBLANES, tn), lambda j: (0, j)),
        out_shape=jax.ShapeDtypeStruct((SUBLANES, n), F32),
        compiler_params=_cparams("arbitrary"),
        name="ada",
    )(cvecs, w, b.reshape(1, n))


def _rms(x, gain):
    return x * lax.rsqrt(jnp.mean(x * x, axis=-1, keepdims=True) + EPS) * gain


def _rope(x, cos, sin_a, sin_b):
    q = HEAD_DIM // 4
    return x * cos + pltpu.roll(x, HEAD_DIM - q, 1) * sin_a + pltpu.roll(x, q, 1) * sin_b


def _inproj_kernel(x_ref, sh_ref, sc_ref, gain_ref, w_ref, cos_ref, sa_ref, sb_ref, qg_ref, kg_ref,
                   k_ref, v_ref, ff_ref, fb_ref, i_ref, q_ref, qh_ref, g_ref):
    x = x_ref[0]
    h = _rms(x, gain_ref[...]) * (1.0 + sc_ref[0]) + sh_ref[0]
    hb = h.astype(BF16)

    def seg(lo, width):
        return jnp.dot(hb, w_ref[:, lo:lo + width], preferred_element_type=F32)

    cos, sa, sb = cos_ref[...], sa_ref[...], sb_ref[...]

    def normed_heads(p, gain, n_heads, out_ref):
        for hd in range(n_heads):
            ph = p[:, hd * HEAD_DIM:(hd + 1) * HEAD_DIM]
            out_ref[0, :, hd * HEAD_DIM:(hd + 1) * HEAD_DIM] = _rope(_rms(ph, gain), cos, sa, sb).astype(BF16)

    normed_heads(seg(COL_K, KV_WIDTH), kg_ref[...], ATTN_KV_HEADS, k_ref)
    v_ref[0] = seg(COL_V, KV_WIDTH).astype(BF16)
    ff_ref[0] = seg(COL_FF, HGRN_WIDTH)
    fb_ref[0] = seg(COL_FB, HGRN_WIDTH)
    i_ref[0] = seg(COL_I, HGRN_WIDTH).astype(BF16)
    normed_heads(seg(COL_Q, ATTN_WIDTH), qg_ref[...], ATTN_HEADS, q_ref)
    qh_ref[0] = seg(COL_QH, HGRN_WIDTH).astype(BF16)
    g_ref[0] = seg(COL_G, HGRN_WIDTH).astype(BF16)


def _inproj(x, shift, scale, gain, w_bf16, rope, q_gain, k_gain):
    B, T, _ = x.shape
    tm = min(ROW_TILE, T)
    row = lambda w: pl.BlockSpec((1, tm, w), lambda b, t: (b, t, 0))
    vec = pl.BlockSpec((1, 1, D_MODEL), lambda b, t: (b, 0, 0))
    tab = pl.BlockSpec((tm, HEAD_DIM), lambda b, t: (t, 0))
    out_w = [(KV_WIDTH, BF16), (KV_WIDTH, BF16), (HGRN_WIDTH, F32), (HGRN_WIDTH, F32), (HGRN_WIDTH, BF16),
             (ATTN_WIDTH, BF16), (HGRN_WIDTH, BF16), (HGRN_WIDTH, BF16)]
    return pl.pallas_call(
        _inproj_kernel,
        grid=(B, T // tm),
        in_specs=[row(D_MODEL), vec, vec, _resident((1, D_MODEL)), _resident((D_MODEL, N_IN_COLS)),
                  tab, tab, tab, _resident((1, HEAD_DIM)), _resident((1, HEAD_DIM))],
        out_specs=[row(w) for w, _ in out_w],
        out_shape=[jax.ShapeDtypeStruct((B, T, w), dt) for w, dt in out_w],
        compiler_params=_cparams("parallel", "parallel"),
        name="inproj",
    )(x, shift, scale, gain.reshape(1, D_MODEL), w_bf16, *rope,
      q_gain.reshape(1, HEAD_DIM), k_gain.reshape(1, HEAD_DIM))


def _rope_tables(T):
    rows = T // GRID_W
    row_pos = jnp.repeat(jnp.arange(rows, dtype=F32), GRID_W)
    col_pos = jnp.tile(jnp.arange(GRID_W, dtype=F32), rows)
    half = HEAD_DIM // 2
    inv_freq = jnp.power(ROPE_THETA, -jnp.arange(0, half, 2, dtype=F32) / half)
    ang_r = row_pos[:, None] * inv_freq
    ang_c = col_pos[:, None] * inv_freq
    cr, sr, cc, sc = jnp.cos(ang_r), jnp.sin(ang_r), jnp.cos(ang_c), jnp.sin(ang_c)
    z = jnp.zeros_like(sr)
    return (jnp.concatenate([cr, cr, cc, cc], -1),
            jnp.concatenate([-sr, z, -sc, z], -1),
            jnp.concatenate([z, sr, z, sc], -1))


def _identity_rope(T):
    return (jnp.ones((T, HEAD_DIM), F32), jnp.zeros((T, HEAD_DIM), F32), jnp.zeros((T, HEAD_DIM), F32))


_NT = (((1,), (1,)), ((), ()))
_TN = (((0,), (0,)), ((), ()))


def _attn_kernel(sink_ref, q_ref, kp_ref, kc_ref, kn_ref, vp_ref, vc_ref, vn_ref, kx_ref, vx_ref, o_ref):
    n = pl.program_id(1)
    nb = pl.num_programs(1)
    blk = BAND_BLOCK
    rows = ATTN_GROUP * blk
    qi = lax.broadcasted_iota(I32, (rows, blk), 0) & (blk - 1)
    kj = lax.broadcasted_iota(I32, (rows, blk), 1)
    prev_ok = kj >= qi + jnp.where(n > 0, 0, blk)
    next_ok = kj <= qi - jnp.where(n < nb - 1, 0, blk)
    scale = HEAD_DIM ** -0.5
    for h in range(ATTN_KV_HEADS):
        cs = slice(h * HEAD_DIM, (h + 1) * HEAD_DIM)
        heads = [h * ATTN_GROUP + g for g in range(ATTN_GROUP)]
        q4 = jnp.concatenate([q_ref[0, :, hd * HEAD_DIM:(hd + 1) * HEAD_DIM] for hd in heads], axis=0)

        def scores(k_ref):
            return lax.dot_general(q4, k_ref[0, :, cs], _NT, preferred_element_type=F32) * scale

        s_p = jnp.where(prev_ok, scores(kp_ref), NEG_INF)
        s_c = scores(kc_ref)
        s_n = jnp.where(next_ok, scores(kn_ref), NEG_INF)
        s_x = scores(kx_ref)
        sink = jnp.concatenate([jnp.full((blk, 1), sink_ref[hd], F32) for hd in heads], axis=0)
        m = jnp.maximum(jnp.maximum(jnp.max(s_p, -1, keepdims=True), jnp.max(s_c, -1, keepdims=True)),
                        jnp.maximum(jnp.max(s_n, -1, keepdims=True), jnp.max(s_x, -1, keepdims=True)))
        m = jnp.maximum(m, sink)
        p_p, p_c, p_n, p_x = jnp.exp(s_p - m), jnp.exp(s_c - m), jnp.exp(s_n - m), jnp.exp(s_x - m)
        denom = (jnp.sum(p_p, -1, keepdims=True) + jnp.sum(p_c, -1, keepdims=True)
                 + jnp.sum(p_n, -1, keepdims=True) + jnp.sum(p_x, -1, keepdims=True) + jnp.exp(sink - m))

        def pv(p, v_ref):
            return jnp.dot(p.astype(BF16), v_ref[0, :, cs], preferred_element_type=F32)

        o = (pv(p_p, vp_ref) + pv(p_c, vc_ref) + pv(p_n, vn_ref) + pv(p_x, vx_ref)) / denom
        for g, hd in enumerate(heads):
            o_ref[0, :, hd * HEAD_DIM:(hd + 1) * HEAD_DIM] = o[g * blk:(g + 1) * blk].astype(BF16)


def _attention(q, k, v, k_ctx, v_ctx, sink):
    B, T, _ = q.shape
    L = k_ctx.shape[1]
    nb = T // BAND_BLOCK
    kv = lambda f: pl.BlockSpec((1, BAND_BLOCK, KV_WIDTH), lambda b, n: (b, f(n), 0))
    prev, cur, nxt = (lambda n: jnp.maximum(n - 1, 0)), (lambda n: n), (lambda n: jnp.minimum(n + 1, nb - 1))
    ctx = pl.BlockSpec((1, L, KV_WIDTH), lambda b, n: (b, 0, 0))
    return pl.pallas_call(
        _attn_kernel,
        grid=(B, nb),
        in_specs=[pl.BlockSpec(memory_space=pltpu.SMEM),
                  pl.BlockSpec((1, BAND_BLOCK, ATTN_WIDTH), lambda b, n: (b, n, 0)),
                  kv(prev), kv(cur), kv(nxt), kv(prev), kv(cur), kv(nxt), ctx, ctx],
        out_specs=pl.BlockSpec((1, BAND_BLOCK, ATTN_WIDTH), lambda b, n: (b, n, 0)),
        out_shape=jax.ShapeDtypeStruct((B, T, ATTN_WIDTH), BF16),
        compiler_params=_cparams("parallel", "parallel"),
        name="attn",
    )(sink, q, k, k, k, v, v, v, k_ctx, v_ctx)


_DIAG = SUBLANES
_LEVELS = (64, 32, 16, 8)


def _hgrn_chunk(logit, v, q, lb, st_ref, reverse):
    C = HGRN_CHUNK
    f = lb + (1.0 - lb) * jax.nn.sigmoid(logit)
    lf = jnp.log(f)
    kk = 1.0 - f
    qf = q.astype(F32)
    r = lax.broadcasted_iota(I32, (C, C), 0)
    c = lax.broadcasted_iota(I32, (C, C), 1)
    incl = (r <= c) if reverse else (r >= c)
    a = jnp.dot(incl.astype(F32), lf, precision=lax.Precision.HIGHEST, preferred_element_type=F32)
    a_end = a[0:1] if reverse else a[C - 1:C]

    st = st_ref[...]
    inter = lax.dot_general((qf * jnp.exp(a)).astype(BF16), st.astype(BF16), _NT, preferred_element_type=F32)
    kd = (kk * jnp.exp(a_end - a)).astype(BF16)
    st_ref[...] = st * jnp.exp(a_end) + lax.dot_general(v, kd, _TN, preferred_element_type=F32)

    later = (r < c) if reverse else (r > c)
    att = jnp.zeros((C, C), F32)
    for m in _LEVELS:
        a3 = a.reshape(C // (2 * m), 2 * m, HGRN_D)
        edge = a3[:, m:m + 1, :] if reverse else a3[:, m - 1:m, :]
        e = jnp.exp(-jnp.abs(a3 - edge)).reshape(C, HGRN_D)
        p = lax.dot_general((qf * e).astype(BF16), (kk * e).astype(BF16), _NT, preferred_element_type=F32)
        pair = ((r ^ c) >> (m.bit_length() - 1)) == 1
        att = jnp.where(pair & later, p, att)

    lane = lax.broadcasted_iota(I32, (_DIAG, C), 1)
    sub = lax.broadcasted_iota(I32, (_DIAG, C), 0)
    blocks = []
    for j in range(C // _DIAG):
        rows = slice(j * _DIAG, (j + 1) * _DIAG)
        a_j, q_j, k_j = a[rows], qf[rows], kk[rows]
        blk = jnp.zeros((_DIAG, C), F32)
        for s in range(_DIAG):
            e = jnp.exp(jnp.minimum(a_j - a_j[s:s + 1], 0.0))
            col = jnp.sum(q_j * e * k_j[s:s + 1], axis=-1, keepdims=True)
            ok = (sub <= s) if reverse else (sub >= s)
            blk = jnp.where((lane == j * _DIAG + s) & ok, col, blk)
        blocks.append(blk)
    att = att + jnp.concatenate(blocks, axis=0)
    return inter + jnp.dot(att.astype(BF16), v, preferred_element_type=F32)


def _hgrn_kernel(ff_ref, vf_ref, qf_ref, fb_ref, vb_ref, qb_ref, lbf_ref, lbb_ref, s0f_ref, s0b_ref,
                 of_ref, ob_ref, sf_ref, sb_ref, stf, stb):
    step = pl.program_id(2)
    nsub = ff_ref.shape[1] // HGRN_CHUNK

    @pl.when(step == 0)
    def _():
        stf[...] = s0f_ref[0, 0]
        stb[...] = s0b_ref[0, 0]

    def body(j, carry):
        fo = pl.multiple_of(j * HGRN_CHUNK, HGRN_CHUNK)
        rows = pl.ds(fo, HGRN_CHUNK)
        of_ref[0, rows, :] = _hgrn_chunk(ff_ref[0, rows, :], vf_ref[0, rows, :], qf_ref[0, rows, :],
                                         lbf_ref[...], stf, False)
        bo = pl.multiple_of((nsub - 1 - j) * HGRN_CHUNK, HGRN_CHUNK)
        rows = pl.ds(bo, HGRN_CHUNK)
        ob_ref[0, rows, :] = _hgrn_chunk(fb_ref[0, rows, :], vb_ref[0, rows, :], qb_ref[0, rows, :],
                                         lbb_ref[...], stb, True)
        return carry

    lax.fori_loop(0, nsub, body, 0)

    @pl.when(step == pl.num_programs(2) - 1)
    def _():
        sf_ref[0, 0] = stf[...]
        sb_ref[0, 0] = stb[...]


def _hgrn(ff, fb, val, q, lb_f, lb_b, s0f, s0b):
    B, T, _ = ff.shape
    ts = min(HGRN_STEP, T)
    ns = T // ts
    fwd = pl.BlockSpec((1, ts, HGRN_D), lambda b, h, s: (b, s, h))
    bwd = pl.BlockSpec((1, ts, HGRN_D), lambda b, h, s: (b, ns - 1 - s, h))
    lbs = pl.BlockSpec((1, HGRN_D), lambda b, h, s: (0, h))
    st = pl.BlockSpec((1, 1, HGRN_D, HGRN_D), lambda b, h, s: (b, h, 0, 0))
    o_sds = jax.ShapeDtypeStruct((B, T, HGRN_WIDTH), F32)
    s_sds = jax.ShapeDtypeStruct((B, HGRN_HEADS, HGRN_D, HGRN_D), F32)
    return pl.pallas_call(
        _hgrn_kernel,
        grid=(B, HGRN_HEADS, ns),
        in_specs=[fwd, fwd, fwd, bwd, bwd, bwd, lbs, lbs, st, st],
        out_specs=[fwd, bwd, st, st],
        out_shape=[o_sds, o_sds, s_sds, s_sds],
        scratch_shapes=[pltpu.VMEM((HGRN_D, HGRN_D), F32), pltpu.VMEM((HGRN_D, HGRN_D), F32)],
        compiler_params=_cparams("parallel", "parallel", "arbitrary"),
        name="hgrn",
    )(ff, val, q, fb, val, q, lb_f, lb_b, s0f, s0b)


def _outproj_kernel(attn_ref, of_ref, ob_ref, g_ref, x_ref, g1_ref, sh2_ref, sc2_ref, og_ref, nf_ref,
                    wo_ref, wq_ref, sk_ref, x1_ref, h2_ref, s_ref):
    o = of_ref[0] + ob_ref[0]
    og = og_ref[...]
    parts = []
    for hd in range(HGRN_HEADS):
        cs = slice(hd * HGRN_D, (hd + 1) * HGRN_D)
        gh = g_ref[0, :, cs].astype(F32)
        parts.append((_rms(o[:, cs], og) * (gh * jax.nn.sigmoid(gh))).astype(BF16))
    hg = jnp.concatenate(parts, axis=-1)
    mix = (jnp.dot(attn_ref[0], wo_ref[:ATTN_WIDTH], preferred_element_type=F32)
           + jnp.dot(hg, wo_ref[ATTN_WIDTH:], preferred_element_type=F32))
    x1 = x_ref[0] + g1_ref[0] * mix
    x1_ref[0] = x1
    h2 = _rms(x1, nf_ref[...]) * (1.0 + sc2_ref[0]) + sh2_ref[0]
    h2_ref[0] = h2
    pq = jnp.dot(h2.astype(BF16), wq_ref[...], preferred_element_type=F32)
    for hp in range(2 * PEER_HEADS):
        cs = slice(hp * PEER_DHALF, (hp + 1) * PEER_DHALF)
        for jb in range(pq.shape[0] // LANES):
            rows = slice(jb * LANES, (jb + 1) * LANES)
            s_ref[hp, 0, jb * PEER_NKEYS:(jb + 1) * PEER_NKEYS, :] = lax.dot_general(
                sk_ref[hp], pq[rows, cs].astype(BF16), _NT, preferred_element_type=F32)


def _outproj(attn, of, ob, g, x, g1, sh2, sc2, o_gain, norm_ffn, wo_bf16, wq_bf16, sk_bf16):
    B, T, _ = x.shape
    tm = min(ROW_TILE, T)
    nt = T // tm
    per = TOPK_TOKENS // tm
    assert (B * T) % TOPK_TOKENS == 0 and TOPK_TOKENS % tm == 0 and tm % LANES == 0
    row = lambda w: pl.BlockSpec((1, tm, w), lambda b, t: (b, t, 0))
    vec = pl.BlockSpec((1, 1, D_MODEL), lambda b, t: (b, 0, 0))
    n_sk = 2 * PEER_HEADS
    sds = lambda w: jax.ShapeDtypeStruct((B, T, w), F32)
    s_rows = tm // LANES * PEER_NKEYS
    s_spec = pl.BlockSpec((n_sk, 1, s_rows, LANES), lambda b, t: (0, (b * nt + t) // per, (b * nt + t) % per, 0))
    s_sds = jax.ShapeDtypeStruct((n_sk, B * T // TOPK_TOKENS, TOPK_TOKENS // LANES * PEER_NKEYS, LANES), F32)
    return pl.pallas_call(
        _outproj_kernel,
        grid=(B, T // tm),
        in_specs=[row(ATTN_WIDTH), row(HGRN_WIDTH), row(HGRN_WIDTH), row(HGRN_WIDTH), row(D_MODEL),
                  vec, vec, vec, _resident((1, HGRN_D)), _resident((1, D_MODEL)),
                  _resident((ATTN_WIDTH + HGRN_WIDTH, D_MODEL)), _resident((D_MODEL, n_sk * PEER_DHALF)),
                  _resident((n_sk, PEER_NKEYS, PEER_DHALF))],
        out_specs=[row(D_MODEL), row(D_MODEL), s_spec],
        out_shape=[sds(D_MODEL), sds(D_MODEL), s_sds],
        compiler_params=_cparams("parallel", "parallel"),
        name="outproj",
    )(attn, of, ob, g, x, g1, sh2, sc2, o_gain.reshape(1, HGRN_D), norm_ffn.reshape(1, D_MODEL),
      wo_bf16, wq_bf16, sk_bf16)


TOPK_TOKENS = SUBLANES * LANES
_CAND_PAIRS = [(a, b) for a in range(PEER_TOPK) for b in range(PEER_TOPK) if (a + 1) * (b + 1) <= PEER_TOPK]


def _first_argmax(values, ids, n_chains):
    per = -(-len(values) // n_chains)
    parts = []
    for lo in range(0, len(values), per):
        m, i = values[lo], ids[lo]
        if not isinstance(i, jax.Array):
            i = jnp.full(m.shape, i, F32)
        for v, vid in zip(values[lo + 1:lo + per], ids[lo + 1:lo + per]):
            c = v > m
            m = jnp.where(c, v, m)
            i = jnp.where(c, vid, i)
        parts.append((m, i))
    m, i = parts[0]
    for pm, pi in parts[1:]:
        c = pm > m
        m = jnp.where(c, pm, m)
        i = jnp.where(c, pi, i)
    return m, i


def _topk_kernel(s_ref, idx_ref, gate_ref, wk, tv, ti, cv, ci, bv):
    shape = (SUBLANES, LANES)
    none = jnp.full(shape, -1.0, F32)

    def head(h, carry):
        for p in range(2):
            hp = 2 * h + p
            for k in range(PEER_NKEYS):
                wk[k] = s_ref[hp, 0, pl.ds(k, SUBLANES, stride=PEER_NKEYS), :]

            def extract(r, prev, p=p):
                vals = []
                for k in range(PEER_NKEYS):
                    s = jnp.where(prev == float(k), NEG_INF, wk[k])
                    wk[k] = s
                    vals.append(s)
                m, i = _first_argmax(vals, [float(k) for k in range(PEER_NKEYS)], 4)
                tv[p, r] = m
                ti[p, r] = i
                return i

            lax.fori_loop(0, PEER_TOPK, extract, none)

        for c, (a, b) in enumerate(_CAND_PAIRS):
            cv[c] = tv[0, a] + tv[1, b]
            ci[c] = ti[0, a] * float(PEER_NKEYS) + ti[1, b]

        def pick(r, prev):
            vals, ids = [], []
            for c in range(len(_CAND_PAIRS)):
                cid = ci[c]
                s = jnp.where(cid == prev, NEG_INF, cv[c])
                cv[c] = s
                vals.append(s)
                ids.append(cid)
            m, i = _first_argmax(vals, ids, 2)
            bv[r] = m
            idx_ref[h * PEER_TOPK + r, 0] = i.astype(I32)
            return i

        lax.fori_loop(0, PEER_TOPK, pick, none)

        es = [jnp.exp(bv[r] - bv[0]) for r in range(PEER_TOPK)]
        tot = es[0]
        for e in es[1:]:
            tot = tot + e
        for r in range(PEER_TOPK):
            gate_ref[h * PEER_TOPK + r, 0] = es[r] / tot
        return carry

    lax.fori_loop(0, PEER_HEADS, head, 0)


def _topk(s):
    nt = s.shape[1]
    vreg = (SUBLANES, LANES)
    out_spec = pl.BlockSpec((PEER_SLOTS, 1) + vreg, lambda i: (0, i, 0, 0))
    return pl.pallas_call(
        _topk_kernel,
        grid=(nt,),
        in_specs=[pl.BlockSpec((s.shape[0], 1) + s.shape[2:], lambda i: (0, i, 0, 0))],
        out_specs=[out_spec, out_spec],
        out_shape=[jax.ShapeDtypeStruct((PEER_SLOTS, nt) + vreg, I32),
                   jax.ShapeDtypeStruct((PEER_SLOTS, nt) + vreg, F32)],
        scratch_shapes=[pltpu.VMEM((PEER_NKEYS,) + vreg, F32),
                        pltpu.VMEM((2, PEER_TOPK) + vreg, F32), pltpu.VMEM((2, PEER_TOPK) + vreg, F32),
                        pltpu.VMEM((len(_CAND_PAIRS),) + vreg, F32), pltpu.VMEM((len(_CAND_PAIRS),) + vreg, F32),
                        pltpu.VMEM((PEER_TOPK,) + vreg, F32)],
        compiler_params=_cparams("parallel"),
        name="topk",
    )(s)


SC_CORES = 2
SC_SUBCORES = 16
SC_LANES = 16
SC_WORKERS = SC_CORES * SC_SUBCORES
PEER_GROUP = 8
PEER_ITEMS = PEER_GROUP * PEER_HEADS
SC_UNROLL = 8
PACKED_WORDS = D_MODEL // 2
EXPERT_SLAB = (PACKED_WORDS // LANES, LANES)


_SC_PARAMS = pltpu.CompilerParams(needs_layout_passes=False)


def _sc_mesh():
    return plsc.VectorSubcoreMesh(core_axis_name="c", subcore_axis_name="s")


def _sc_worker():
    return lax.axis_index("s") * SC_CORES + lax.axis_index("c")


def _pack_kernel(t_ref, o_ref):
    x = t_ref[...]
    lo = pltpu.bitcast(x[:, :PACKED_WORDS].astype(BF16).astype(F32), jnp.uint32)
    hi = pltpu.bitcast(x[:, PACKED_WORDS:].astype(BF16).astype(F32), jnp.uint32)
    word = pltpu.bitcast((hi & jnp.uint32(0xFFFF0000)) | (lo >> 16), I32)
    for s in range(EXPERT_SLAB[0]):
        o_ref[:, s, :] = word[:, s * LANES:(s + 1) * LANES]


def _pack_table(table):
    e = table.shape[0]
    tr = 256
    return pl.pallas_call(
        _pack_kernel,
        grid=(e // tr,),
        in_specs=[pl.BlockSpec((tr, D_MODEL), lambda i: (i, 0))],
        out_specs=pl.BlockSpec((tr,) + EXPERT_SLAB, lambda i: (i, 0, 0)),
        out_shape=jax.ShapeDtypeStruct((e,) + EXPERT_SLAB, I32),
        compiler_params=_cparams("parallel"),
        name="pack_table",
    )(table)


def _row_words(rows, k):
    per = LANES // SC_LANES
    sub, cols = k // per, pl.ds((k % per) * SC_LANES, SC_LANES)
    return tuple(rows[r, sub, cols] for r in range(PEER_TOPK))


def _unpack(word):
    return lax.bitcast_convert_type(word << 16, F32), lax.bitcast_convert_type(word & -65536, F32)


def _half_cols(k):
    return pl.ds(k * SC_LANES, SC_LANES), pl.ds(PACKED_WORDS + k * SC_LANES, SC_LANES)


def _sc_item_pipeline(table_hbm, idx_v, rows0, rows1, sem0, sem1, compute):
    def gather(j, rows, sem):
        return pltpu.make_async_copy(table_hbm.at[idx_v.at[j]], rows, sem)

    gather(0, rows0, sem0).start()

    @pl.loop(0, PEER_ITEMS // 2)
    def _(i):
        j = 2 * i
        gather(j + 1, rows1, sem1).start()
        gather(j, rows0, sem0).wait()
        compute(j, rows0)

        @pl.when(i < PEER_ITEMS // 2 - 1)
        def _():
            gather(j + 2, rows0, sem0).start()

        gather(j + 1, rows1, sem1).wait()
        compute(j + 1, rows1)


def _peer_dot_kernel(x_hbm, idx_hbm, u_hbm, a_hbm, x_v, idx_v, rows0, rows1, a_v, sem0, sem1):
    tok_per_w = x_hbm.shape[0] // SC_WORKERS
    wid = _sc_worker()
    lane = lax.iota(I32, SC_LANES)

    def compute(j, rows):
        t = j // PEER_HEADS

        zero = jnp.zeros((SC_LANES,), F32)

        def load(k):
            lo, hi = _half_cols(k)
            return (x_v[t, lo], x_v[t, hi]) + _row_words(rows, k)

        def fma(acc, vals):
            out = []
            for r in range(PEER_TOPK):
                lo, hi = _unpack(vals[2 + r])
                out.append(acc[r] + lo * vals[0] + hi * vals[1])
            return tuple(out)

        @plsc.parallel_loop(1, PACKED_WORDS // SC_LANES, unroll=SC_UNROLL, carry=((zero,) * PEER_TOPK, load(0)))
        def state(k, state):
            acc, vals = state
            nxt = load(k)
            return fma(acc, vals), nxt

        acc = fma(*state)
        out = zero
        for r in range(PEER_TOPK):
            out = jnp.where(lane == r, jnp.sum(acc[r]), out)
        a_v[j, :] = out

    @pl.loop(0, tok_per_w // PEER_GROUP)
    def _(g):
        tok0 = wid * tok_per_w + g * PEER_GROUP
        item0 = tok0 * PEER_HEADS
        pltpu.sync_copy(x_hbm.at[pl.ds(tok0, PEER_GROUP)], x_v)
        pltpu.sync_copy(idx_hbm.at[pl.ds(item0, PEER_ITEMS)], idx_v)
        _sc_item_pipeline(u_hbm, idx_v, rows0, rows1, sem0, sem1, compute)
        pltpu.sync_copy(a_v, a_hbm.at[pl.ds(item0, PEER_ITEMS)])


def _peer_sum_kernel(w_hbm, idx_hbm, v_hbm, o_hbm, w_v, idx_v, rows0, rows1, o_v, sem0, sem1):
    tok_per_w = o_hbm.shape[0] // SC_WORKERS
    wid = _sc_worker()
    zero = jnp.zeros((SC_LANES,), F32)

    def compute(j, rows):
        t = j // PEER_HEADS
        jv = jnp.full((SC_LANES,), j, I32)
        ws = [plsc.load_gather(w_v, [jv, jnp.full((SC_LANES,), r, I32)]) for r in range(PEER_TOPK)]

        def load(k):
            lo, hi = _half_cols(k)
            return (o_v[t, lo], o_v[t, hi]) + _row_words(rows, k)

        def finish(k, vals):
            acc_lo, acc_hi = vals[0], vals[1]
            for r in range(PEER_TOPK):
                lo, hi = _unpack(vals[2 + r])
                acc_lo = acc_lo + lo * ws[r]
                acc_hi = acc_hi + hi * ws[r]
            lo, hi = _half_cols(k)
            o_v[t, lo] = acc_lo
            o_v[t, hi] = acc_hi

        @plsc.parallel_loop(1, PACKED_WORDS // SC_LANES, unroll=SC_UNROLL, carry=load(0))
        def vals(k, vals):
            nxt = load(k)
            finish(k - 1, vals)
            return nxt

        finish(PACKED_WORDS // SC_LANES - 1, vals)

    @pl.loop(0, tok_per_w // PEER_GROUP)
    def _(g):
        tok0 = wid * tok_per_w + g * PEER_GROUP
        item0 = tok0 * PEER_HEADS
        pltpu.sync_copy(w_hbm.at[pl.ds(item0, PEER_ITEMS)], w_v)
        pltpu.sync_copy(idx_hbm.at[pl.ds(item0, PEER_ITEMS)], idx_v)

        @pl.loop(0, PEER_GROUP)
        def _(t):
            @pl.loop(0, D_MODEL // SC_LANES)
            def _(k):
                o_v[t, pl.ds(k * SC_LANES, SC_LANES)] = zero

        _sc_item_pipeline(v_hbm, idx_v, rows0, rows1, sem0, sem1, compute)
        pltpu.sync_copy(o_v, o_hbm.at[pl.ds(tok0, PEER_GROUP)])


def _sc_scratch(first):
    return [first,
            pltpu.VMEM((PEER_ITEMS, PEER_TOPK), I32),
            pltpu.VMEM((PEER_TOPK,) + EXPERT_SLAB, I32),
            pltpu.VMEM((PEER_TOPK,) + EXPERT_SLAB, I32)]


def _peer_dot(h2, idx, u):
    n = h2.shape[0]
    assert n % (SC_WORKERS * PEER_GROUP) == 0
    call = pl.kernel(
        _peer_dot_kernel,
        out_type=jax.ShapeDtypeStruct((n * PEER_HEADS, PEER_TOPK), F32),
        mesh=_sc_mesh(),
        scratch_types=_sc_scratch(pltpu.VMEM((PEER_GROUP, D_MODEL), F32))
        + [pltpu.VMEM((PEER_ITEMS, PEER_TOPK), F32), pltpu.SemaphoreType.DMA, pltpu.SemaphoreType.DMA],
        compiler_params=_SC_PARAMS,
    )
    return call(h2, idx, u)


def _peer_sum(w, idx, v, n):
    assert n % (SC_WORKERS * PEER_GROUP) == 0
    call = pl.kernel(
        _peer_sum_kernel,
        out_type=jax.ShapeDtypeStruct((n, D_MODEL), F32),
        mesh=_sc_mesh(),
        scratch_types=_sc_scratch(pltpu.VMEM((PEER_ITEMS, PEER_TOPK), F32))
        + [pltpu.VMEM((PEER_GROUP, D_MODEL), F32), pltpu.SemaphoreType.DMA, pltpu.SemaphoreType.DMA],
        compiler_params=_SC_PARAMS,
    )
    return call(w, idx, v)


def _gelu_gate_kernel(a_ref, g_ref, w_ref):
    a = a_ref[...]
    w_ref[...] = g_ref[...] * (0.5 * a * (1.0 + lax.erf(a * (2.0 ** -0.5))))


def _gelu_gate(a, gate):
    n = a.shape[0]
    tm = min(2048, n)
    spec = pl.BlockSpec((tm, PEER_SLOTS), lambda i: (i, 0))
    return pl.pallas_call(
        _gelu_gate_kernel, grid=(n // tm,), in_specs=[spec, spec], out_specs=spec,
        out_shape=jax.ShapeDtypeStruct(a.shape, F32), compiler_params=_cparams("parallel"), name="gelu_gate",
    )(a, gate)


def _residual_kernel(x_ref, g_ref, p_ref, o_ref):
    o_ref[0] = x_ref[0] + g_ref[0] * p_ref[0]


def _residual(x1, g2, peer):
    B, T, _ = x1.shape
    tm = min(2 * ROW_TILE, T)
    row = pl.BlockSpec((1, tm, D_MODEL), lambda b, t: (b, t, 0))
    vec = pl.BlockSpec((1, 1, D_MODEL), lambda b, t: (b, 0, 0))
    return pl.pallas_call(
        _residual_kernel, grid=(B, T // tm), in_specs=[row, vec, row], out_specs=row,
        out_shape=jax.ShapeDtypeStruct(x1.shape, F32), compiler_params=_cparams("parallel", "parallel"),
        name="residual",
    )(x1, g2, peer)


def kernel(x, c, ctx, c_ctx, w_ada, b_ada, norm_mix, norm_ffn, w_in, q_norm, k_norm, attn_sink, hgrn_lb_logits,
           hgrn_norm, w_out, peer_w_q, peer_sub_keys, peer_u, peer_v):
    assert w_ada.shape[0] == 1, "single-layer block"
    B, T, D = x.shape
    L = ctx.shape[1]
    n = B * T

    cvecs = jnp.zeros((SUBLANES, D), F32).at[:B].set(c).at[B].set(c_ctx)
    mod = _ada(cvecs, w_ada[0], b_ada[0])
    part = lambda rows, i: rows[:, None, i * D:(i + 1) * D]
    mod_x = mod[:B]
    mod_c = jnp.broadcast_to(mod[B:B + 1], (B, 6 * D))
    sh1, sc1, g1, sh2, sc2, g2 = (part(mod_x, i) for i in range(6))

    lbs = jnp.cumsum(jax.nn.softmax(hgrn_lb_logits.astype(F32), axis=1), axis=1)
    lb_f, lb_b = lbs[0, 0].reshape(1, HGRN_WIDTH), lbs[1, 0].reshape(1, HGRN_WIDTH)

    w_in_b = w_in[0].astype(BF16)
    rope = _rope_tables(T)
    kc, vc, ffc, fbc, ic, _, qhc, _ = _inproj(ctx, part(mod_c, 0), part(mod_c, 1), norm_mix[0], w_in_b,
                                              _identity_rope(L), q_norm[0], k_norm[0])
    s0 = jnp.zeros((B, HGRN_HEADS, HGRN_D, HGRN_D), F32)
    _, _, sfc, sbc = _hgrn(ffc, fbc, ic, qhc, lb_f, lb_b, s0, s0)

    sk = peer_sub_keys[0].reshape(2 * PEER_HEADS, PEER_NKEYS, PEER_DHALF).astype(BF16)
    w_out_b, w_q_b = w_out[0].astype(BF16), peer_w_q[0].astype(BF16)
    u_packed, v_packed = _pack_table(peer_u[0]), _pack_table(peer_v[0])

    outs = []
    for b in range(B):
        one = lambda t: t[b:b + 1]
        xb = one(x)
        kx, vx, ffx, fbx, ix, qx, qhx, gx = _inproj(xb, one(sh1), one(sc1), norm_mix[0], w_in_b, rope,
                                                    q_norm[0], k_norm[0])
        attn = _attention(qx, kx, vx, one(kc), one(vc), attn_sink[0])
        of, ob, _, _ = _hgrn(ffx, fbx, ix, qhx, lb_f, lb_b, one(sfc), one(sbc))
        x1, h2, s = _outproj(attn, of, ob, gx, xb, one(g1), one(sh2), one(sc2), hgrn_norm[0], norm_ffn[0],
                             w_out_b, w_q_b, sk)
        idx_t, gate_t = _topk(s)
        idx16 = idx_t.reshape(PEER_SLOTS, T).T.reshape(T * PEER_HEADS, PEER_TOPK)
        gate = gate_t.reshape(PEER_SLOTS, T).T
        a = _peer_dot(h2.reshape(T, D), idx16, u_packed)
        w = _gelu_gate(a.reshape(T, PEER_SLOTS), gate)
        peer = _peer_sum(w.reshape(T * PEER_HEADS, PEER_TOPK), idx16, v_packed, T)
        outs.append(_residual(x1, one(g2), peer.reshape(1, T, D)))
    return jnp.concatenate(outs, axis=0)
```

```python
import functools

import jax
import jax.numpy as jnp
from jax import lax
from jax.experimental import pallas as pl
from jax.experimental.pallas import tpu as pltpu
from jax.experimental.pallas import tpu_sc as plsc

F32 = jnp.float32
BF16 = jnp.bfloat16
I32 = jnp.int32

D_MODEL = 2048
GRID_W = 64
EPS = 1e-6
HEAD_DIM = 128
ATTN_HEADS = 8
ATTN_KV_HEADS = 2
ATTN_GROUP = ATTN_HEADS // ATTN_KV_HEADS
BAND_BLOCK = 128
ROPE_THETA = 10000.0
HGRN_HEADS = 8
HGRN_D = 128
ATTN_WIDTH = ATTN_HEADS * HEAD_DIM
KV_WIDTH = ATTN_KV_HEADS * HEAD_DIM
HGRN_WIDTH = HGRN_HEADS * HGRN_D
COL_K = 0
COL_V = COL_K + KV_WIDTH
COL_FF = COL_V + KV_WIDTH
COL_FB = COL_FF + HGRN_WIDTH
COL_I = COL_FB + HGRN_WIDTH
COL_Q = COL_I + HGRN_WIDTH
COL_QH = COL_Q + ATTN_WIDTH
COL_G = COL_QH + HGRN_WIDTH
N_IN_COLS = COL_G + HGRN_WIDTH
PEER_HEADS = 8
PEER_NKEYS = 128
PEER_DHALF = 128
PEER_TOPK = 16
PEER_SLOTS = PEER_HEADS * PEER_TOPK

LANES = 128
SUBLANES = 8
VMEM_LIMIT_BYTES = 56 * 1024 * 1024

ROW_TILE = 256
HGRN_CHUNK = 128
HGRN_STEP = 512
NEG_INF = float("-inf")


def _cparams(*sem):
    return pltpu.CompilerParams(dimension_semantics=sem, vmem_limit_bytes=VMEM_LIMIT_BYTES)


def _cost(flops, bytes_accessed, transcendentals=0):
    return pl.CostEstimate(flops=int(flops), bytes_accessed=int(bytes_accessed), transcendentals=int(transcendentals))


def _resident(shape):
    nd = len(shape)
    return pl.BlockSpec(shape, lambda *_: (0,) * nd, pipeline_mode=pl.Buffered(1))


def _ada_kernel(c_ref, w_ref, b_ref, o_ref):
    c = c_ref[...]
    s = (c * jax.nn.sigmoid(c)).astype(BF16)
    o_ref[...] = jnp.dot(s, w_ref[...].astype(BF16), preferred_element_type=F32) + b_ref[...]


def _ada(cvecs, w, b):
    n = w.shape[1]
    tn = 1024
    return pl.pallas_call(
        _ada_kernel,
        grid=(n // tn,),
        in_specs=[pl.BlockSpec((SUBLANES, D_MODEL), lambda j: (0, 0)),
                  pl.BlockSpec((D_MODEL, tn), lambda j: (0, j)),
                  pl.BlockSpec((1, tn), lambda j: (0, j))],
        out_specs=pl.BlockSpec((SUBLANES, tn), lambda j: (0, j)),
        out_shape=jax.ShapeDtypeStruct((SUBLANES, n), F32),
        compiler_params=_cparams("arbitrary"),
        name="ada",
    )(cvecs, w, b.reshape(1, n))


def _pack_bf16_pairs(x):
    w = x.shape[1] // 2
    lo = pltpu.bitcast(x[:, :w].astype(BF16).astype(F32), jnp.uint32)
    hi = pltpu.bitcast(x[:, w:].astype(BF16).astype(F32), jnp.uint32)
    return pltpu.bitcast((hi & jnp.uint32(0xFFFF0000)) | (lo >> 16), I32)


def _rms(x, gain):
    return x * lax.rsqrt(jnp.mean(x * x, axis=-1, keepdims=True) + EPS) * gain


def _rope(x, cos, sin_a, sin_b):
    q = HEAD_DIM // 4
    return x * cos + pltpu.roll(x, HEAD_DIM - q, 1) * sin_a + pltpu.roll(x, q, 1) * sin_b


def _inproj_kernel(x_ref, sh_ref, sc_ref, gain_ref, w_ref, cos_ref, sa_ref, sb_ref, qg_ref, kg_ref,
                   k_ref, v_ref, ff_ref, fb_ref, i_ref, q_ref, qh_ref, g_ref):
    x = x_ref[0]
    h = _rms(x, gain_ref[...]) * (1.0 + sc_ref[0]) + sh_ref[0]
    hb = h.astype(BF16)

    def seg(lo, width):
        return jnp.dot(hb, w_ref[:, lo:lo + width], preferred_element_type=F32)

    cos, sa, sb = cos_ref[...], sa_ref[...], sb_ref[...]

    def normed_heads(p, gain, n_heads, out_ref):
        for hd in range(n_heads):
            ph = p[:, hd * HEAD_DIM:(hd + 1) * HEAD_DIM]
            out_ref[0, :, hd * HEAD_DIM:(hd + 1) * HEAD_DIM] = _rope(_rms(ph, gain), cos, sa, sb).astype(BF16)

    normed_heads(seg(COL_K, KV_WIDTH), kg_ref[...], ATTN_KV_HEADS, k_ref)
    v_ref[0] = seg(COL_V, KV_WIDTH).astype(BF16)
    ff_ref[0] = seg(COL_FF, HGRN_WIDTH)
    fb_ref[0] = seg(COL_FB, HGRN_WIDTH)
    i_ref[0] = seg(COL_I, HGRN_WIDTH).astype(BF16)
    normed_heads(seg(COL_Q, ATTN_WIDTH), qg_ref[...], ATTN_HEADS, q_ref)
    qh_ref[0] = seg(COL_QH, HGRN_WIDTH).astype(BF16)
    g_ref[0] = seg(COL_G, HGRN_WIDTH).astype(BF16)


def _inproj(x, shift, scale, gain, w_bf16, rope, q_gain, k_gain, after=()):
    B, T, _ = x.shape
    n_in = 10

    def body(*refs):
        _inproj_kernel(*refs[:n_in], *refs[n_in + len(after):])
    tm = min(ROW_TILE, T)
    row = lambda w: pl.BlockSpec((1, tm, w), lambda b, t: (b, t, 0))
    vec = pl.BlockSpec((1, 1, D_MODEL), lambda b, t: (b, 0, 0))
    tab = pl.BlockSpec((tm, HEAD_DIM), lambda b, t: (t, 0))
    out_w = [(KV_WIDTH, BF16), (KV_WIDTH, BF16), (HGRN_WIDTH, F32), (HGRN_WIDTH, F32), (HGRN_WIDTH, BF16),
             (ATTN_WIDTH, BF16), (HGRN_WIDTH, BF16), (HGRN_WIDTH, BF16)]
    return pl.pallas_call(
        body,
        grid=(B, T // tm),
        in_specs=[row(D_MODEL), vec, vec, _resident((1, D_MODEL)), _resident((D_MODEL, N_IN_COLS)),
                  tab, tab, tab, _resident((1, HEAD_DIM)), _resident((1, HEAD_DIM))]
        + [pl.BlockSpec(memory_space=pl.ANY)] * len(after),
        out_specs=[row(w) for w, _ in out_w],
        out_shape=[jax.ShapeDtypeStruct((B, T, w), dt) for w, dt in out_w],
        compiler_params=_cparams("parallel", "parallel"),
        cost_estimate=_cost(2 * B * T * D_MODEL * N_IN_COLS, B * T * (D_MODEL * 4 + N_IN_COLS * 3)),
        name="inproj",
    )(x, shift, scale, gain.reshape(1, D_MODEL), w_bf16, *rope,
      q_gain.reshape(1, HEAD_DIM), k_gain.reshape(1, HEAD_DIM), *after)


def _rope_tables(T):
    rows = T // GRID_W
    row_pos = jnp.repeat(jnp.arange(rows, dtype=F32), GRID_W)
    col_pos = jnp.tile(jnp.arange(GRID_W, dtype=F32), rows)
    half = HEAD_DIM // 2
    inv_freq = jnp.power(ROPE_THETA, -jnp.arange(0, half, 2, dtype=F32) / half)
    ang_r = row_pos[:, None] * inv_freq
    ang_c = col_pos[:, None] * inv_freq
    cr, sr, cc, sc = jnp.cos(ang_r), jnp.sin(ang_r), jnp.cos(ang_c), jnp.sin(ang_c)
    z = jnp.zeros_like(sr)
    return (jnp.concatenate([cr, cr, cc, cc], -1),
            jnp.concatenate([-sr, z, -sc, z], -1),
            jnp.concatenate([z, sr, z, sc], -1))


def _identity_rope(T):
    return (jnp.ones((T, HEAD_DIM), F32), jnp.zeros((T, HEAD_DIM), F32), jnp.zeros((T, HEAD_DIM), F32))


_NT = (((1,), (1,)), ((), ()))
_TN = (((0,), (0,)), ((), ()))


def _attn_kernel(sink_ref, q_ref, kp_ref, kc_ref, kn_ref, vp_ref, vc_ref, vn_ref, kx_ref, vx_ref, o_ref):
    n = pl.program_id(1)
    nb = pl.num_programs(1)
    blk = BAND_BLOCK
    rows = ATTN_GROUP * blk
    qi = lax.broadcasted_iota(I32, (rows, blk), 0) & (blk - 1)
    kj = lax.broadcasted_iota(I32, (rows, blk), 1)
    prev_ok = kj >= qi + jnp.where(n > 0, 0, blk)
    next_ok = kj <= qi - jnp.where(n < nb - 1, 0, blk)
    scale = HEAD_DIM ** -0.5
    for h in range(ATTN_KV_HEADS):
        cs = slice(h * HEAD_DIM, (h + 1) * HEAD_DIM)
        heads = [h * ATTN_GROUP + g for g in range(ATTN_GROUP)]
        q4 = jnp.concatenate([q_ref[0, :, hd * HEAD_DIM:(hd + 1) * HEAD_DIM] for hd in heads], axis=0)

        def scores(k_ref):
            return lax.dot_general(q4, k_ref[0, :, cs], _NT, preferred_element_type=F32) * scale

        s_p = jnp.where(prev_ok, scores(kp_ref), NEG_INF)
        s_c = scores(kc_ref)
        s_n = jnp.where(next_ok, scores(kn_ref), NEG_INF)
        s_x = scores(kx_ref)
        sink = jnp.concatenate([jnp.full((blk, 1), sink_ref[hd], F32) for hd in heads], axis=0)
        m = jnp.maximum(jnp.maximum(jnp.max(s_p, -1, keepdims=True), jnp.max(s_c, -1, keepdims=True)),
                        jnp.maximum(jnp.max(s_n, -1, keepdims=True), jnp.max(s_x, -1, keepdims=True)))
        m = jnp.maximum(m, sink)
        p_p, p_c, p_n, p_x = jnp.exp(s_p - m), jnp.exp(s_c - m), jnp.exp(s_n - m), jnp.exp(s_x - m)
        denom = (jnp.sum(p_p, -1, keepdims=True) + jnp.sum(p_c, -1, keepdims=True)
                 + jnp.sum(p_n, -1, keepdims=True) + jnp.sum(p_x, -1, keepdims=True) + jnp.exp(sink - m))

        def pv(p, v_ref):
            return jnp.dot(p.astype(BF16), v_ref[0, :, cs], preferred_element_type=F32)

        o = (pv(p_p, vp_ref) + pv(p_c, vc_ref) + pv(p_n, vn_ref) + pv(p_x, vx_ref)) / denom
        for g, hd in enumerate(heads):
            o_ref[0, :, hd * HEAD_DIM:(hd + 1) * HEAD_DIM] = o[g * blk:(g + 1) * blk].astype(BF16)


def _attention(q, k, v, k_ctx, v_ctx, sink):
    B, T, _ = q.shape
    L = k_ctx.shape[1]
    nb = T // BAND_BLOCK
    kv = lambda f: pl.BlockSpec((1, BAND_BLOCK, KV_WIDTH), lambda b, n: (b, f(n), 0))
    prev, cur, nxt = (lambda n: jnp.maximum(n - 1, 0)), (lambda n: n), (lambda n: jnp.minimum(n + 1, nb - 1))
    ctx = pl.BlockSpec((1, L, KV_WIDTH), lambda b, n: (b, 0, 0))
    return pl.pallas_call(
        _attn_kernel,
        grid=(B, nb),
        in_specs=[pl.BlockSpec(memory_space=pltpu.SMEM),
                  pl.BlockSpec((1, BAND_BLOCK, ATTN_WIDTH), lambda b, n: (b, n, 0)),
                  kv(prev), kv(cur), kv(nxt), kv(prev), kv(cur), kv(nxt), ctx, ctx],
        out_specs=pl.BlockSpec((1, BAND_BLOCK, ATTN_WIDTH), lambda b, n: (b, n, 0)),
        out_shape=jax.ShapeDtypeStruct((B, T, ATTN_WIDTH), BF16),
        compiler_params=_cparams("parallel", "parallel"),
        cost_estimate=_cost(4 * B * T * ATTN_WIDTH * (3 * BAND_BLOCK + L), 8 * B * T * ATTN_WIDTH,
                            B * T * ATTN_HEADS * (3 * BAND_BLOCK + L)),
        name="attn",
    )(sink, q, k, k, k, v, v, v, k_ctx, v_ctx)


_DIAG = SUBLANES
_LEVELS = (64, 32, 16, 8)


def _hgrn_chunk(logit, v, q, lb, st_ref, reverse):
    C = HGRN_CHUNK
    f = lb + (1.0 - lb) * jax.nn.sigmoid(logit)
    lf = jnp.log(f)
    kk = 1.0 - f
    qf = q.astype(F32)
    r = lax.broadcasted_iota(I32, (C, C), 0)
    c = lax.broadcasted_iota(I32, (C, C), 1)
    incl = (r <= c) if reverse else (r >= c)
    a = jnp.dot(incl.astype(F32), lf, precision=lax.Precision.HIGHEST, preferred_element_type=F32)
    a_end = a[0:1] if reverse else a[C - 1:C]

    st = st_ref[...]
    inter = lax.dot_general((qf * jnp.exp(a)).astype(BF16), st.astype(BF16), _NT, preferred_element_type=F32)
    kd = (kk * jnp.exp(a_end - a)).astype(BF16)
    st_ref[...] = st * jnp.exp(a_end) + lax.dot_general(v, kd, _TN, preferred_element_type=F32)

    later = (r < c) if reverse else (r > c)
    att = jnp.zeros((C, C), F32)
    for m in _LEVELS:
        a3 = a.reshape(C // (2 * m), 2 * m, HGRN_D)
        edge = a3[:, m:m + 1, :] if reverse else a3[:, m - 1:m, :]
        e = jnp.exp(-jnp.abs(a3 - edge)).reshape(C, HGRN_D)
        p = lax.dot_general((qf * e).astype(BF16), (kk * e).astype(BF16), _NT, preferred_element_type=F32)
        pair = ((r ^ c) >> (m.bit_length() - 1)) == 1
        att = jnp.where(pair & later, p, att)

    lane = lax.broadcasted_iota(I32, (_DIAG, C), 1)
    sub = lax.broadcasted_iota(I32, (_DIAG, C), 0)
    blocks = []
    for j in range(C // _DIAG):
        rows = slice(j * _DIAG, (j + 1) * _DIAG)
        a_j, q_j, k_j = a[rows], qf[rows], kk[rows]
        blk = jnp.zeros((_DIAG, C), F32)
        for s in range(_DIAG):
            e = jnp.exp(jnp.minimum(a_j - a_j[s:s + 1], 0.0))
            col = jnp.sum(q_j * e * k_j[s:s + 1], axis=-1, keepdims=True)
            ok = (sub <= s) if reverse else (sub >= s)
            blk = jnp.where((lane == j * _DIAG + s) & ok, col, blk)
        blocks.append(blk)
    att = att + jnp.concatenate(blocks, axis=0)
    return inter + jnp.dot(att.astype(BF16), v, preferred_element_type=F32)


def _hgrn_kernel(ff_ref, vf_ref, qf_ref, fb_ref, vb_ref, qb_ref, lbf_ref, lbb_ref, s0f_ref, s0b_ref,
                 of_ref, ob_ref, sf_ref, sb_ref, stf, stb):
    step = pl.program_id(2)
    nsub = ff_ref.shape[1] // HGRN_CHUNK

    @pl.when(step == 0)
    def _():
        stf[...] = s0f_ref[0, 0]
        stb[...] = s0b_ref[0, 0]

    def body(j, carry):
        fo = pl.multiple_of(j * HGRN_CHUNK, HGRN_CHUNK)
        rows = pl.ds(fo, HGRN_CHUNK)
        of_ref[0, rows, :] = _hgrn_chunk(ff_ref[0, rows, :], vf_ref[0, rows, :], qf_ref[0, rows, :],
                                         lbf_ref[...], stf, False)
        bo = pl.multiple_of((nsub - 1 - j) * HGRN_CHUNK, HGRN_CHUNK)
        rows = pl.ds(bo, HGRN_CHUNK)
        ob_ref[0, rows, :] = _hgrn_chunk(fb_ref[0, rows, :], vb_ref[0, rows, :], qb_ref[0, rows, :],
                                         lbb_ref[...], stb, True)
        return carry

    lax.fori_loop(0, nsub, body, 0)

    @pl.when(step == pl.num_programs(2) - 1)
    def _():
        sf_ref[0, 0] = stf[...]
        sb_ref[0, 0] = stb[...]


def _hgrn(ff, fb, val, q, lb_f, lb_b, s0f, s0b):
    B, T, _ = ff.shape
    ts = min(HGRN_STEP, T)
    ns = T // ts
    fwd = pl.BlockSpec((1, ts, HGRN_D), lambda b, h, s: (b, s, h))
    bwd = pl.BlockSpec((1, ts, HGRN_D), lambda b, h, s: (b, ns - 1 - s, h))
    lbs = pl.BlockSpec((1, HGRN_D), lambda b, h, s: (0, h))
    st = pl.BlockSpec((1, 1, HGRN_D, HGRN_D), lambda b, h, s: (b, h, 0, 0))
    o_sds = jax.ShapeDtypeStruct((B, T, HGRN_WIDTH), F32)
    s_sds = jax.ShapeDtypeStruct((B, HGRN_HEADS, HGRN_D, HGRN_D), F32)
    return pl.pallas_call(
        _hgrn_kernel,
        grid=(B, HGRN_HEADS, ns),
        in_specs=[fwd, fwd, fwd, bwd, bwd, bwd, lbs, lbs, st, st],
        out_specs=[fwd, bwd, st, st],
        out_shape=[o_sds, o_sds, s_sds, s_sds],
        scratch_shapes=[pltpu.VMEM((HGRN_D, HGRN_D), F32), pltpu.VMEM((HGRN_D, HGRN_D), F32)],
        compiler_params=_cparams("parallel", "parallel", "arbitrary"),
        cost_estimate=_cost(32 * B * T * HGRN_WIDTH * HGRN_CHUNK, 24 * B * T * HGRN_WIDTH, 8 * B * T * HGRN_WIDTH),
        name="hgrn",
    )(ff, val, q, fb, val, q, lb_f, lb_b, s0f, s0b)


def _outproj_kernel(attn_ref, of_ref, ob_ref, g_ref, x_ref, g1_ref, sh2_ref, sc2_ref, og_ref, nf_ref,
                    wo_ref, wq_ref, sk_ref, x1_ref, h2_ref, s_ref):
    o = of_ref[0] + ob_ref[0]
    og = og_ref[...]
    parts = []
    for hd in range(HGRN_HEADS):
        cs = slice(hd * HGRN_D, (hd + 1) * HGRN_D)
        gh = g_ref[0, :, cs].astype(F32)
        parts.append((_rms(o[:, cs], og) * (gh * jax.nn.sigmoid(gh))).astype(BF16))
    hg = jnp.concatenate(parts, axis=-1)
    mix = (jnp.dot(attn_ref[0], wo_ref[:ATTN_WIDTH], preferred_element_type=F32)
           + jnp.dot(hg, wo_ref[ATTN_WIDTH:], preferred_element_type=F32))
    x1 = x_ref[0] + g1_ref[0] * mix
    x1_ref[0] = x1
    h2 = _rms(x1, nf_ref[...]) * (1.0 + sc2_ref[0]) + sh2_ref[0]
    h2_ref[0] = _pack_bf16_pairs(h2)
    pq = jnp.dot(h2.astype(BF16), wq_ref[...], preferred_element_type=F32)
    for hp in range(2 * PEER_HEADS):
        cs = slice(hp * PEER_DHALF, (hp + 1) * PEER_DHALF)
        for jb in range(pq.shape[0] // LANES):
            rows = slice(jb * LANES, (jb + 1) * LANES)
            s_ref[hp, 0, jb * PEER_NKEYS:(jb + 1) * PEER_NKEYS, :] = lax.dot_general(
                sk_ref[hp], pq[rows, cs].astype(BF16), _NT, preferred_element_type=F32)


def _outproj(attn, of, ob, g, x, g1, sh2, sc2, o_gain, norm_ffn, wo_bf16, wq_bf16, sk_bf16):
    B, T, _ = x.shape
    tm = min(ROW_TILE, T)
    nt = T // tm
    per = TOPK_TOKENS // tm
    assert (B * T) % TOPK_TOKENS == 0 and TOPK_TOKENS % tm == 0 and tm % LANES == 0
    row = lambda w: pl.BlockSpec((1, tm, w), lambda b, t: (b, t, 0))
    vec = pl.BlockSpec((1, 1, D_MODEL), lambda b, t: (b, 0, 0))
    n_sk = 2 * PEER_HEADS
    sds = lambda w: jax.ShapeDtypeStruct((B, T, w), F32)
    s_rows = tm // LANES * PEER_NKEYS
    s_spec = pl.BlockSpec((n_sk, 1, s_rows, LANES), lambda b, t: (0, (b * nt + t) // per, (b * nt + t) % per, 0))
    s_sds = jax.ShapeDtypeStruct((n_sk, B * T // TOPK_TOKENS, TOPK_TOKENS // LANES * PEER_NKEYS, LANES), F32)
    return pl.pallas_call(
        _outproj_kernel,
        grid=(B, T // tm),
        in_specs=[row(ATTN_WIDTH), row(HGRN_WIDTH), row(HGRN_WIDTH), row(HGRN_WIDTH), row(D_MODEL),
                  vec, vec, vec, _resident((1, HGRN_D)), _resident((1, D_MODEL)),
                  _resident((ATTN_WIDTH + HGRN_WIDTH, D_MODEL)), _resident((D_MODEL, n_sk * PEER_DHALF)),
                  _resident((n_sk, PEER_NKEYS, PEER_DHALF))],
        out_specs=[row(D_MODEL), row(D_MODEL // 2), s_spec],
        out_shape=[sds(D_MODEL), jax.ShapeDtypeStruct((B, T, D_MODEL // 2), I32), s_sds],
        compiler_params=_cparams("parallel", "parallel"),
        cost_estimate=_cost(2 * B * T * D_MODEL * (2 * D_MODEL + PEER_NKEYS), 40 * B * T * D_MODEL),
        name="outproj",
    )(attn, of, ob, g, x, g1, sh2, sc2, o_gain.reshape(1, HGRN_D), norm_ffn.reshape(1, D_MODEL),
      wo_bf16, wq_bf16, sk_bf16)


TOPK_TOKENS = SUBLANES * LANES
_CAND_PAIRS = [(a, b) for a in range(PEER_TOPK) for b in range(PEER_TOPK) if (a + 1) * (b + 1) <= PEER_TOPK]


def _first_argmax(values, ids, n_chains):
    per = -(-len(values) // n_chains)
    parts = []
    for lo in range(0, len(values), per):
        m, i = values[lo], ids[lo]
        if not isinstance(i, jax.Array):
            i = jnp.full(m.shape, i, F32)
        for v, vid in zip(values[lo + 1:lo + per], ids[lo + 1:lo + per]):
            c = v > m
            m = jnp.where(c, v, m)
            i = jnp.where(c, vid, i)
        parts.append((m, i))
    m, i = parts[0]
    for pm, pi in parts[1:]:
        c = pm > m
        m = jnp.where(c, pm, m)
        i = jnp.where(c, pi, i)
    return m, i


def _topk_kernel(s_ref, idx_ref, gate_ref, wk, tv, ti, cv, ci, bv):
    shape = (SUBLANES, LANES)
    none = jnp.full(shape, -1.0, F32)

    def head(h, carry):
        for p in range(2):
            hp = 2 * h + p
            for k in range(PEER_NKEYS):
                wk[k] = s_ref[hp, 0, pl.ds(k, SUBLANES, stride=PEER_NKEYS), :]

            def extract(r, prev, p=p):
                vals = []
                for k in range(PEER_NKEYS):
                    s = jnp.where(prev == float(k), NEG_INF, wk[k])
                    wk[k] = s
                    vals.append(s)
                m, i = _first_argmax(vals, [float(k) for k in range(PEER_NKEYS)], 4)
                tv[p, r] = m
                ti[p, r] = i
                return i

            lax.fori_loop(0, PEER_TOPK, extract, none)

        for c, (a, b) in enumerate(_CAND_PAIRS):
            cv[c] = tv[0, a] + tv[1, b]
            ci[c] = ti[0, a] * float(PEER_NKEYS) + ti[1, b]

        def pick(r, prev):
            vals, ids = [], []
            for c in range(len(_CAND_PAIRS)):
                cid = ci[c]
                s = jnp.where(cid == prev, NEG_INF, cv[c])
                cv[c] = s
                vals.append(s)
                ids.append(cid)
            m, i = _first_argmax(vals, ids, 2)
            bv[r] = m
            idx_ref[h * PEER_TOPK + r, 0] = i.astype(I32)
            return i

        lax.fori_loop(0, PEER_TOPK, pick, none)

        es = [jnp.exp(bv[r] - bv[0]) for r in range(PEER_TOPK)]
        tot = es[0]
        for e in es[1:]:
            tot = tot + e
        for r in range(PEER_TOPK):
            gate_ref[h * PEER_TOPK + r, 0] = es[r] / tot
        return carry

    lax.fori_loop(0, PEER_HEADS, head, 0)


def _topk(s):
    nt = s.shape[1]
    vreg = (SUBLANES, LANES)
    out_spec = pl.BlockSpec((PEER_SLOTS, 1) + vreg, lambda i: (0, i, 0, 0))
    return pl.pallas_call(
        _topk_kernel,
        grid=(nt,),
        in_specs=[pl.BlockSpec((s.shape[0], 1) + s.shape[2:], lambda i: (0, i, 0, 0))],
        out_specs=[out_spec, out_spec],
        out_shape=[jax.ShapeDtypeStruct((PEER_SLOTS, nt) + vreg, I32),
                   jax.ShapeDtypeStruct((PEER_SLOTS, nt) + vreg, F32)],
        scratch_shapes=[pltpu.VMEM((PEER_NKEYS,) + vreg, F32),
                        pltpu.VMEM((2, PEER_TOPK) + vreg, F32), pltpu.VMEM((2, PEER_TOPK) + vreg, F32),
                        pltpu.VMEM((len(_CAND_PAIRS),) + vreg, F32), pltpu.VMEM((len(_CAND_PAIRS),) + vreg, F32),
                        pltpu.VMEM((PEER_TOPK,) + vreg, F32)],
        compiler_params=_cparams("parallel"),
        cost_estimate=_cost(200 * s.size, 4 * s.size),
        name="topk",
    )(s)


SC_CORES = 2
SC_SUBCORES = 16
SC_LANES = 16
SC_WORKERS = SC_CORES * SC_SUBCORES
PEER_GROUP = 8
PEER_ITEMS = PEER_GROUP * PEER_HEADS
SC_UNROLL = 8
PACKED_WORDS = D_MODEL // 2
EXPERT_SLAB = (PACKED_WORDS // LANES, LANES)


_SC_PARAMS = pltpu.CompilerParams(needs_layout_passes=False)


def _sc_mesh():
    return plsc.VectorSubcoreMesh(core_axis_name="c", subcore_axis_name="s")


def _sc_worker():
    return lax.axis_index("s") * SC_CORES + lax.axis_index("c")


def _pack_kernel(t_ref, o_ref):
    word = _pack_bf16_pairs(t_ref[...])
    for s in range(EXPERT_SLAB[0]):
        o_ref[:, s, :] = word[:, s * LANES:(s + 1) * LANES]


def _pack_table(table):
    e = table.shape[0]
    tr = 256
    return pl.pallas_call(
        _pack_kernel,
        grid=(e // tr,),
        in_specs=[pl.BlockSpec((tr, D_MODEL), lambda i: (i, 0))],
        out_specs=pl.BlockSpec((tr,) + EXPERT_SLAB, lambda i: (i, 0, 0)),
        out_shape=jax.ShapeDtypeStruct((e,) + EXPERT_SLAB, I32),
        compiler_params=_cparams("parallel"),
        name="pack_table",
    )(table)


def _row_words(rows, k, first=0, count=PEER_TOPK):
    per = LANES // SC_LANES
    sub, cols = k // per, pl.ds((k % per) * SC_LANES, SC_LANES)
    return tuple(rows[r, sub, cols] for r in range(first, first + count))


def _bf16_lanes(word):
    return plsc.bitcast(word, BF16)


def _f32_halves(pairs):
    return plsc.unpack(pairs, format=plsc.PackFormat.INTERLEAVED)


def _half_cols(k):
    return pl.ds(k * SC_LANES, SC_LANES), pl.ds(PACKED_WORDS + k * SC_LANES, SC_LANES)


def _sc_item_pipeline(table_hbm, idx_v, rows0, rows1, sem0, sem1, compute):
    def gather(j, rows, sem):
        return pltpu.make_async_copy(table_hbm.at[idx_v.at[j]], rows, sem)

    gather(0, rows0, sem0).start()

    @pl.loop(0, PEER_ITEMS // 2)
    def _(i):
        j = 2 * i
        gather(j + 1, rows1, sem1).start()
        gather(j, rows0, sem0).wait()
        compute(j, rows0)

        @pl.when(i < PEER_ITEMS // 2 - 1)
        def _():
            gather(j + 2, rows0, sem0).start()

        gather(j + 1, rows1, sem1).wait()
        compute(j + 1, rows1)


def _peer_dot_kernel(x_hbm, idx_hbm, u_hbm, a_hbm, x_v, idx_v, rows0, rows1, a_v, sem0, sem1):
    tok_per_w = x_hbm.shape[0] // SC_WORKERS
    wid = _sc_worker()
    lane = lax.iota(I32, SC_LANES)

    def compute(j, rows):
        t = j // PEER_HEADS

        zero = jnp.zeros((SC_LANES,), F32)

        half = PEER_TOPK // 2
        steps = PACKED_WORDS // SC_LANES // 2

        def load(kp, first):
            out = ()
            for k in (2 * kp, 2 * kp + 1):
                out += (x_v[t, pl.ds(k * SC_LANES, SC_LANES)],) + _row_words(rows, k, first, half)
            return out

        def fma(acc, vals):
            x0, x1 = _bf16_lanes(vals[0]), _bf16_lanes(vals[1 + half])
            out = []
            for i in range(half):
                lo, hi = _f32_halves(_bf16_lanes(vals[1 + i]) * x0 + _bf16_lanes(vals[2 + half + i]) * x1)
                out.append(acc[i] + lo + hi)
            return tuple(out)

        @plsc.parallel_loop(0, steps, unroll=8, carry=((zero,) * half, (zero,) * half, load(0, half)))
        def state(kp, state):
            acc_a, acc_b, vals_b = state
            vals_a = load(kp, 0)
            acc_b = fma(acc_b, vals_b)
            vals_b = load(jnp.minimum(kp + 1, steps - 1), half)
            return fma(acc_a, vals_a), acc_b, vals_b

        acc = state[0] + state[1]
        out = zero
        for r in range(PEER_TOPK):
            out = jnp.where(lane == r, jnp.sum(acc[r]), out)
        a_v[j, :] = out

    @pl.loop(0, tok_per_w // PEER_GROUP)
    def _(g):
        tok0 = wid * tok_per_w + g * PEER_GROUP
        item0 = tok0 * PEER_HEADS
        pltpu.sync_copy(x_hbm.at[pl.ds(tok0, PEER_GROUP)], x_v)
        pltpu.sync_copy(idx_hbm.at[pl.ds(item0, PEER_ITEMS)], idx_v)
        _sc_item_pipeline(u_hbm, idx_v, rows0, rows1, sem0, sem1, compute)
        pltpu.sync_copy(a_v, a_hbm.at[pl.ds(item0, PEER_ITEMS)])


def _peer_sum_kernel(w_hbm, idx_hbm, v_hbm, o_hbm, w_v, idx_v, rows0, rows1, o_v, sem0, sem1):
    tok_per_w = o_hbm.shape[0] // SC_WORKERS
    wid = _sc_worker()
    zero = jnp.zeros((SC_LANES,), F32)

    def compute(j, rows):
        t = j // PEER_HEADS
        jv = jnp.full((SC_LANES,), j, I32)
        ws = []
        for r in range(PEER_TOPK):
            wr = plsc.load_gather(w_v, [jv, jnp.full((SC_LANES,), r, I32)])
            ws.append(plsc.pack(wr, wr, format=plsc.PackFormat.INTERLEAVED))

        def load(k):
            lo, hi = _half_cols(k)
            return (o_v[t, lo], o_v[t, hi]) + _row_words(rows, k)

        def finish(k, vals):
            acc_lo, acc_hi = vals[0], vals[1]
            for r in range(0, PEER_TOPK, 2):
                lo, hi = _f32_halves(_bf16_lanes(vals[2 + r]) * ws[r] + _bf16_lanes(vals[3 + r]) * ws[r + 1])
                acc_lo = acc_lo + lo
                acc_hi = acc_hi + hi
            lo, hi = _half_cols(k)
            o_v[t, lo] = acc_lo
            o_v[t, hi] = acc_hi

        @plsc.parallel_loop(1, PACKED_WORDS // SC_LANES, unroll=SC_UNROLL, carry=load(0))
        def vals(k, vals):
            nxt = load(k)
            finish(k - 1, vals)
            return nxt

        finish(PACKED_WORDS // SC_LANES - 1, vals)

    @pl.loop(0, tok_per_w // PEER_GROUP)
    def _(g):
        tok0 = wid * tok_per_w + g * PEER_GROUP
        item0 = tok0 * PEER_HEADS
        pltpu.sync_copy(w_hbm.at[pl.ds(item0, PEER_ITEMS)], w_v)
        pltpu.sync_copy(idx_hbm.at[pl.ds(item0, PEER_ITEMS)], idx_v)

        @pl.loop(0, PEER_GROUP)
        def _(t):
            @pl.loop(0, D_MODEL // SC_LANES)
            def _(k):
                o_v[t, pl.ds(k * SC_LANES, SC_LANES)] = zero

        _sc_item_pipeline(v_hbm, idx_v, rows0, rows1, sem0, sem1, compute)
        pltpu.sync_copy(o_v, o_hbm.at[pl.ds(tok0, PEER_GROUP)])


def _sc_scratch(first):
    return [first,
            pltpu.VMEM((PEER_ITEMS, PEER_TOPK), I32),
            pltpu.VMEM((PEER_TOPK,) + EXPERT_SLAB, I32),
            pltpu.VMEM((PEER_TOPK,) + EXPERT_SLAB, I32)]


def _peer_dot(h2, idx, u):
    n = h2.shape[0]
    assert n % (SC_WORKERS * PEER_GROUP) == 0
    call = pl.kernel(
        _peer_dot_kernel,
        out_type=jax.ShapeDtypeStruct((n * PEER_HEADS, PEER_TOPK), F32),
        mesh=_sc_mesh(),
        scratch_types=_sc_scratch(pltpu.VMEM((PEER_GROUP, PACKED_WORDS), I32))
        + [pltpu.VMEM((PEER_ITEMS, PEER_TOPK), F32), pltpu.SemaphoreType.DMA, pltpu.SemaphoreType.DMA],
        compiler_params=_SC_PARAMS,
        cost_estimate=_cost(2 * n * PEER_SLOTS * D_MODEL, 2 * n * PEER_SLOTS * D_MODEL),
    )
    return call(h2, idx, u)


def _peer_sum(w, idx, v, n):
    assert n % (SC_WORKERS * PEER_GROUP) == 0
    call = pl.kernel(
        _peer_sum_kernel,
        out_type=jax.ShapeDtypeStruct((n, D_MODEL), F32),
        mesh=_sc_mesh(),
        scratch_types=_sc_scratch(pltpu.VMEM((PEER_ITEMS, PEER_TOPK), F32))
        + [pltpu.VMEM((PEER_GROUP, D_MODEL), F32), pltpu.SemaphoreType.DMA, pltpu.SemaphoreType.DMA],
        compiler_params=_SC_PARAMS,
        cost_estimate=_cost(2 * n * PEER_SLOTS * D_MODEL, 2 * n * PEER_SLOTS * D_MODEL),
    )
    return call(w, idx, v)


def _gelu_gate_kernel(a_ref, g_ref, w_ref):
    a = a_ref[...]
    w_ref[...] = g_ref[...] * (0.5 * a * (1.0 + lax.erf(a * (2.0 ** -0.5))))


def _gelu_gate(a, gate):
    n = a.shape[0]
    tm = min(2048, n)
    spec = pl.BlockSpec((tm, PEER_SLOTS), lambda i: (i, 0))
    return pl.pallas_call(
        _gelu_gate_kernel, grid=(n // tm,), in_specs=[spec, spec], out_specs=spec,
        out_shape=jax.ShapeDtypeStruct(a.shape, F32), compiler_params=_cparams("parallel"), name="gelu_gate",
    )(a, gate)


def _residual_kernel(x_ref, g_ref, p_ref, o_ref):
    o_ref[0] = x_ref[0] + g_ref[0] * p_ref[0]


def _residual(x1, g2, peer):
    B, T, _ = x1.shape
    tm = min(2 * ROW_TILE, T)
    row = pl.BlockSpec((1, tm, D_MODEL), lambda b, t: (b, t, 0))
    vec = pl.BlockSpec((1, 1, D_MODEL), lambda b, t: (b, 0, 0))
    return pl.pallas_call(
        _residual_kernel, grid=(B, T // tm), in_specs=[row, vec, row], out_specs=row,
        out_shape=jax.ShapeDtypeStruct(x1.shape, F32), compiler_params=_cparams("parallel", "parallel"),
        name="residual",
    )(x1, g2, peer)


def kernel(x, c, ctx, c_ctx, w_ada, b_ada, norm_mix, norm_ffn, w_in, q_norm, k_norm, attn_sink, hgrn_lb_logits,
           hgrn_norm, w_out, peer_w_q, peer_sub_keys, peer_u, peer_v):
    assert w_ada.shape[0] == 1, "single-layer block"
    B, T, D = x.shape
    L = ctx.shape[1]
    n = B * T

    cvecs = jnp.zeros((SUBLANES, D), F32).at[:B].set(c).at[B].set(c_ctx)
    mod = _ada(cvecs, w_ada[0], b_ada[0])
    part = lambda rows, i: rows[:, None, i * D:(i + 1) * D]
    mod_x = mod[:B]
    mod_c = jnp.broadcast_to(mod[B:B + 1], (B, 6 * D))
    sh1, sc1, g1, sh2, sc2, g2 = (part(mod_x, i) for i in range(6))

    lbs = jnp.cumsum(jax.nn.softmax(hgrn_lb_logits.astype(F32), axis=1), axis=1)
    lb_f, lb_b = lbs[0, 0].reshape(1, HGRN_WIDTH), lbs[1, 0].reshape(1, HGRN_WIDTH)

    w_in_b = w_in[0].astype(BF16)
    rope = _rope_tables(T)
    kc, vc, ffc, fbc, ic, _, qhc, _ = _inproj(ctx, part(mod_c, 0), part(mod_c, 1), norm_mix[0], w_in_b,
                                              _identity_rope(L), q_norm[0], k_norm[0])
    s0 = jnp.zeros((B, HGRN_HEADS, HGRN_D, HGRN_D), F32)
    _, _, sfc, sbc = _hgrn(ffc, fbc, ic, qhc, lb_f, lb_b, s0, s0)

    sk = peer_sub_keys[0].reshape(2 * PEER_HEADS, PEER_NKEYS, PEER_DHALF).astype(BF16)
    w_out_b, w_q_b = w_out[0].astype(BF16), peer_w_q[0].astype(BF16)
    u_packed, v_packed = _pack_table(peer_u[0]), _pack_table(peer_v[0])

    kx, vx, ffx, fbx, ix, qx, qhx, gx = _inproj(x, sh1, sc1, norm_mix[0], w_in_b, rope, q_norm[0], k_norm[0])
    attn = _attention(qx, kx, vx, kc, vc, attn_sink[0])
    of, ob, _, _ = _hgrn(ffx, fbx, ix, qhx, lb_f, lb_b, sfc, sbc)
    x1, h2p, s = _outproj(attn, of, ob, gx, x, g1, sh2, sc2, hgrn_norm[0], norm_ffn[0], w_out_b, w_q_b, sk)
    idx_t, gate_t = _topk(s)
    idx16 = idx_t.reshape(PEER_SLOTS, n).T.reshape(n * PEER_HEADS, PEER_TOPK)
    gate = gate_t.reshape(PEER_SLOTS, n).T
    a = _peer_dot(h2p.reshape(n, PACKED_WORDS), idx16, u_packed)
    w = _gelu_gate(a.reshape(n, PEER_SLOTS), gate)
    peer = _peer_sum(w.reshape(n * PEER_HEADS, PEER_TOPK), idx16, v_packed, n)
    return _residual(x1, g2, peer.reshape(B, T, D))
```

```python
import functools

import jax
import jax.numpy as jnp
from jax import lax
from jax.experimental import pallas as pl
from jax.experimental.pallas import tpu as pltpu
from jax.experimental.pallas import tpu_sc as plsc

F32 = jnp.float32
BF16 = jnp.bfloat16
I32 = jnp.int32

D_MODEL = 2048
GRID_W = 64
EPS = 1e-6
HEAD_DIM = 128
ATTN_HEADS = 8
ATTN_KV_HEADS = 2
ATTN_GROUP = ATTN_HEADS // ATTN_KV_HEADS
BAND_BLOCK = 128
ROPE_THETA = 10000.0
HGRN_HEADS = 8
HGRN_D = 128
ATTN_WIDTH = ATTN_HEADS * HEAD_DIM
KV_WIDTH = ATTN_KV_HEADS * HEAD_DIM
HGRN_WIDTH = HGRN_HEADS * HGRN_D
COL_K = 0
COL_V = COL_K + KV_WIDTH
COL_FF = COL_V + KV_WIDTH
COL_FB = COL_FF + HGRN_WIDTH
COL_I = COL_FB + HGRN_WIDTH
COL_Q = COL_I + HGRN_WIDTH
COL_QH = COL_Q + ATTN_WIDTH
COL_G = COL_QH + HGRN_WIDTH
N_IN_COLS = COL_G + HGRN_WIDTH
PEER_HEADS = 8
PEER_NKEYS = 128
PEER_DHALF = 128
PEER_TOPK = 16
PEER_SLOTS = PEER_HEADS * PEER_TOPK

LANES = 128
SUBLANES = 8
VMEM_LIMIT_BYTES = 56 * 1024 * 1024

ROW_TILE = 256
HGRN_CHUNK = 128
HGRN_STEP = 512
NEG_INF = float("-inf")


def _cparams(*sem):
    return pltpu.CompilerParams(dimension_semantics=sem, vmem_limit_bytes=VMEM_LIMIT_BYTES)


def _cost(flops, bytes_accessed, transcendentals=0):
    return pl.CostEstimate(flops=int(flops), bytes_accessed=int(bytes_accessed), transcendentals=int(transcendentals))


def _resident(shape):
    nd = len(shape)
    return pl.BlockSpec(shape, lambda *_: (0,) * nd, pipeline_mode=pl.Buffered(1))


def _ada_kernel(c_ref, w_ref, b_ref, o_ref):
    c = c_ref[...]
    s = (c * jax.nn.sigmoid(c)).astype(BF16)
    o_ref[...] = jnp.dot(s, w_ref[...].astype(BF16), preferred_element_type=F32) + b_ref[...]


def _ada(cvecs, w, b):
    n = w.shape[1]
    tn = 1024
    return pl.pallas_call(
        _ada_kernel,
        grid=(n // tn,),
        in_specs=[pl.BlockSpec((SUBLANES, D_MODEL), lambda j: (0, 0)),
                  pl.BlockSpec((D_MODEL, tn), lambda j: (0, j)),
                  pl.BlockSpec((1, tn), lambda j: (0, j))],
        out_specs=pl.BlockSpec((SUBLANES, tn), lambda j: (0, j)),
        out_shape=jax.ShapeDtypeStruct((SUBLANES, n), F32),
        compiler_params=_cparams("arbitrary"),
        name="ada",
    )(cvecs, w, b.reshape(1, n))


def _pack_bf16_pairs(x):
    w = x.shape[1] // 2
    lo = pltpu.bitcast(x[:, :w].astype(BF16).astype(F32), jnp.uint32)
    hi = pltpu.bitcast(x[:, w:].astype(BF16).astype(F32), jnp.uint32)
    return pltpu.bitcast((hi & jnp.uint32(0xFFFF0000)) | (lo >> 16), I32)


def _rms(x, gain):
    return x * lax.rsqrt(jnp.mean(x * x, axis=-1, keepdims=True) + EPS) * gain


def _rope(x, cos, sin_a, sin_b):
    q = HEAD_DIM // 4
    return x * cos + pltpu.roll(x, HEAD_DIM - q, 1) * sin_a + pltpu.roll(x, q, 1) * sin_b


def _inproj_kernel(x_ref, sh_ref, sc_ref, gain_ref, w_ref, cos_ref, sa_ref, sb_ref, qg_ref, kg_ref,
                   k_ref, v_ref, ff_ref, fb_ref, i_ref, q_ref, qh_ref, g_ref):
    x = x_ref[0]
    h = _rms(x, gain_ref[...]) * (1.0 + sc_ref[0]) + sh_ref[0]
    hb = h.astype(BF16)

    def seg(lo, width):
        return jnp.dot(hb, w_ref[:, lo:lo + width], preferred_element_type=F32)

    cos, sa, sb = cos_ref[...], sa_ref[...], sb_ref[...]

    def normed_heads(p, gain, n_heads, out_ref):
        for hd in range(n_heads):
            ph = p[:, hd * HEAD_DIM:(hd + 1) * HEAD_DIM]
            out_ref[0, :, hd * HEAD_DIM:(hd + 1) * HEAD_DIM] = _rope(_rms(ph, gain), cos, sa, sb).astype(BF16)

    normed_heads(seg(COL_K, KV_WIDTH), kg_ref[...], ATTN_KV_HEADS, k_ref)
    v_ref[0] = seg(COL_V, KV_WIDTH).astype(BF16)
    ff_ref[0] = seg(COL_FF, HGRN_WIDTH)
    fb_ref[0] = seg(COL_FB, HGRN_WIDTH)
    i_ref[0] = seg(COL_I, HGRN_WIDTH).astype(BF16)
    normed_heads(seg(COL_Q, ATTN_WIDTH), qg_ref[...], ATTN_HEADS, q_ref)
    qh_ref[0] = seg(COL_QH, HGRN_WIDTH).astype(BF16)
    g_ref[0] = seg(COL_G, HGRN_WIDTH).astype(BF16)


def _inproj(x, shift, scale, gain, w_bf16, rope, q_gain, k_gain, after=()):
    B, T, _ = x.shape
    n_in = 10

    def body(*refs):
        _inproj_kernel(*refs[:n_in], *refs[n_in + len(after):])
    tm = min(ROW_TILE, T)
    row = lambda w: pl.BlockSpec((1, tm, w), lambda b, t: (b, t, 0))
    vec = pl.BlockSpec((1, 1, D_MODEL), lambda b, t: (b, 0, 0))
    tab = pl.BlockSpec((tm, HEAD_DIM), lambda b, t: (t, 0))
    out_w = [(KV_WIDTH, BF16), (KV_WIDTH, BF16), (HGRN_WIDTH, F32), (HGRN_WIDTH, F32), (HGRN_WIDTH, BF16),
             (ATTN_WIDTH, BF16), (HGRN_WIDTH, BF16), (HGRN_WIDTH, BF16)]
    return pl.pallas_call(
        body,
        grid=(B, T // tm),
        in_specs=[row(D_MODEL), vec, vec, _resident((1, D_MODEL)), _resident((D_MODEL, N_IN_COLS)),
                  tab, tab, tab, _resident((1, HEAD_DIM)), _resident((1, HEAD_DIM))]
        + [pl.BlockSpec(memory_space=pl.ANY)] * len(after),
        out_specs=[row(w) for w, _ in out_w],
        out_shape=[jax.ShapeDtypeStruct((B, T, w), dt) for w, dt in out_w],
        compiler_params=_cparams("parallel", "parallel"),
        cost_estimate=_cost(2 * B * T * D_MODEL * N_IN_COLS, B * T * (D_MODEL * 4 + N_IN_COLS * 3)),
        name="inproj",
    )(x, shift, scale, gain.reshape(1, D_MODEL), w_bf16, *rope,
      q_gain.reshape(1, HEAD_DIM), k_gain.reshape(1, HEAD_DIM), *after)


def _rope_tables(T):
    rows = T // GRID_W
    row_pos = jnp.repeat(jnp.arange(rows, dtype=F32), GRID_W)
    col_pos = jnp.tile(jnp.arange(GRID_W, dtype=F32), rows)
    half = HEAD_DIM // 2
    inv_freq = jnp.power(ROPE_THETA, -jnp.arange(0, half, 2, dtype=F32) / half)
    ang_r = row_pos[:, None] * inv_freq
    ang_c = col_pos[:, None] * inv_freq
    cr, sr, cc, sc = jnp.cos(ang_r), jnp.sin(ang_r), jnp.cos(ang_c), jnp.sin(ang_c)
    z = jnp.zeros_like(sr)
    return (jnp.concatenate([cr, cr, cc, cc], -1),
            jnp.concatenate([-sr, z, -sc, z], -1),
            jnp.concatenate([z, sr, z, sc], -1))


def _identity_rope(T):
    return (jnp.ones((T, HEAD_DIM), F32), jnp.zeros((T, HEAD_DIM), F32), jnp.zeros((T, HEAD_DIM), F32))


_NT = (((1,), (1,)), ((), ()))
_TN = (((0,), (0,)), ((), ()))


def _attn_kernel(sink_ref, q_ref, kp_ref, kc_ref, kn_ref, vp_ref, vc_ref, vn_ref, kx_ref, vx_ref, o_ref):
    n = pl.program_id(1)
    nb = pl.num_programs(1)
    blk = BAND_BLOCK
    rows = ATTN_GROUP * blk
    qi = lax.broadcasted_iota(I32, (rows, blk), 0) & (blk - 1)
    kj = lax.broadcasted_iota(I32, (rows, blk), 1)
    prev_ok = kj >= qi + jnp.where(n > 0, 0, blk)
    next_ok = kj <= qi - jnp.where(n < nb - 1, 0, blk)
    scale = HEAD_DIM ** -0.5
    for h in range(ATTN_KV_HEADS):
        cs = slice(h * HEAD_DIM, (h + 1) * HEAD_DIM)
        heads = [h * ATTN_GROUP + g for g in range(ATTN_GROUP)]
        q4 = jnp.concatenate([q_ref[0, :, hd * HEAD_DIM:(hd + 1) * HEAD_DIM] for hd in heads], axis=0)

        def scores(k_ref):
            return lax.dot_general(q4, k_ref[0, :, cs], _NT, preferred_element_type=F32) * scale

        s_p = jnp.where(prev_ok, scores(kp_ref), NEG_INF)
        s_c = scores(kc_ref)
        s_n = jnp.where(next_ok, scores(kn_ref), NEG_INF)
        s_x = scores(kx_ref)
        sink = jnp.concatenate([jnp.full((blk, 1), sink_ref[hd], F32) for hd in heads], axis=0)
        m = jnp.maximum(jnp.maximum(jnp.max(s_p, -1, keepdims=True), jnp.max(s_c, -1, keepdims=True)),
                        jnp.maximum(jnp.max(s_n, -1, keepdims=True), jnp.max(s_x, -1, keepdims=True)))
        m = jnp.maximum(m, sink)
        p_p, p_c, p_n, p_x = jnp.exp(s_p - m), jnp.exp(s_c - m), jnp.exp(s_n - m), jnp.exp(s_x - m)
        denom = (jnp.sum(p_p, -1, keepdims=True) + jnp.sum(p_c, -1, keepdims=True)
                 + jnp.sum(p_n, -1, keepdims=True) + jnp.sum(p_x, -1, keepdims=True) + jnp.exp(sink - m))

        def pv(p, v_ref):
            return jnp.dot(p.astype(BF16), v_ref[0, :, cs], preferred_element_type=F32)

        o = (pv(p_p, vp_ref) + pv(p_c, vc_ref) + pv(p_n, vn_ref) + pv(p_x, vx_ref)) / denom
        for g, hd in enumerate(heads):
            o_ref[0, :, hd * HEAD_DIM:(hd + 1) * HEAD_DIM] = o[g * blk:(g + 1) * blk].astype(BF16)


def _attention(q, k, v, k_ctx, v_ctx, sink):
    B, T, _ = q.shape
    L = k_ctx.shape[1]
    nb = T // BAND_BLOCK
    kv = lambda f: pl.BlockSpec((1, BAND_BLOCK, KV_WIDTH), lambda b, n: (b, f(n), 0))
    prev, cur, nxt = (lambda n: jnp.maximum(n - 1, 0)), (lambda n: n), (lambda n: jnp.minimum(n + 1, nb - 1))
    ctx = pl.BlockSpec((1, L, KV_WIDTH), lambda b, n: (b, 0, 0))
    return pl.pallas_call(
        _attn_kernel,
        grid=(B, nb),
        in_specs=[pl.BlockSpec(memory_space=pltpu.SMEM),
                  pl.BlockSpec((1, BAND_BLOCK, ATTN_WIDTH), lambda b, n: (b, n, 0)),
                  kv(prev), kv(cur), kv(nxt), kv(prev), kv(cur), kv(nxt), ctx, ctx],
        out_specs=pl.BlockSpec((1, BAND_BLOCK, ATTN_WIDTH), lambda b, n: (b, n, 0)),
        out_shape=jax.ShapeDtypeStruct((B, T, ATTN_WIDTH), BF16),
        compiler_params=_cparams("parallel", "parallel"),
        cost_estimate=_cost(4 * B * T * ATTN_WIDTH * (3 * BAND_BLOCK + L), 8 * B * T * ATTN_WIDTH,
                            B * T * ATTN_HEADS * (3 * BAND_BLOCK + L)),
        name="attn",
    )(sink, q, k, k, k, v, v, v, k_ctx, v_ctx)


_DIAG = SUBLANES
_LEVELS = (64, 32, 16, 8)


def _hgrn_chunk(logit, v, q, lb, st_ref, reverse):
    C = HGRN_CHUNK
    f = lb + (1.0 - lb) * jax.nn.sigmoid(logit)
    lf = jnp.log(f)
    kk = 1.0 - f
    qf = q.astype(F32)
    r = lax.broadcasted_iota(I32, (C, C), 0)
    c = lax.broadcasted_iota(I32, (C, C), 1)
    incl = (r <= c) if reverse else (r >= c)
    a = jnp.dot(incl.astype(F32), lf, precision=lax.Precision.HIGHEST, preferred_element_type=F32)
    a_end = a[0:1] if reverse else a[C - 1:C]

    st = st_ref[...]
    inter = lax.dot_general((qf * jnp.exp(a)).astype(BF16), st.astype(BF16), _NT, preferred_element_type=F32)
    kd = (kk * jnp.exp(a_end - a)).astype(BF16)
    st_ref[...] = st * jnp.exp(a_end) + lax.dot_general(v, kd, _TN, preferred_element_type=F32)

    later = (r < c) if reverse else (r > c)
    att = jnp.zeros((C, C), F32)
    for m in _LEVELS:
        a3 = a.reshape(C // (2 * m), 2 * m, HGRN_D)
        edge = a3[:, m:m + 1, :] if reverse else a3[:, m - 1:m, :]
        e = jnp.exp(-jnp.abs(a3 - edge)).reshape(C, HGRN_D)
        p = lax.dot_general((qf * e).astype(BF16), (kk * e).astype(BF16), _NT, preferred_element_type=F32)
        pair = ((r ^ c) >> (m.bit_length() - 1)) == 1
        att = jnp.where(pair & later, p, att)

    lane = lax.broadcasted_iota(I32, (_DIAG, C), 1)
    sub = lax.broadcasted_iota(I32, (_DIAG, C), 0)
    blocks = []
    for j in range(C // _DIAG):
        rows = slice(j * _DIAG, (j + 1) * _DIAG)
        a_j, q_j, k_j = a[rows], qf[rows], kk[rows]
        blk = jnp.zeros((_DIAG, C), F32)
        for s in range(_DIAG):
            e = jnp.exp(jnp.minimum(a_j - a_j[s:s + 1], 0.0))
            col = jnp.sum(q_j * e * k_j[s:s + 1], axis=-1, keepdims=True)
            ok = (sub <= s) if reverse else (sub >= s)
            blk = jnp.where((lane == j * _DIAG + s) & ok, col, blk)
        blocks.append(blk)
    att = att + jnp.concatenate(blocks, axis=0)
    return inter + jnp.dot(att.astype(BF16), v, preferred_element_type=F32)


def _hgrn_kernel(ff_ref, vf_ref, qf_ref, fb_ref, vb_ref, qb_ref, lbf_ref, lbb_ref, s0f_ref, s0b_ref,
                 of_ref, ob_ref, sf_ref, sb_ref, stf, stb):
    step = pl.program_id(2)
    nsub = ff_ref.shape[1] // HGRN_CHUNK

    @pl.when(step == 0)
    def _():
        stf[...] = s0f_ref[0, 0]
        stb[...] = s0b_ref[0, 0]

    def body(j, carry):
        fo = pl.multiple_of(j * HGRN_CHUNK, HGRN_CHUNK)
        rows = pl.ds(fo, HGRN_CHUNK)
        of_ref[0, rows, :] = _hgrn_chunk(ff_ref[0, rows, :], vf_ref[0, rows, :], qf_ref[0, rows, :],
                                         lbf_ref[...], stf, False)
        bo = pl.multiple_of((nsub - 1 - j) * HGRN_CHUNK, HGRN_CHUNK)
        rows = pl.ds(bo, HGRN_CHUNK)
        ob_ref[0, rows, :] = _hgrn_chunk(fb_ref[0, rows, :], vb_ref[0, rows, :], qb_ref[0, rows, :],
                                         lbb_ref[...], stb, True)
        return carry

    lax.fori_loop(0, nsub, body, 0)

    @pl.when(step == pl.num_programs(2) - 1)
    def _():
        sf_ref[0, 0] = stf[...]
        sb_ref[0, 0] = stb[...]


def _hgrn(ff, fb, val, q, lb_f, lb_b, s0f, s0b):
    B, T, _ = ff.shape
    ts = min(HGRN_STEP, T)
    ns = T // ts
    fwd = pl.BlockSpec((1, ts, HGRN_D), lambda b, h, s: (b, s, h))
    bwd = pl.BlockSpec((1, ts, HGRN_D), lambda b, h, s: (b, ns - 1 - s, h))
    lbs = pl.BlockSpec((1, HGRN_D), lambda b, h, s: (0, h))
    st = pl.BlockSpec((1, 1, HGRN_D, HGRN_D), lambda b, h, s: (b, h, 0, 0))
    o_sds = jax.ShapeDtypeStruct((B, T, HGRN_WIDTH), F32)
    s_sds = jax.ShapeDtypeStruct((B, HGRN_HEADS, HGRN_D, HGRN_D), F32)
    return pl.pallas_call(
        _hgrn_kernel,
        grid=(B, HGRN_HEADS, ns),
        in_specs=[fwd, fwd, fwd, bwd, bwd, bwd, lbs, lbs, st, st],
        out_specs=[fwd, bwd, st, st],
        out_shape=[o_sds, o_sds, s_sds, s_sds],
        scratch_shapes=[pltpu.VMEM((HGRN_D, HGRN_D), F32), pltpu.VMEM((HGRN_D, HGRN_D), F32)],
        compiler_params=_cparams("parallel", "parallel", "arbitrary"),
        cost_estimate=_cost(32 * B * T * HGRN_WIDTH * HGRN_CHUNK, 24 * B * T * HGRN_WIDTH, 8 * B * T * HGRN_WIDTH),
        name="hgrn",
    )(ff, val, q, fb, val, q, lb_f, lb_b, s0f, s0b)


def _outproj_kernel(attn_ref, of_ref, ob_ref, g_ref, x_ref, g1_ref, sh2_ref, sc2_ref, og_ref, nf_ref,
                    wo_ref, wq_ref, sk_ref, x1_ref, h2_ref, s_ref):
    o = of_ref[0] + ob_ref[0]
    og = og_ref[...]
    parts = []
    for hd in range(HGRN_HEADS):
        cs = slice(hd * HGRN_D, (hd + 1) * HGRN_D)
        gh = g_ref[0, :, cs].astype(F32)
        parts.append((_rms(o[:, cs], og) * (gh * jax.nn.sigmoid(gh))).astype(BF16))
    hg = jnp.concatenate(parts, axis=-1)
    mix = (jnp.dot(attn_ref[0], wo_ref[:ATTN_WIDTH], preferred_element_type=F32)
           + jnp.dot(hg, wo_ref[ATTN_WIDTH:], preferred_element_type=F32))
    x1 = x_ref[0] + g1_ref[0] * mix
    x1_ref[0] = x1
    h2 = _rms(x1, nf_ref[...]) * (1.0 + sc2_ref[0]) + sh2_ref[0]
    h2_ref[0] = _pack_bf16_pairs(h2)
    pq = jnp.dot(h2.astype(BF16), wq_ref[...], preferred_element_type=F32)
    for hp in range(2 * PEER_HEADS):
        cs = slice(hp * PEER_DHALF, (hp + 1) * PEER_DHALF)
        for jb in range(pq.shape[0] // LANES):
            rows = slice(jb * LANES, (jb + 1) * LANES)
            s_ref[hp, 0, jb * PEER_NKEYS:(jb + 1) * PEER_NKEYS, :] = lax.dot_general(
                sk_ref[hp], pq[rows, cs].astype(BF16), _NT, preferred_element_type=F32)


def _outproj(attn, of, ob, g, x, g1, sh2, sc2, o_gain, norm_ffn, wo_bf16, wq_bf16, sk_bf16):
    B, T, _ = x.shape
    tm = min(ROW_TILE, T)
    nt = T // tm
    per = TOPK_TOKENS // tm
    assert (B * T) % TOPK_TOKENS == 0 and TOPK_TOKENS % tm == 0 and tm % LANES == 0
    row = lambda w: pl.BlockSpec((1, tm, w), lambda b, t: (b, t, 0))
    vec = pl.BlockSpec((1, 1, D_MODEL), lambda b, t: (b, 0, 0))
    n_sk = 2 * PEER_HEADS
    sds = lambda w: jax.ShapeDtypeStruct((B, T, w), F32)
    s_rows = tm // LANES * PEER_NKEYS
    s_spec = pl.BlockSpec((n_sk, 1, s_rows, LANES), lambda b, t: (0, (b * nt + t) // per, (b * nt + t) % per, 0))
    s_sds = jax.ShapeDtypeStruct((n_sk, B * T // TOPK_TOKENS, TOPK_TOKENS // LANES * PEER_NKEYS, LANES), F32)
    return pl.pallas_call(
        _outproj_kernel,
        grid=(B, T // tm),
        in_specs=[row(ATTN_WIDTH), row(HGRN_WIDTH), row(HGRN_WIDTH), row(HGRN_WIDTH), row(D_MODEL),
                  vec, vec, vec, _resident((1, HGRN_D)), _resident((1, D_MODEL)),
                  _resident((ATTN_WIDTH + HGRN_WIDTH, D_MODEL)), _resident((D_MODEL, n_sk * PEER_DHALF)),
                  _resident((n_sk, PEER_NKEYS, PEER_DHALF))],
        out_specs=[row(D_MODEL), row(D_MODEL // 2), s_spec],
        out_shape=[sds(D_MODEL), jax.ShapeDtypeStruct((B, T, D_MODEL // 2), I32), s_sds],
        compiler_params=_cparams("parallel", "parallel"),
        cost_estimate=_cost(2 * B * T * D_MODEL * (2 * D_MODEL + PEER_NKEYS), 40 * B * T * D_MODEL),
        name="outproj",
    )(attn, of, ob, g, x, g1, sh2, sc2, o_gain.reshape(1, HGRN_D), norm_ffn.reshape(1, D_MODEL),
      wo_bf16, wq_bf16, sk_bf16)


TOPK_TOKENS = SUBLANES * LANES
_CAND_PAIRS = [(a, b) for a in range(PEER_TOPK) for b in range(PEER_TOPK) if (a + 1) * (b + 1) <= PEER_TOPK]


def _first_argmax(values, ids, n_chains):
    per = -(-len(values) // n_chains)
    parts = []
    for lo in range(0, len(values), per):
        m, i = values[lo], ids[lo]
        if not isinstance(i, jax.Array):
            i = jnp.full(m.shape, i, F32)
        for v, vid in zip(values[lo + 1:lo + per], ids[lo + 1:lo + per]):
            c = v > m
            m = jnp.where(c, v, m)
            i = jnp.where(c, vid, i)
        parts.append((m, i))
    m, i = parts[0]
    for pm, pi in parts[1:]:
        c = pm > m
        m = jnp.where(c, pm, m)
        i = jnp.where(c, pi, i)
    return m, i


def _topk_kernel(s_ref, idx_ref, gate_ref, wk, tv, ti, cv, ci, bv):
    shape = (SUBLANES, LANES)
    none = jnp.full(shape, -1.0, F32)

    def head(h, carry):
        for p in range(2):
            hp = 2 * h + p
            for k in range(PEER_NKEYS):
                wk[k] = s_ref[hp, 0, pl.ds(k, SUBLANES, stride=PEER_NKEYS), :]

            def extract(r, prev, p=p):
                vals = []
                for k in range(PEER_NKEYS):
                    s = jnp.where(prev == float(k), NEG_INF, wk[k])
                    wk[k] = s
                    vals.append(s)
                m, i = _first_argmax(vals, [float(k) for k in range(PEER_NKEYS)], 4)
                tv[p, r] = m
                ti[p, r] = i
                return i

            lax.fori_loop(0, PEER_TOPK, extract, none)

        for c, (a, b) in enumerate(_CAND_PAIRS):
            cv[c] = tv[0, a] + tv[1, b]
            ci[c] = ti[0, a] * float(PEER_NKEYS) + ti[1, b]

        def pick(r, prev):
            vals, ids = [], []
            for c in range(len(_CAND_PAIRS)):
                cid = ci[c]
                s = jnp.where(cid == prev, NEG_INF, cv[c])
                cv[c] = s
                vals.append(s)
                ids.append(cid)
            m, i = _first_argmax(vals, ids, 2)
            bv[r] = m
            idx_ref[h * PEER_TOPK + r, 0] = i.astype(I32)
            return i

        lax.fori_loop(0, PEER_TOPK, pick, none)

        es = [jnp.exp(bv[r] - bv[0]) for r in range(PEER_TOPK)]
        tot = es[0]
        for e in es[1:]:
            tot = tot + e
        for r in range(PEER_TOPK):
            gate_ref[h * PEER_TOPK + r, 0] = es[r] / tot
        return carry

    lax.fori_loop(0, PEER_HEADS, head, 0)


def _topk(s):
    nt = s.shape[1]
    vreg = (SUBLANES, LANES)
    out_spec = pl.BlockSpec((PEER_SLOTS, 1) + vreg, lambda i: (0, i, 0, 0))
    return pl.pallas_call(
        _topk_kernel,
        grid=(nt,),
        in_specs=[pl.BlockSpec((s.shape[0], 1) + s.shape[2:], lambda i: (0, i, 0, 0))],
        out_specs=[out_spec, out_spec],
        out_shape=[jax.ShapeDtypeStruct((PEER_SLOTS, nt) + vreg, I32),
                   jax.ShapeDtypeStruct((PEER_SLOTS, nt) + vreg, F32)],
        scratch_shapes=[pltpu.VMEM((PEER_NKEYS,) + vreg, F32),
                        pltpu.VMEM((2, PEER_TOPK) + vreg, F32), pltpu.VMEM((2, PEER_TOPK) + vreg, F32),
                        pltpu.VMEM((len(_CAND_PAIRS),) + vreg, F32), pltpu.VMEM((len(_CAND_PAIRS),) + vreg, F32),
                        pltpu.VMEM((PEER_TOPK,) + vreg, F32)],
        compiler_params=_cparams("parallel"),
        cost_estimate=_cost(200 * s.size, 4 * s.size),
        name="topk",
    )(s)


SC_CORES = 2
SC_SUBCORES = 16
SC_LANES = 16
SC_WORKERS = SC_CORES * SC_SUBCORES
PEER_GROUP = 16
SC_BUFFERS = 4
PEER_ITEMS = PEER_GROUP * PEER_HEADS
SC_UNROLL = 8
PACKED_WORDS = D_MODEL // 2
EXPERT_SLAB = (PACKED_WORDS // LANES, LANES)


_SC_PARAMS = pltpu.CompilerParams(needs_layout_passes=False)


def _sc_mesh():
    return plsc.VectorSubcoreMesh(core_axis_name="c", subcore_axis_name="s")


def _sc_worker():
    return lax.axis_index("s") * SC_CORES + lax.axis_index("c")


def _pack_kernel(t_ref, o_ref):
    word = _pack_bf16_pairs(t_ref[...])
    for s in range(EXPERT_SLAB[0]):
        o_ref[:, s, :] = word[:, s * LANES:(s + 1) * LANES]


def _pack_table(table):
    e = table.shape[0]
    tr = 256
    return pl.pallas_call(
        _pack_kernel,
        grid=(e // tr,),
        in_specs=[pl.BlockSpec((tr, D_MODEL), lambda i: (i, 0))],
        out_specs=pl.BlockSpec((tr,) + EXPERT_SLAB, lambda i: (i, 0, 0)),
        out_shape=jax.ShapeDtypeStruct((e,) + EXPERT_SLAB, I32),
        compiler_params=_cparams("parallel"),
        name="pack_table",
    )(table)


def _row_words(rows, k, first=0, count=PEER_TOPK):
    per = LANES // SC_LANES
    sub, cols = k // per, pl.ds((k % per) * SC_LANES, SC_LANES)
    return tuple(rows[r, sub, cols] for r in range(first, first + count))


def _bf16_lanes(word):
    return plsc.bitcast(word, BF16)


def _f32_halves(pairs):
    return plsc.unpack(pairs, format=plsc.PackFormat.INTERLEAVED)


def _half_cols(k):
    return pl.ds(k * SC_LANES, SC_LANES), pl.ds(PACKED_WORDS + k * SC_LANES, SC_LANES)


def _sc_item_pipeline(table_hbm, idx_v, bufs, sems, compute):
    nb = len(bufs)

    def gather(j, b):
        return pltpu.make_async_copy(table_hbm.at[idx_v.at[j]], bufs[b], sems[b])

    for b in range(nb - 1):
        gather(b, b).start()

    @pl.loop(0, PEER_ITEMS // nb)
    def _(i):
        for b in range(nb):
            j = nb * i + b
            ahead = j + nb - 1

            @pl.when(ahead < PEER_ITEMS)
            def _():
                gather(ahead, (b + nb - 1) % nb).start()

            gather(j, b).wait()
            compute(j, bufs[b])


def _peer_dot_kernel(x_hbm, idx_hbm, u_hbm, a_hbm, x_v, idx_v, a_v, *ring):
    tok_per_w = x_hbm.shape[0] // SC_WORKERS
    wid = _sc_worker()
    lane = lax.iota(I32, SC_LANES)

    def compute(j, rows):
        t = j // PEER_HEADS

        zero = jnp.zeros((SC_LANES,), F32)

        half = PEER_TOPK // 2
        steps = PACKED_WORDS // SC_LANES // 2

        def load(kp, first):
            out = ()
            for k in (2 * kp, 2 * kp + 1):
                out += (x_v[t, pl.ds(k * SC_LANES, SC_LANES)],) + _row_words(rows, k, first, half)
            return out

        def fma(acc, vals):
            x0, x1 = _bf16_lanes(vals[0]), _bf16_lanes(vals[1 + half])
            out = []
            for i in range(half):
                lo, hi = _f32_halves(_bf16_lanes(vals[1 + i]) * x0 + _bf16_lanes(vals[2 + half + i]) * x1)
                out.append(acc[i] + lo + hi)
            return tuple(out)

        @plsc.parallel_loop(0, steps, unroll=8, carry=((zero,) * half, (zero,) * half, load(0, half)))
        def state(kp, state):
            acc_a, acc_b, vals_b = state
            vals_a = load(kp, 0)
            acc_b = fma(acc_b, vals_b)
            vals_b = load(jnp.minimum(kp + 1, steps - 1), half)
            return fma(acc_a, vals_a), acc_b, vals_b

        acc = state[0] + state[1]
        out = zero
        for r in range(PEER_TOPK):
            out = jnp.where(lane == r, jnp.sum(acc[r]), out)
        a_v[j, :] = out

    @pl.loop(0, tok_per_w // PEER_GROUP)
    def _(g):
        tok0 = wid * tok_per_w + g * PEER_GROUP
        item0 = tok0 * PEER_HEADS
        pltpu.sync_copy(x_hbm.at[pl.ds(tok0, PEER_GROUP)], x_v)
        pltpu.sync_copy(idx_hbm.at[pl.ds(item0, PEER_ITEMS)], idx_v)
        _sc_item_pipeline(u_hbm, idx_v, ring[:SC_BUFFERS], ring[SC_BUFFERS:], compute)
        pltpu.sync_copy(a_v, a_hbm.at[pl.ds(item0, PEER_ITEMS)])


def _peer_sum_kernel(w_hbm, idx_hbm, v_hbm, o_hbm, w_v, idx_v, o_v, *ring):
    tok_per_w = o_hbm.shape[0] // SC_WORKERS
    wid = _sc_worker()
    zero = jnp.zeros((SC_LANES,), F32)

    def compute(j, rows):
        t = j // PEER_HEADS
        jv = jnp.full((SC_LANES,), j, I32)
        ws = []
        for r in range(PEER_TOPK):
            wr = plsc.load_gather(w_v, [jv, jnp.full((SC_LANES,), r, I32)])
            ws.append(plsc.pack(wr, wr, format=plsc.PackFormat.INTERLEAVED))

        def load(k):
            lo, hi = _half_cols(k)
            return (o_v[t, lo], o_v[t, hi]) + _row_words(rows, k)

        def finish(k, vals):
            acc_lo, acc_hi = vals[0], vals[1]
            for r in range(0, PEER_TOPK, 2):
                lo, hi = _f32_halves(_bf16_lanes(vals[2 + r]) * ws[r] + _bf16_lanes(vals[3 + r]) * ws[r + 1])
                acc_lo = acc_lo + lo
                acc_hi = acc_hi + hi
            lo, hi = _half_cols(k)
            o_v[t, lo] = acc_lo
            o_v[t, hi] = acc_hi

        @plsc.parallel_loop(1, PACKED_WORDS // SC_LANES, unroll=SC_UNROLL, carry=load(0))
        def vals(k, vals):
            nxt = load(k)
            finish(k - 1, vals)
            return nxt

        finish(PACKED_WORDS // SC_LANES - 1, vals)

    @pl.loop(0, tok_per_w // PEER_GROUP)
    def _(g):
        tok0 = wid * tok_per_w + g * PEER_GROUP
        item0 = tok0 * PEER_HEADS
        pltpu.sync_copy(w_hbm.at[pl.ds(item0, PEER_ITEMS)], w_v)
        pltpu.sync_copy(idx_hbm.at[pl.ds(item0, PEER_ITEMS)], idx_v)

        @pl.loop(0, PEER_GROUP)
        def _(t):
            @pl.loop(0, D_MODEL // SC_LANES)
            def _(k):
                o_v[t, pl.ds(k * SC_LANES, SC_LANES)] = zero

        _sc_item_pipeline(v_hbm, idx_v, ring[:SC_BUFFERS], ring[SC_BUFFERS:], compute)
        pltpu.sync_copy(o_v, o_hbm.at[pl.ds(tok0, PEER_GROUP)])


def _sc_scratch(first, last):
    return ([first, pltpu.VMEM((PEER_ITEMS, PEER_TOPK), I32), last]
            + [pltpu.VMEM((PEER_TOPK,) + EXPERT_SLAB, I32)] * SC_BUFFERS
            + [pltpu.SemaphoreType.DMA] * SC_BUFFERS)


def _peer_dot(h2, idx, u):
    n = h2.shape[0]
    assert n % (SC_WORKERS * PEER_GROUP) == 0
    call = pl.kernel(
        _peer_dot_kernel,
        out_type=jax.ShapeDtypeStruct((n * PEER_HEADS, PEER_TOPK), F32),
        mesh=_sc_mesh(),
        scratch_types=_sc_scratch(pltpu.VMEM((PEER_GROUP, PACKED_WORDS), I32),
                                  pltpu.VMEM((PEER_ITEMS, PEER_TOPK), F32)),
        compiler_params=_SC_PARAMS,
        cost_estimate=_cost(2 * n * PEER_SLOTS * D_MODEL, 2 * n * PEER_SLOTS * D_MODEL),
    )
    return call(h2, idx, u)


def _peer_sum(w, idx, v, n):
    assert n % (SC_WORKERS * PEER_GROUP) == 0
    call = pl.kernel(
        _peer_sum_kernel,
        out_type=jax.ShapeDtypeStruct((n, D_MODEL), F32),
        mesh=_sc_mesh(),
        scratch_types=_sc_scratch(pltpu.VMEM((PEER_ITEMS, PEER_TOPK), F32),
                                  pltpu.VMEM((PEER_GROUP, D_MODEL), F32)),
        compiler_params=_SC_PARAMS,
        cost_estimate=_cost(2 * n * PEER_SLOTS * D_MODEL, 2 * n * PEER_SLOTS * D_MODEL),
    )
    return call(w, idx, v)


def _gelu_gate_kernel(a_ref, g_ref, w_ref):
    a = a_ref[...]
    w_ref[...] = g_ref[...] * (0.5 * a * (1.0 + lax.erf(a * (2.0 ** -0.5))))


def _gelu_gate(a, gate):
    n = a.shape[0]
    tm = min(2048, n)
    spec = pl.BlockSpec((tm, PEER_SLOTS), lambda i: (i, 0))
    return pl.pallas_call(
        _gelu_gate_kernel, grid=(n // tm,), in_specs=[spec, spec], out_specs=spec,
        out_shape=jax.ShapeDtypeStruct(a.shape, F32), compiler_params=_cparams("parallel"), name="gelu_gate",
    )(a, gate)


def _residual_kernel(x_ref, g_ref, p_ref, o_ref):
    o_ref[0] = x_ref[0] + g_ref[0] * p_ref[0]


def _residual(x1, g2, peer):
    B, T, _ = x1.shape
    tm = min(2 * ROW_TILE, T)
    row = pl.BlockSpec((1, tm, D_MODEL), lambda b, t: (b, t, 0))
    vec = pl.BlockSpec((1, 1, D_MODEL), lambda b, t: (b, 0, 0))
    return pl.pallas_call(
        _residual_kernel, grid=(B, T // tm), in_specs=[row, vec, row], out_specs=row,
        out_shape=jax.ShapeDtypeStruct(x1.shape, F32), compiler_params=_cparams("parallel", "parallel"),
        name="residual",
    )(x1, g2, peer)


def kernel(x, c, ctx, c_ctx, w_ada, b_ada, norm_mix, norm_ffn, w_in, q_norm, k_norm, attn_sink, hgrn_lb_logits,
           hgrn_norm, w_out, peer_w_q, peer_sub_keys, peer_u, peer_v):
    assert w_ada.shape[0] == 1, "single-layer block"
    B, T, D = x.shape
    L = ctx.shape[1]
    n = B * T

    cvecs = jnp.zeros((SUBLANES, D), F32).at[:B].set(c).at[B].set(c_ctx)
    mod = _ada(cvecs, w_ada[0], b_ada[0])
    part = lambda rows, i: rows[:, None, i * D:(i + 1) * D]
    mod_x = mod[:B]
    mod_c = jnp.broadcast_to(mod[B:B + 1], (B, 6 * D))
    sh1, sc1, g1, sh2, sc2, g2 = (part(mod_x, i) for i in range(6))

    lbs = jnp.cumsum(jax.nn.softmax(hgrn_lb_logits.astype(F32), axis=1), axis=1)
    lb_f, lb_b = lbs[0, 0].reshape(1, HGRN_WIDTH), lbs[1, 0].reshape(1, HGRN_WIDTH)

    w_in_b = w_in[0].astype(BF16)
    rope = _rope_tables(T)
    kc, vc, ffc, fbc, ic, _, qhc, _ = _inproj(ctx, part(mod_c, 0), part(mod_c, 1), norm_mix[0], w_in_b,
                                              _identity_rope(L), q_norm[0], k_norm[0])
    s0 = jnp.zeros((B, HGRN_HEADS, HGRN_D, HGRN_D), F32)
    _, _, sfc, sbc = _hgrn(ffc, fbc, ic, qhc, lb_f, lb_b, s0, s0)

    sk = peer_sub_keys[0].reshape(2 * PEER_HEADS, PEER_NKEYS, PEER_DHALF).astype(BF16)
    w_out_b, w_q_b = w_out[0].astype(BF16), peer_w_q[0].astype(BF16)
    u_packed, v_packed = _pack_table(peer_u[0]), _pack_table(peer_v[0])

    kx, vx, ffx, fbx, ix, qx, qhx, gx = _inproj(x, sh1, sc1, norm_mix[0], w_in_b, rope, q_norm[0], k_norm[0])
    attn = _attention(qx, kx, vx, kc, vc, attn_sink[0])
    of, ob, _, _ = _hgrn(ffx, fbx, ix, qhx, lb_f, lb_b, sfc, sbc)
    x1, h2p, s = _outproj(attn, of, ob, gx, x, g1, sh2, sc2, hgrn_norm[0], norm_ffn[0], w_out_b, w_q_b, sk)
    idx_t, gate_t = _topk(s)
    idx16 = idx_t.reshape(PEER_SLOTS, n).T.reshape(n * PEER_HEADS, PEER_TOPK)
    gate = gate_t.reshape(PEER_SLOTS, n).T
    a = _peer_dot(h2p.reshape(n, PACKED_WORDS), idx16, u_packed)
    w = _gelu_gate(a.reshape(n, PEER_SLOTS), gate)
    peer = _peer_sum(w.reshape(n * PEER_HEADS, PEER_TOPK), idx16, v_packed, n)
    return _residual(x1, g2, peer.reshape(B, T, D))
```

```python
import functools

import jax
import jax.numpy as jnp
from jax import lax
from jax.experimental import pallas as pl
from jax.experimental.pallas import tpu as pltpu
from jax.experimental.pallas import tpu_sc as plsc

F32 = jnp.float32
BF16 = jnp.bfloat16
I32 = jnp.int32

D_MODEL = 2048
GRID_W = 64
EPS = 1e-6
HEAD_DIM = 128
ATTN_HEADS = 8
ATTN_KV_HEADS = 2
ATTN_GROUP = ATTN_HEADS // ATTN_KV_HEADS
BAND_BLOCK = 128
ROPE_THETA = 10000.0
HGRN_HEADS = 8
HGRN_D = 128
ATTN_WIDTH = ATTN_HEADS * HEAD_DIM
KV_WIDTH = ATTN_KV_HEADS * HEAD_DIM
HGRN_WIDTH = HGRN_HEADS * HGRN_D
COL_K = 0
COL_V = COL_K + KV_WIDTH
COL_FF = COL_V + KV_WIDTH
COL_FB = COL_FF + HGRN_WIDTH
COL_I = COL_FB + HGRN_WIDTH
COL_Q = COL_I + HGRN_WIDTH
COL_QH = COL_Q + ATTN_WIDTH
COL_G = COL_QH + HGRN_WIDTH
N_IN_COLS = COL_G + HGRN_WIDTH
PEER_HEADS = 8
PEER_NKEYS = 128
PEER_DHALF = 128
PEER_TOPK = 16
PEER_SLOTS = PEER_HEADS * PEER_TOPK

LANES = 128
SUBLANES = 8
VMEM_LIMIT_BYTES = 56 * 1024 * 1024

ROW_TILE = 256
HGRN_CHUNK = 128
HGRN_STEP = 512
HGRN_HEADS_PER_STEP = 4
NEG_INF = float("-inf")


def _cparams(*sem):
    return pltpu.CompilerParams(dimension_semantics=sem, vmem_limit_bytes=VMEM_LIMIT_BYTES)


def _cost(flops, bytes_accessed, transcendentals=0):
    return pl.CostEstimate(flops=int(flops), bytes_accessed=int(bytes_accessed), transcendentals=int(transcendentals))


def _resident(shape):
    nd = len(shape)
    return pl.BlockSpec(shape, lambda *_: (0,) * nd, pipeline_mode=pl.Buffered(1))


def _ada_kernel(c_ref, w_ref, b_ref, o_ref):
    c = c_ref[...]
    s = (c * jax.nn.sigmoid(c)).astype(BF16)
    o_ref[...] = jnp.dot(s, w_ref[...].astype(BF16), preferred_element_type=F32) + b_ref[...]


def _ada(cvecs, w, b):
    n = w.shape[1]
    tn = 1024
    return pl.pallas_call(
        _ada_kernel,
        grid=(n // tn,),
        in_specs=[pl.BlockSpec((SUBLANES, D_MODEL), lambda j: (0, 0)),
                  pl.BlockSpec((D_MODEL, tn), lambda j: (0, j)),
                  pl.BlockSpec((1, tn), lambda j: (0, j))],
        out_specs=pl.BlockSpec((SUBLANES, tn), lambda j: (0, j)),
        out_shape=jax.ShapeDtypeStruct((SUBLANES, n), F32),
        compiler_params=_cparams("arbitrary"),
        name="ada",
    )(cvecs, w, b.reshape(1, n))


def _pack_bf16_pairs(x):
    w = x.shape[1] // 2
    lo = pltpu.bitcast(x[:, :w].astype(BF16).astype(F32), jnp.uint32)
    hi = pltpu.bitcast(x[:, w:].astype(BF16).astype(F32), jnp.uint32)
    return pltpu.bitcast((hi & jnp.uint32(0xFFFF0000)) | (lo >> 16), I32)


def _rms(x, gain):
    return x * lax.rsqrt(jnp.mean(x * x, axis=-1, keepdims=True) + EPS) * gain


def _rope(x, cos, sin_a, sin_b):
    q = HEAD_DIM // 4
    return x * cos + pltpu.roll(x, HEAD_DIM - q, 1) * sin_a + pltpu.roll(x, q, 1) * sin_b


def _inproj_kernel(x_ref, sh_ref, sc_ref, gain_ref, w_ref, cos_ref, sa_ref, sb_ref, qg_ref, kg_ref,
                   k_ref, v_ref, ff_ref, fb_ref, i_ref, q_ref, qh_ref, g_ref):
    x = x_ref[0]
    h = _rms(x, gain_ref[...]) * (1.0 + sc_ref[0]) + sh_ref[0]
    hb = h.astype(BF16)

    def seg(lo, width):
        return jnp.dot(hb, w_ref[:, lo:lo + width], preferred_element_type=F32)

    cos, sa, sb = cos_ref[...], sa_ref[...], sb_ref[...]

    def normed_heads(p, gain, n_heads, out_ref):
        for hd in range(n_heads):
            ph = p[:, hd * HEAD_DIM:(hd + 1) * HEAD_DIM]
            out_ref[0, :, hd * HEAD_DIM:(hd + 1) * HEAD_DIM] = _rope(_rms(ph, gain), cos, sa, sb).astype(BF16)

    normed_heads(seg(COL_K, KV_WIDTH), kg_ref[...], ATTN_KV_HEADS, k_ref)
    v_ref[0] = seg(COL_V, KV_WIDTH).astype(BF16)
    ff_ref[0] = seg(COL_FF, HGRN_WIDTH)
    fb_ref[0] = seg(COL_FB, HGRN_WIDTH)
    i_ref[0] = seg(COL_I, HGRN_WIDTH).astype(BF16)
    normed_heads(seg(COL_Q, ATTN_WIDTH), qg_ref[...], ATTN_HEADS, q_ref)
    qh_ref[0] = seg(COL_QH, HGRN_WIDTH).astype(BF16)
    g_ref[0] = seg(COL_G, HGRN_WIDTH).astype(BF16)


def _inproj(x, shift, scale, gain, w_bf16, rope, q_gain, k_gain, after=()):
    B, T, _ = x.shape
    n_in = 10

    def body(*refs):
        _inproj_kernel(*refs[:n_in], *refs[n_in + len(after):])
    tm = min(ROW_TILE, T)
    row = lambda w: pl.BlockSpec((1, tm, w), lambda b, t: (b, t, 0))
    vec = pl.BlockSpec((1, 1, D_MODEL), lambda b, t: (b, 0, 0))
    tab = pl.BlockSpec((tm, HEAD_DIM), lambda b, t: (t, 0))
    out_w = [(KV_WIDTH, BF16), (KV_WIDTH, BF16), (HGRN_WIDTH, F32), (HGRN_WIDTH, F32), (HGRN_WIDTH, BF16),
             (ATTN_WIDTH, BF16), (HGRN_WIDTH, BF16), (HGRN_WIDTH, BF16)]
    return pl.pallas_call(
        body,
        grid=(B, T // tm),
        in_specs=[row(D_MODEL), vec, vec, _resident((1, D_MODEL)), _resident((D_MODEL, N_IN_COLS)),
                  tab, tab, tab, _resident((1, HEAD_DIM)), _resident((1, HEAD_DIM))]
        + [pl.BlockSpec(memory_space=pl.ANY)] * len(after),
        out_specs=[row(w) for w, _ in out_w],
        out_shape=[jax.ShapeDtypeStruct((B, T, w), dt) for w, dt in out_w],
        compiler_params=_cparams("parallel", "parallel"),
        cost_estimate=_cost(2 * B * T * D_MODEL * N_IN_COLS, B * T * (D_MODEL * 4 + N_IN_COLS * 3)),
        name="inproj",
    )(x, shift, scale, gain.reshape(1, D_MODEL), w_bf16, *rope,
      q_gain.reshape(1, HEAD_DIM), k_gain.reshape(1, HEAD_DIM), *after)


def _rope_tables(T):
    rows = T // GRID_W
    row_pos = jnp.repeat(jnp.arange(rows, dtype=F32), GRID_W)
    col_pos = jnp.tile(jnp.arange(GRID_W, dtype=F32), rows)
    half = HEAD_DIM // 2
    inv_freq = jnp.power(ROPE_THETA, -jnp.arange(0, half, 2, dtype=F32) / half)
    ang_r = row_pos[:, None] * inv_freq
    ang_c = col_pos[:, None] * inv_freq
    cr, sr, cc, sc = jnp.cos(ang_r), jnp.sin(ang_r), jnp.cos(ang_c), jnp.sin(ang_c)
    z = jnp.zeros_like(sr)
    return (jnp.concatenate([cr, cr, cc, cc], -1),
            jnp.concatenate([-sr, z, -sc, z], -1),
            jnp.concatenate([z, sr, z, sc], -1))


def _identity_rope(T):
    return (jnp.ones((T, HEAD_DIM), F32), jnp.zeros((T, HEAD_DIM), F32), jnp.zeros((T, HEAD_DIM), F32))


_NT = (((1,), (1,)), ((), ()))
_TN = (((0,), (0,)), ((), ()))


def _attn_kernel(sink_ref, q_ref, kp_ref, kc_ref, kn_ref, vp_ref, vc_ref, vn_ref, kx_ref, vx_ref, o_ref):
    n = pl.program_id(1)
    nb = pl.num_programs(1)
    blk = BAND_BLOCK
    rows = ATTN_GROUP * blk
    qi = lax.broadcasted_iota(I32, (rows, blk), 0) & (blk - 1)
    kj = lax.broadcasted_iota(I32, (rows, blk), 1)
    prev_ok = kj >= qi + jnp.where(n > 0, 0, blk)
    next_ok = kj <= qi - jnp.where(n < nb - 1, 0, blk)
    scale = HEAD_DIM ** -0.5
    for h in range(ATTN_KV_HEADS):
        cs = slice(h * HEAD_DIM, (h + 1) * HEAD_DIM)
        heads = [h * ATTN_GROUP + g for g in range(ATTN_GROUP)]
        q4 = jnp.concatenate([q_ref[0, :, hd * HEAD_DIM:(hd + 1) * HEAD_DIM] for hd in heads], axis=0)

        def scores(k_ref):
            return lax.dot_general(q4, k_ref[0, :, cs], _NT, preferred_element_type=F32) * scale

        s_p = jnp.where(prev_ok, scores(kp_ref), NEG_INF)
        s_c = scores(kc_ref)
        s_n = jnp.where(next_ok, scores(kn_ref), NEG_INF)
        s_x = scores(kx_ref)
        sink = jnp.concatenate([jnp.full((blk, 1), sink_ref[hd], F32) for hd in heads], axis=0)
        m = jnp.maximum(jnp.maximum(jnp.max(s_p, -1, keepdims=True), jnp.max(s_c, -1, keepdims=True)),
                        jnp.maximum(jnp.max(s_n, -1, keepdims=True), jnp.max(s_x, -1, keepdims=True)))
        m = jnp.maximum(m, sink)
        p_p, p_c, p_n, p_x = jnp.exp(s_p - m), jnp.exp(s_c - m), jnp.exp(s_n - m), jnp.exp(s_x - m)
        denom = (jnp.sum(p_p, -1, keepdims=True) + jnp.sum(p_c, -1, keepdims=True)
                 + jnp.sum(p_n, -1, keepdims=True) + jnp.sum(p_x, -1, keepdims=True) + jnp.exp(sink - m))

        def pv(p, v_ref):
            return jnp.dot(p.astype(BF16), v_ref[0, :, cs], preferred_element_type=F32)

        o = (pv(p_p, vp_ref) + pv(p_c, vc_ref) + pv(p_n, vn_ref) + pv(p_x, vx_ref)) / denom
        for g, hd in enumerate(heads):
            o_ref[0, :, hd * HEAD_DIM:(hd + 1) * HEAD_DIM] = o[g * blk:(g + 1) * blk].astype(BF16)


def _attention(q, k, v, k_ctx, v_ctx, sink):
    B, T, _ = q.shape
    L = k_ctx.shape[1]
    nb = T // BAND_BLOCK
    kv = lambda f: pl.BlockSpec((1, BAND_BLOCK, KV_WIDTH), lambda b, n: (b, f(n), 0))
    prev, cur, nxt = (lambda n: jnp.maximum(n - 1, 0)), (lambda n: n), (lambda n: jnp.minimum(n + 1, nb - 1))
    ctx = pl.BlockSpec((1, L, KV_WIDTH), lambda b, n: (b, 0, 0))
    return pl.pallas_call(
        _attn_kernel,
        grid=(B, nb),
        in_specs=[pl.BlockSpec(memory_space=pltpu.SMEM),
                  pl.BlockSpec((1, BAND_BLOCK, ATTN_WIDTH), lambda b, n: (b, n, 0)),
                  kv(prev), kv(cur), kv(nxt), kv(prev), kv(cur), kv(nxt), ctx, ctx],
        out_specs=pl.BlockSpec((1, BAND_BLOCK, ATTN_WIDTH), lambda b, n: (b, n, 0)),
        out_shape=jax.ShapeDtypeStruct((B, T, ATTN_WIDTH), BF16),
        compiler_params=_cparams("parallel", "parallel"),
        cost_estimate=_cost(4 * B * T * ATTN_WIDTH * (3 * BAND_BLOCK + L), 8 * B * T * ATTN_WIDTH,
                            B * T * ATTN_HEADS * (3 * BAND_BLOCK + L)),
        name="attn",
    )(sink, q, k, k, k, v, v, v, k_ctx, v_ctx)


_LOG2E = 1.4426950408889634
_DIAG = SUBLANES
_LEVELS = (64, 32, 16, 8)


def _hgrn_chunk(logit, v, q, lb, st_ref, reverse):
    C = HGRN_CHUNK
    f = lb + (1.0 - lb) * jax.nn.sigmoid(logit)
    lf = jnp.log(f)
    kk = 1.0 - f
    qf = q.astype(F32)
    r = lax.broadcasted_iota(I32, (C, C), 0)
    c = lax.broadcasted_iota(I32, (C, C), 1)
    incl = (r <= c) if reverse else (r >= c)
    a = jnp.dot(incl.astype(F32), lf, precision=lax.Precision.HIGHEST, preferred_element_type=F32) * _LOG2E
    a_end = a[0:1] if reverse else a[C - 1:C]

    st = st_ref[...]
    inter = lax.dot_general((qf * jnp.exp2(a)).astype(BF16), st.astype(BF16), _NT, preferred_element_type=F32)
    kd = (kk * jnp.exp2(a_end - a)).astype(BF16)
    st_ref[...] = st * jnp.exp2(a_end) + lax.dot_general(v, kd, _TN, preferred_element_type=F32)

    later = (r < c) if reverse else (r > c)
    att = jnp.zeros((C, C), F32)
    for m in _LEVELS:
        a3 = a.reshape(C // (2 * m), 2 * m, HGRN_D)
        edge = a3[:, m:m + 1, :] if reverse else a3[:, m - 1:m, :]
        e = jnp.exp2(-jnp.abs(a3 - edge)).reshape(C, HGRN_D)
        p = lax.dot_general((qf * e).astype(BF16), (kk * e).astype(BF16), _NT, preferred_element_type=F32)
        pair = ((r ^ c) >> (m.bit_length() - 1)) == 1
        att = jnp.where(pair & later, p, att)

    lane = lax.broadcasted_iota(I32, (_DIAG, C), 1)
    sub = lax.broadcasted_iota(I32, (_DIAG, C), 0)
    keep = [(lane == s) & ((sub <= s) if reverse else (sub >= s)) for s in range(_DIAG)]
    blocks = []
    for j in range(C // _DIAG):
        rows = slice(j * _DIAG, (j + 1) * _DIAG)
        a_j, q_j, k_j = a[rows], qf[rows], kk[rows]
        blk = jnp.zeros((_DIAG, C), F32)
        for s in range(_DIAG):
            e = jnp.exp2(a_j - a_j[s:s + 1])
            col = jnp.sum(q_j * e * k_j[s:s + 1], axis=-1, keepdims=True)
            blk = jnp.where(keep[s], col, blk)
        blocks.append(pltpu.roll(blk, j * _DIAG, 1) if j else blk)
    att = att + jnp.concatenate(blocks, axis=0)
    return inter + jnp.dot(att.astype(BF16), v, preferred_element_type=F32)


def _hgrn_kernel(ff_ref, vf_ref, qf_ref, fb_ref, vb_ref, qb_ref, lbf_ref, lbb_ref, s0f_ref, s0b_ref,
                 of_ref, ob_ref, sf_ref, sb_ref, stf, stb):
    step = pl.program_id(2)
    nsub = ff_ref.shape[1] // HGRN_CHUNK

    @pl.when(step == 0)
    def _():
        stf[...] = s0f_ref[0]
        stb[...] = s0b_ref[0]

    def body(j, carry):
        fo = pl.multiple_of(j * HGRN_CHUNK, HGRN_CHUNK)
        bo = pl.multiple_of((nsub - 1 - j) * HGRN_CHUNK, HGRN_CHUNK)
        for hh in range(HGRN_HEADS_PER_STEP):
            cs = slice(hh * HGRN_D, (hh + 1) * HGRN_D)
            rows = pl.ds(fo, HGRN_CHUNK)
            of_ref[0, rows, cs] = _hgrn_chunk(ff_ref[0, rows, cs], vf_ref[0, rows, cs], qf_ref[0, rows, cs],
                                              lbf_ref[:, cs], stf.at[hh], False)
            rows = pl.ds(bo, HGRN_CHUNK)
            ob_ref[0, rows, cs] = _hgrn_chunk(fb_ref[0, rows, cs], vb_ref[0, rows, cs], qb_ref[0, rows, cs],
                                              lbb_ref[:, cs], stb.at[hh], True)
        return carry

    lax.fori_loop(0, nsub, body, 0)

    @pl.when(step == pl.num_programs(2) - 1)
    def _():
        sf_ref[0] = stf[...]
        sb_ref[0] = stb[...]


def _hgrn(ff, fb, val, q, lb_f, lb_b, s0f, s0b):
    B, T, _ = ff.shape
    ts = min(HGRN_STEP, T)
    ns = T // ts
    hp = HGRN_HEADS_PER_STEP
    fwd = pl.BlockSpec((1, ts, hp * HGRN_D), lambda b, h, s: (b, s, h))
    bwd = pl.BlockSpec((1, ts, hp * HGRN_D), lambda b, h, s: (b, ns - 1 - s, h))
    lbs = pl.BlockSpec((1, hp * HGRN_D), lambda b, h, s: (0, h))
    st = pl.BlockSpec((1, hp, HGRN_D, HGRN_D), lambda b, h, s: (b, h, 0, 0))
    o_sds = jax.ShapeDtypeStruct((B, T, HGRN_WIDTH), F32)
    s_sds = jax.ShapeDtypeStruct((B, HGRN_HEADS, HGRN_D, HGRN_D), F32)
    return pl.pallas_call(
        _hgrn_kernel,
        grid=(B, HGRN_HEADS // hp, ns),
        in_specs=[fwd, fwd, fwd, bwd, bwd, bwd, lbs, lbs, st, st],
        out_specs=[fwd, bwd, st, st],
        out_shape=[o_sds, o_sds, s_sds, s_sds],
        scratch_shapes=[pltpu.VMEM((hp, HGRN_D, HGRN_D), F32), pltpu.VMEM((hp, HGRN_D, HGRN_D), F32)],
        compiler_params=_cparams("parallel", "parallel", "arbitrary"),
        cost_estimate=_cost(32 * B * T * HGRN_WIDTH * HGRN_CHUNK, 24 * B * T * HGRN_WIDTH, 8 * B * T * HGRN_WIDTH),
        name="hgrn",
    )(ff, val, q, fb, val, q, lb_f, lb_b, s0f, s0b)


def _outproj_kernel(attn_ref, of_ref, ob_ref, g_ref, x_ref, g1_ref, sh2_ref, sc2_ref, og_ref, nf_ref,
                    wo_ref, wq_ref, sk_ref, x1_ref, h2_ref, s_ref):
    o = of_ref[0] + ob_ref[0]
    og = og_ref[...]
    parts = []
    for hd in range(HGRN_HEADS):
        cs = slice(hd * HGRN_D, (hd + 1) * HGRN_D)
        gh = g_ref[0, :, cs].astype(F32)
        parts.append((_rms(o[:, cs], og) * (gh * jax.nn.sigmoid(gh))).astype(BF16))
    hg = jnp.concatenate(parts, axis=-1)
    mix = (jnp.dot(attn_ref[0], wo_ref[:ATTN_WIDTH], preferred_element_type=F32)
           + jnp.dot(hg, wo_ref[ATTN_WIDTH:], preferred_element_type=F32))
    x1 = x_ref[0] + g1_ref[0] * mix
    x1_ref[0] = x1
    h2 = _rms(x1, nf_ref[...]) * (1.0 + sc2_ref[0]) + sh2_ref[0]
    h2_ref[0] = _pack_bf16_pairs(h2)
    pq = jnp.dot(h2.astype(BF16), wq_ref[...], preferred_element_type=F32)
    for hp in range(2 * PEER_HEADS):
        cs = slice(hp * PEER_DHALF, (hp + 1) * PEER_DHALF)
        for jb in range(pq.shape[0] // LANES):
            rows = slice(jb * LANES, (jb + 1) * LANES)
            s_ref[hp, 0, jb * PEER_NKEYS:(jb + 1) * PEER_NKEYS, :] = lax.dot_general(
                sk_ref[hp], pq[rows, cs].astype(BF16), _NT, preferred_element_type=F32)


def _outproj(attn, of, ob, g, x, g1, sh2, sc2, o_gain, norm_ffn, wo_bf16, wq_bf16, sk_bf16):
    B, T, _ = x.shape
    tm = min(ROW_TILE, T)
    nt = T // tm
    per = TOPK_TOKENS // tm
    assert (B * T) % TOPK_TOKENS == 0 and TOPK_TOKENS % tm == 0 and tm % LANES == 0
    row = lambda w: pl.BlockSpec((1, tm, w), lambda b, t: (b, t, 0))
    vec = pl.BlockSpec((1, 1, D_MODEL), lambda b, t: (b, 0, 0))
    n_sk = 2 * PEER_HEADS
    sds = lambda w: jax.ShapeDtypeStruct((B, T, w), F32)
    s_rows = tm // LANES * PEER_NKEYS
    s_spec = pl.BlockSpec((n_sk, 1, s_rows, LANES), lambda b, t: (0, (b * nt + t) // per, (b * nt + t) % per, 0))
    s_sds = jax.ShapeDtypeStruct((n_sk, B * T // TOPK_TOKENS, TOPK_TOKENS // LANES * PEER_NKEYS, LANES), F32)
    return pl.pallas_call(
        _outproj_kernel,
        grid=(B, T // tm),
        in_specs=[row(ATTN_WIDTH), row(HGRN_WIDTH), row(HGRN_WIDTH), row(HGRN_WIDTH), row(D_MODEL),
                  vec, vec, vec, _resident((1, HGRN_D)), _resident((1, D_MODEL)),
                  _resident((ATTN_WIDTH + HGRN_WIDTH, D_MODEL)), _resident((D_MODEL, n_sk * PEER_DHALF)),
                  _resident((n_sk, PEER_NKEYS, PEER_DHALF))],
        out_specs=[row(D_MODEL), row(D_MODEL // 2), s_spec],
        out_shape=[sds(D_MODEL), jax.ShapeDtypeStruct((B, T, D_MODEL // 2), I32), s_sds],
        compiler_params=_cparams("parallel", "parallel"),
        cost_estimate=_cost(2 * B * T * D_MODEL * (2 * D_MODEL + PEER_NKEYS), 40 * B * T * D_MODEL),
        name="outproj",
    )(attn, of, ob, g, x, g1, sh2, sc2, o_gain.reshape(1, HGRN_D), norm_ffn.reshape(1, D_MODEL),
      wo_bf16, wq_bf16, sk_bf16)


TOPK_TOKENS = SUBLANES * LANES
_CAND_PAIRS = [(a, b) for a in range(PEER_TOPK) for b in range(PEER_TOPK) if (a + 1) * (b + 1) <= PEER_TOPK]


def _first_argmax(values, ids, n_chains):
    per = -(-len(values) // n_chains)
    parts = []
    for lo in range(0, len(values), per):
        m, i = values[lo], ids[lo]
        if not isinstance(i, jax.Array):
            i = jnp.full(m.shape, i, F32)
        for v, vid in zip(values[lo + 1:lo + per], ids[lo + 1:lo + per]):
            c = v > m
            m = jnp.where(c, v, m)
            i = jnp.where(c, vid, i)
        parts.append((m, i))
    m, i = parts[0]
    for pm, pi in parts[1:]:
        c = pm > m
        m = jnp.where(c, pm, m)
        i = jnp.where(c, pi, i)
    return m, i


def _topk_kernel(s_ref, idx_ref, gate_ref, wk, tv, ti, cv, ci, bv):
    shape = (SUBLANES, LANES)
    none = jnp.full(shape, -1.0, F32)

    def head(h, carry):
        for p in range(2):
            hp = 2 * h + p
            for k in range(PEER_NKEYS):
                wk[k] = s_ref[hp, 0, pl.ds(k, SUBLANES, stride=PEER_NKEYS), :]

            def extract(r, prev, p=p):
                vals = []
                for k in range(PEER_NKEYS):
                    s = jnp.where(prev == float(k), NEG_INF, wk[k])
                    wk[k] = s
                    vals.append(s)
                m, i = _first_argmax(vals, [float(k) for k in range(PEER_NKEYS)], 4)
                tv[p, r] = m
                ti[p, r] = i
                return i

            lax.fori_loop(0, PEER_TOPK, extract, none)

        for c, (a, b) in enumerate(_CAND_PAIRS):
            cv[c] = tv[0, a] + tv[1, b]
            ci[c] = ti[0, a] * float(PEER_NKEYS) + ti[1, b]

        def pick(r, prev):
            vals, ids = [], []
            for c in range(len(_CAND_PAIRS)):
                cid = ci[c]
                s = jnp.where(cid == prev, NEG_INF, cv[c])
                cv[c] = s
                vals.append(s)
                ids.append(cid)
            m, i = _first_argmax(vals, ids, 2)
            bv[r] = m
            idx_ref[h * PEER_TOPK + r, 0] = i.astype(I32)
            return i

        lax.fori_loop(0, PEER_TOPK, pick, none)

        es = [jnp.exp(bv[r] - bv[0]) for r in range(PEER_TOPK)]
        tot = es[0]
        for e in es[1:]:
            tot = tot + e
        for r in range(PEER_TOPK):
            gate_ref[h * PEER_TOPK + r, 0] = es[r] / tot
        return carry

    lax.fori_loop(0, PEER_HEADS, head, 0)


def _topk(s):
    nt = s.shape[1]
    vreg = (SUBLANES, LANES)
    out_spec = pl.BlockSpec((PEER_SLOTS, 1) + vreg, lambda i: (0, i, 0, 0))
    return pl.pallas_call(
        _topk_kernel,
        grid=(nt,),
        in_specs=[pl.BlockSpec((s.shape[0], 1) + s.shape[2:], lambda i: (0, i, 0, 0))],
        out_specs=[out_spec, out_spec],
        out_shape=[jax.ShapeDtypeStruct((PEER_SLOTS, nt) + vreg, I32),
                   jax.ShapeDtypeStruct((PEER_SLOTS, nt) + vreg, F32)],
        scratch_shapes=[pltpu.VMEM((PEER_NKEYS,) + vreg, F32),
                        pltpu.VMEM((2, PEER_TOPK) + vreg, F32), pltpu.VMEM((2, PEER_TOPK) + vreg, F32),
                        pltpu.VMEM((len(_CAND_PAIRS),) + vreg, F32), pltpu.VMEM((len(_CAND_PAIRS),) + vreg, F32),
                        pltpu.VMEM((PEER_TOPK,) + vreg, F32)],
        compiler_params=_cparams("parallel"),
        cost_estimate=_cost(200 * s.size, 4 * s.size),
        name="topk",
    )(s)


SC_CORES = 2
SC_SUBCORES = 16
SC_LANES = 16
SC_WORKERS = SC_CORES * SC_SUBCORES
PEER_GROUP = 16
SC_BUFFERS = 4
PEER_ITEMS = PEER_GROUP * PEER_HEADS
SC_UNROLL = 8
PACKED_WORDS = D_MODEL // 2
EXPERT_SLAB = (PACKED_WORDS // LANES, LANES)


_SC_PARAMS = pltpu.CompilerParams(needs_layout_passes=False)


def _sc_mesh():
    return plsc.VectorSubcoreMesh(core_axis_name="c", subcore_axis_name="s")


def _sc_worker():
    return lax.axis_index("s") * SC_CORES + lax.axis_index("c")


def _pack_kernel(t_ref, o_ref):
    word = _pack_bf16_pairs(t_ref[...])
    for s in range(EXPERT_SLAB[0]):
        o_ref[:, s, :] = word[:, s * LANES:(s + 1) * LANES]


def _pack_table(table):
    e = table.shape[0]
    tr = 256
    return pl.pallas_call(
        _pack_kernel,
        grid=(e // tr,),
        in_specs=[pl.BlockSpec((tr, D_MODEL), lambda i: (i, 0))],
        out_specs=pl.BlockSpec((tr,) + EXPERT_SLAB, lambda i: (i, 0, 0)),
        out_shape=jax.ShapeDtypeStruct((e,) + EXPERT_SLAB, I32),
        compiler_params=_cparams("parallel"),
        name="pack_table",
    )(table)


def _row_words(rows, k, first=0, count=PEER_TOPK):
    per = LANES // SC_LANES
    sub, cols = k // per, pl.ds((k % per) * SC_LANES, SC_LANES)
    return tuple(rows[r, sub, cols] for r in range(first, first + count))


def _bf16_lanes(word):
    return plsc.bitcast(word, BF16)


def _f32_halves(pairs):
    return plsc.unpack(pairs, format=plsc.PackFormat.INTERLEAVED)


def _half_cols(k):
    return pl.ds(k * SC_LANES, SC_LANES), pl.ds(PACKED_WORDS + k * SC_LANES, SC_LANES)


def _sc_item_pipeline(table_hbm, idx_v, bufs, sems, compute):
    nb = len(bufs)

    def gather(j, b):
        return pltpu.make_async_copy(table_hbm.at[idx_v.at[j]], bufs[b], sems[b])

    for b in range(nb - 1):
        gather(b, b).start()

    @pl.loop(0, PEER_ITEMS // nb)
    def _(i):
        for b in range(nb):
            j = nb * i + b
            ahead = j + nb - 1

            @pl.when(ahead < PEER_ITEMS)
            def _():
                gather(ahead, (b + nb - 1) % nb).start()

            gather(j, b).wait()
            compute(j, bufs[b])


def _peer_dot_kernel(x_hbm, idx_hbm, u_hbm, a_hbm, x_v, idx_v, a_v, *ring):
    tok_per_w = x_hbm.shape[0] // SC_WORKERS
    wid = _sc_worker()
    lane = lax.iota(I32, SC_LANES)

    def compute(j, rows):
        t = j // PEER_HEADS

        zero = jnp.zeros((SC_LANES,), F32)

        half = PEER_TOPK // 2
        steps = PACKED_WORDS // SC_LANES // 2

        def load(kp, first):
            out = ()
            for k in (2 * kp, 2 * kp + 1):
                out += (x_v[t, pl.ds(k * SC_LANES, SC_LANES)],) + _row_words(rows, k, first, half)
            return out

        def fma(acc, vals):
            x0, x1 = _bf16_lanes(vals[0]), _bf16_lanes(vals[1 + half])
            out = []
            for i in range(half):
                lo, hi = _f32_halves(_bf16_lanes(vals[1 + i]) * x0 + _bf16_lanes(vals[2 + half + i]) * x1)
                out.append(acc[i] + lo + hi)
            return tuple(out)

        @plsc.parallel_loop(0, steps, unroll=8, carry=((zero,) * half, (zero,) * half, load(0, half)))
        def state(kp, state):
            acc_a, acc_b, vals_b = state
            vals_a = load(kp, 0)
            acc_b = fma(acc_b, vals_b)
            vals_b = load(jnp.minimum(kp + 1, steps - 1), half)
            return fma(acc_a, vals_a), acc_b, vals_b

        acc = state[0] + state[1]
        out = zero
        for r in range(PEER_TOPK):
            out = jnp.where(lane == r, jnp.sum(acc[r]), out)
        a_v[j, :] = out

    @pl.loop(0, tok_per_w // PEER_GROUP)
    def _(g):
        tok0 = wid * tok_per_w + g * PEER_GROUP
        item0 = tok0 * PEER_HEADS
        pltpu.sync_copy(x_hbm.at[pl.ds(tok0, PEER_GROUP)], x_v)
        pltpu.sync_copy(idx_hbm.at[pl.ds(item0, PEER_ITEMS)], idx_v)
        _sc_item_pipeline(u_hbm, idx_v, ring[:SC_BUFFERS], ring[SC_BUFFERS:], compute)
        pltpu.sync_copy(a_v, a_hbm.at[pl.ds(item0, PEER_ITEMS)])


def _peer_sum_kernel(w_hbm, idx_hbm, v_hbm, o_hbm, w_v, idx_v, o_v, *ring):
    tok_per_w = o_hbm.shape[0] // SC_WORKERS
    wid = _sc_worker()
    zero = jnp.zeros((SC_LANES,), F32)

    def compute(j, rows):
        t = j // PEER_HEADS
        jv = jnp.full((SC_LANES,), j, I32)
        ws = []
        for r in range(PEER_TOPK):
            wr = plsc.load_gather(w_v, [jv, jnp.full((SC_LANES,), r, I32)])
            ws.append(plsc.pack(wr, wr, format=plsc.PackFormat.INTERLEAVED))

        def load(k):
            lo, hi = _half_cols(k)
            return (o_v[t, lo], o_v[t, hi]) + _row_words(rows, k)

        def finish(k, vals):
            acc_lo, acc_hi = vals[0], vals[1]
            for r in range(0, PEER_TOPK, 2):
                lo, hi = _f32_halves(_bf16_lanes(vals[2 + r]) * ws[r] + _bf16_lanes(vals[3 + r]) * ws[r + 1])
                acc_lo = acc_lo + lo
                acc_hi = acc_hi + hi
            lo, hi = _half_cols(k)
            o_v[t, lo] = acc_lo
            o_v[t, hi] = acc_hi

        @plsc.parallel_loop(1, PACKED_WORDS // SC_LANES, unroll=SC_UNROLL, carry=load(0))
        def vals(k, vals):
            nxt = load(k)
            finish(k - 1, vals)
            return nxt

        finish(PACKED_WORDS // SC_LANES - 1, vals)

    @pl.loop(0, tok_per_w // PEER_GROUP)
    def _(g):
        tok0 = wid * tok_per_w + g * PEER_GROUP
        item0 = tok0 * PEER_HEADS
        pltpu.sync_copy(w_hbm.at[pl.ds(item0, PEER_ITEMS)], w_v)
        pltpu.sync_copy(idx_hbm.at[pl.ds(item0, PEER_ITEMS)], idx_v)

        @pl.loop(0, PEER_GROUP)
        def _(t):
            @plsc.parallel_loop(0, D_MODEL // SC_LANES, unroll=16)
            def _(k):
                o_v[t, pl.ds(k * SC_LANES, SC_LANES)] = zero

        _sc_item_pipeline(v_hbm, idx_v, ring[:SC_BUFFERS], ring[SC_BUFFERS:], compute)
        pltpu.sync_copy(o_v, o_hbm.at[pl.ds(tok0, PEER_GROUP)])


def _sc_scratch(first, last):
    return ([first, pltpu.VMEM((PEER_ITEMS, PEER_TOPK), I32), last]
            + [pltpu.VMEM((PEER_TOPK,) + EXPERT_SLAB, I32)] * SC_BUFFERS
            + [pltpu.SemaphoreType.DMA] * SC_BUFFERS)


def _peer_dot(h2, idx, u):
    n = h2.shape[0]
    assert n % (SC_WORKERS * PEER_GROUP) == 0
    call = pl.kernel(
        _peer_dot_kernel,
        out_type=jax.ShapeDtypeStruct((n * PEER_HEADS, PEER_TOPK), F32),
        mesh=_sc_mesh(),
        scratch_types=_sc_scratch(pltpu.VMEM((PEER_GROUP, PACKED_WORDS), I32),
                                  pltpu.VMEM((PEER_ITEMS, PEER_TOPK), F32)),
        compiler_params=_SC_PARAMS,
        cost_estimate=_cost(2 * n * PEER_SLOTS * D_MODEL, 2 * n * PEER_SLOTS * D_MODEL),
    )
    return call(h2, idx, u)


def _peer_sum(w, idx, v, n):
    assert n % (SC_WORKERS * PEER_GROUP) == 0
    call = pl.kernel(
        _peer_sum_kernel,
        out_type=jax.ShapeDtypeStruct((n, D_MODEL), F32),
        mesh=_sc_mesh(),
        scratch_types=_sc_scratch(pltpu.VMEM((PEER_ITEMS, PEER_TOPK), F32),
                                  pltpu.VMEM((PEER_GROUP, D_MODEL), F32)),
        compiler_params=_SC_PARAMS,
        cost_estimate=_cost(2 * n * PEER_SLOTS * D_MODEL, 2 * n * PEER_SLOTS * D_MODEL),
    )
    return call(w, idx, v)


def _gelu_gate_kernel(a_ref, g_ref, w_ref):
    a = a_ref[...]
    w_ref[...] = g_ref[...] * (0.5 * a * (1.0 + lax.erf(a * (2.0 ** -0.5))))


def _gelu_gate(a, gate):
    n = a.shape[0]
    tm = min(2048, n)
    spec = pl.BlockSpec((tm, PEER_SLOTS), lambda i: (i, 0))
    return pl.pallas_call(
        _gelu_gate_kernel, grid=(n // tm,), in_specs=[spec, spec], out_specs=spec,
        out_shape=jax.ShapeDtypeStruct(a.shape, F32), compiler_params=_cparams("parallel"), name="gelu_gate",
    )(a, gate)


def _residual_kernel(x_ref, g_ref, p_ref, o_ref):
    o_ref[0] = x_ref[0] + g_ref[0] * p_ref[0]


def _residual(x1, g2, peer):
    B, T, _ = x1.shape
    tm = min(2 * ROW_TILE, T)
    row = pl.BlockSpec((1, tm, D_MODEL), lambda b, t: (b, t, 0))
    vec = pl.BlockSpec((1, 1, D_MODEL), lambda b, t: (b, 0, 0))
    return pl.pallas_call(
        _residual_kernel, grid=(B, T // tm), in_specs=[row, vec, row], out_specs=row,
        out_shape=jax.ShapeDtypeStruct(x1.shape, F32), compiler_params=_cparams("parallel", "parallel"),
        name="residual",
    )(x1, g2, peer)


def kernel(x, c, ctx, c_ctx, w_ada, b_ada, norm_mix, norm_ffn, w_in, q_norm, k_norm, attn_sink, hgrn_lb_logits,
           hgrn_norm, w_out, peer_w_q, peer_sub_keys, peer_u, peer_v):
    assert w_ada.shape[0] == 1, "single-layer block"
    B, T, D = x.shape
    L = ctx.shape[1]
    n = B * T

    cvecs = jnp.zeros((SUBLANES, D), F32).at[:B].set(c).at[B].set(c_ctx)
    mod = _ada(cvecs, w_ada[0], b_ada[0])
    part = lambda rows, i: rows[:, None, i * D:(i + 1) * D]
    mod_x = mod[:B]
    mod_c = jnp.broadcast_to(mod[B:B + 1], (B, 6 * D))
    sh1, sc1, g1, sh2, sc2, g2 = (part(mod_x, i) for i in range(6))

    lbs = jnp.cumsum(jax.nn.softmax(hgrn_lb_logits.astype(F32), axis=1), axis=1)
    lb_f, lb_b = lbs[0, 0].reshape(1, HGRN_WIDTH), lbs[1, 0].reshape(1, HGRN_WIDTH)

    w_in_b = w_in[0].astype(BF16)
    rope = _rope_tables(T)
    kc, vc, ffc, fbc, ic, _, qhc, _ = _inproj(ctx, part(mod_c, 0), part(mod_c, 1), norm_mix[0], w_in_b,
                                              _identity_rope(L), q_norm[0], k_norm[0])
    s0 = jnp.zeros((B, HGRN_HEADS, HGRN_D, HGRN_D), F32)
    _, _, sfc, sbc = _hgrn(ffc, fbc, ic, qhc, lb_f, lb_b, s0, s0)

    sk = peer_sub_keys[0].reshape(2 * PEER_HEADS, PEER_NKEYS, PEER_DHALF).astype(BF16)
    w_out_b, w_q_b = w_out[0].astype(BF16), peer_w_q[0].astype(BF16)
    u_packed, v_packed = _pack_table(peer_u[0]), _pack_table(peer_v[0])

    kx, vx, ffx, fbx, ix, qx, qhx, gx = _inproj(x, sh1, sc1, norm_mix[0], w_in_b, rope, q_norm[0], k_norm[0])
    attn = _attention(qx, kx, vx, kc, vc, attn_sink[0])
    of, ob, _, _ = _hgrn(ffx, fbx, ix, qhx, lb_f, lb_b, sfc, sbc)
    x1, h2p, s = _outproj(attn, of, ob, gx, x, g1, sh2, sc2, hgrn_norm[0], norm_ffn[0], w_out_b, w_q_b, sk)
    idx_t, gate_t = _topk(s)
    idx16 = idx_t.reshape(PEER_SLOTS, n).T.reshape(n * PEER_HEADS, PEER_TOPK)
    gate = gate_t.reshape(PEER_SLOTS, n).T
    a = _peer_dot(h2p.reshape(n, PACKED_WORDS), idx16, u_packed)
    w = _gelu_gate(a.reshape(n, PEER_SLOTS), gate)
    peer = _peer_sum(w.reshape(n * PEER_HEADS, PEER_TOPK), idx16, v_packed, n)
    return _residual(x1, g2, peer.reshape(B, T, D))
```

```python
import functools

import jax
import jax.numpy as jnp
from jax import lax
from jax.experimental import pallas as pl
from jax.experimental.pallas import tpu as pltpu
from jax.experimental.pallas import tpu_sc as plsc

F32 = jnp.float32
BF16 = jnp.bfloat16
I32 = jnp.int32

D_MODEL = 2048
GRID_W = 64
EPS = 1e-6
HEAD_DIM = 128
ATTN_HEADS = 8
ATTN_KV_HEADS = 2
ATTN_GROUP = ATTN_HEADS // ATTN_KV_HEADS
BAND_BLOCK = 128
ROPE_THETA = 10000.0
HGRN_HEADS = 8
HGRN_D = 128
ATTN_WIDTH = ATTN_HEADS * HEAD_DIM
KV_WIDTH = ATTN_KV_HEADS * HEAD_DIM
HGRN_WIDTH = HGRN_HEADS * HGRN_D
COL_K = 0
COL_V = COL_K + KV_WIDTH
COL_FF = COL_V + KV_WIDTH
COL_FB = COL_FF + HGRN_WIDTH
COL_I = COL_FB + HGRN_WIDTH
COL_Q = COL_I + HGRN_WIDTH
COL_QH = COL_Q + ATTN_WIDTH
COL_G = COL_QH + HGRN_WIDTH
N_IN_COLS = COL_G + HGRN_WIDTH
PEER_HEADS = 8
PEER_NKEYS = 128
PEER_DHALF = 128
PEER_TOPK = 16
PEER_SLOTS = PEER_HEADS * PEER_TOPK

LANES = 128
SUBLANES = 8
VMEM_LIMIT_BYTES = 56 * 1024 * 1024

ROW_TILE = 256
HGRN_CHUNK = 128
HGRN_STEP = 512
HGRN_HEADS_PER_STEP = 8
NEG_INF = float("-inf")


def _cparams(*sem):
    return pltpu.CompilerParams(dimension_semantics=sem, vmem_limit_bytes=VMEM_LIMIT_BYTES)


def _cost(flops, bytes_accessed, transcendentals=0):
    return pl.CostEstimate(flops=int(flops), bytes_accessed=int(bytes_accessed), transcendentals=int(transcendentals))


def _resident(shape):
    nd = len(shape)
    return pl.BlockSpec(shape, lambda *_: (0,) * nd, pipeline_mode=pl.Buffered(1))


def _ada_kernel(c_ref, w_ref, b_ref, o_ref):
    c = c_ref[...]
    s = (c * jax.nn.sigmoid(c)).astype(BF16)
    o_ref[...] = jnp.dot(s, w_ref[...].astype(BF16), preferred_element_type=F32) + b_ref[...]


def _ada(cvecs, w, b):
    n = w.shape[1]
    tn = 1024
    return pl.pallas_call(
        _ada_kernel,
        grid=(n // tn,),
        in_specs=[pl.BlockSpec((SUBLANES, D_MODEL), lambda j: (0, 0)),
                  pl.BlockSpec((D_MODEL, tn), lambda j: (0, j)),
                  pl.BlockSpec((1, tn), lambda j: (0, j))],
        out_specs=pl.BlockSpec((SUBLANES, tn), lambda j: (0, j)),
        out_shape=jax.ShapeDtypeStruct((SUBLANES, n), F32),
        compiler_params=_cparams("arbitrary"),
        name="ada",
    )(cvecs, w, b.reshape(1, n))


def _pack_bf16_pairs(x):
    w = x.shape[1] // 2
    lo = pltpu.bitcast(x[:, :w].astype(BF16).astype(F32), jnp.uint32)
    hi = pltpu.bitcast(x[:, w:].astype(BF16).astype(F32), jnp.uint32)
    return pltpu.bitcast((hi & jnp.uint32(0xFFFF0000)) | (lo >> 16), I32)


def _rms(x, gain):
    return x * lax.rsqrt(jnp.mean(x * x, axis=-1, keepdims=True) + EPS) * gain


def _rope(x, cos, sin_a, sin_b):
    q = HEAD_DIM // 4
    return x * cos + pltpu.roll(x, HEAD_DIM - q, 1) * sin_a + pltpu.roll(x, q, 1) * sin_b


def _inproj_kernel(x_ref, sh_ref, sc_ref, gain_ref, w_ref, cos_ref, sa_ref, sb_ref, qg_ref, kg_ref,
                   k_ref, v_ref, ff_ref, fb_ref, i_ref, q_ref, qh_ref, g_ref):
    x = x_ref[0]
    h = _rms(x, gain_ref[...]) * (1.0 + sc_ref[0]) + sh_ref[0]
    hb = h.astype(BF16)

    def seg(lo, width):
        return jnp.dot(hb, w_ref[:, lo:lo + width], preferred_element_type=F32)

    cos, sa, sb = cos_ref[...], sa_ref[...], sb_ref[...]

    def normed_heads(p, gain, n_heads, out_ref):
        for hd in range(n_heads):
            ph = p[:, hd * HEAD_DIM:(hd + 1) * HEAD_DIM]
            out_ref[0, :, hd * HEAD_DIM:(hd + 1) * HEAD_DIM] = _rope(_rms(ph, gain), cos, sa, sb).astype(BF16)

    normed_heads(seg(COL_K, KV_WIDTH), kg_ref[...], ATTN_KV_HEADS, k_ref)
    v_ref[0] = seg(COL_V, KV_WIDTH).astype(BF16)
    ff_ref[0] = seg(COL_FF, HGRN_WIDTH)
    fb_ref[0] = seg(COL_FB, HGRN_WIDTH)
    i_ref[0] = seg(COL_I, HGRN_WIDTH).astype(BF16)
    normed_heads(seg(COL_Q, ATTN_WIDTH), qg_ref[...], ATTN_HEADS, q_ref)
    qh_ref[0] = seg(COL_QH, HGRN_WIDTH).astype(BF16)
    g_ref[0] = seg(COL_G, HGRN_WIDTH).astype(BF16)


def _inproj(x, shift, scale, gain, w_bf16, rope, q_gain, k_gain, after=()):
    B, T, _ = x.shape
    n_in = 10

    def body(*refs):
        _inproj_kernel(*refs[:n_in], *refs[n_in + len(after):])
    tm = min(ROW_TILE, T)
    row = lambda w: pl.BlockSpec((1, tm, w), lambda b, t: (b, t, 0))
    vec = pl.BlockSpec((1, 1, D_MODEL), lambda b, t: (b, 0, 0))
    tab = pl.BlockSpec((tm, HEAD_DIM), lambda b, t: (t, 0))
    out_w = [(KV_WIDTH, BF16), (KV_WIDTH, BF16), (HGRN_WIDTH, F32), (HGRN_WIDTH, F32), (HGRN_WIDTH, BF16),
             (ATTN_WIDTH, BF16), (HGRN_WIDTH, BF16), (HGRN_WIDTH, BF16)]
    return pl.pallas_call(
        body,
        grid=(B, T // tm),
        in_specs=[row(D_MODEL), vec, vec, _resident((1, D_MODEL)), _resident((D_MODEL, N_IN_COLS)),
                  tab, tab, tab, _resident((1, HEAD_DIM)), _resident((1, HEAD_DIM))]
        + [pl.BlockSpec(memory_space=pl.ANY)] * len(after),
        out_specs=[row(w) for w, _ in out_w],
        out_shape=[jax.ShapeDtypeStruct((B, T, w), dt) for w, dt in out_w],
        compiler_params=_cparams("parallel", "parallel"),
        cost_estimate=_cost(2 * B * T * D_MODEL * N_IN_COLS, B * T * (D_MODEL * 4 + N_IN_COLS * 3)),
        name="inproj",
    )(x, shift, scale, gain.reshape(1, D_MODEL), w_bf16, *rope,
      q_gain.reshape(1, HEAD_DIM), k_gain.reshape(1, HEAD_DIM), *after)


def _rope_tables(T):
    rows = T // GRID_W
    row_pos = jnp.repeat(jnp.arange(rows, dtype=F32), GRID_W)
    col_pos = jnp.tile(jnp.arange(GRID_W, dtype=F32), rows)
    half = HEAD_DIM // 2
    inv_freq = jnp.power(ROPE_THETA, -jnp.arange(0, half, 2, dtype=F32) / half)
    ang_r = row_pos[:, None] * inv_freq
    ang_c = col_pos[:, None] * inv_freq
    cr, sr, cc, sc = jnp.cos(ang_r), jnp.sin(ang_r), jnp.cos(ang_c), jnp.sin(ang_c)
    z = jnp.zeros_like(sr)
    return (jnp.concatenate([cr, cr, cc, cc], -1),
            jnp.concatenate([-sr, z, -sc, z], -1),
            jnp.concatenate([z, sr, z, sc], -1))


def _identity_rope(T):
    return (jnp.ones((T, HEAD_DIM), F32), jnp.zeros((T, HEAD_DIM), F32), jnp.zeros((T, HEAD_DIM), F32))


_NT = (((1,), (1,)), ((), ()))
_TN = (((0,), (0,)), ((), ()))


def _attn_kernel(sink_ref, q_ref, kp_ref, kc_ref, kn_ref, vp_ref, vc_ref, vn_ref, kx_ref, vx_ref, o_ref):
    n = pl.program_id(1)
    nb = pl.num_programs(1)
    blk = BAND_BLOCK
    rows = ATTN_GROUP * blk
    qi = lax.broadcasted_iota(I32, (rows, blk), 0) & (blk - 1)
    kj = lax.broadcasted_iota(I32, (rows, blk), 1)
    prev_ok = kj >= qi + jnp.where(n > 0, 0, blk)
    next_ok = kj <= qi - jnp.where(n < nb - 1, 0, blk)
    scale = HEAD_DIM ** -0.5
    for h in range(ATTN_KV_HEADS):
        cs = slice(h * HEAD_DIM, (h + 1) * HEAD_DIM)
        heads = [h * ATTN_GROUP + g for g in range(ATTN_GROUP)]
        q4 = jnp.concatenate([q_ref[0, :, hd * HEAD_DIM:(hd + 1) * HEAD_DIM] for hd in heads], axis=0)

        def scores(k_ref):
            return lax.dot_general(q4, k_ref[0, :, cs], _NT, preferred_element_type=F32) * scale

        s_p = jnp.where(prev_ok, scores(kp_ref), NEG_INF)
        s_c = scores(kc_ref)
        s_n = jnp.where(next_ok, scores(kn_ref), NEG_INF)
        s_x = scores(kx_ref)
        sink = jnp.concatenate([jnp.full((blk, 1), sink_ref[hd], F32) for hd in heads], axis=0)
        def lane_tiles(t):
            return [t[:, i:i + LANES] for i in range(0, t.shape[1], LANES)]

        tiles = [s_p, s_c, s_n] + lane_tiles(s_x)
        m = functools.reduce(jnp.maximum, tiles)
        m = jnp.maximum(jnp.max(m, -1, keepdims=True), sink)
        p_p, p_c, p_n, p_x = jnp.exp(s_p - m), jnp.exp(s_c - m), jnp.exp(s_n - m), jnp.exp(s_x - m)
        total = functools.reduce(jnp.add, [p_p, p_c, p_n] + lane_tiles(p_x))
        denom = jnp.sum(total, -1, keepdims=True) + jnp.exp(sink - m)

        def pv(p, v_ref):
            return jnp.dot(p.astype(BF16), v_ref[0, :, cs], preferred_element_type=F32)

        o = (pv(p_p, vp_ref) + pv(p_c, vc_ref) + pv(p_n, vn_ref) + pv(p_x, vx_ref)) / denom
        for g, hd in enumerate(heads):
            o_ref[0, :, hd * HEAD_DIM:(hd + 1) * HEAD_DIM] = o[g * blk:(g + 1) * blk].astype(BF16)


def _attention(q, k, v, k_ctx, v_ctx, sink):
    B, T, _ = q.shape
    L = k_ctx.shape[1]
    nb = T // BAND_BLOCK
    kv = lambda f: pl.BlockSpec((1, BAND_BLOCK, KV_WIDTH), lambda b, n: (b, f(n), 0))
    prev, cur, nxt = (lambda n: jnp.maximum(n - 1, 0)), (lambda n: n), (lambda n: jnp.minimum(n + 1, nb - 1))
    ctx = pl.BlockSpec((1, L, KV_WIDTH), lambda b, n: (b, 0, 0))
    return pl.pallas_call(
        _attn_kernel,
        grid=(B, nb),
        in_specs=[pl.BlockSpec(memory_space=pltpu.SMEM),
                  pl.BlockSpec((1, BAND_BLOCK, ATTN_WIDTH), lambda b, n: (b, n, 0)),
                  kv(prev), kv(cur), kv(nxt), kv(prev), kv(cur), kv(nxt), ctx, ctx],
        out_specs=pl.BlockSpec((1, BAND_BLOCK, ATTN_WIDTH), lambda b, n: (b, n, 0)),
        out_shape=jax.ShapeDtypeStruct((B, T, ATTN_WIDTH), BF16),
        compiler_params=_cparams("parallel", "parallel"),
        cost_estimate=_cost(4 * B * T * ATTN_WIDTH * (3 * BAND_BLOCK + L), 8 * B * T * ATTN_WIDTH,
                            B * T * ATTN_HEADS * (3 * BAND_BLOCK + L)),
        name="attn",
    )(sink, q, k, k, k, v, v, v, k_ctx, v_ctx)


_LOG2E = 1.4426950408889634
_DIAG = SUBLANES
_LEVELS = (64, 32, 16, 8)


def _hgrn_chunk(logit, v, q, lb, st_ref, reverse):
    C = HGRN_CHUNK
    f = lb + (1.0 - lb) * jax.nn.sigmoid(logit)
    lf = jnp.log(f)
    kk = 1.0 - f
    qf = q.astype(F32)
    r = lax.broadcasted_iota(I32, (C, C), 0)
    c = lax.broadcasted_iota(I32, (C, C), 1)
    incl = (r <= c) if reverse else (r >= c)
    a = jnp.dot(incl.astype(F32), lf, precision=lax.Precision.HIGHEST, preferred_element_type=F32) * _LOG2E
    a_end = a[0:1] if reverse else a[C - 1:C]

    st = st_ref[...]
    inter = lax.dot_general((qf * jnp.exp2(a)).astype(BF16), st.astype(BF16), _NT, preferred_element_type=F32)
    kd = (kk * jnp.exp2(a_end - a)).astype(BF16)
    st_ref[...] = st * jnp.exp2(a_end) + lax.dot_general(v, kd, _TN, preferred_element_type=F32)

    later = (r < c) if reverse else (r > c)
    att = jnp.zeros((C, C), F32)
    for m in _LEVELS:
        a3 = a.reshape(C // (2 * m), 2 * m, HGRN_D)
        edge = a3[:, m:m + 1, :] if reverse else a3[:, m - 1:m, :]
        e = jnp.exp2(-jnp.abs(a3 - edge)).reshape(C, HGRN_D)
        p = lax.dot_general((qf * e).astype(BF16), (kk * e).astype(BF16), _NT, preferred_element_type=F32)
        pair = ((r ^ c) >> (m.bit_length() - 1)) == 1
        att = jnp.where(pair & later, p, att)

    lane = lax.broadcasted_iota(I32, (_DIAG, C), 1)
    sub = lax.broadcasted_iota(I32, (_DIAG, C), 0)
    keep = [(lane == s) & ((sub <= s) if reverse else (sub >= s)) for s in range(_DIAG)]
    blocks = []
    for j in range(C // _DIAG):
        rows = slice(j * _DIAG, (j + 1) * _DIAG)
        a_j, q_j, k_j = a[rows], qf[rows], kk[rows]
        blk = jnp.zeros((_DIAG, C), F32)
        for s in range(_DIAG):
            e = jnp.exp2(a_j - a_j[s:s + 1])
            col = jnp.sum(q_j * e * k_j[s:s + 1], axis=-1, keepdims=True)
            blk = jnp.where(keep[s], col, blk)
        blocks.append(pltpu.roll(blk, j * _DIAG, 1) if j else blk)
    att = att + jnp.concatenate(blocks, axis=0)
    return inter + jnp.dot(att.astype(BF16), v, preferred_element_type=F32)


def _hgrn_kernel(ff_ref, vf_ref, qf_ref, fb_ref, vb_ref, qb_ref, lbf_ref, lbb_ref, s0f_ref, s0b_ref,
                 of_ref, ob_ref, sf_ref, sb_ref, stf, stb):
    step = pl.program_id(2)
    nsub = ff_ref.shape[1] // HGRN_CHUNK

    @pl.when(step == 0)
    def _():
        stf[...] = s0f_ref[0]
        stb[...] = s0b_ref[0]

    def body(j, carry):
        fo = pl.multiple_of(j * HGRN_CHUNK, HGRN_CHUNK)
        bo = pl.multiple_of((nsub - 1 - j) * HGRN_CHUNK, HGRN_CHUNK)
        for hh in range(HGRN_HEADS_PER_STEP):
            cs = slice(hh * HGRN_D, (hh + 1) * HGRN_D)
            rows = pl.ds(fo, HGRN_CHUNK)
            of_ref[0, rows, cs] = _hgrn_chunk(ff_ref[0, rows, cs], vf_ref[0, rows, cs], qf_ref[0, rows, cs],
                                              lbf_ref[:, cs], stf.at[hh], False)
            rows = pl.ds(bo, HGRN_CHUNK)
            ob_ref[0, rows, cs] = _hgrn_chunk(fb_ref[0, rows, cs], vb_ref[0, rows, cs], qb_ref[0, rows, cs],
                                              lbb_ref[:, cs], stb.at[hh], True)
        return carry

    lax.fori_loop(0, nsub, body, 0)

    @pl.when(step == pl.num_programs(2) - 1)
    def _():
        sf_ref[0] = stf[...]
        sb_ref[0] = stb[...]


def _hgrn(ff, fb, val, q, lb_f, lb_b, s0f, s0b):
    B, T, _ = ff.shape
    ts = min(HGRN_STEP, T)
    ns = T // ts
    hp = HGRN_HEADS_PER_STEP
    fwd = pl.BlockSpec((1, ts, hp * HGRN_D), lambda b, h, s: (b, s, h))
    bwd = pl.BlockSpec((1, ts, hp * HGRN_D), lambda b, h, s: (b, ns - 1 - s, h))
    lbs = pl.BlockSpec((1, hp * HGRN_D), lambda b, h, s: (0, h))
    st = pl.BlockSpec((1, hp, HGRN_D, HGRN_D), lambda b, h, s: (b, h, 0, 0))
    o_sds = jax.ShapeDtypeStruct((B, T, HGRN_WIDTH), F32)
    s_sds = jax.ShapeDtypeStruct((B, HGRN_HEADS, HGRN_D, HGRN_D), F32)
    return pl.pallas_call(
        _hgrn_kernel,
        grid=(B, HGRN_HEADS // hp, ns),
        in_specs=[fwd, fwd, fwd, bwd, bwd, bwd, lbs, lbs, st, st],
        out_specs=[fwd, bwd, st, st],
        out_shape=[o_sds, o_sds, s_sds, s_sds],
        scratch_shapes=[pltpu.VMEM((hp, HGRN_D, HGRN_D), F32), pltpu.VMEM((hp, HGRN_D, HGRN_D), F32)],
        compiler_params=_cparams("parallel", "parallel", "arbitrary"),
        cost_estimate=_cost(32 * B * T * HGRN_WIDTH * HGRN_CHUNK, 24 * B * T * HGRN_WIDTH, 8 * B * T * HGRN_WIDTH),
        name="hgrn",
    )(ff, val, q, fb, val, q, lb_f, lb_b, s0f, s0b)


def _outproj_kernel(attn_ref, of_ref, ob_ref, g_ref, x_ref, g1_ref, sh2_ref, sc2_ref, og_ref, nf_ref,
                    wo_ref, wq_ref, sk_ref, x1_ref, h2_ref, s_ref):
    o = of_ref[0] + ob_ref[0]
    og = og_ref[...]
    parts = []
    for hd in range(HGRN_HEADS):
        cs = slice(hd * HGRN_D, (hd + 1) * HGRN_D)
        gh = g_ref[0, :, cs].astype(F32)
        parts.append((_rms(o[:, cs], og) * (gh * jax.nn.sigmoid(gh))).astype(BF16))
    hg = jnp.concatenate(parts, axis=-1)
    mix = (jnp.dot(attn_ref[0], wo_ref[:ATTN_WIDTH], preferred_element_type=F32)
           + jnp.dot(hg, wo_ref[ATTN_WIDTH:], preferred_element_type=F32))
    x1 = x_ref[0] + g1_ref[0] * mix
    x1_ref[0] = x1
    h2 = _rms(x1, nf_ref[...]) * (1.0 + sc2_ref[0]) + sh2_ref[0]
    h2_ref[0] = _pack_bf16_pairs(h2)
    pq = jnp.dot(h2.astype(BF16), wq_ref[...], preferred_element_type=F32)
    for hp in range(2 * PEER_HEADS):
        cs = slice(hp * PEER_DHALF, (hp + 1) * PEER_DHALF)
        for jb in range(pq.shape[0] // LANES):
            rows = slice(jb * LANES, (jb + 1) * LANES)
            s_ref[hp, 0, jb * PEER_NKEYS:(jb + 1) * PEER_NKEYS, :] = lax.dot_general(
                sk_ref[hp], pq[rows, cs].astype(BF16), _NT, preferred_element_type=F32)


def _outproj(attn, of, ob, g, x, g1, sh2, sc2, o_gain, norm_ffn, wo_bf16, wq_bf16, sk_bf16):
    B, T, _ = x.shape
    tm = min(ROW_TILE, T)
    nt = T // tm
    per = TOPK_TOKENS // tm
    assert (B * T) % TOPK_TOKENS == 0 and TOPK_TOKENS % tm == 0 and tm % LANES == 0
    row = lambda w: pl.BlockSpec((1, tm, w), lambda b, t: (b, t, 0))
    vec = pl.BlockSpec((1, 1, D_MODEL), lambda b, t: (b, 0, 0))
    n_sk = 2 * PEER_HEADS
    sds = lambda w: jax.ShapeDtypeStruct((B, T, w), F32)
    s_rows = tm // LANES * PEER_NKEYS
    s_spec = pl.BlockSpec((n_sk, 1, s_rows, LANES), lambda b, t: (0, (b * nt + t) // per, (b * nt + t) % per, 0))
    s_sds = jax.ShapeDtypeStruct((n_sk, B * T // TOPK_TOKENS, TOPK_TOKENS // LANES * PEER_NKEYS, LANES), F32)
    return pl.pallas_call(
        _outproj_kernel,
        grid=(B, T // tm),
        in_specs=[row(ATTN_WIDTH), row(HGRN_WIDTH), row(HGRN_WIDTH), row(HGRN_WIDTH), row(D_MODEL),
                  vec, vec, vec, _resident((1, HGRN_D)), _resident((1, D_MODEL)),
                  _resident((ATTN_WIDTH + HGRN_WIDTH, D_MODEL)), _resident((D_MODEL, n_sk * PEER_DHALF)),
                  _resident((n_sk, PEER_NKEYS, PEER_DHALF))],
        out_specs=[row(D_MODEL), row(D_MODEL // 2), s_spec],
        out_shape=[sds(D_MODEL), jax.ShapeDtypeStruct((B, T, D_MODEL // 2), I32), s_sds],
        compiler_params=_cparams("parallel", "parallel"),
        cost_estimate=_cost(2 * B * T * D_MODEL * (2 * D_MODEL + PEER_NKEYS), 40 * B * T * D_MODEL),
        name="outproj",
    )(attn, of, ob, g, x, g1, sh2, sc2, o_gain.reshape(1, HGRN_D), norm_ffn.reshape(1, D_MODEL),
      wo_bf16, wq_bf16, sk_bf16)


TOPK_TOKENS = SUBLANES * LANES
_CAND_PAIRS = [(a, b) for a in range(PEER_TOPK) for b in range(PEER_TOPK) if (a + 1) * (b + 1) <= PEER_TOPK]


def _first_argmax(values, ids, n_chains):
    per = -(-len(values) // n_chains)
    parts = []
    for lo in range(0, len(values), per):
        m, i = values[lo], ids[lo]
        if not isinstance(i, jax.Array):
            i = jnp.full(m.shape, i, F32)
        for v, vid in zip(values[lo + 1:lo + per], ids[lo + 1:lo + per]):
            c = v > m
            m = jnp.where(c, v, m)
            i = jnp.where(c, vid, i)
        parts.append((m, i))
    m, i = parts[0]
    for pm, pi in parts[1:]:
        c = pm > m
        m = jnp.where(c, pm, m)
        i = jnp.where(c, pi, i)
    return m, i


def _topk_kernel(s_ref, idx_ref, gate_ref, wk, tv, ti, cv, ci, bv):
    shape = (SUBLANES, LANES)
    none = jnp.full(shape, -1.0, F32)

    def head(h, carry):
        for p in range(2):
            hp = 2 * h + p
            for k in range(PEER_NKEYS):
                wk[k] = s_ref[hp, 0, pl.ds(k, SUBLANES, stride=PEER_NKEYS), :]

            def extract(r, prev, p=p):
                vals = []
                for k in range(PEER_NKEYS):
                    s = jnp.where(prev == float(k), NEG_INF, wk[k])
                    wk[k] = s
                    vals.append(s)
                m, i = _first_argmax(vals, [float(k) for k in range(PEER_NKEYS)], 4)
                tv[p, r] = m
                ti[p, r] = i
                return i

            lax.fori_loop(0, PEER_TOPK, extract, none)

        for c, (a, b) in enumerate(_CAND_PAIRS):
            cv[c] = tv[0, a] + tv[1, b]
            ci[c] = ti[0, a] * float(PEER_NKEYS) + ti[1, b]

        def pick(r, prev):
            vals, ids = [], []
            for c in range(len(_CAND_PAIRS)):
                cid = ci[c]
                s = jnp.where(cid == prev, NEG_INF, cv[c])
                cv[c] = s
                vals.append(s)
                ids.append(cid)
            m, i = _first_argmax(vals, ids, 2)
            bv[r] = m
            idx_ref[h * PEER_TOPK + r, 0] = i.astype(I32)
            return i

        lax.fori_loop(0, PEER_TOPK, pick, none)

        es = [jnp.exp(bv[r] - bv[0]) for r in range(PEER_TOPK)]
        tot = es[0]
        for e in es[1:]:
            tot = tot + e
        for r in range(PEER_TOPK):
            gate_ref[h * PEER_TOPK + r, 0] = es[r] / tot
        return carry

    lax.fori_loop(0, PEER_HEADS, head, 0)


def _topk(s):
    nt = s.shape[1]
    vreg = (SUBLANES, LANES)
    out_spec = pl.BlockSpec((PEER_SLOTS, 1) + vreg, lambda i: (0, i, 0, 0))
    return pl.pallas_call(
        _topk_kernel,
        grid=(nt,),
        in_specs=[pl.BlockSpec((s.shape[0], 1) + s.shape[2:], lambda i: (0, i, 0, 0))],
        out_specs=[out_spec, out_spec],
        out_shape=[jax.ShapeDtypeStruct((PEER_SLOTS, nt) + vreg, I32),
                   jax.ShapeDtypeStruct((PEER_SLOTS, nt) + vreg, F32)],
        scratch_shapes=[pltpu.VMEM((PEER_NKEYS,) + vreg, F32),
                        pltpu.VMEM((2, PEER_TOPK) + vreg, F32), pltpu.VMEM((2, PEER_TOPK) + vreg, F32),
                        pltpu.VMEM((len(_CAND_PAIRS),) + vreg, F32), pltpu.VMEM((len(_CAND_PAIRS),) + vreg, F32),
                        pltpu.VMEM((PEER_TOPK,) + vreg, F32)],
        compiler_params=_cparams("parallel"),
        cost_estimate=_cost(200 * s.size, 4 * s.size),
        name="topk",
    )(s)


SC_CORES = 2
SC_SUBCORES = 16
SC_LANES = 16
SC_WORKERS = SC_CORES * SC_SUBCORES
PEER_GROUP = 16
SC_BUFFERS = 4
PEER_ITEMS = PEER_GROUP * PEER_HEADS
SC_UNROLL = 8
PACKED_WORDS = D_MODEL // 2
EXPERT_SLAB = (PACKED_WORDS // LANES, LANES)


_SC_PARAMS = pltpu.CompilerParams(needs_layout_passes=False)


def _sc_mesh():
    return plsc.VectorSubcoreMesh(core_axis_name="c", subcore_axis_name="s")


def _sc_worker():
    return lax.axis_index("s") * SC_CORES + lax.axis_index("c")


def _pack_kernel(t_ref, o_ref):
    word = _pack_bf16_pairs(t_ref[...])
    for s in range(EXPERT_SLAB[0]):
        o_ref[:, s, :] = word[:, s * LANES:(s + 1) * LANES]


def _pack_table(table):
    e = table.shape[0]
    tr = 256
    return pl.pallas_call(
        _pack_kernel,
        grid=(e // tr,),
        in_specs=[pl.BlockSpec((tr, D_MODEL), lambda i: (i, 0))],
        out_specs=pl.BlockSpec((tr,) + EXPERT_SLAB, lambda i: (i, 0, 0)),
        out_shape=jax.ShapeDtypeStruct((e,) + EXPERT_SLAB, I32),
        compiler_params=_cparams("parallel"),
        name="pack_table",
    )(table)


def _row_words(rows, k, first=0, count=PEER_TOPK):
    per = LANES // SC_LANES
    sub, cols = k // per, pl.ds((k % per) * SC_LANES, SC_LANES)
    return tuple(rows[r, sub, cols] for r in range(first, first + count))


def _bf16_lanes(word):
    return plsc.bitcast(word, BF16)


def _f32_halves(pairs):
    return plsc.unpack(pairs, format=plsc.PackFormat.INTERLEAVED)


def _half_cols(k):
    return pl.ds(k * SC_LANES, SC_LANES), pl.ds(PACKED_WORDS + k * SC_LANES, SC_LANES)


def _sc_item_pipeline(table_hbm, idx_v, bufs, sems, compute):
    nb = len(bufs)

    def gather(j, b):
        return pltpu.make_async_copy(table_hbm.at[idx_v.at[j]], bufs[b], sems[b])

    for b in range(nb - 1):
        gather(b, b).start()

    @pl.loop(0, PEER_ITEMS // nb)
    def _(i):
        for b in range(nb):
            j = nb * i + b
            ahead = j + nb - 1

            @pl.when(ahead < PEER_ITEMS)
            def _():
                gather(ahead, (b + nb - 1) % nb).start()

            gather(j, b).wait()
            compute(j, bufs[b])


def _peer_dot_kernel(x_hbm, idx_hbm, u_hbm, a_hbm, x_v, idx_v, a_v, *ring):
    tok_per_w = x_hbm.shape[0] // SC_WORKERS
    wid = _sc_worker()
    lane = lax.iota(I32, SC_LANES)

    def compute(j, rows):
        t = j // PEER_HEADS

        zero = jnp.zeros((SC_LANES,), F32)

        half = PEER_TOPK // 2
        steps = PACKED_WORDS // SC_LANES // 2

        def load(kp, first):
            out = ()
            for k in (2 * kp, 2 * kp + 1):
                out += (x_v[t, pl.ds(k * SC_LANES, SC_LANES)],) + _row_words(rows, k, first, half)
            return out

        def fma(acc, vals):
            x0, x1 = _bf16_lanes(vals[0]), _bf16_lanes(vals[1 + half])
            out = []
            for i in range(half):
                lo, hi = _f32_halves(_bf16_lanes(vals[1 + i]) * x0 + _bf16_lanes(vals[2 + half + i]) * x1)
                out.append(acc[i] + lo + hi)
            return tuple(out)

        @plsc.parallel_loop(0, steps, unroll=8, carry=((zero,) * half, (zero,) * half, load(0, half)))
        def state(kp, state):
            acc_a, acc_b, vals_b = state
            vals_a = load(kp, 0)
            acc_b = fma(acc_b, vals_b)
            vals_b = load(jnp.minimum(kp + 1, steps - 1), half)
            return fma(acc_a, vals_a), acc_b, vals_b

        acc = state[0] + state[1]
        out = zero
        for r in range(PEER_TOPK):
            out = jnp.where(lane == r, jnp.sum(acc[r]), out)
        a_v[j, :] = out

    @pl.loop(0, tok_per_w // PEER_GROUP)
    def _(g):
        tok0 = wid * tok_per_w + g * PEER_GROUP
        item0 = tok0 * PEER_HEADS
        pltpu.sync_copy(x_hbm.at[pl.ds(tok0, PEER_GROUP)], x_v)
        pltpu.sync_copy(idx_hbm.at[pl.ds(item0, PEER_ITEMS)], idx_v)
        _sc_item_pipeline(u_hbm, idx_v, ring[:SC_BUFFERS], ring[SC_BUFFERS:], compute)
        pltpu.sync_copy(a_v, a_hbm.at[pl.ds(item0, PEER_ITEMS)])


def _peer_sum_kernel(w_hbm, idx_hbm, v_hbm, o_hbm, w_v, idx_v, o_v, *ring):
    tok_per_w = o_hbm.shape[0] // SC_WORKERS
    wid = _sc_worker()
    zero = jnp.zeros((SC_LANES,), F32)

    def compute(j, rows):
        t = j // PEER_HEADS
        jv = jnp.full((SC_LANES,), j, I32)
        ws = []
        for r in range(PEER_TOPK):
            wr = plsc.load_gather(w_v, [jv, jnp.full((SC_LANES,), r, I32)])
            ws.append(plsc.pack(wr, wr, format=plsc.PackFormat.INTERLEAVED))

        def load(k):
            lo, hi = _half_cols(k)
            return (o_v[t, lo], o_v[t, hi]) + _row_words(rows, k)

        def finish(k, vals):
            acc_lo, acc_hi = vals[0], vals[1]
            for r in range(0, PEER_TOPK, 2):
                lo, hi = _f32_halves(_bf16_lanes(vals[2 + r]) * ws[r] + _bf16_lanes(vals[3 + r]) * ws[r + 1])
                acc_lo = acc_lo + lo
                acc_hi = acc_hi + hi
            lo, hi = _half_cols(k)
            o_v[t, lo] = acc_lo
            o_v[t, hi] = acc_hi

        @plsc.parallel_loop(1, PACKED_WORDS // SC_LANES, unroll=SC_UNROLL, carry=load(0))
        def vals(k, vals):
            nxt = load(k)
            finish(k - 1, vals)
            return nxt

        finish(PACKED_WORDS // SC_LANES - 1, vals)

    @pl.loop(0, tok_per_w // PEER_GROUP)
    def _(g):
        tok0 = wid * tok_per_w + g * PEER_GROUP
        item0 = tok0 * PEER_HEADS
        pltpu.sync_copy(w_hbm.at[pl.ds(item0, PEER_ITEMS)], w_v)
        pltpu.sync_copy(idx_hbm.at[pl.ds(item0, PEER_ITEMS)], idx_v)

        @pl.loop(0, PEER_GROUP)
        def _(t):
            @plsc.parallel_loop(0, D_MODEL // SC_LANES, unroll=16)
            def _(k):
                o_v[t, pl.ds(k * SC_LANES, SC_LANES)] = zero

        _sc_item_pipeline(v_hbm, idx_v, ring[:SC_BUFFERS], ring[SC_BUFFERS:], compute)
        pltpu.sync_copy(o_v, o_hbm.at[pl.ds(tok0, PEER_GROUP)])


def _sc_scratch(first, last):
    return ([first, pltpu.VMEM((PEER_ITEMS, PEER_TOPK), I32), last]
            + [pltpu.VMEM((PEER_TOPK,) + EXPERT_SLAB, I32)] * SC_BUFFERS
            + [pltpu.SemaphoreType.DMA] * SC_BUFFERS)


def _peer_dot(h2, idx, u):
    n = h2.shape[0]
    assert n % (SC_WORKERS * PEER_GROUP) == 0
    call = pl.kernel(
        _peer_dot_kernel,
        out_type=jax.ShapeDtypeStruct((n * PEER_HEADS, PEER_TOPK), F32),
        mesh=_sc_mesh(),
        scratch_types=_sc_scratch(pltpu.VMEM((PEER_GROUP, PACKED_WORDS), I32),
                                  pltpu.VMEM((PEER_ITEMS, PEER_TOPK), F32)),
        compiler_params=_SC_PARAMS,
        cost_estimate=_cost(2 * n * PEER_SLOTS * D_MODEL, 2 * n * PEER_SLOTS * D_MODEL),
    )
    return call(h2, idx, u)


def _peer_sum(w, idx, v, n):
    assert n % (SC_WORKERS * PEER_GROUP) == 0
    call = pl.kernel(
        _peer_sum_kernel,
        out_type=jax.ShapeDtypeStruct((n, D_MODEL), F32),
        mesh=_sc_mesh(),
        scratch_types=_sc_scratch(pltpu.VMEM((PEER_ITEMS, PEER_TOPK), F32),
                                  pltpu.VMEM((PEER_GROUP, D_MODEL), F32)),
        compiler_params=_SC_PARAMS,
        cost_estimate=_cost(2 * n * PEER_SLOTS * D_MODEL, 2 * n * PEER_SLOTS * D_MODEL),
    )
    return call(w, idx, v)


def _gelu_gate_kernel(a_ref, g_ref, w_ref):
    a = a_ref[...]
    w_ref[...] = g_ref[...] * (0.5 * a * (1.0 + lax.erf(a * (2.0 ** -0.5))))


def _gelu_gate(a, gate):
    n = a.shape[0]
    tm = min(2048, n)
    spec = pl.BlockSpec((tm, PEER_SLOTS), lambda i: (i, 0))
    return pl.pallas_call(
        _gelu_gate_kernel, grid=(n // tm,), in_specs=[spec, spec], out_specs=spec,
        out_shape=jax.ShapeDtypeStruct(a.shape, F32), compiler_params=_cparams("parallel"), name="gelu_gate",
    )(a, gate)


def _residual_kernel(x_ref, g_ref, p_ref, o_ref):
    o_ref[0] = x_ref[0] + g_ref[0] * p_ref[0]


def _residual(x1, g2, peer):
    B, T, _ = x1.shape
    tm = min(2 * ROW_TILE, T)
    row = pl.BlockSpec((1, tm, D_MODEL), lambda b, t: (b, t, 0))
    vec = pl.BlockSpec((1, 1, D_MODEL), lambda b, t: (b, 0, 0))
    return pl.pallas_call(
        _residual_kernel, grid=(B, T // tm), in_specs=[row, vec, row], out_specs=row,
        out_shape=jax.ShapeDtypeStruct(x1.shape, F32), compiler_params=_cparams("parallel", "parallel"),
        name="residual",
    )(x1, g2, peer)


def kernel(x, c, ctx, c_ctx, w_ada, b_ada, norm_mix, norm_ffn, w_in, q_norm, k_norm, attn_sink, hgrn_lb_logits,
           hgrn_norm, w_out, peer_w_q, peer_sub_keys, peer_u, peer_v):
    assert w_ada.shape[0] == 1, "single-layer block"
    B, T, D = x.shape
    L = ctx.shape[1]
    n = B * T

    cvecs = jnp.zeros((SUBLANES, D), F32).at[:B].set(c).at[B].set(c_ctx)
    mod = _ada(cvecs, w_ada[0], b_ada[0])
    part = lambda rows, i: rows[:, None, i * D:(i + 1) * D]
    mod_x = mod[:B]
    mod_c = jnp.broadcast_to(mod[B:B + 1], (B, 6 * D))
    sh1, sc1, g1, sh2, sc2, g2 = (part(mod_x, i) for i in range(6))

    lbs = jnp.cumsum(jax.nn.softmax(hgrn_lb_logits.astype(F32), axis=1), axis=1)
    lb_f, lb_b = lbs[0, 0].reshape(1, HGRN_WIDTH), lbs[1, 0].reshape(1, HGRN_WIDTH)

    w_in_b = w_in[0].astype(BF16)
    rope = _rope_tables(T)
    kc, vc, ffc, fbc, ic, _, qhc, _ = _inproj(ctx, part(mod_c, 0), part(mod_c, 1), norm_mix[0], w_in_b,
                                              _identity_rope(L), q_norm[0], k_norm[0])
    s0 = jnp.zeros((B, HGRN_HEADS, HGRN_D, HGRN_D), F32)
    _, _, sfc, sbc = _hgrn(ffc, fbc, ic, qhc, lb_f, lb_b, s0, s0)

    sk = peer_sub_keys[0].reshape(2 * PEER_HEADS, PEER_NKEYS, PEER_DHALF).astype(BF16)
    w_out_b, w_q_b = w_out[0].astype(BF16), peer_w_q[0].astype(BF16)
    u_packed, v_packed = _pack_table(peer_u[0]), _pack_table(peer_v[0])

    kx, vx, ffx, fbx, ix, qx, qhx, gx = _inproj(x, sh1, sc1, norm_mix[0], w_in_b, rope, q_norm[0], k_norm[0])
    attn = _attention(qx, kx, vx, kc, vc, attn_sink[0])
    of, ob, _, _ = _hgrn(ffx, fbx, ix, qhx, lb_f, lb_b, sfc, sbc)
    x1, h2p, s = _outproj(attn, of, ob, gx, x, g1, sh2, sc2, hgrn_norm[0], norm_ffn[0], w_out_b, w_q_b, sk)
    idx_t, gate_t = _topk(s)
    idx16 = idx_t.reshape(PEER_SLOTS, n).T.reshape(n * PEER_HEADS, PEER_TOPK)
    gate = gate_t.reshape(PEER_SLOTS, n).T
    a = _peer_dot(h2p.reshape(n, PACKED_WORDS), idx16, u_packed)
    w = _gelu_gate(a.reshape(n, PEER_SLOTS), gate)
    peer = _peer_sum(w.reshape(n * PEER_HEADS, PEER_TOPK), idx16, v_packed, n)
    return _residual(x1, g2, peer.reshape(B, T, D))
```

```python
import functools

import jax
import jax.numpy as jnp
from jax import lax
from jax.experimental import pallas as pl
from jax.experimental.pallas import tpu as pltpu
from jax.experimental.pallas import tpu_sc as plsc

F32 = jnp.float32
BF16 = jnp.bfloat16
I32 = jnp.int32

D_MODEL = 2048
GRID_W = 64
EPS = 1e-6
HEAD_DIM = 128
ATTN_HEADS = 8
ATTN_KV_HEADS = 2
ATTN_GROUP = ATTN_HEADS // ATTN_KV_HEADS
BAND_BLOCK = 128
ROPE_THETA = 10000.0
HGRN_HEADS = 8
HGRN_D = 128
ATTN_WIDTH = ATTN_HEADS * HEAD_DIM
KV_WIDTH = ATTN_KV_HEADS * HEAD_DIM
HGRN_WIDTH = HGRN_HEADS * HGRN_D
COL_K = 0
COL_V = COL_K + KV_WIDTH
COL_FF = COL_V + KV_WIDTH
COL_FB = COL_FF + HGRN_WIDTH
COL_I = COL_FB + HGRN_WIDTH
COL_Q = COL_I + HGRN_WIDTH
COL_QH = COL_Q + ATTN_WIDTH
COL_G = COL_QH + HGRN_WIDTH
N_IN_COLS = COL_G + HGRN_WIDTH
PEER_HEADS = 8
PEER_NKEYS = 128
PEER_DHALF = 128
PEER_TOPK = 16
PEER_SLOTS = PEER_HEADS * PEER_TOPK

LANES = 128
SUBLANES = 8
VMEM_LIMIT_BYTES = 56 * 1024 * 1024

ROW_TILE = 256
HGRN_CHUNK = 128
HGRN_STEP = 512
HGRN_HEADS_PER_STEP = 8
NEG_INF = float("-inf")


def _cparams(*sem):
    return pltpu.CompilerParams(dimension_semantics=sem, vmem_limit_bytes=VMEM_LIMIT_BYTES)


def _resident(shape):
    nd = len(shape)
    return pl.BlockSpec(shape, lambda *_: (0,) * nd, pipeline_mode=pl.Buffered(1))


def _ada_kernel(c_ref, w_ref, b_ref, o_ref):
    c = c_ref[...]
    s = (c * jax.nn.sigmoid(c)).astype(BF16)
    o_ref[...] = jnp.dot(s, w_ref[...].astype(BF16), preferred_element_type=F32) + b_ref[...]


def _ada(cvecs, w, b):
    n = w.shape[1]
    tn = 1024
    return pl.pallas_call(
        _ada_kernel,
        grid=(n // tn,),
        in_specs=[pl.BlockSpec((SUBLANES, D_MODEL), lambda j: (0, 0)),
                  pl.BlockSpec((D_MODEL, tn), lambda j: (0, j)),
                  pl.BlockSpec((1, tn), lambda j: (0, j))],
        out_specs=pl.BlockSpec((SUBLANES, tn), lambda j: (0, j)),
        out_shape=jax.ShapeDtypeStruct((SUBLANES, n), F32),
        compiler_params=_cparams("arbitrary"),
        name="ada",
    )(cvecs, w, b.reshape(1, n))


def _pack_bf16_pairs(x):
    w = x.shape[1] // 2
    lo = pltpu.bitcast(x[:, :w].astype(BF16).astype(F32), jnp.uint32)
    hi = pltpu.bitcast(x[:, w:].astype(BF16).astype(F32), jnp.uint32)
    return pltpu.bitcast((hi & jnp.uint32(0xFFFF0000)) | (lo >> 16), I32)


def _rms(x, gain):
    return x * lax.rsqrt(jnp.mean(x * x, axis=-1, keepdims=True) + EPS) * gain


def _rope(x, cos, sin_a, sin_b):
    q = HEAD_DIM // 4
    return x * cos + pltpu.roll(x, HEAD_DIM - q, 1) * sin_a + pltpu.roll(x, q, 1) * sin_b


def _inproj_kernel(x_ref, sh_ref, sc_ref, gain_ref, w_ref, cos_ref, sa_ref, sb_ref, qg_ref, kg_ref,
                   k_ref, v_ref, ff_ref, fb_ref, i_ref, q_ref, qh_ref, g_ref):
    x = x_ref[0]
    h = _rms(x, gain_ref[...]) * (1.0 + sc_ref[0]) + sh_ref[0]
    hb = h.astype(BF16)

    def seg(lo, width):
        return jnp.dot(hb, w_ref[:, lo:lo + width], preferred_element_type=F32)

    cos, sa, sb = cos_ref[...], sa_ref[...], sb_ref[...]

    def normed_heads(p, gain, n_heads, out_ref):
        for hd in range(n_heads):
            ph = p[:, hd * HEAD_DIM:(hd + 1) * HEAD_DIM]
            out_ref[0, :, hd * HEAD_DIM:(hd + 1) * HEAD_DIM] = _rope(_rms(ph, gain), cos, sa, sb).astype(BF16)

    normed_heads(seg(COL_K, KV_WIDTH), kg_ref[...], ATTN_KV_HEADS, k_ref)
    v_ref[0] = seg(COL_V, KV_WIDTH).astype(BF16)
    ff_ref[0] = seg(COL_FF, HGRN_WIDTH)
    fb_ref[0] = seg(COL_FB, HGRN_WIDTH)
    i_ref[0] = seg(COL_I, HGRN_WIDTH).astype(BF16)
    normed_heads(seg(COL_Q, ATTN_WIDTH), qg_ref[...], ATTN_HEADS, q_ref)
    qh_ref[0] = seg(COL_QH, HGRN_WIDTH).astype(BF16)
    g_ref[0] = seg(COL_G, HGRN_WIDTH).astype(BF16)


def _inproj(x, shift, scale, gain, w_bf16, rope, q_gain, k_gain):
    B, T, _ = x.shape
    tm = min(ROW_TILE, T)
    row = lambda w: pl.BlockSpec((1, tm, w), lambda b, t: (b, t, 0))
    vec = pl.BlockSpec((1, 1, D_MODEL), lambda b, t: (b, 0, 0))
    tab = pl.BlockSpec((tm, HEAD_DIM), lambda b, t: (t, 0))
    out_w = [(KV_WIDTH, BF16), (KV_WIDTH, BF16), (HGRN_WIDTH, F32), (HGRN_WIDTH, F32), (HGRN_WIDTH, BF16),
             (ATTN_WIDTH, BF16), (HGRN_WIDTH, BF16), (HGRN_WIDTH, BF16)]
    return pl.pallas_call(
        _inproj_kernel,
        grid=(B, T // tm),
        in_specs=[row(D_MODEL), vec, vec, _resident((1, D_MODEL)), _resident((D_MODEL, N_IN_COLS)),
                  tab, tab, tab, _resident((1, HEAD_DIM)), _resident((1, HEAD_DIM))],
        out_specs=[row(w) for w, _ in out_w],
        out_shape=[jax.ShapeDtypeStruct((B, T, w), dt) for w, dt in out_w],
        compiler_params=_cparams("parallel", "parallel"),
        name="inproj",
    )(x, shift, scale, gain.reshape(1, D_MODEL), w_bf16, *rope,
      q_gain.reshape(1, HEAD_DIM), k_gain.reshape(1, HEAD_DIM))


def _rope_tables(T):
    rows = T // GRID_W
    row_pos = jnp.repeat(jnp.arange(rows, dtype=F32), GRID_W)
    col_pos = jnp.tile(jnp.arange(GRID_W, dtype=F32), rows)
    half = HEAD_DIM // 2
    inv_freq = jnp.power(ROPE_THETA, -jnp.arange(0, half, 2, dtype=F32) / half)
    ang_r = row_pos[:, None] * inv_freq
    ang_c = col_pos[:, None] * inv_freq
    cr, sr, cc, sc = jnp.cos(ang_r), jnp.sin(ang_r), jnp.cos(ang_c), jnp.sin(ang_c)
    z = jnp.zeros_like(sr)
    return (jnp.concatenate([cr, cr, cc, cc], -1),
            jnp.concatenate([-sr, z, -sc, z], -1),
            jnp.concatenate([z, sr, z, sc], -1))


def _identity_rope(T):
    return (jnp.ones((T, HEAD_DIM), F32), jnp.zeros((T, HEAD_DIM), F32), jnp.zeros((T, HEAD_DIM), F32))


_NT = (((1,), (1,)), ((), ()))
_TN = (((0,), (0,)), ((), ()))


def _attn_kernel(sink_ref, q_ref, kp_ref, kc_ref, kn_ref, vp_ref, vc_ref, vn_ref, kx_ref, vx_ref, o_ref):
    n = pl.program_id(1)
    nb = pl.num_programs(1)
    blk = BAND_BLOCK
    rows = ATTN_GROUP * blk
    qi = lax.broadcasted_iota(I32, (rows, blk), 0) & (blk - 1)
    kj = lax.broadcasted_iota(I32, (rows, blk), 1)
    prev_ok = kj >= qi + jnp.where(n > 0, 0, blk)
    next_ok = kj <= qi - jnp.where(n < nb - 1, 0, blk)
    scale = HEAD_DIM ** -0.5
    for h in range(ATTN_KV_HEADS):
        cs = slice(h * HEAD_DIM, (h + 1) * HEAD_DIM)
        heads = [h * ATTN_GROUP + g for g in range(ATTN_GROUP)]
        q4 = jnp.concatenate([q_ref[0, :, hd * HEAD_DIM:(hd + 1) * HEAD_DIM] for hd in heads], axis=0)

        def scores(k_ref):
            return lax.dot_general(q4, k_ref[0, :, cs], _NT, preferred_element_type=F32) * scale

        s_p = jnp.where(prev_ok, scores(kp_ref), NEG_INF)
        s_c = scores(kc_ref)
        s_n = jnp.where(next_ok, scores(kn_ref), NEG_INF)
        s_x = scores(kx_ref)
        sink = jnp.concatenate([jnp.full((blk, 1), sink_ref[hd], F32) for hd in heads], axis=0)
        def lane_tiles(t):
            return [t[:, i:i + LANES] for i in range(0, t.shape[1], LANES)]

        tiles = [s_p, s_c, s_n] + lane_tiles(s_x)
        m = functools.reduce(jnp.maximum, tiles)
        m = jnp.maximum(jnp.max(m, -1, keepdims=True), sink)
        p_p, p_c, p_n, p_x = jnp.exp(s_p - m), jnp.exp(s_c - m), jnp.exp(s_n - m), jnp.exp(s_x - m)
        total = functools.reduce(jnp.add, [p_p, p_c, p_n] + lane_tiles(p_x))
        denom = jnp.sum(total, -1, keepdims=True) + jnp.exp(sink - m)

        def pv(p, v_ref):
            return jnp.dot(p.astype(BF16), v_ref[0, :, cs], preferred_element_type=F32)

        o = (pv(p_p, vp_ref) + pv(p_c, vc_ref) + pv(p_n, vn_ref) + pv(p_x, vx_ref)) / denom
        for g, hd in enumerate(heads):
            o_ref[0, :, hd * HEAD_DIM:(hd + 1) * HEAD_DIM] = o[g * blk:(g + 1) * blk].astype(BF16)


def _attention(q, k, v, k_ctx, v_ctx, sink):
    B, T, _ = q.shape
    L = k_ctx.shape[1]
    nb = T // BAND_BLOCK
    kv = lambda f: pl.BlockSpec((1, BAND_BLOCK, KV_WIDTH), lambda b, n: (b, f(n), 0))
    prev, cur, nxt = (lambda n: jnp.maximum(n - 1, 0)), (lambda n: n), (lambda n: jnp.minimum(n + 1, nb - 1))
    ctx = pl.BlockSpec((1, L, KV_WIDTH), lambda b, n: (b, 0, 0))
    return pl.pallas_call(
        _attn_kernel,
        grid=(B, nb),
        in_specs=[pl.BlockSpec(memory_space=pltpu.SMEM),
                  pl.BlockSpec((1, BAND_BLOCK, ATTN_WIDTH), lambda b, n: (b, n, 0)),
                  kv(prev), kv(cur), kv(nxt), kv(prev), kv(cur), kv(nxt), ctx, ctx],
        out_specs=pl.BlockSpec((1, BAND_BLOCK, ATTN_WIDTH), lambda b, n: (b, n, 0)),
        out_shape=jax.ShapeDtypeStruct((B, T, ATTN_WIDTH), BF16),
        compiler_params=_cparams("parallel", "parallel"),
        name="attn",
    )(sink, q, k, k, k, v, v, v, k_ctx, v_ctx)


_LOG2E = 1.4426950408889634
_DIAG = SUBLANES
_LEVELS = (64, 32, 16, 8)


def _hgrn_chunk(logit, v, q, lb, st_ref, reverse):
    C = HGRN_CHUNK
    f = lb + (1.0 - lb) * jax.nn.sigmoid(logit)
    lf = jnp.log(f)
    kk = 1.0 - f
    qf = q.astype(F32)
    r = lax.broadcasted_iota(I32, (C, C), 0)
    c = lax.broadcasted_iota(I32, (C, C), 1)
    incl = (r <= c) if reverse else (r >= c)
    a = jnp.dot(incl.astype(F32), lf, precision=lax.Precision.HIGHEST, preferred_element_type=F32) * _LOG2E
    a_end = a[0:1] if reverse else a[C - 1:C]

    st = st_ref[...]
    inter = lax.dot_general((qf * jnp.exp2(a)).astype(BF16), st.astype(BF16), _NT, preferred_element_type=F32)
    kd = (kk * jnp.exp2(a_end - a)).astype(BF16)
    st_ref[...] = st * jnp.exp2(a_end) + lax.dot_general(v, kd, _TN, preferred_element_type=F32)

    later = (r < c) if reverse else (r > c)
    att = jnp.zeros((C, C), F32)
    for m in _LEVELS:
        a3 = a.reshape(C // (2 * m), 2 * m, HGRN_D)
        edge = a3[:, m:m + 1, :] if reverse else a3[:, m - 1:m, :]
        e = jnp.exp2(-jnp.abs(a3 - edge)).reshape(C, HGRN_D)
        p = lax.dot_general((qf * e).astype(BF16), (kk * e).astype(BF16), _NT, preferred_element_type=F32)
        pair = ((r ^ c) >> (m.bit_length() - 1)) == 1
        att = jnp.where(pair & later, p, att)

    lane = lax.broadcasted_iota(I32, (_DIAG, C), 1)
    sub = lax.broadcasted_iota(I32, (_DIAG, C), 0)
    keep = [(lane == s) & ((sub <= s) if reverse else (sub >= s)) for s in range(_DIAG)]
    blocks = []
    for j in range(C // _DIAG):
        rows = slice(j * _DIAG, (j + 1) * _DIAG)
        a_j, q_j, k_j = a[rows], qf[rows], kk[rows]
        blk = jnp.zeros((_DIAG, C), F32)
        for s in range(_DIAG):
            e = jnp.exp2(a_j - a_j[s:s + 1])
            col = jnp.sum(q_j * e * k_j[s:s + 1], axis=-1, keepdims=True)
            blk = jnp.where(keep[s], col, blk)
        blocks.append(pltpu.roll(blk, j * _DIAG, 1) if j else blk)
    att = att + jnp.concatenate(blocks, axis=0)
    return inter + jnp.dot(att.astype(BF16), v, preferred_element_type=F32)


def _hgrn_kernel(ff_ref, vf_ref, qf_ref, fb_ref, vb_ref, qb_ref, lbf_ref, lbb_ref, s0f_ref, s0b_ref,
                 of_ref, ob_ref, sf_ref, sb_ref, stf, stb):
    step = pl.program_id(2)
    nsub = ff_ref.shape[1] // HGRN_CHUNK

    @pl.when(step == 0)
    def _():
        stf[...] = s0f_ref[0]
        stb[...] = s0b_ref[0]

    def body(j, carry):
        fo = pl.multiple_of(j * HGRN_CHUNK, HGRN_CHUNK)
        bo = pl.multiple_of((nsub - 1 - j) * HGRN_CHUNK, HGRN_CHUNK)
        for hh in range(HGRN_HEADS_PER_STEP):
            cs = slice(hh * HGRN_D, (hh + 1) * HGRN_D)
            rows = pl.ds(fo, HGRN_CHUNK)
            of_ref[0, rows, cs] = _hgrn_chunk(ff_ref[0, rows, cs], vf_ref[0, rows, cs], qf_ref[0, rows, cs],
                                              lbf_ref[:, cs], stf.at[hh], False)
            rows = pl.ds(bo, HGRN_CHUNK)
            ob_ref[0, rows, cs] = _hgrn_chunk(fb_ref[0, rows, cs], vb_ref[0, rows, cs], qb_ref[0, rows, cs],
                                              lbb_ref[:, cs], stb.at[hh], True)
        return carry

    lax.fori_loop(0, nsub, body, 0)

    @pl.when(step == pl.num_programs(2) - 1)
    def _():
        sf_ref[0] = stf[...]
        sb_ref[0] = stb[...]


def _hgrn(ff, fb, val, q, lb_f, lb_b, s0f, s0b):
    B, T, _ = ff.shape
    ts = min(HGRN_STEP, T)
    ns = T // ts
    hp = HGRN_HEADS_PER_STEP
    fwd = pl.BlockSpec((1, ts, hp * HGRN_D), lambda b, h, s: (b, s, h))
    bwd = pl.BlockSpec((1, ts, hp * HGRN_D), lambda b, h, s: (b, ns - 1 - s, h))
    lbs = pl.BlockSpec((1, hp * HGRN_D), lambda b, h, s: (0, h))
    st = pl.BlockSpec((1, hp, HGRN_D, HGRN_D), lambda b, h, s: (b, h, 0, 0))
    o_sds = jax.ShapeDtypeStruct((B, T, HGRN_WIDTH), F32)
    s_sds = jax.ShapeDtypeStruct((B, HGRN_HEADS, HGRN_D, HGRN_D), F32)
    return pl.pallas_call(
        _hgrn_kernel,
        grid=(B, HGRN_HEADS // hp, ns),
        in_specs=[fwd, fwd, fwd, bwd, bwd, bwd, lbs, lbs, st, st],
        out_specs=[fwd, bwd, st, st],
        out_shape=[o_sds, o_sds, s_sds, s_sds],
        scratch_shapes=[pltpu.VMEM((hp, HGRN_D, HGRN_D), F32), pltpu.VMEM((hp, HGRN_D, HGRN_D), F32)],
        compiler_params=_cparams("parallel", "parallel", "arbitrary"),
        name="hgrn",
    )(ff, val, q, fb, val, q, lb_f, lb_b, s0f, s0b)


def _outproj_kernel(attn_ref, of_ref, ob_ref, g_ref, x_ref, g1_ref, sh2_ref, sc2_ref, og_ref, nf_ref,
                    wo_ref, wq_ref, sk_ref, x1_ref, h2_ref, s_ref):
    o = of_ref[0] + ob_ref[0]
    og = og_ref[...]
    parts = []
    for hd in range(HGRN_HEADS):
        cs = slice(hd * HGRN_D, (hd + 1) * HGRN_D)
        gh = g_ref[0, :, cs].astype(F32)
        parts.append((_rms(o[:, cs], og) * (gh * jax.nn.sigmoid(gh))).astype(BF16))
    hg = jnp.concatenate(parts, axis=-1)
    mix = (jnp.dot(attn_ref[0], wo_ref[:ATTN_WIDTH], preferred_element_type=F32)
           + jnp.dot(hg, wo_ref[ATTN_WIDTH:], preferred_element_type=F32))
    x1 = x_ref[0] + g1_ref[0] * mix
    x1_ref[0] = x1
    h2 = _rms(x1, nf_ref[...]) * (1.0 + sc2_ref[0]) + sh2_ref[0]
    h2_ref[0] = _pack_bf16_pairs(h2)
    pq = jnp.dot(h2.astype(BF16), wq_ref[...], preferred_element_type=F32)
    for hp in range(2 * PEER_HEADS):
        cs = slice(hp * PEER_DHALF, (hp + 1) * PEER_DHALF)
        for jb in range(pq.shape[0] // LANES):
            rows = slice(jb * LANES, (jb + 1) * LANES)
            s_ref[hp, 0, jb * PEER_NKEYS:(jb + 1) * PEER_NKEYS, :] = lax.dot_general(
                sk_ref[hp], pq[rows, cs].astype(BF16), _NT, preferred_element_type=F32)


def _outproj(attn, of, ob, g, x, g1, sh2, sc2, o_gain, norm_ffn, wo_bf16, wq_bf16, sk_bf16):
    B, T, _ = x.shape
    tm = min(ROW_TILE, T)
    nt = T // tm
    per = TOPK_TOKENS // tm
    assert (B * T) % TOPK_TOKENS == 0 and TOPK_TOKENS % tm == 0 and tm % LANES == 0
    row = lambda w: pl.BlockSpec((1, tm, w), lambda b, t: (b, t, 0))
    vec = pl.BlockSpec((1, 1, D_MODEL), lambda b, t: (b, 0, 0))
    n_sk = 2 * PEER_HEADS
    sds = lambda w: jax.ShapeDtypeStruct((B, T, w), F32)
    s_rows = tm // LANES * PEER_NKEYS
    s_spec = pl.BlockSpec((n_sk, 1, s_rows, LANES), lambda b, t: (0, (b * nt + t) // per, (b * nt + t) % per, 0))
    s_sds = jax.ShapeDtypeStruct((n_sk, B * T // TOPK_TOKENS, TOPK_TOKENS // LANES * PEER_NKEYS, LANES), F32)
    return pl.pallas_call(
        _outproj_kernel,
        grid=(B, T // tm),
        in_specs=[row(ATTN_WIDTH), row(HGRN_WIDTH), row(HGRN_WIDTH), row(HGRN_WIDTH), row(D_MODEL),
                  vec, vec, vec, _resident((1, HGRN_D)), _resident((1, D_MODEL)),
                  _resident((ATTN_WIDTH + HGRN_WIDTH, D_MODEL)), _resident((D_MODEL, n_sk * PEER_DHALF)),
                  _resident((n_sk, PEER_NKEYS, PEER_DHALF))],
        out_specs=[row(D_MODEL), row(D_MODEL // 2), s_spec],
        out_shape=[sds(D_MODEL), jax.ShapeDtypeStruct((B, T, D_MODEL // 2), I32), s_sds],
        compiler_params=_cparams("parallel", "parallel"),
        name="outproj",
    )(attn, of, ob, g, x, g1, sh2, sc2, o_gain.reshape(1, HGRN_D), norm_ffn.reshape(1, D_MODEL),
      wo_bf16, wq_bf16, sk_bf16)


TOPK_TOKENS = SUBLANES * LANES
_CAND_PAIRS = [(a, b) for a in range(PEER_TOPK) for b in range(PEER_TOPK) if (a + 1) * (b + 1) <= PEER_TOPK]


def _first_argmax(values, ids, n_chains):
    per = -(-len(values) // n_chains)
    parts = []
    for lo in range(0, len(values), per):
        m, i = values[lo], ids[lo]
        if not isinstance(i, jax.Array):
            i = jnp.full(m.shape, i, F32)
        for v, vid in zip(values[lo + 1:lo + per], ids[lo + 1:lo + per]):
            c = v > m
            m = jnp.where(c, v, m)
            i = jnp.where(c, vid, i)
        parts.append((m, i))
    m, i = parts[0]
    for pm, pi in parts[1:]:
        c = pm > m
        m = jnp.where(c, pm, m)
        i = jnp.where(c, pi, i)
    return m, i


def _topk_kernel(s_ref, idx_ref, gate_ref, wk, tv, ti, cv, ci, bv):
    shape = (SUBLANES, LANES)
    none = jnp.full(shape, -1.0, F32)

    def head(h, carry):
        for p in range(2):
            hp = 2 * h + p
            for k in range(PEER_NKEYS):
                wk[k] = s_ref[hp, 0, pl.ds(k, SUBLANES, stride=PEER_NKEYS), :]

            def extract(r, prev, p=p):
                vals = []
                for k in range(PEER_NKEYS):
                    s = jnp.where(prev == float(k), NEG_INF, wk[k])
                    wk[k] = s
                    vals.append(s)
                m, i = _first_argmax(vals, [float(k) for k in range(PEER_NKEYS)], 4)
                tv[p, r] = m
                ti[p, r] = i
                return i

            lax.fori_loop(0, PEER_TOPK, extract, none)

        for c, (a, b) in enumerate(_CAND_PAIRS):
            cv[c] = tv[0, a] + tv[1, b]
            ci[c] = ti[0, a] * float(PEER_NKEYS) + ti[1, b]

        def pick(r, prev):
            vals, ids = [], []
            for c in range(len(_CAND_PAIRS)):
                cid = ci[c]
                s = jnp.where(cid == prev, NEG_INF, cv[c])
                cv[c] = s
                vals.append(s)
                ids.append(cid)
            m, i = _first_argmax(vals, ids, 2)
            bv[r] = m
            idx_ref[h * PEER_TOPK + r, 0] = i.astype(I32)
            return i

        lax.fori_loop(0, PEER_TOPK, pick, none)

        es = [jnp.exp(bv[r] - bv[0]) for r in range(PEER_TOPK)]
        tot = es[0]
        for e in es[1:]:
            tot = tot + e
        for r in range(PEER_TOPK):
            gate_ref[h * PEER_TOPK + r, 0] = es[r] / tot
        return carry

    lax.fori_loop(0, PEER_HEADS, head, 0)


def _topk(s):
    nt = s.shape[1]
    vreg = (SUBLANES, LANES)
    out_spec = pl.BlockSpec((PEER_SLOTS, 1) + vreg, lambda i: (0, i, 0, 0))
    return pl.pallas_call(
        _topk_kernel,
        grid=(nt,),
        in_specs=[pl.BlockSpec((s.shape[0], 1) + s.shape[2:], lambda i: (0, i, 0, 0))],
        out_specs=[out_spec, out_spec],
        out_shape=[jax.ShapeDtypeStruct((PEER_SLOTS, nt) + vreg, I32),
                   jax.ShapeDtypeStruct((PEER_SLOTS, nt) + vreg, F32)],
        scratch_shapes=[pltpu.VMEM((PEER_NKEYS,) + vreg, F32),
                        pltpu.VMEM((2, PEER_TOPK) + vreg, F32), pltpu.VMEM((2, PEER_TOPK) + vreg, F32),
                        pltpu.VMEM((len(_CAND_PAIRS),) + vreg, F32), pltpu.VMEM((len(_CAND_PAIRS),) + vreg, F32),
                        pltpu.VMEM((PEER_TOPK,) + vreg, F32)],
        compiler_params=_cparams("parallel"),
        name="topk",
    )(s)


SC_CORES = 2
SC_SUBCORES = 16
SC_LANES = 16
SC_WORKERS = SC_CORES * SC_SUBCORES
PEER_GROUP = 16
SC_BUFFERS = 4
PEER_ITEMS = PEER_GROUP * PEER_HEADS
SC_UNROLL = 8
PACKED_WORDS = D_MODEL // 2
EXPERT_SLAB = (PACKED_WORDS // LANES, LANES)


_SC_PARAMS = pltpu.CompilerParams(needs_layout_passes=False)


def _sc_mesh():
    return plsc.VectorSubcoreMesh(core_axis_name="c", subcore_axis_name="s")


def _sc_worker():
    return lax.axis_index("s") * SC_CORES + lax.axis_index("c")


def _pack_kernel(t_ref, o_ref):
    word = _pack_bf16_pairs(t_ref[...])
    for s in range(EXPERT_SLAB[0]):
        o_ref[:, s, :] = word[:, s * LANES:(s + 1) * LANES]


def _pack_table(table):
    e = table.shape[0]
    tr = 256
    return pl.pallas_call(
        _pack_kernel,
        grid=(e // tr,),
        in_specs=[pl.BlockSpec((tr, D_MODEL), lambda i: (i, 0))],
        out_specs=pl.BlockSpec((tr,) + EXPERT_SLAB, lambda i: (i, 0, 0)),
        out_shape=jax.ShapeDtypeStruct((e,) + EXPERT_SLAB, I32),
        compiler_params=_cparams("parallel"),
        name="pack_table",
    )(table)


def _row_words(rows, k, first=0, count=PEER_TOPK):
    per = LANES // SC_LANES
    sub, cols = k // per, pl.ds((k % per) * SC_LANES, SC_LANES)
    return tuple(rows[r, sub, cols] for r in range(first, first + count))


def _bf16_lanes(word):
    return plsc.bitcast(word, BF16)


def _f32_halves(pairs):
    return plsc.unpack(pairs, format=plsc.PackFormat.INTERLEAVED)


def _half_cols(k):
    return pl.ds(k * SC_LANES, SC_LANES), pl.ds(PACKED_WORDS + k * SC_LANES, SC_LANES)


def _sc_item_pipeline(table_hbm, idx_v, bufs, sems, compute):
    nb = len(bufs)

    def gather(j, b):
        return pltpu.make_async_copy(table_hbm.at[idx_v.at[j]], bufs[b], sems[b])

    for b in range(nb - 1):
        gather(b, b).start()

    @pl.loop(0, PEER_ITEMS // nb)
    def _(i):
        for b in range(nb):
            j = nb * i + b
            ahead = j + nb - 1

            @pl.when(ahead < PEER_ITEMS)
            def _():
                gather(ahead, (b + nb - 1) % nb).start()

            gather(j, b).wait()
            compute(j, bufs[b])


def _peer_dot_kernel(x_hbm, idx_hbm, u_hbm, a_hbm, x_v, idx_v, a_v, *ring):
    tok_per_w = x_hbm.shape[0] // SC_WORKERS
    wid = _sc_worker()
    lane = lax.iota(I32, SC_LANES)

    def compute(j, rows):
        t = j // PEER_HEADS

        zero = jnp.zeros((SC_LANES,), F32)

        half = PEER_TOPK // 2
        steps = PACKED_WORDS // SC_LANES // 2

        def load(kp, first):
            out = ()
            for k in (2 * kp, 2 * kp + 1):
                out += (x_v[t, pl.ds(k * SC_LANES, SC_LANES)],) + _row_words(rows, k, first, half)
            return out

        def fma(acc, vals):
            x0, x1 = _bf16_lanes(vals[0]), _bf16_lanes(vals[1 + half])
            out = []
            for i in range(half):
                lo, hi = _f32_halves(_bf16_lanes(vals[1 + i]) * x0 + _bf16_lanes(vals[2 + half + i]) * x1)
                out.append(acc[i] + lo + hi)
            return tuple(out)

        @plsc.parallel_loop(0, steps, unroll=8, carry=((zero,) * half, (zero,) * half, load(0, half)))
        def state(kp, state):
            acc_a, acc_b, vals_b = state
            vals_a = load(kp, 0)
            acc_b = fma(acc_b, vals_b)
            vals_b = load(jnp.minimum(kp + 1, steps - 1), half)
            return fma(acc_a, vals_a), acc_b, vals_b

        acc = state[0] + state[1]
        out = zero
        for r in range(PEER_TOPK):
            out = jnp.where(lane == r, jnp.sum(acc[r]), out)
        a_v[t, pl.ds((j % PEER_HEADS) * PEER_TOPK, PEER_TOPK)] = out

    @pl.loop(0, tok_per_w // PEER_GROUP)
    def _(g):
        tok0 = wid * tok_per_w + g * PEER_GROUP
        item0 = tok0 * PEER_HEADS
        pltpu.sync_copy(x_hbm.at[pl.ds(tok0, PEER_GROUP)], x_v)
        pltpu.sync_copy(idx_hbm.at[pl.ds(item0, PEER_ITEMS)], idx_v)
        _sc_item_pipeline(u_hbm, idx_v, ring[:SC_BUFFERS], ring[SC_BUFFERS:], compute)
        pltpu.sync_copy(a_v, a_hbm.at[pl.ds(tok0, PEER_GROUP)])


def _peer_sum_kernel(w_hbm, idx_hbm, v_hbm, o_hbm, w_v, idx_v, o_v, *ring):
    tok_per_w = o_hbm.shape[0] // SC_WORKERS
    wid = _sc_worker()
    zero = jnp.zeros((SC_LANES,), F32)

    def compute(j, rows):
        t = j // PEER_HEADS
        tv = jnp.full((SC_LANES,), t, I32)
        slot0 = (j % PEER_HEADS) * PEER_TOPK
        ws = []
        for r in range(PEER_TOPK):
            wr = plsc.load_gather(w_v, [tv, jnp.full((SC_LANES,), slot0 + r, I32)])
            ws.append(plsc.pack(wr, wr, format=plsc.PackFormat.INTERLEAVED))

        def load(k):
            lo, hi = _half_cols(k)
            return (o_v[t, lo], o_v[t, hi]) + _row_words(rows, k)

        def finish(k, vals):
            acc_lo, acc_hi = vals[0], vals[1]
            for r in range(0, PEER_TOPK, 2):
                lo, hi = _f32_halves(_bf16_lanes(vals[2 + r]) * ws[r] + _bf16_lanes(vals[3 + r]) * ws[r + 1])
                acc_lo = acc_lo + lo
                acc_hi = acc_hi + hi
            lo, hi = _half_cols(k)
            o_v[t, lo] = acc_lo
            o_v[t, hi] = acc_hi

        @plsc.parallel_loop(1, PACKED_WORDS // SC_LANES, unroll=SC_UNROLL, carry=load(0))
        def vals(k, vals):
            nxt = load(k)
            finish(k - 1, vals)
            return nxt

        finish(PACKED_WORDS // SC_LANES - 1, vals)

    @pl.loop(0, tok_per_w // PEER_GROUP)
    def _(g):
        tok0 = wid * tok_per_w + g * PEER_GROUP
        item0 = tok0 * PEER_HEADS
        pltpu.sync_copy(w_hbm.at[pl.ds(tok0, PEER_GROUP)], w_v)
        pltpu.sync_copy(idx_hbm.at[pl.ds(item0, PEER_ITEMS)], idx_v)

        @pl.loop(0, PEER_GROUP)
        def _(t):
            @plsc.parallel_loop(0, D_MODEL // SC_LANES, unroll=16)
            def _(k):
                o_v[t, pl.ds(k * SC_LANES, SC_LANES)] = zero

        _sc_item_pipeline(v_hbm, idx_v, ring[:SC_BUFFERS], ring[SC_BUFFERS:], compute)
        pltpu.sync_copy(o_v, o_hbm.at[pl.ds(tok0, PEER_GROUP)])


def _sc_scratch(first, last):
    return ([first, pltpu.VMEM((PEER_ITEMS, PEER_TOPK), I32), last]
            + [pltpu.VMEM((PEER_TOPK,) + EXPERT_SLAB, I32)] * SC_BUFFERS
            + [pltpu.SemaphoreType.DMA] * SC_BUFFERS)


def _peer_dot(h2, idx, u):
    n = h2.shape[0]
    assert n % (SC_WORKERS * PEER_GROUP) == 0
    call = pl.kernel(
        _peer_dot_kernel,
        out_type=jax.ShapeDtypeStruct((n, PEER_SLOTS), F32),
        mesh=_sc_mesh(),
        scratch_types=_sc_scratch(pltpu.VMEM((PEER_GROUP, PACKED_WORDS), I32),
                                  pltpu.VMEM((PEER_GROUP, PEER_SLOTS), F32)),
        compiler_params=_SC_PARAMS,
    )
    return call(h2, idx, u)


def _peer_sum(w, idx, v, n):
    assert n % (SC_WORKERS * PEER_GROUP) == 0
    call = pl.kernel(
        _peer_sum_kernel,
        out_type=jax.ShapeDtypeStruct((n, D_MODEL), F32),
        mesh=_sc_mesh(),
        scratch_types=_sc_scratch(pltpu.VMEM((PEER_GROUP, PEER_SLOTS), F32),
                                  pltpu.VMEM((PEER_GROUP, D_MODEL), F32)),
        compiler_params=_SC_PARAMS,
    )
    return call(w, idx, v)


def _gelu_gate_kernel(a_ref, g_ref, w_ref):
    a = a_ref[...]
    w_ref[...] = g_ref[...] * (0.5 * a * (1.0 + lax.erf(a * (2.0 ** -0.5))))


def _gelu_gate(a, gate):
    n = a.shape[0]
    tm = min(2048, n)
    spec = pl.BlockSpec((tm, PEER_SLOTS), lambda i: (i, 0))
    return pl.pallas_call(
        _gelu_gate_kernel, grid=(n // tm,), in_specs=[spec, spec], out_specs=spec,
        out_shape=jax.ShapeDtypeStruct(a.shape, F32), compiler_params=_cparams("parallel"), name="gelu_gate",
    )(a, gate)


def _residual_kernel(x_ref, g_ref, p_ref, o_ref):
    o_ref[0] = x_ref[0] + g_ref[0] * p_ref[0]


def _residual(x1, g2, peer):
    B, T, _ = x1.shape
    tm = min(2 * ROW_TILE, T)
    row = pl.BlockSpec((1, tm, D_MODEL), lambda b, t: (b, t, 0))
    vec = pl.BlockSpec((1, 1, D_MODEL), lambda b, t: (b, 0, 0))
    return pl.pallas_call(
        _residual_kernel, grid=(B, T // tm), in_specs=[row, vec, row], out_specs=row,
        out_shape=jax.ShapeDtypeStruct(x1.shape, F32), compiler_params=_cparams("parallel", "parallel"),
        name="residual",
    )(x1, g2, peer)


def kernel(x, c, ctx, c_ctx, w_ada, b_ada, norm_mix, norm_ffn, w_in, q_norm, k_norm, attn_sink, hgrn_lb_logits,
           hgrn_norm, w_out, peer_w_q, peer_sub_keys, peer_u, peer_v):
    assert w_ada.shape[0] == 1, "single-layer block"
    B, T, D = x.shape
    L = ctx.shape[1]
    n = B * T

    cvecs = jnp.zeros((SUBLANES, D), F32).at[:B].set(c).at[B].set(c_ctx)
    mod = _ada(cvecs, w_ada[0], b_ada[0])
    part = lambda rows, i: rows[:, None, i * D:(i + 1) * D]
    mod_x = mod[:B]
    mod_c = jnp.broadcast_to(mod[B:B + 1], (B, 6 * D))
    sh1, sc1, g1, sh2, sc2, g2 = (part(mod_x, i) for i in range(6))

    lbs = jnp.cumsum(jax.nn.softmax(hgrn_lb_logits.astype(F32), axis=1), axis=1)
    lb_f, lb_b = lbs[0, 0].reshape(1, HGRN_WIDTH), lbs[1, 0].reshape(1, HGRN_WIDTH)

    w_in_b = w_in[0].astype(BF16)
    rope = _rope_tables(T)
    kc, vc, ffc, fbc, ic, _, qhc, _ = _inproj(ctx, part(mod_c, 0), part(mod_c, 1), norm_mix[0], w_in_b,
                                              _identity_rope(L), q_norm[0], k_norm[0])
    s0 = jnp.zeros((B, HGRN_HEADS, HGRN_D, HGRN_D), F32)
    _, _, sfc, sbc = _hgrn(ffc, fbc, ic, qhc, lb_f, lb_b, s0, s0)

    sk = peer_sub_keys[0].reshape(2 * PEER_HEADS, PEER_NKEYS, PEER_DHALF).astype(BF16)
    w_out_b, w_q_b = w_out[0].astype(BF16), peer_w_q[0].astype(BF16)
    u_packed, v_packed = _pack_table(peer_u[0]), _pack_table(peer_v[0])

    kx, vx, ffx, fbx, ix, qx, qhx, gx = _inproj(x, sh1, sc1, norm_mix[0], w_in_b, rope, q_norm[0], k_norm[0])
    attn = _attention(qx, kx, vx, kc, vc, attn_sink[0])
    of, ob, _, _ = _hgrn(ffx, fbx, ix, qhx, lb_f, lb_b, sfc, sbc)
    x1, h2p, s = _outproj(attn, of, ob, gx, x, g1, sh2, sc2, hgrn_norm[0], norm_ffn[0], w_out_b, w_q_b, sk)
    idx_t, gate_t = _topk(s)
    idx16 = idx_t.reshape(PEER_SLOTS, n).T.reshape(n * PEER_HEADS, PEER_TOPK)
    gate = gate_t.reshape(PEER_SLOTS, n).T
    a = _peer_dot(h2p.reshape(n, PACKED_WORDS), idx16, u_packed)
    w = _gelu_gate(a, gate)
    peer = _peer_sum(w, idx16, v_packed, n)
    return _residual(x1, g2, peer.reshape(B, T, D))
```

```python
import functools

import jax
import jax.numpy as jnp
from jax import lax
from jax.experimental import pallas as pl
from jax.experimental.pallas import tpu as pltpu
from jax.experimental.pallas import tpu_sc as plsc

F32 = jnp.float32
BF16 = jnp.bfloat16
I32 = jnp.int32

D_MODEL = 2048
GRID_W = 64
EPS = 1e-6
HEAD_DIM = 128
ATTN_HEADS = 8
ATTN_KV_HEADS = 2
ATTN_GROUP = ATTN_HEADS // ATTN_KV_HEADS
BAND_BLOCK = 128
ROPE_THETA = 10000.0
HGRN_HEADS = 8
HGRN_D = 128
ATTN_WIDTH = ATTN_HEADS * HEAD_DIM
KV_WIDTH = ATTN_KV_HEADS * HEAD_DIM
HGRN_WIDTH = HGRN_HEADS * HGRN_D
COL_K = 0
COL_V = COL_K + KV_WIDTH
COL_FF = COL_V + KV_WIDTH
COL_FB = COL_FF + HGRN_WIDTH
COL_I = COL_FB + HGRN_WIDTH
COL_Q = COL_I + HGRN_WIDTH
COL_QH = COL_Q + ATTN_WIDTH
COL_G = COL_QH + HGRN_WIDTH
N_IN_COLS = COL_G + HGRN_WIDTH
PEER_HEADS = 8
PEER_NKEYS = 128
PEER_DHALF = 128
PEER_TOPK = 16
PEER_SLOTS = PEER_HEADS * PEER_TOPK

LANES = 128
SUBLANES = 8
VMEM_LIMIT_BYTES = 56 * 1024 * 1024

ROW_TILE = 256
HGRN_CHUNK = 128
HGRN_STEP = 512
HGRN_HEADS_PER_STEP = 8
NEG_INF = float("-inf")


def _cparams(*sem):
    return pltpu.CompilerParams(dimension_semantics=sem, vmem_limit_bytes=VMEM_LIMIT_BYTES)


def _resident(shape):
    nd = len(shape)
    return pl.BlockSpec(shape, lambda *_: (0,) * nd, pipeline_mode=pl.Buffered(1))


def _ada_kernel(c_ref, w_ref, b_ref, o_ref):
    c = c_ref[...]
    s = (c * jax.nn.sigmoid(c)).astype(BF16)
    o_ref[...] = jnp.dot(s, w_ref[...].astype(BF16), preferred_element_type=F32) + b_ref[...]


def _ada(cvecs, w, b):
    n = w.shape[1]
    tn = 1024
    return pl.pallas_call(
        _ada_kernel,
        grid=(n // tn,),
        in_specs=[pl.BlockSpec((SUBLANES, D_MODEL), lambda j: (0, 0)),
                  pl.BlockSpec((D_MODEL, tn), lambda j: (0, j)),
                  pl.BlockSpec((1, tn), lambda j: (0, j))],
        out_specs=pl.BlockSpec((SUBLANES, tn), lambda j: (0, j)),
        out_shape=jax.ShapeDtypeStruct((SUBLANES, n), F32),
        compiler_params=_cparams("arbitrary"),
        name="ada",
    )(cvecs, w, b.reshape(1, n))


def _pack_bf16_pairs(x):
    w = x.shape[1] // 2
    lo = pltpu.bitcast(x[:, :w].astype(BF16).astype(F32), jnp.uint32)
    hi = pltpu.bitcast(x[:, w:].astype(BF16).astype(F32), jnp.uint32)
    return pltpu.bitcast((hi & jnp.uint32(0xFFFF0000)) | (lo >> 16), I32)


def _rms(x, gain):
    return x * lax.rsqrt(jnp.mean(x * x, axis=-1, keepdims=True) + EPS) * gain


def _rope(x, cos, sin_a, sin_b):
    q = HEAD_DIM // 4
    return x * cos + pltpu.roll(x, HEAD_DIM - q, 1) * sin_a + pltpu.roll(x, q, 1) * sin_b


def _inproj_kernel(x_ref, sh_ref, sc_ref, gain_ref, w_ref, cos_ref, sa_ref, sb_ref, qg_ref, kg_ref,
                   k_ref, v_ref, ff_ref, fb_ref, i_ref, q_ref, qh_ref, g_ref):
    x = x_ref[0]
    h = _rms(x, gain_ref[...]) * (1.0 + sc_ref[0]) + sh_ref[0]
    hb = h.astype(BF16)

    def seg(lo, width):
        return jnp.dot(hb, w_ref[:, lo:lo + width], preferred_element_type=F32)

    cos, sa, sb = cos_ref[...], sa_ref[...], sb_ref[...]

    def normed_heads(p, gain, n_heads, out_ref):
        for hd in range(n_heads):
            ph = p[:, hd * HEAD_DIM:(hd + 1) * HEAD_DIM]
            out_ref[0, :, hd * HEAD_DIM:(hd + 1) * HEAD_DIM] = _rope(_rms(ph, gain), cos, sa, sb).astype(BF16)

    normed_heads(seg(COL_K, KV_WIDTH), kg_ref[...], ATTN_KV_HEADS, k_ref)
    v_ref[0] = seg(COL_V, KV_WIDTH).astype(BF16)
    ff_ref[0] = seg(COL_FF, HGRN_WIDTH)
    fb_ref[0] = seg(COL_FB, HGRN_WIDTH)
    i_ref[0] = seg(COL_I, HGRN_WIDTH).astype(BF16)
    normed_heads(seg(COL_Q, ATTN_WIDTH), qg_ref[...], ATTN_HEADS, q_ref)
    qh_ref[0] = seg(COL_QH, HGRN_WIDTH).astype(BF16)
    g_ref[0] = seg(COL_G, HGRN_WIDTH).astype(BF16)


def _inproj(x, shift, scale, gain, w_bf16, rope, q_gain, k_gain):
    B, T, _ = x.shape
    tm = min(ROW_TILE, T)
    row = lambda w: pl.BlockSpec((1, tm, w), lambda b, t: (b, t, 0))
    vec = pl.BlockSpec((1, 1, D_MODEL), lambda b, t: (b, 0, 0))
    tab = pl.BlockSpec((tm, HEAD_DIM), lambda b, t: (t, 0))
    out_w = [(KV_WIDTH, BF16), (KV_WIDTH, BF16), (HGRN_WIDTH, F32), (HGRN_WIDTH, F32), (HGRN_WIDTH, BF16),
             (ATTN_WIDTH, BF16), (HGRN_WIDTH, BF16), (HGRN_WIDTH, BF16)]
    return pl.pallas_call(
        _inproj_kernel,
        grid=(B, T // tm),
        in_specs=[row(D_MODEL), vec, vec, _resident((1, D_MODEL)), _resident((D_MODEL, N_IN_COLS)),
                  tab, tab, tab, _resident((1, HEAD_DIM)), _resident((1, HEAD_DIM))],
        out_specs=[row(w) for w, _ in out_w],
        out_shape=[jax.ShapeDtypeStruct((B, T, w), dt) for w, dt in out_w],
        compiler_params=_cparams("parallel", "parallel"),
        name="inproj",
    )(x, shift, scale, gain.reshape(1, D_MODEL), w_bf16, *rope,
      q_gain.reshape(1, HEAD_DIM), k_gain.reshape(1, HEAD_DIM))


def _rope_tables(T):
    rows = T // GRID_W
    row_pos = jnp.repeat(jnp.arange(rows, dtype=F32), GRID_W)
    col_pos = jnp.tile(jnp.arange(GRID_W, dtype=F32), rows)
    half = HEAD_DIM // 2
    inv_freq = jnp.power(ROPE_THETA, -jnp.arange(0, half, 2, dtype=F32) / half)
    ang_r = row_pos[:, None] * inv_freq
    ang_c = col_pos[:, None] * inv_freq
    cr, sr, cc, sc = jnp.cos(ang_r), jnp.sin(ang_r), jnp.cos(ang_c), jnp.sin(ang_c)
    z = jnp.zeros_like(sr)
    return (jnp.concatenate([cr, cr, cc, cc], -1),
            jnp.concatenate([-sr, z, -sc, z], -1),
            jnp.concatenate([z, sr, z, sc], -1))


def _identity_rope(T):
    return (jnp.ones((T, HEAD_DIM), F32), jnp.zeros((T, HEAD_DIM), F32), jnp.zeros((T, HEAD_DIM), F32))


_NT = (((1,), (1,)), ((), ()))
_TN = (((0,), (0,)), ((), ()))


def _attn_kernel(sink_ref, q_ref, kp_ref, kc_ref, kn_ref, vp_ref, vc_ref, vn_ref, kx_ref, vx_ref, o_ref):
    n = pl.program_id(1)
    nb = pl.num_programs(1)
    blk = BAND_BLOCK
    rows = ATTN_GROUP * blk
    qi = lax.broadcasted_iota(I32, (rows, blk), 0) & (blk - 1)
    kj = lax.broadcasted_iota(I32, (rows, blk), 1)
    prev_ok = kj >= qi + jnp.where(n > 0, 0, blk)
    next_ok = kj <= qi - jnp.where(n < nb - 1, 0, blk)
    scale = HEAD_DIM ** -0.5
    for h in range(ATTN_KV_HEADS):
        cs = slice(h * HEAD_DIM, (h + 1) * HEAD_DIM)
        heads = [h * ATTN_GROUP + g for g in range(ATTN_GROUP)]
        q4 = jnp.concatenate([q_ref[0, :, hd * HEAD_DIM:(hd + 1) * HEAD_DIM] for hd in heads], axis=0)

        def scores(k_ref):
            return lax.dot_general(q4, k_ref[0, :, cs], _NT, preferred_element_type=F32) * scale

        s_p = jnp.where(prev_ok, scores(kp_ref), NEG_INF)
        s_c = scores(kc_ref)
        s_n = jnp.where(next_ok, scores(kn_ref), NEG_INF)
        s_x = scores(kx_ref)
        sink = jnp.concatenate([jnp.full((blk, 1), sink_ref[hd], F32) for hd in heads], axis=0)
        def lane_tiles(t):
            return [t[:, i:i + LANES] for i in range(0, t.shape[1], LANES)]

        tiles = [s_p, s_c, s_n] + lane_tiles(s_x)
        m = functools.reduce(jnp.maximum, tiles)
        m = jnp.maximum(jnp.max(m, -1, keepdims=True), sink)
        p_p, p_c, p_n, p_x = jnp.exp(s_p - m), jnp.exp(s_c - m), jnp.exp(s_n - m), jnp.exp(s_x - m)
        total = functools.reduce(jnp.add, [p_p, p_c, p_n] + lane_tiles(p_x))
        denom = jnp.sum(total, -1, keepdims=True) + jnp.exp(sink - m)

        def pv(p, v_ref):
            return jnp.dot(p.astype(BF16), v_ref[0, :, cs], preferred_element_type=F32)

        o = (pv(p_p, vp_ref) + pv(p_c, vc_ref) + pv(p_n, vn_ref) + pv(p_x, vx_ref)) / denom
        for g, hd in enumerate(heads):
            o_ref[0, :, hd * HEAD_DIM:(hd + 1) * HEAD_DIM] = o[g * blk:(g + 1) * blk].astype(BF16)


def _attention(q, k, v, k_ctx, v_ctx, sink):
    B, T, _ = q.shape
    L = k_ctx.shape[1]
    nb = T // BAND_BLOCK
    kv = lambda f: pl.BlockSpec((1, BAND_BLOCK, KV_WIDTH), lambda b, n: (b, f(n), 0))
    prev, cur, nxt = (lambda n: jnp.maximum(n - 1, 0)), (lambda n: n), (lambda n: jnp.minimum(n + 1, nb - 1))
    ctx = pl.BlockSpec((1, L, KV_WIDTH), lambda b, n: (b, 0, 0))
    return pl.pallas_call(
        _attn_kernel,
        grid=(B, nb),
        in_specs=[pl.BlockSpec(memory_space=pltpu.SMEM),
                  pl.BlockSpec((1, BAND_BLOCK, ATTN_WIDTH), lambda b, n: (b, n, 0)),
                  kv(prev), kv(cur), kv(nxt), kv(prev), kv(cur), kv(nxt), ctx, ctx],
        out_specs=pl.BlockSpec((1, BAND_BLOCK, ATTN_WIDTH), lambda b, n: (b, n, 0)),
        out_shape=jax.ShapeDtypeStruct((B, T, ATTN_WIDTH), BF16),
        compiler_params=_cparams("parallel", "parallel"),
        name="attn",
    )(sink, q, k, k, k, v, v, v, k_ctx, v_ctx)


_LOG2E = 1.4426950408889634
_DIAG = SUBLANES
_LEVELS = (64, 32, 16, 8)


def _hgrn_chunk(logit, v, q, lb, st_ref, reverse):
    C = HGRN_CHUNK
    f = lb + (1.0 - lb) * jax.nn.sigmoid(logit)
    lf = jnp.log(f)
    kk = 1.0 - f
    qf = q.astype(F32)
    r = lax.broadcasted_iota(I32, (C, C), 0)
    c = lax.broadcasted_iota(I32, (C, C), 1)
    incl = (r <= c) if reverse else (r >= c)
    a = jnp.dot(incl.astype(F32), lf, precision=lax.Precision.HIGHEST, preferred_element_type=F32) * _LOG2E
    a_end = a[0:1] if reverse else a[C - 1:C]

    st = st_ref[...]
    inter = lax.dot_general((qf * jnp.exp2(a)).astype(BF16), st.astype(BF16), _NT, preferred_element_type=F32)
    kd = (kk * jnp.exp2(a_end - a)).astype(BF16)
    st_ref[...] = st * jnp.exp2(a_end) + lax.dot_general(v, kd, _TN, preferred_element_type=F32)

    later = (r < c) if reverse else (r > c)
    att = jnp.zeros((C, C), F32)
    for m in _LEVELS:
        a3 = a.reshape(C // (2 * m), 2 * m, HGRN_D)
        edge = a3[:, m:m + 1, :] if reverse else a3[:, m - 1:m, :]
        e = jnp.exp2(-jnp.abs(a3 - edge)).reshape(C, HGRN_D)
        p = lax.dot_general((qf * e).astype(BF16), (kk * e).astype(BF16), _NT, preferred_element_type=F32)
        pair = ((r ^ c) >> (m.bit_length() - 1)) == 1
        att = jnp.where(pair & later, p, att)

    lane = lax.broadcasted_iota(I32, (_DIAG, C), 1)
    sub = lax.broadcasted_iota(I32, (_DIAG, C), 0)
    keep = [(lane == s) & ((sub <= s) if reverse else (sub >= s)) for s in range(_DIAG)]
    blocks = []
    for j in range(C // _DIAG):
        rows = slice(j * _DIAG, (j + 1) * _DIAG)
        a_j, q_j, k_j = a[rows], qf[rows], kk[rows]
        blk = jnp.zeros((_DIAG, C), F32)
        for s in range(_DIAG):
            e = jnp.exp2(a_j - a_j[s:s + 1])
            col = jnp.sum(q_j * e * k_j[s:s + 1], axis=-1, keepdims=True)
            blk = jnp.where(keep[s], col, blk)
        blocks.append(pltpu.roll(blk, j * _DIAG, 1) if j else blk)
    att = att + jnp.concatenate(blocks, axis=0)
    return inter + jnp.dot(att.astype(BF16), v, preferred_element_type=F32)


def _hgrn_kernel(ff_ref, vf_ref, qf_ref, fb_ref, vb_ref, qb_ref, lbf_ref, lbb_ref, s0f_ref, s0b_ref,
                 of_ref, ob_ref, sf_ref, sb_ref, stf, stb):
    step = pl.program_id(2)
    nsub = ff_ref.shape[1] // HGRN_CHUNK

    @pl.when(step == 0)
    def _():
        stf[...] = s0f_ref[0]
        stb[...] = s0b_ref[0]

    def body(j, carry):
        fo = pl.multiple_of(j * HGRN_CHUNK, HGRN_CHUNK)
        bo = pl.multiple_of((nsub - 1 - j) * HGRN_CHUNK, HGRN_CHUNK)
        for hh in range(HGRN_HEADS_PER_STEP):
            cs = slice(hh * HGRN_D, (hh + 1) * HGRN_D)
            rows = pl.ds(fo, HGRN_CHUNK)
            of_ref[0, rows, cs] = _hgrn_chunk(ff_ref[0, rows, cs], vf_ref[0, rows, cs], qf_ref[0, rows, cs],
                                              lbf_ref[:, cs], stf.at[hh], False)
            rows = pl.ds(bo, HGRN_CHUNK)
            ob_ref[0, rows, cs] = _hgrn_chunk(fb_ref[0, rows, cs], vb_ref[0, rows, cs], qb_ref[0, rows, cs],
                                              lbb_ref[:, cs], stb.at[hh], True)
        return carry

    lax.fori_loop(0, nsub, body, 0)

    @pl.when(step == pl.num_programs(2) - 1)
    def _():
        sf_ref[0] = stf[...]
        sb_ref[0] = stb[...]


def _hgrn(ff, fb, val, q, lb_f, lb_b, s0f, s0b):
    B, T, _ = ff.shape
    ts = min(HGRN_STEP, T)
    ns = T // ts
    hp = HGRN_HEADS_PER_STEP
    fwd = pl.BlockSpec((1, ts, hp * HGRN_D), lambda b, h, s: (b, s, h))
    bwd = pl.BlockSpec((1, ts, hp * HGRN_D), lambda b, h, s: (b, ns - 1 - s, h))
    lbs = pl.BlockSpec((1, hp * HGRN_D), lambda b, h, s: (0, h))
    st = pl.BlockSpec((1, hp, HGRN_D, HGRN_D), lambda b, h, s: (b, h, 0, 0))
    o_sds = jax.ShapeDtypeStruct((B, T, HGRN_WIDTH), F32)
    s_sds = jax.ShapeDtypeStruct((B, HGRN_HEADS, HGRN_D, HGRN_D), F32)
    return pl.pallas_call(
        _hgrn_kernel,
        grid=(B, HGRN_HEADS // hp, ns),
        in_specs=[fwd, fwd, fwd, bwd, bwd, bwd, lbs, lbs, st, st],
        out_specs=[fwd, bwd, st, st],
        out_shape=[o_sds, o_sds, s_sds, s_sds],
        scratch_shapes=[pltpu.VMEM((hp, HGRN_D, HGRN_D), F32), pltpu.VMEM((hp, HGRN_D, HGRN_D), F32)],
        compiler_params=_cparams("parallel", "parallel", "arbitrary"),
        name="hgrn",
    )(ff, val, q, fb, val, q, lb_f, lb_b, s0f, s0b)


def _outproj_kernel(attn_ref, of_ref, ob_ref, g_ref, x_ref, g1_ref, sh2_ref, sc2_ref, og_ref, nf_ref,
                    wo_ref, wq_ref, sk_ref, x1_ref, h2_ref, s_ref):
    o = of_ref[0] + ob_ref[0]
    og = og_ref[...]
    parts = []
    for hd in range(HGRN_HEADS):
        cs = slice(hd * HGRN_D, (hd + 1) * HGRN_D)
        gh = g_ref[0, :, cs].astype(F32)
        parts.append((_rms(o[:, cs], og) * (gh * jax.nn.sigmoid(gh))).astype(BF16))
    hg = jnp.concatenate(parts, axis=-1)
    mix = (jnp.dot(attn_ref[0], wo_ref[:ATTN_WIDTH], preferred_element_type=F32)
           + jnp.dot(hg, wo_ref[ATTN_WIDTH:], preferred_element_type=F32))
    x1 = x_ref[0] + g1_ref[0] * mix
    x1_ref[0] = x1
    h2 = _rms(x1, nf_ref[...]) * (1.0 + sc2_ref[0]) + sh2_ref[0]
    h2_ref[0] = _pack_bf16_pairs(h2)
    pq = jnp.dot(h2.astype(BF16), wq_ref[...], preferred_element_type=F32)
    for hp in range(2 * PEER_HEADS):
        cs = slice(hp * PEER_DHALF, (hp + 1) * PEER_DHALF)
        for jb in range(pq.shape[0] // LANES):
            rows = slice(jb * LANES, (jb + 1) * LANES)
            s_ref[hp, 0, jb * PEER_NKEYS:(jb + 1) * PEER_NKEYS, :] = lax.dot_general(
                sk_ref[hp], pq[rows, cs].astype(BF16), _NT, preferred_element_type=F32)


def _outproj(attn, of, ob, g, x, g1, sh2, sc2, o_gain, norm_ffn, wo_bf16, wq_bf16, sk_bf16):
    B, T, _ = x.shape
    tm = min(ROW_TILE, T)
    nt = T // tm
    per = TOPK_TOKENS // tm
    assert (B * T) % TOPK_TOKENS == 0 and TOPK_TOKENS % tm == 0 and tm % LANES == 0
    row = lambda w: pl.BlockSpec((1, tm, w), lambda b, t: (b, t, 0))
    vec = pl.BlockSpec((1, 1, D_MODEL), lambda b, t: (b, 0, 0))
    n_sk = 2 * PEER_HEADS
    sds = lambda w: jax.ShapeDtypeStruct((B, T, w), F32)
    s_rows = tm // LANES * PEER_NKEYS
    s_spec = pl.BlockSpec((n_sk, 1, s_rows, LANES), lambda b, t: (0, (b * nt + t) // per, (b * nt + t) % per, 0))
    s_sds = jax.ShapeDtypeStruct((n_sk, B * T // TOPK_TOKENS, TOPK_TOKENS // LANES * PEER_NKEYS, LANES), F32)
    return pl.pallas_call(
        _outproj_kernel,
        grid=(B, T // tm),
        in_specs=[row(ATTN_WIDTH), row(HGRN_WIDTH), row(HGRN_WIDTH), row(HGRN_WIDTH), row(D_MODEL),
                  vec, vec, vec, _resident((1, HGRN_D)), _resident((1, D_MODEL)),
                  _resident((ATTN_WIDTH + HGRN_WIDTH, D_MODEL)), _resident((D_MODEL, n_sk * PEER_DHALF)),
                  _resident((n_sk, PEER_NKEYS, PEER_DHALF))],
        out_specs=[row(D_MODEL), row(D_MODEL // 2), s_spec],
        out_shape=[sds(D_MODEL), jax.ShapeDtypeStruct((B, T, D_MODEL // 2), I32), s_sds],
        compiler_params=_cparams("parallel", "parallel"),
        name="outproj",
    )(attn, of, ob, g, x, g1, sh2, sc2, o_gain.reshape(1, HGRN_D), norm_ffn.reshape(1, D_MODEL),
      wo_bf16, wq_bf16, sk_bf16)


TOPK_TOKENS = SUBLANES * LANES
_CAND_PAIRS = [(a, b) for a in range(PEER_TOPK) for b in range(PEER_TOPK) if (a + 1) * (b + 1) <= PEER_TOPK]


def _first_argmax(values, ids, n_chains):
    per = -(-len(values) // n_chains)
    parts = []
    for lo in range(0, len(values), per):
        m, i = values[lo], ids[lo]
        if not isinstance(i, jax.Array):
            i = jnp.full(m.shape, i, F32)
        for v, vid in zip(values[lo + 1:lo + per], ids[lo + 1:lo + per]):
            c = v > m
            m = jnp.where(c, v, m)
            i = jnp.where(c, vid, i)
        parts.append((m, i))
    m, i = parts[0]
    for pm, pi in parts[1:]:
        c = pm > m
        m = jnp.where(c, pm, m)
        i = jnp.where(c, pi, i)
    return m, i


def _topk_kernel(s_ref, idx_ref, gate_ref, wk, tv, ti, cv, ci, bv):
    shape = (SUBLANES, LANES)
    none = jnp.full(shape, -1.0, F32)

    def head(h, carry):
        for p in range(2):
            hp = 2 * h + p
            for k in range(PEER_NKEYS):
                wk[k] = s_ref[hp, 0, pl.ds(k, SUBLANES, stride=PEER_NKEYS), :]

            def extract(r, prev, p=p):
                vals = []
                for k in range(PEER_NKEYS):
                    s = jnp.where(prev == float(k), NEG_INF, wk[k])
                    wk[k] = s
                    vals.append(s)
                m, i = _first_argmax(vals, [float(k) for k in range(PEER_NKEYS)], 4)
                tv[p, r] = m
                ti[p, r] = i
                return i

            lax.fori_loop(0, PEER_TOPK, extract, none)

        for c, (a, b) in enumerate(_CAND_PAIRS):
            cv[c] = tv[0, a] + tv[1, b]
            ci[c] = ti[0, a] * float(PEER_NKEYS) + ti[1, b]

        def pick(r, prev):
            vals, ids = [], []
            for c in range(len(_CAND_PAIRS)):
                cid = ci[c]
                s = jnp.where(cid == prev, NEG_INF, cv[c])
                cv[c] = s
                vals.append(s)
                ids.append(cid)
            m, i = _first_argmax(vals, ids, 2)
            bv[r] = m
            idx_ref[h * PEER_TOPK + r, 0] = i.astype(I32)
            return i

        lax.fori_loop(0, PEER_TOPK, pick, none)

        es = [jnp.exp(bv[r] - bv[0]) for r in range(PEER_TOPK)]
        tot = es[0]
        for e in es[1:]:
            tot = tot + e
        for r in range(PEER_TOPK):
            gate_ref[h * PEER_TOPK + r, 0] = es[r] / tot
        return carry

    lax.fori_loop(0, PEER_HEADS, head, 0)


def _topk(s):
    nt = s.shape[1]
    vreg = (SUBLANES, LANES)
    out_spec = pl.BlockSpec((PEER_SLOTS, 1) + vreg, lambda i: (0, i, 0, 0))
    return pl.pallas_call(
        _topk_kernel,
        grid=(nt,),
        in_specs=[pl.BlockSpec((s.shape[0], 1) + s.shape[2:], lambda i: (0, i, 0, 0))],
        out_specs=[out_spec, out_spec],
        out_shape=[jax.ShapeDtypeStruct((PEER_SLOTS, nt) + vreg, I32),
                   jax.ShapeDtypeStruct((PEER_SLOTS, nt) + vreg, F32)],
        scratch_shapes=[pltpu.VMEM((PEER_NKEYS,) + vreg, F32),
                        pltpu.VMEM((2, PEER_TOPK) + vreg, F32), pltpu.VMEM((2, PEER_TOPK) + vreg, F32),
                        pltpu.VMEM((len(_CAND_PAIRS),) + vreg, F32), pltpu.VMEM((len(_CAND_PAIRS),) + vreg, F32),
                        pltpu.VMEM((PEER_TOPK,) + vreg, F32)],
        compiler_params=_cparams("parallel"),
        name="topk",
    )(s)


SC_CORES = 2
SC_SUBCORES = 16
SC_LANES = 16
SC_WORKERS = SC_CORES * SC_SUBCORES
PEER_GROUP = 16
SC_BUFFERS = 4
PEER_ITEMS = PEER_GROUP * PEER_HEADS
SC_UNROLL = 16
PACKED_WORDS = D_MODEL // 2
EXPERT_SLAB = (PACKED_WORDS // LANES, LANES)


_SC_PARAMS = pltpu.CompilerParams(needs_layout_passes=False)


def _sc_mesh():
    return plsc.VectorSubcoreMesh(core_axis_name="c", subcore_axis_name="s")


def _sc_worker():
    return lax.axis_index("s") * SC_CORES + lax.axis_index("c")


def _pack_kernel(t_ref, o_ref):
    word = _pack_bf16_pairs(t_ref[...])
    for s in range(EXPERT_SLAB[0]):
        o_ref[:, s, :] = word[:, s * LANES:(s + 1) * LANES]


def _pack_table(table):
    e = table.shape[0]
    tr = 256
    return pl.pallas_call(
        _pack_kernel,
        grid=(e // tr,),
        in_specs=[pl.BlockSpec((tr, D_MODEL), lambda i: (i, 0))],
        out_specs=pl.BlockSpec((tr,) + EXPERT_SLAB, lambda i: (i, 0, 0)),
        out_shape=jax.ShapeDtypeStruct((e,) + EXPERT_SLAB, I32),
        compiler_params=_cparams("parallel"),
        name="pack_table",
    )(table)


def _row_words(rows, k, first=0, count=PEER_TOPK):
    per = LANES // SC_LANES
    sub, cols = k // per, pl.ds((k % per) * SC_LANES, SC_LANES)
    return tuple(rows[r, sub, cols] for r in range(first, first + count))


def _bf16_lanes(word):
    return plsc.bitcast(word, BF16)


def _f32_halves(pairs):
    return plsc.unpack(pairs, format=plsc.PackFormat.INTERLEAVED)


def _half_cols(k):
    return pl.ds(k * SC_LANES, SC_LANES), pl.ds(PACKED_WORDS + k * SC_LANES, SC_LANES)


def _sc_item_pipeline(table_hbm, idx_v, bufs, sems, compute):
    nb = len(bufs)

    def gather(j, b):
        return pltpu.make_async_copy(table_hbm.at[idx_v.at[j]], bufs[b], sems[b])

    for b in range(nb - 1):
        gather(b, b).start()

    @pl.loop(0, PEER_ITEMS // nb)
    def _(i):
        for b in range(nb):
            j = nb * i + b
            ahead = j + nb - 1

            @pl.when(ahead < PEER_ITEMS)
            def _():
                gather(ahead, (b + nb - 1) % nb).start()

            gather(j, b).wait()
            compute(j, bufs[b])


def _peer_dot_kernel(x_hbm, idx_hbm, u_hbm, a_hbm, x_v, idx_v, a_v, *ring):
    tok_per_w = x_hbm.shape[0] // SC_WORKERS
    wid = _sc_worker()
    lane = lax.iota(I32, SC_LANES)

    def compute(j, rows):
        t = j // PEER_HEADS

        zero = jnp.zeros((SC_LANES,), F32)

        half = PEER_TOPK // 2
        steps = PACKED_WORDS // SC_LANES // 2

        def load(kp, first):
            out = ()
            for k in (2 * kp, 2 * kp + 1):
                out += (x_v[t, pl.ds(k * SC_LANES, SC_LANES)],) + _row_words(rows, k, first, half)
            return out

        def fma(acc, vals):
            x0, x1 = _bf16_lanes(vals[0]), _bf16_lanes(vals[1 + half])
            out = []
            for i in range(half):
                lo, hi = _f32_halves(_bf16_lanes(vals[1 + i]) * x0 + _bf16_lanes(vals[2 + half + i]) * x1)
                out.append(acc[i] + lo + hi)
            return tuple(out)

        @plsc.parallel_loop(0, steps, unroll=16, carry=((zero,) * half, (zero,) * half, load(0, half)))
        def state(kp, state):
            acc_a, acc_b, vals_b = state
            vals_a = load(kp, 0)
            acc_b = fma(acc_b, vals_b)
            vals_b = load(jnp.minimum(kp + 1, steps - 1), half)
            return fma(acc_a, vals_a), acc_b, vals_b

        acc = state[0] + state[1]
        out = zero
        for r in range(PEER_TOPK):
            out = jnp.where(lane == r, jnp.sum(acc[r]), out)
        a_v[t, pl.ds((j % PEER_HEADS) * PEER_TOPK, PEER_TOPK)] = out

    @pl.loop(0, tok_per_w // PEER_GROUP)
    def _(g):
        tok0 = wid * tok_per_w + g * PEER_GROUP
        item0 = tok0 * PEER_HEADS
        pltpu.sync_copy(x_hbm.at[pl.ds(tok0, PEER_GROUP)], x_v)
        pltpu.sync_copy(idx_hbm.at[pl.ds(item0, PEER_ITEMS)], idx_v)
        _sc_item_pipeline(u_hbm, idx_v, ring[:SC_BUFFERS], ring[SC_BUFFERS:], compute)
        pltpu.sync_copy(a_v, a_hbm.at[pl.ds(tok0, PEER_GROUP)])


def _peer_sum_kernel(w_hbm, idx_hbm, v_hbm, o_hbm, w_v, idx_v, o_v, *ring):
    tok_per_w = o_hbm.shape[0] // SC_WORKERS
    wid = _sc_worker()
    zero = jnp.zeros((SC_LANES,), F32)

    def compute(j, rows):
        t = j // PEER_HEADS
        tv = jnp.full((SC_LANES,), t, I32)
        slot0 = (j % PEER_HEADS) * PEER_TOPK
        ws = []
        for r in range(PEER_TOPK):
            wr = plsc.load_gather(w_v, [tv, jnp.full((SC_LANES,), slot0 + r, I32)])
            ws.append(plsc.pack(wr, wr, format=plsc.PackFormat.INTERLEAVED))

        def load(k):
            lo, hi = _half_cols(k)
            return (o_v[t, lo], o_v[t, hi]) + _row_words(rows, k)

        def finish(k, vals):
            acc_lo, acc_hi = vals[0], vals[1]
            for r in range(0, PEER_TOPK, 2):
                lo, hi = _f32_halves(_bf16_lanes(vals[2 + r]) * ws[r] + _bf16_lanes(vals[3 + r]) * ws[r + 1])
                acc_lo = acc_lo + lo
                acc_hi = acc_hi + hi
            lo, hi = _half_cols(k)
            o_v[t, lo] = acc_lo
            o_v[t, hi] = acc_hi

        @plsc.parallel_loop(1, PACKED_WORDS // SC_LANES, unroll=SC_UNROLL, carry=load(0))
        def vals(k, vals):
            nxt = load(k)
            finish(k - 1, vals)
            return nxt

        finish(PACKED_WORDS // SC_LANES - 1, vals)

    @pl.loop(0, tok_per_w // PEER_GROUP)
    def _(g):
        tok0 = wid * tok_per_w + g * PEER_GROUP
        item0 = tok0 * PEER_HEADS
        pltpu.sync_copy(w_hbm.at[pl.ds(tok0, PEER_GROUP)], w_v)
        pltpu.sync_copy(idx_hbm.at[pl.ds(item0, PEER_ITEMS)], idx_v)

        @pl.loop(0, PEER_GROUP)
        def _(t):
            @plsc.parallel_loop(0, D_MODEL // SC_LANES, unroll=16)
            def _(k):
                o_v[t, pl.ds(k * SC_LANES, SC_LANES)] = zero

        _sc_item_pipeline(v_hbm, idx_v, ring[:SC_BUFFERS], ring[SC_BUFFERS:], compute)
        pltpu.sync_copy(o_v, o_hbm.at[pl.ds(tok0, PEER_GROUP)])


def _sc_scratch(first, last):
    return ([first, pltpu.VMEM((PEER_ITEMS, PEER_TOPK), I32), last]
            + [pltpu.VMEM((PEER_TOPK,) + EXPERT_SLAB, I32)] * SC_BUFFERS
            + [pltpu.SemaphoreType.DMA] * SC_BUFFERS)


def _peer_dot(h2, idx, u):
    n = h2.shape[0]
    assert n % (SC_WORKERS * PEER_GROUP) == 0
    call = pl.kernel(
        _peer_dot_kernel,
        out_type=jax.ShapeDtypeStruct((n, PEER_SLOTS), F32),
        mesh=_sc_mesh(),
        scratch_types=_sc_scratch(pltpu.VMEM((PEER_GROUP, PACKED_WORDS), I32),
                                  pltpu.VMEM((PEER_GROUP, PEER_SLOTS), F32)),
        compiler_params=_SC_PARAMS,
    )
    return call(h2, idx, u)


def _peer_sum(w, idx, v, n):
    assert n % (SC_WORKERS * PEER_GROUP) == 0
    call = pl.kernel(
        _peer_sum_kernel,
        out_type=jax.ShapeDtypeStruct((n, D_MODEL), F32),
        mesh=_sc_mesh(),
        scratch_types=_sc_scratch(pltpu.VMEM((PEER_GROUP, PEER_SLOTS), F32),
                                  pltpu.VMEM((PEER_GROUP, D_MODEL), F32)),
        compiler_params=_SC_PARAMS,
    )
    return call(w, idx, v)


def _gelu_gate_kernel(a_ref, g_ref, w_ref):
    a = a_ref[...]
    w_ref[...] = g_ref[...] * (0.5 * a * (1.0 + lax.erf(a * (2.0 ** -0.5))))


def _gelu_gate(a, gate):
    n = a.shape[0]
    tm = min(2048, n)
    spec = pl.BlockSpec((tm, PEER_SLOTS), lambda i: (i, 0))
    return pl.pallas_call(
        _gelu_gate_kernel, grid=(n // tm,), in_specs=[spec, spec], out_specs=spec,
        out_shape=jax.ShapeDtypeStruct(a.shape, F32), compiler_params=_cparams("parallel"), name="gelu_gate",
    )(a, gate)


def _residual_kernel(x_ref, g_ref, p_ref, o_ref):
    o_ref[0] = x_ref[0] + g_ref[0] * p_ref[0]


def _residual(x1, g2, peer):
    B, T, _ = x1.shape
    tm = min(2 * ROW_TILE, T)
    row = pl.BlockSpec((1, tm, D_MODEL), lambda b, t: (b, t, 0))
    vec = pl.BlockSpec((1, 1, D_MODEL), lambda b, t: (b, 0, 0))
    return pl.pallas_call(
        _residual_kernel, grid=(B, T // tm), in_specs=[row, vec, row], out_specs=row,
        out_shape=jax.ShapeDtypeStruct(x1.shape, F32), compiler_params=_cparams("parallel", "parallel"),
        name="residual",
    )(x1, g2, peer)


def kernel(x, c, ctx, c_ctx, w_ada, b_ada, norm_mix, norm_ffn, w_in, q_norm, k_norm, attn_sink, hgrn_lb_logits,
           hgrn_norm, w_out, peer_w_q, peer_sub_keys, peer_u, peer_v):
    assert w_ada.shape[0] == 1, "single-layer block"
    B, T, D = x.shape
    L = ctx.shape[1]
    n = B * T

    cvecs = jnp.zeros((SUBLANES, D), F32).at[:B].set(c).at[B].set(c_ctx)
    mod = _ada(cvecs, w_ada[0], b_ada[0])
    part = lambda rows, i: rows[:, None, i * D:(i + 1) * D]
    mod_x = mod[:B]
    mod_c = jnp.broadcast_to(mod[B:B + 1], (B, 6 * D))
    sh1, sc1, g1, sh2, sc2, g2 = (part(mod_x, i) for i in range(6))

    lbs = jnp.cumsum(jax.nn.softmax(hgrn_lb_logits.astype(F32), axis=1), axis=1)
    lb_f, lb_b = lbs[0, 0].reshape(1, HGRN_WIDTH), lbs[1, 0].reshape(1, HGRN_WIDTH)

    w_in_b = w_in[0].astype(BF16)
    rope = _rope_tables(T)
    kc, vc, ffc, fbc, ic, _, qhc, _ = _inproj(ctx, part(mod_c, 0), part(mod_c, 1), norm_mix[0], w_in_b,
                                              _identity_rope(L), q_norm[0], k_norm[0])
    s0 = jnp.zeros((B, HGRN_HEADS, HGRN_D, HGRN_D), F32)
    _, _, sfc, sbc = _hgrn(ffc, fbc, ic, qhc, lb_f, lb_b, s0, s0)

    sk = peer_sub_keys[0].reshape(2 * PEER_HEADS, PEER_NKEYS, PEER_DHALF).astype(BF16)
    w_out_b, w_q_b = w_out[0].astype(BF16), peer_w_q[0].astype(BF16)
    u_packed, v_packed = _pack_table(peer_u[0]), _pack_table(peer_v[0])

    kx, vx, ffx, fbx, ix, qx, qhx, gx = _inproj(x, sh1, sc1, norm_mix[0], w_in_b, rope, q_norm[0], k_norm[0])
    attn = _attention(qx, kx, vx, kc, vc, attn_sink[0])
    of, ob, _, _ = _hgrn(ffx, fbx, ix, qhx, lb_f, lb_b, sfc, sbc)
    x1, h2p, s = _outproj(attn, of, ob, gx, x, g1, sh2, sc2, hgrn_norm[0], norm_ffn[0], w_out_b, w_q_b, sk)
    idx_t, gate_t = _topk(s)
    idx16 = idx_t.reshape(PEER_SLOTS, n).T.reshape(n * PEER_HEADS, PEER_TOPK)
    gate = gate_t.reshape(PEER_SLOTS, n).T
    a = _peer_dot(h2p.reshape(n, PACKED_WORDS), idx16, u_packed)
    w = _gelu_gate(a, gate)
    peer = _peer_sum(w, idx16, v_packed, n)
    return _residual(x1, g2, peer.reshape(B, T, D))
```

```python
import functools

import jax
import jax.numpy as jnp
from jax import lax
from jax.experimental import pallas as pl
from jax.experimental.pallas import tpu as pltpu
from jax.experimental.pallas import tpu_sc as plsc

F32 = jnp.float32
BF16 = jnp.bfloat16
I32 = jnp.int32

D_MODEL = 2048
GRID_W = 64
EPS = 1e-6
HEAD_DIM = 128
ATTN_HEADS = 8
ATTN_KV_HEADS = 2
ATTN_GROUP = ATTN_HEADS // ATTN_KV_HEADS
BAND_BLOCK = 128
ROPE_THETA = 10000.0
HGRN_HEADS = 8
HGRN_D = 128
ATTN_WIDTH = ATTN_HEADS * HEAD_DIM
KV_WIDTH = ATTN_KV_HEADS * HEAD_DIM
HGRN_WIDTH = HGRN_HEADS * HGRN_D
COL_K = 0
COL_V = COL_K + KV_WIDTH
COL_FF = COL_V + KV_WIDTH
COL_FB = COL_FF + HGRN_WIDTH
COL_I = COL_FB + HGRN_WIDTH
COL_Q = COL_I + HGRN_WIDTH
COL_QH = COL_Q + ATTN_WIDTH
COL_G = COL_QH + HGRN_WIDTH
N_IN_COLS = COL_G + HGRN_WIDTH
PEER_HEADS = 8
PEER_NKEYS = 128
PEER_DHALF = 128
PEER_TOPK = 16
PEER_SLOTS = PEER_HEADS * PEER_TOPK

LANES = 128
SUBLANES = 8
VMEM_LIMIT_BYTES = 56 * 1024 * 1024

ROW_TILE = 256
HGRN_CHUNK = 128
HGRN_STEP = 512
HGRN_HEADS_PER_STEP = 8
NEG_INF = float("-inf")


def _cparams(*sem):
    return pltpu.CompilerParams(dimension_semantics=sem, vmem_limit_bytes=VMEM_LIMIT_BYTES)


def _resident(shape):
    nd = len(shape)
    return pl.BlockSpec(shape, lambda *_: (0,) * nd, pipeline_mode=pl.Buffered(1))


def _ada_kernel(c_ref, w_ref, b_ref, o_ref):
    c = c_ref[...]
    s = (c * jax.nn.sigmoid(c)).astype(BF16)
    o_ref[...] = jnp.dot(s, w_ref[...].astype(BF16), preferred_element_type=F32) + b_ref[...]


def _ada(cvecs, w, b):
    n = w.shape[1]
    tn = 1024
    return pl.pallas_call(
        _ada_kernel,
        grid=(n // tn,),
        in_specs=[pl.BlockSpec((SUBLANES, D_MODEL), lambda j: (0, 0)),
                  pl.BlockSpec((D_MODEL, tn), lambda j: (0, j)),
                  pl.BlockSpec((1, tn), lambda j: (0, j))],
        out_specs=pl.BlockSpec((SUBLANES, tn), lambda j: (0, j)),
        out_shape=jax.ShapeDtypeStruct((SUBLANES, n), F32),
        compiler_params=_cparams("arbitrary"),
        name="ada",
    )(cvecs, w, b.reshape(1, n))


def _pack_bf16_pairs(x):
    w = x.shape[1] // 2
    lo = pltpu.bitcast(x[:, :w].astype(BF16).astype(F32), jnp.uint32)
    hi = pltpu.bitcast(x[:, w:].astype(BF16).astype(F32), jnp.uint32)
    return pltpu.bitcast((hi & jnp.uint32(0xFFFF0000)) | (lo >> 16), I32)


def _rms(x, gain):
    return x * lax.rsqrt(jnp.mean(x * x, axis=-1, keepdims=True) + EPS) * gain


def _rope(x, cos, sin_a, sin_b):
    q = HEAD_DIM // 4
    return x * cos + pltpu.roll(x, HEAD_DIM - q, 1) * sin_a + pltpu.roll(x, q, 1) * sin_b


def _inproj_kernel(x_ref, sh_ref, sc_ref, gain_ref, w_ref, cos_ref, sa_ref, sb_ref, qg_ref, kg_ref,
                   k_ref, v_ref, ff_ref, fb_ref, i_ref, q_ref, qh_ref, g_ref):
    x = x_ref[0]
    h = _rms(x, gain_ref[...]) * (1.0 + sc_ref[0]) + sh_ref[0]
    hb = h.astype(BF16)

    def seg(lo, width):
        return jnp.dot(hb, w_ref[:, lo:lo + width], preferred_element_type=F32)

    cos, sa, sb = cos_ref[...], sa_ref[...], sb_ref[...]

    def normed_heads(p, gain, n_heads, out_ref):
        for hd in range(n_heads):
            ph = p[:, hd * HEAD_DIM:(hd + 1) * HEAD_DIM]
            out_ref[0, :, hd * HEAD_DIM:(hd + 1) * HEAD_DIM] = _rope(_rms(ph, gain), cos, sa, sb).astype(BF16)

    normed_heads(seg(COL_K, KV_WIDTH), kg_ref[...], ATTN_KV_HEADS, k_ref)
    v_ref[0] = seg(COL_V, KV_WIDTH).astype(BF16)
    ff_ref[0] = seg(COL_FF, HGRN_WIDTH)
    fb_ref[0] = seg(COL_FB, HGRN_WIDTH)
    i_ref[0] = seg(COL_I, HGRN_WIDTH).astype(BF16)
    normed_heads(seg(COL_Q, ATTN_WIDTH), qg_ref[...], ATTN_HEADS, q_ref)
    qh_ref[0] = seg(COL_QH, HGRN_WIDTH).astype(BF16)
    g_ref[0] = seg(COL_G, HGRN_WIDTH).astype(BF16)


def _inproj(x, shift, scale, gain, w_bf16, rope, q_gain, k_gain):
    B, T, _ = x.shape
    tm = min(ROW_TILE, T)
    row = lambda w: pl.BlockSpec((1, tm, w), lambda b, t: (b, t, 0))
    vec = pl.BlockSpec((1, 1, D_MODEL), lambda b, t: (b, 0, 0))
    tab = pl.BlockSpec((tm, HEAD_DIM), lambda b, t: (t, 0))
    out_w = [(KV_WIDTH, BF16), (KV_WIDTH, BF16), (HGRN_WIDTH, F32), (HGRN_WIDTH, F32), (HGRN_WIDTH, BF16),
             (ATTN_WIDTH, BF16), (HGRN_WIDTH, BF16), (HGRN_WIDTH, BF16)]
    return pl.pallas_call(
        _inproj_kernel,
        grid=(B, T // tm),
        in_specs=[row(D_MODEL), vec, vec, _resident((1, D_MODEL)), _resident((D_MODEL, N_IN_COLS)),
                  tab, tab, tab, _resident((1, HEAD_DIM)), _resident((1, HEAD_DIM))],
        out_specs=[row(w) for w, _ in out_w],
        out_shape=[jax.ShapeDtypeStruct((B, T, w), dt) for w, dt in out_w],
        compiler_params=_cparams("parallel", "parallel"),
        name="inproj",
    )(x, shift, scale, gain.reshape(1, D_MODEL), w_bf16, *rope,
      q_gain.reshape(1, HEAD_DIM), k_gain.reshape(1, HEAD_DIM))


def _rope_tables(T):
    rows = T // GRID_W
    row_pos = jnp.repeat(jnp.arange(rows, dtype=F32), GRID_W)
    col_pos = jnp.tile(jnp.arange(GRID_W, dtype=F32), rows)
    half = HEAD_DIM // 2
    inv_freq = jnp.power(ROPE_THETA, -jnp.arange(0, half, 2, dtype=F32) / half)
    ang_r = row_pos[:, None] * inv_freq
    ang_c = col_pos[:, None] * inv_freq
    cr, sr, cc, sc = jnp.cos(ang_r), jnp.sin(ang_r), jnp.cos(ang_c), jnp.sin(ang_c)
    z = jnp.zeros_like(sr)
    return (jnp.concatenate([cr, cr, cc, cc], -1),
            jnp.concatenate([-sr, z, -sc, z], -1),
            jnp.concatenate([z, sr, z, sc], -1))


def _identity_rope(T):
    return (jnp.ones((T, HEAD_DIM), F32), jnp.zeros((T, HEAD_DIM), F32), jnp.zeros((T, HEAD_DIM), F32))


_NT = (((1,), (1,)), ((), ()))
_TN = (((0,), (0,)), ((), ()))


def _attn_kernel(sink_ref, q_ref, kp_ref, kc_ref, kn_ref, vp_ref, vc_ref, vn_ref, kx_ref, vx_ref, o_ref):
    n = pl.program_id(1)
    nb = pl.num_programs(1)
    blk = BAND_BLOCK
    rows = ATTN_GROUP * blk
    qi = lax.broadcasted_iota(I32, (rows, blk), 0) & (blk - 1)
    kj = lax.broadcasted_iota(I32, (rows, blk), 1)
    prev_ok = kj >= qi + jnp.where(n > 0, 0, blk)
    next_ok = kj <= qi - jnp.where(n < nb - 1, 0, blk)
    scale = HEAD_DIM ** -0.5
    for h in range(ATTN_KV_HEADS):
        cs = slice(h * HEAD_DIM, (h + 1) * HEAD_DIM)
        heads = [h * ATTN_GROUP + g for g in range(ATTN_GROUP)]
        q4 = jnp.concatenate([q_ref[0, :, hd * HEAD_DIM:(hd + 1) * HEAD_DIM] for hd in heads], axis=0)

        def scores(k_ref):
            return lax.dot_general(q4, k_ref[0, :, cs], _NT, preferred_element_type=F32) * scale

        s_p = jnp.where(prev_ok, scores(kp_ref), NEG_INF)
        s_c = scores(kc_ref)
        s_n = jnp.where(next_ok, scores(kn_ref), NEG_INF)
        s_x = scores(kx_ref)
        sink = jnp.concatenate([jnp.full((blk, 1), sink_ref[hd], F32) for hd in heads], axis=0)
        def lane_tiles(t):
            return [t[:, i:i + LANES] for i in range(0, t.shape[1], LANES)]

        tiles = [s_p, s_c, s_n] + lane_tiles(s_x)
        m = functools.reduce(jnp.maximum, tiles)
        m = jnp.maximum(jnp.max(m, -1, keepdims=True), sink)
        p_p, p_c, p_n, p_x = jnp.exp(s_p - m), jnp.exp(s_c - m), jnp.exp(s_n - m), jnp.exp(s_x - m)
        total = functools.reduce(jnp.add, [p_p, p_c, p_n] + lane_tiles(p_x))
        denom = jnp.sum(total, -1, keepdims=True) + jnp.exp(sink - m)

        def pv(p, v_ref):
            return jnp.dot(p.astype(BF16), v_ref[0, :, cs], preferred_element_type=F32)

        o = (pv(p_p, vp_ref) + pv(p_c, vc_ref) + pv(p_n, vn_ref) + pv(p_x, vx_ref)) / denom
        for g, hd in enumerate(heads):
            o_ref[0, :, hd * HEAD_DIM:(hd + 1) * HEAD_DIM] = o[g * blk:(g + 1) * blk].astype(BF16)


def _attention(q, k, v, k_ctx, v_ctx, sink):
    B, T, _ = q.shape
    L = k_ctx.shape[1]
    nb = T // BAND_BLOCK
    kv = lambda f: pl.BlockSpec((1, BAND_BLOCK, KV_WIDTH), lambda b, n: (b, f(n), 0))
    prev, cur, nxt = (lambda n: jnp.maximum(n - 1, 0)), (lambda n: n), (lambda n: jnp.minimum(n + 1, nb - 1))
    ctx = pl.BlockSpec((1, L, KV_WIDTH), lambda b, n: (b, 0, 0))
    return pl.pallas_call(
        _attn_kernel,
        grid=(B, nb),
        in_specs=[pl.BlockSpec(memory_space=pltpu.SMEM),
                  pl.BlockSpec((1, BAND_BLOCK, ATTN_WIDTH), lambda b, n: (b, n, 0)),
                  kv(prev), kv(cur), kv(nxt), kv(prev), kv(cur), kv(nxt), ctx, ctx],
        out_specs=pl.BlockSpec((1, BAND_BLOCK, ATTN_WIDTH), lambda b, n: (b, n, 0)),
        out_shape=jax.ShapeDtypeStruct((B, T, ATTN_WIDTH), BF16),
        compiler_params=_cparams("parallel", "parallel"),
        name="attn",
    )(sink, q, k, k, k, v, v, v, k_ctx, v_ctx)


_LOG2E = 1.4426950408889634
_DIAG = SUBLANES
_LEVELS = (64, 32, 16, 8)


def _hgrn_chunk(logit, v, q, lb, st_ref, reverse):
    C = HGRN_CHUNK
    f = lb + (1.0 - lb) * jax.nn.sigmoid(logit)
    lf = jnp.log(f)
    kk = 1.0 - f
    qf = q.astype(F32)
    r = lax.broadcasted_iota(I32, (C, C), 0)
    c = lax.broadcasted_iota(I32, (C, C), 1)
    incl = (r <= c) if reverse else (r >= c)
    a = jnp.dot(incl.astype(F32), lf, precision=lax.Precision.HIGHEST, preferred_element_type=F32) * _LOG2E
    a_end = a[0:1] if reverse else a[C - 1:C]

    st = st_ref[...]
    inter = lax.dot_general((qf * jnp.exp2(a)).astype(BF16), st.astype(BF16), _NT, preferred_element_type=F32)
    kd = (kk * jnp.exp2(a_end - a)).astype(BF16)
    st_ref[...] = st * jnp.exp2(a_end) + lax.dot_general(v, kd, _TN, preferred_element_type=F32)

    later = (r < c) if reverse else (r > c)
    att = jnp.zeros((C, C), F32)
    for m in _LEVELS:
        a3 = a.reshape(C // (2 * m), 2 * m, HGRN_D)
        edge = a3[:, m:m + 1, :] if reverse else a3[:, m - 1:m, :]
        e = jnp.exp2(-jnp.abs(a3 - edge)).reshape(C, HGRN_D)
        p = lax.dot_general((qf * e).astype(BF16), (kk * e).astype(BF16), _NT, preferred_element_type=F32)
        pair = ((r ^ c) >> (m.bit_length() - 1)) == 1
        att = jnp.where(pair & later, p, att)

    lane = lax.broadcasted_iota(I32, (_DIAG, C), 1)
    sub = lax.broadcasted_iota(I32, (_DIAG, C), 0)
    keep = [(lane == s) & ((sub <= s) if reverse else (sub >= s)) for s in range(_DIAG)]
    blocks = []
    for j in range(C // _DIAG):
        rows = slice(j * _DIAG, (j + 1) * _DIAG)
        a_j, q_j, k_j = a[rows], qf[rows], kk[rows]
        blk = jnp.zeros((_DIAG, C), F32)
        for s in range(_DIAG):
            e = jnp.exp2(a_j - a_j[s:s + 1])
            col = jnp.sum(q_j * e * k_j[s:s + 1], axis=-1, keepdims=True)
            blk = jnp.where(keep[s], col, blk)
        blocks.append(pltpu.roll(blk, j * _DIAG, 1) if j else blk)
    att = att + jnp.concatenate(blocks, axis=0)
    return inter + jnp.dot(att.astype(BF16), v, preferred_element_type=F32)


def _hgrn_kernel(ff_ref, vf_ref, qf_ref, fb_ref, vb_ref, qb_ref, lbf_ref, lbb_ref, s0f_ref, s0b_ref,
                 of_ref, ob_ref, sf_ref, sb_ref, stf, stb):
    step = pl.program_id(2)
    nsub = ff_ref.shape[1] // HGRN_CHUNK

    @pl.when(step == 0)
    def _():
        stf[...] = s0f_ref[0]
        stb[...] = s0b_ref[0]

    def body(j, carry):
        fo = pl.multiple_of(j * HGRN_CHUNK, HGRN_CHUNK)
        bo = pl.multiple_of((nsub - 1 - j) * HGRN_CHUNK, HGRN_CHUNK)
        for hh in range(HGRN_HEADS_PER_STEP):
            cs = slice(hh * HGRN_D, (hh + 1) * HGRN_D)
            rows = pl.ds(fo, HGRN_CHUNK)
            of_ref[0, rows, cs] = _hgrn_chunk(ff_ref[0, rows, cs], vf_ref[0, rows, cs], qf_ref[0, rows, cs],
                                              lbf_ref[:, cs], stf.at[hh], False)
            rows = pl.ds(bo, HGRN_CHUNK)
            ob_ref[0, rows, cs] = _hgrn_chunk(fb_ref[0, rows, cs], vb_ref[0, rows, cs], qb_ref[0, rows, cs],
                                              lbb_ref[:, cs], stb.at[hh], True)
        return carry

    lax.fori_loop(0, nsub, body, 0)

    @pl.when(step == pl.num_programs(2) - 1)
    def _():
        sf_ref[0] = stf[...]
        sb_ref[0] = stb[...]


def _hgrn(ff, fb, val, q, lb_f, lb_b, s0f, s0b):
    B, T, _ = ff.shape
    ts = min(HGRN_STEP, T)
    ns = T // ts
    hp = HGRN_HEADS_PER_STEP
    fwd = pl.BlockSpec((1, ts, hp * HGRN_D), lambda b, h, s: (b, s, h))
    bwd = pl.BlockSpec((1, ts, hp * HGRN_D), lambda b, h, s: (b, ns - 1 - s, h))
    lbs = pl.BlockSpec((1, hp * HGRN_D), lambda b, h, s: (0, h))
    st = pl.BlockSpec((1, hp, HGRN_D, HGRN_D), lambda b, h, s: (b, h, 0, 0))
    o_sds = jax.ShapeDtypeStruct((B, T, HGRN_WIDTH), F32)
    s_sds = jax.ShapeDtypeStruct((B, HGRN_HEADS, HGRN_D, HGRN_D), F32)
    return pl.pallas_call(
        _hgrn_kernel,
        grid=(B, HGRN_HEADS // hp, ns),
        in_specs=[fwd, fwd, fwd, bwd, bwd, bwd, lbs, lbs, st, st],
        out_specs=[fwd, bwd, st, st],
        out_shape=[o_sds, o_sds, s_sds, s_sds],
        scratch_shapes=[pltpu.VMEM((hp, HGRN_D, HGRN_D), F32), pltpu.VMEM((hp, HGRN_D, HGRN_D), F32)],
        compiler_params=_cparams("parallel", "parallel", "arbitrary"),
        name="hgrn",
    )(ff, val, q, fb, val, q, lb_f, lb_b, s0f, s0b)


def _outproj_kernel(attn_ref, of_ref, ob_ref, g_ref, x_ref, g1_ref, sh2_ref, sc2_ref, og_ref, nf_ref,
                    wo_ref, wq_ref, sk_ref, x1_ref, h2_ref, s_ref):
    o = of_ref[0] + ob_ref[0]
    og = og_ref[...]
    parts = []
    for hd in range(HGRN_HEADS):
        cs = slice(hd * HGRN_D, (hd + 1) * HGRN_D)
        gh = g_ref[0, :, cs].astype(F32)
        parts.append((_rms(o[:, cs], og) * (gh * jax.nn.sigmoid(gh))).astype(BF16))
    hg = jnp.concatenate(parts, axis=-1)
    mix = (jnp.dot(attn_ref[0], wo_ref[:ATTN_WIDTH], preferred_element_type=F32)
           + jnp.dot(hg, wo_ref[ATTN_WIDTH:], preferred_element_type=F32))
    x1 = x_ref[0] + g1_ref[0] * mix
    x1_ref[0] = x1
    h2 = _rms(x1, nf_ref[...]) * (1.0 + sc2_ref[0]) + sh2_ref[0]
    h2_ref[0] = _pack_bf16_pairs(h2)
    pq = jnp.dot(h2.astype(BF16), wq_ref[...], preferred_element_type=F32)
    for hp in range(2 * PEER_HEADS):
        cs = slice(hp * PEER_DHALF, (hp + 1) * PEER_DHALF)
        for jb in range(pq.shape[0] // LANES):
            rows = slice(jb * LANES, (jb + 1) * LANES)
            s_ref[hp, 0, jb * PEER_NKEYS:(jb + 1) * PEER_NKEYS, :] = lax.dot_general(
                sk_ref[hp], pq[rows, cs].astype(BF16), _NT, preferred_element_type=F32)


def _outproj(attn, of, ob, g, x, g1, sh2, sc2, o_gain, norm_ffn, wo_bf16, wq_bf16, sk_bf16):
    B, T, _ = x.shape
    tm = min(ROW_TILE, T)
    nt = T // tm
    per = TOPK_TOKENS // tm
    assert (B * T) % TOPK_TOKENS == 0 and TOPK_TOKENS % tm == 0 and tm % LANES == 0
    row = lambda w: pl.BlockSpec((1, tm, w), lambda b, t: (b, t, 0))
    vec = pl.BlockSpec((1, 1, D_MODEL), lambda b, t: (b, 0, 0))
    n_sk = 2 * PEER_HEADS
    sds = lambda w: jax.ShapeDtypeStruct((B, T, w), F32)
    s_rows = tm // LANES * PEER_NKEYS
    s_spec = pl.BlockSpec((n_sk, 1, s_rows, LANES), lambda b, t: (0, (b * nt + t) // per, (b * nt + t) % per, 0))
    s_sds = jax.ShapeDtypeStruct((n_sk, B * T // TOPK_TOKENS, TOPK_TOKENS // LANES * PEER_NKEYS, LANES), F32)
    return pl.pallas_call(
        _outproj_kernel,
        grid=(B, T // tm),
        in_specs=[row(ATTN_WIDTH), row(HGRN_WIDTH), row(HGRN_WIDTH), row(HGRN_WIDTH), row(D_MODEL),
                  vec, vec, vec, _resident((1, HGRN_D)), _resident((1, D_MODEL)),
                  _resident((ATTN_WIDTH + HGRN_WIDTH, D_MODEL)), _resident((D_MODEL, n_sk * PEER_DHALF)),
                  _resident((n_sk, PEER_NKEYS, PEER_DHALF))],
        out_specs=[row(D_MODEL), row(D_MODEL // 2), s_spec],
        out_shape=[sds(D_MODEL), jax.ShapeDtypeStruct((B, T, D_MODEL // 2), I32), s_sds],
        compiler_params=_cparams("parallel", "parallel"),
        name="outproj",
    )(attn, of, ob, g, x, g1, sh2, sc2, o_gain.reshape(1, HGRN_D), norm_ffn.reshape(1, D_MODEL),
      wo_bf16, wq_bf16, sk_bf16)


TOPK_TOKENS = SUBLANES * LANES
_CAND_PAIRS = [(a, b) for a in range(PEER_TOPK) for b in range(PEER_TOPK) if (a + 1) * (b + 1) <= PEER_TOPK]


def _first_argmax(values, ids, n_chains):
    per = -(-len(values) // n_chains)
    parts = []
    for lo in range(0, len(values), per):
        m, i = values[lo], ids[lo]
        if not isinstance(i, jax.Array):
            i = jnp.full(m.shape, i, F32)
        for v, vid in zip(values[lo + 1:lo + per], ids[lo + 1:lo + per]):
            c = v > m
            m = jnp.where(c, v, m)
            i = jnp.where(c, vid, i)
        parts.append((m, i))
    m, i = parts[0]
    for pm, pi in parts[1:]:
        c = pm > m
        m = jnp.where(c, pm, m)
        i = jnp.where(c, pi, i)
    return m, i


def _topk_kernel(s_ref, idx_ref, gate_ref, wk, tv, ti, cv, ci, bv):
    shape = (SUBLANES, LANES)
    none = jnp.full(shape, -1.0, F32)

    def head(h, carry):
        for p in range(2):
            hp = 2 * h + p
            for k in range(PEER_NKEYS):
                wk[k] = s_ref[hp, 0, pl.ds(k, SUBLANES, stride=PEER_NKEYS), :]

            def extract(r, prev, p=p):
                vals = []
                for k in range(PEER_NKEYS):
                    s = jnp.where(prev == float(k), NEG_INF, wk[k])
                    wk[k] = s
                    vals.append(s)
                m, i = _first_argmax(vals, [float(k) for k in range(PEER_NKEYS)], 4)
                tv[p, r] = m
                ti[p, r] = i
                return i

            lax.fori_loop(0, PEER_TOPK, extract, none)

        for c, (a, b) in enumerate(_CAND_PAIRS):
            cv[c] = tv[0, a] + tv[1, b]
            ci[c] = ti[0, a] * float(PEER_NKEYS) + ti[1, b]

        def pick(r, prev):
            vals, ids = [], []
            for c in range(len(_CAND_PAIRS)):
                cid = ci[c]
                s = jnp.where(cid == prev, NEG_INF, cv[c])
                cv[c] = s
                vals.append(s)
                ids.append(cid)
            m, i = _first_argmax(vals, ids, 2)
            bv[r] = m
            idx_ref[h * PEER_TOPK + r, 0] = i.astype(I32)
            return i

        lax.fori_loop(0, PEER_TOPK, pick, none)

        es = [jnp.exp(bv[r] - bv[0]) for r in range(PEER_TOPK)]
        tot = es[0]
        for e in es[1:]:
            tot = tot + e
        for r in range(PEER_TOPK):
            gate_ref[h * PEER_TOPK + r, 0] = es[r] / tot
        return carry

    lax.fori_loop(0, PEER_HEADS, head, 0)


def _topk(s):
    nt = s.shape[1]
    vreg = (SUBLANES, LANES)
    out_spec = pl.BlockSpec((PEER_SLOTS, 1) + vreg, lambda i: (0, i, 0, 0))
    return pl.pallas_call(
        _topk_kernel,
        grid=(nt,),
        in_specs=[pl.BlockSpec((s.shape[0], 1) + s.shape[2:], lambda i: (0, i, 0, 0))],
        out_specs=[out_spec, out_spec],
        out_shape=[jax.ShapeDtypeStruct((PEER_SLOTS, nt) + vreg, I32),
                   jax.ShapeDtypeStruct((PEER_SLOTS, nt) + vreg, F32)],
        scratch_shapes=[pltpu.VMEM((PEER_NKEYS,) + vreg, F32),
                        pltpu.VMEM((2, PEER_TOPK) + vreg, F32), pltpu.VMEM((2, PEER_TOPK) + vreg, F32),
                        pltpu.VMEM((len(_CAND_PAIRS),) + vreg, F32), pltpu.VMEM((len(_CAND_PAIRS),) + vreg, F32),
                        pltpu.VMEM((PEER_TOPK,) + vreg, F32)],
        compiler_params=_cparams("parallel"),
        name="topk",
    )(s)


SC_CORES = 2
SC_SUBCORES = 16
SC_LANES = 16
SC_WORKERS = SC_CORES * SC_SUBCORES
PEER_GROUP = 16
SC_BUFFERS = 4
PEER_ITEMS = PEER_GROUP * PEER_HEADS
SC_UNROLL = 8
PACKED_WORDS = D_MODEL // 2
EXPERT_SLAB = (PACKED_WORDS // LANES, LANES)


_SC_PARAMS = pltpu.CompilerParams(needs_layout_passes=False)


def _sc_mesh():
    return plsc.VectorSubcoreMesh(core_axis_name="c", subcore_axis_name="s")


def _sc_worker():
    return lax.axis_index("s") * SC_CORES + lax.axis_index("c")


def _pack_kernel(t_ref, o_ref):
    word = _pack_bf16_pairs(t_ref[...])
    for s in range(EXPERT_SLAB[0]):
        o_ref[:, s, :] = word[:, s * LANES:(s + 1) * LANES]


def _pack_table(table):
    e = table.shape[0]
    tr = 256
    return pl.pallas_call(
        _pack_kernel,
        grid=(e // tr,),
        in_specs=[pl.BlockSpec((tr, D_MODEL), lambda i: (i, 0))],
        out_specs=pl.BlockSpec((tr,) + EXPERT_SLAB, lambda i: (i, 0, 0)),
        out_shape=jax.ShapeDtypeStruct((e,) + EXPERT_SLAB, I32),
        compiler_params=_cparams("parallel"),
        name="pack_table",
    )(table)


def _row_words(rows, k, first=0, count=PEER_TOPK):
    per = LANES // SC_LANES
    sub, cols = k // per, pl.ds((k % per) * SC_LANES, SC_LANES)
    return tuple(rows[r, sub, cols] for r in range(first, first + count))


def _bf16_lanes(word):
    return plsc.bitcast(word, BF16)


def _f32_halves(pairs):
    return plsc.unpack(pairs, format=plsc.PackFormat.INTERLEAVED)


def _half_cols(k):
    return pl.ds(k * SC_LANES, SC_LANES), pl.ds(PACKED_WORDS + k * SC_LANES, SC_LANES)


def _sc_item_pipeline(table_hbm, idx_v, ring, sems, compute):
    nb = SC_BUFFERS

    def gather(j):
        b = j % nb
        return pltpu.make_async_copy(table_hbm.at[idx_v.at[j]], ring.at[b], sems.at[b])

    for j in range(nb - 1):
        gather(j).start()

    @pl.loop(0, PEER_ITEMS)
    def _(j):
        @pl.when(j + nb - 1 < PEER_ITEMS)
        def _():
            gather(j + nb - 1).start()

        gather(j).wait()
        compute(j, ring.at[j % nb])


def _peer_dot_kernel(x_hbm, idx_hbm, u_hbm, a_hbm, x_v, idx_v, a_v, ring, sems):
    tok_per_w = x_hbm.shape[0] // SC_WORKERS
    wid = _sc_worker()
    lane = lax.iota(I32, SC_LANES)

    def compute(j, rows):
        t = j // PEER_HEADS

        zero = jnp.zeros((SC_LANES,), F32)

        half = PEER_TOPK // 2
        steps = PACKED_WORDS // SC_LANES // 2

        def load(kp, first):
            out = ()
            for k in (2 * kp, 2 * kp + 1):
                out += (x_v[t, pl.ds(k * SC_LANES, SC_LANES)],) + _row_words(rows, k, first, half)
            return out

        def fma(acc, vals):
            x0, x1 = _bf16_lanes(vals[0]), _bf16_lanes(vals[1 + half])
            out = []
            for i in range(half):
                lo, hi = _f32_halves(_bf16_lanes(vals[1 + i]) * x0 + _bf16_lanes(vals[2 + half + i]) * x1)
                out.append(acc[i] + lo + hi)
            return tuple(out)

        @plsc.parallel_loop(0, steps, unroll=SC_UNROLL, carry=((zero,) * half, (zero,) * half, load(0, half)))
        def state(kp, state):
            acc_a, acc_b, vals_b = state
            vals_a = load(kp, 0)
            acc_b = fma(acc_b, vals_b)
            vals_b = load(jnp.minimum(kp + 1, steps - 1), half)
            return fma(acc_a, vals_a), acc_b, vals_b

        acc = state[0] + state[1]
        out = zero
        for r in range(PEER_TOPK):
            out = jnp.where(lane == r, jnp.sum(acc[r]), out)
        a_v[t, pl.ds((j % PEER_HEADS) * PEER_TOPK, PEER_TOPK)] = out

    @pl.loop(0, tok_per_w // PEER_GROUP)
    def _(g):
        tok0 = wid * tok_per_w + g * PEER_GROUP
        item0 = tok0 * PEER_HEADS
        pltpu.sync_copy(x_hbm.at[pl.ds(tok0, PEER_GROUP)], x_v)
        pltpu.sync_copy(idx_hbm.at[pl.ds(item0, PEER_ITEMS)], idx_v)
        _sc_item_pipeline(u_hbm, idx_v, ring, sems, compute)
        pltpu.sync_copy(a_v, a_hbm.at[pl.ds(tok0, PEER_GROUP)])


def _peer_sum_kernel(w_hbm, idx_hbm, v_hbm, o_hbm, w_v, idx_v, o_v, ring, sems):
    tok_per_w = o_hbm.shape[0] // SC_WORKERS
    wid = _sc_worker()
    zero = jnp.zeros((SC_LANES,), F32)

    def compute(j, rows):
        t = j // PEER_HEADS
        tv = jnp.full((SC_LANES,), t, I32)
        slot0 = (j % PEER_HEADS) * PEER_TOPK
        ws = []
        for r in range(PEER_TOPK):
            wr = plsc.load_gather(w_v, [tv, jnp.full((SC_LANES,), slot0 + r, I32)])
            ws.append(plsc.pack(wr, wr, format=plsc.PackFormat.INTERLEAVED))

        def load(k):
            lo, hi = _half_cols(k)
            return (o_v[t, lo], o_v[t, hi]) + _row_words(rows, k)

        def finish(k, vals):
            acc_lo, acc_hi = vals[0], vals[1]
            for r in range(0, PEER_TOPK, 2):
                lo, hi = _f32_halves(_bf16_lanes(vals[2 + r]) * ws[r] + _bf16_lanes(vals[3 + r]) * ws[r + 1])
                acc_lo = acc_lo + lo
                acc_hi = acc_hi + hi
            lo, hi = _half_cols(k)
            o_v[t, lo] = acc_lo
            o_v[t, hi] = acc_hi

        @plsc.parallel_loop(1, PACKED_WORDS // SC_LANES, unroll=SC_UNROLL, carry=load(0))
        def vals(k, vals):
            nxt = load(k)
            finish(k - 1, vals)
            return nxt

        finish(PACKED_WORDS // SC_LANES - 1, vals)

    @pl.loop(0, tok_per_w // PEER_GROUP)
    def _(g):
        tok0 = wid * tok_per_w + g * PEER_GROUP
        item0 = tok0 * PEER_HEADS
        pltpu.sync_copy(w_hbm.at[pl.ds(tok0, PEER_GROUP)], w_v)
        pltpu.sync_copy(idx_hbm.at[pl.ds(item0, PEER_ITEMS)], idx_v)

        @pl.loop(0, PEER_GROUP)
        def _(t):
            @plsc.parallel_loop(0, D_MODEL // SC_LANES, unroll=16)
            def _(k):
                o_v[t, pl.ds(k * SC_LANES, SC_LANES)] = zero

        _sc_item_pipeline(v_hbm, idx_v, ring, sems, compute)
        pltpu.sync_copy(o_v, o_hbm.at[pl.ds(tok0, PEER_GROUP)])


def _sc_scratch(first, last):
    return ([first, pltpu.VMEM((PEER_ITEMS, PEER_TOPK), I32), last]
            + [pltpu.VMEM((SC_BUFFERS, PEER_TOPK) + EXPERT_SLAB, I32), pltpu.SemaphoreType.DMA((SC_BUFFERS,))])


def _peer_dot(h2, idx, u):
    n = h2.shape[0]
    assert n % (SC_WORKERS * PEER_GROUP) == 0
    call = pl.kernel(
        _peer_dot_kernel,
        out_type=jax.ShapeDtypeStruct((n, PEER_SLOTS), F32),
        mesh=_sc_mesh(),
        scratch_types=_sc_scratch(pltpu.VMEM((PEER_GROUP, PACKED_WORDS), I32),
                                  pltpu.VMEM((PEER_GROUP, PEER_SLOTS), F32)),
        compiler_params=_SC_PARAMS,
    )
    return call(h2, idx, u)


def _peer_sum(w, idx, v, n):
    assert n % (SC_WORKERS * PEER_GROUP) == 0
    call = pl.kernel(
        _peer_sum_kernel,
        out_type=jax.ShapeDtypeStruct((n, D_MODEL), F32),
        mesh=_sc_mesh(),
        scratch_types=_sc_scratch(pltpu.VMEM((PEER_GROUP, PEER_SLOTS), F32),
                                  pltpu.VMEM((PEER_GROUP, D_MODEL), F32)),
        compiler_params=_SC_PARAMS,
    )
    return call(w, idx, v)


def _gelu_gate_kernel(a_ref, g_ref, w_ref):
    a = a_ref[...]
    w_ref[...] = g_ref[...] * (0.5 * a * (1.0 + lax.erf(a * (2.0 ** -0.5))))


def _gelu_gate(a, gate):
    n = a.shape[0]
    tm = min(2048, n)
    spec = pl.BlockSpec((tm, PEER_SLOTS), lambda i: (i, 0))
    return pl.pallas_call(
        _gelu_gate_kernel, grid=(n // tm,), in_specs=[spec, spec], out_specs=spec,
        out_shape=jax.ShapeDtypeStruct(a.shape, F32), compiler_params=_cparams("parallel"), name="gelu_gate",
    )(a, gate)


def _residual_kernel(x_ref, g_ref, p_ref, o_ref):
    o_ref[0] = x_ref[0] + g_ref[0] * p_ref[0]


def _residual(x1, g2, peer):
    B, T, _ = x1.shape
    tm = min(2 * ROW_TILE, T)
    row = pl.BlockSpec((1, tm, D_MODEL), lambda b, t: (b, t, 0))
    vec = pl.BlockSpec((1, 1, D_MODEL), lambda b, t: (b, 0, 0))
    return pl.pallas_call(
        _residual_kernel, grid=(B, T // tm), in_specs=[row, vec, row], out_specs=row,
        out_shape=jax.ShapeDtypeStruct(x1.shape, F32), compiler_params=_cparams("parallel", "parallel"),
        name="residual",
    )(x1, g2, peer)


def kernel(x, c, ctx, c_ctx, w_ada, b_ada, norm_mix, norm_ffn, w_in, q_norm, k_norm, attn_sink, hgrn_lb_logits,
           hgrn_norm, w_out, peer_w_q, peer_sub_keys, peer_u, peer_v):
    assert w_ada.shape[0] == 1, "single-layer block"
    B, T, D = x.shape
    L = ctx.shape[1]
    n = B * T

    cvecs = jnp.zeros((SUBLANES, D), F32).at[:B].set(c).at[B].set(c_ctx)
    mod = _ada(cvecs, w_ada[0], b_ada[0])
    part = lambda rows, i: rows[:, None, i * D:(i + 1) * D]
    mod_x = mod[:B]
    mod_c = jnp.broadcast_to(mod[B:B + 1], (B, 6 * D))
    sh1, sc1, g1, sh2, sc2, g2 = (part(mod_x, i) for i in range(6))

    lbs = jnp.cumsum(jax.nn.softmax(hgrn_lb_logits.astype(F32), axis=1), axis=1)
    lb_f, lb_b = lbs[0, 0].reshape(1, HGRN_WIDTH), lbs[1, 0].reshape(1, HGRN_WIDTH)

    w_in_b = w_in[0].astype(BF16)
    rope = _rope_tables(T)
    kc, vc, ffc, fbc, ic, _, qhc, _ = _inproj(ctx, part(mod_c, 0), part(mod_c, 1), norm_mix[0], w_in_b,
                                              _identity_rope(L), q_norm[0], k_norm[0])
    s0 = jnp.zeros((B, HGRN_HEADS, HGRN_D, HGRN_D), F32)
    _, _, sfc, sbc = _hgrn(ffc, fbc, ic, qhc, lb_f, lb_b, s0, s0)

    sk = peer_sub_keys[0].reshape(2 * PEER_HEADS, PEER_NKEYS, PEER_DHALF).astype(BF16)
    w_out_b, w_q_b = w_out[0].astype(BF16), peer_w_q[0].astype(BF16)
    u_packed, v_packed = _pack_table(peer_u[0]), _pack_table(peer_v[0])

    kx, vx, ffx, fbx, ix, qx, qhx, gx = _inproj(x, sh1, sc1, norm_mix[0], w_in_b, rope, q_norm[0], k_norm[0])
    attn = _attention(qx, kx, vx, kc, vc, attn_sink[0])
    of, ob, _, _ = _hgrn(ffx, fbx, ix, qhx, lb_f, lb_b, sfc, sbc)
    x1, h2p, s = _outproj(attn, of, ob, gx, x, g1, sh2, sc2, hgrn_norm[0], norm_ffn[0], w_out_b, w_q_b, sk)
    idx_t, gate_t = _topk(s)
    idx16 = idx_t.reshape(PEER_SLOTS, n).T.reshape(n * PEER_HEADS, PEER_TOPK)
    gate = gate_t.reshape(PEER_SLOTS, n).T
    a = _peer_dot(h2p.reshape(n, PACKED_WORDS), idx16, u_packed)
    w = _gelu_gate(a, gate)
    peer = _peer_sum(w, idx16, v_packed, n)
    return _residual(x1, g2, peer.reshape(B, T, D))
```

```python
import functools

import jax
import jax.numpy as jnp
from jax import lax
from jax.experimental import pallas as pl
from jax.experimental.pallas import tpu as pltpu
from jax.experimental.pallas import tpu_sc as plsc

F32 = jnp.float32
BF16 = jnp.bfloat16
I32 = jnp.int32

D_MODEL = 2048
GRID_W = 64
EPS = 1e-6
HEAD_DIM = 128
ATTN_HEADS = 8
ATTN_KV_HEADS = 2
ATTN_GROUP = ATTN_HEADS // ATTN_KV_HEADS
BAND_BLOCK = 128
ROPE_THETA = 10000.0
HGRN_HEADS = 8
HGRN_D = 128
ATTN_WIDTH = ATTN_HEADS * HEAD_DIM
KV_WIDTH = ATTN_KV_HEADS * HEAD_DIM
HGRN_WIDTH = HGRN_HEADS * HGRN_D
COL_K = 0
COL_V = COL_K + KV_WIDTH
COL_FF = COL_V + KV_WIDTH
COL_FB = COL_FF + HGRN_WIDTH
COL_I = COL_FB + HGRN_WIDTH
COL_Q = COL_I + HGRN_WIDTH
COL_QH = COL_Q + ATTN_WIDTH
COL_G = COL_QH + HGRN_WIDTH
N_IN_COLS = COL_G + HGRN_WIDTH
PEER_HEADS = 8
PEER_NKEYS = 128
PEER_DHALF = 128
PEER_TOPK = 16
PEER_SLOTS = PEER_HEADS * PEER_TOPK

LANES = 128
SUBLANES = 8
VMEM_LIMIT_BYTES = 56 * 1024 * 1024

ROW_TILE = 256
HGRN_CHUNK = 128
HGRN_STEP = 512
HGRN_HEADS_PER_STEP = 8
NEG_INF = float("-inf")


def _cparams(*sem):
    return pltpu.CompilerParams(dimension_semantics=sem, vmem_limit_bytes=VMEM_LIMIT_BYTES)


def _resident(shape):
    nd = len(shape)
    return pl.BlockSpec(shape, lambda *_: (0,) * nd, pipeline_mode=pl.Buffered(1))


def _ada_kernel(c_ref, w_ref, b_ref, o_ref):
    c = c_ref[...]
    s = (c * jax.nn.sigmoid(c)).astype(BF16)
    o_ref[...] = jnp.dot(s, w_ref[...].astype(BF16), preferred_element_type=F32) + b_ref[...]


def _ada(cvecs, w, b):
    n = w.shape[1]
    tn = 1024
    return pl.pallas_call(
        _ada_kernel,
        grid=(n // tn,),
        in_specs=[pl.BlockSpec((SUBLANES, D_MODEL), lambda j: (0, 0)),
                  pl.BlockSpec((D_MODEL, tn), lambda j: (0, j)),
                  pl.BlockSpec((1, tn), lambda j: (0, j))],
        out_specs=pl.BlockSpec((SUBLANES, tn), lambda j: (0, j)),
        out_shape=jax.ShapeDtypeStruct((SUBLANES, n), F32),
        compiler_params=_cparams("arbitrary"),
        name="ada",
    )(cvecs, w, b.reshape(1, n))


def _pack_bf16_pairs(x):
    w = x.shape[1] // 2
    lo = pltpu.bitcast(x[:, :w].astype(BF16).astype(F32), jnp.uint32)
    hi = pltpu.bitcast(x[:, w:].astype(BF16).astype(F32), jnp.uint32)
    return pltpu.bitcast((hi & jnp.uint32(0xFFFF0000)) | (lo >> 16), I32)


def _rms(x, gain):
    return x * lax.rsqrt(jnp.mean(x * x, axis=-1, keepdims=True) + EPS) * gain


def _rope(x, cos, sin_a, sin_b):
    q = HEAD_DIM // 4
    return x * cos + pltpu.roll(x, HEAD_DIM - q, 1) * sin_a + pltpu.roll(x, q, 1) * sin_b


def _inproj_kernel(x_ref, sh_ref, sc_ref, gain_ref, w_ref, cos_ref, sa_ref, sb_ref, qg_ref, kg_ref,
                   k_ref, v_ref, ff_ref, fb_ref, i_ref, q_ref, qh_ref, g_ref):
    x = x_ref[0]
    h = _rms(x, gain_ref[...]) * (1.0 + sc_ref[0]) + sh_ref[0]
    hb = h.astype(BF16)

    def seg(lo, width):
        return jnp.dot(hb, w_ref[:, lo:lo + width], preferred_element_type=F32)

    cos, sa, sb = cos_ref[...], sa_ref[...], sb_ref[...]

    def normed_heads(p, gain, n_heads, out_ref):
        for hd in range(n_heads):
            ph = p[:, hd * HEAD_DIM:(hd + 1) * HEAD_DIM]
            out_ref[0, :, hd * HEAD_DIM:(hd + 1) * HEAD_DIM] = _rope(_rms(ph, gain), cos, sa, sb).astype(BF16)

    normed_heads(seg(COL_K, KV_WIDTH), kg_ref[...], ATTN_KV_HEADS, k_ref)
    v_ref[0] = seg(COL_V, KV_WIDTH).astype(BF16)
    ff_ref[0] = seg(COL_FF, HGRN_WIDTH)
    fb_ref[0] = seg(COL_FB, HGRN_WIDTH)
    i_ref[0] = seg(COL_I, HGRN_WIDTH).astype(BF16)
    normed_heads(seg(COL_Q, ATTN_WIDTH), qg_ref[...], ATTN_HEADS, q_ref)
    qh_ref[0] = seg(COL_QH, HGRN_WIDTH).astype(BF16)
    g_ref[0] = seg(COL_G, HGRN_WIDTH).astype(BF16)


def _inproj(x, shift, scale, gain, w_bf16, rope, q_gain, k_gain):
    B, T, _ = x.shape
    tm = min(ROW_TILE, T)
    row = lambda w: pl.BlockSpec((1, tm, w), lambda b, t: (b, t, 0))
    vec = pl.BlockSpec((1, 1, D_MODEL), lambda b, t: (b, 0, 0))
    tab = pl.BlockSpec((tm, HEAD_DIM), lambda b, t: (t, 0))
    out_w = [(KV_WIDTH, BF16), (KV_WIDTH, BF16), (HGRN_WIDTH, F32), (HGRN_WIDTH, F32), (HGRN_WIDTH, BF16),
             (ATTN_WIDTH, BF16), (HGRN_WIDTH, BF16), (HGRN_WIDTH, BF16)]
    return pl.pallas_call(
        _inproj_kernel,
        grid=(B, T // tm),
        in_specs=[row(D_MODEL), vec, vec, _resident((1, D_MODEL)), _resident((D_MODEL, N_IN_COLS)),
                  tab, tab, tab, _resident((1, HEAD_DIM)), _resident((1, HEAD_DIM))],
        out_specs=[row(w) for w, _ in out_w],
        out_shape=[jax.ShapeDtypeStruct((B, T, w), dt) for w, dt in out_w],
        compiler_params=_cparams("parallel", "parallel"),
        name="inproj",
    )(x, shift, scale, gain.reshape(1, D_MODEL), w_bf16, *rope,
      q_gain.reshape(1, HEAD_DIM), k_gain.reshape(1, HEAD_DIM))


def _rope_tables(T):
    rows = T // GRID_W
    row_pos = jnp.repeat(jnp.arange(rows, dtype=F32), GRID_W)
    col_pos = jnp.tile(jnp.arange(GRID_W, dtype=F32), rows)
    half = HEAD_DIM // 2
    inv_freq = jnp.power(ROPE_THETA, -jnp.arange(0, half, 2, dtype=F32) / half)
    ang_r = row_pos[:, None] * inv_freq
    ang_c = col_pos[:, None] * inv_freq
    cr, sr, cc, sc = jnp.cos(ang_r), jnp.sin(ang_r), jnp.cos(ang_c), jnp.sin(ang_c)
    z = jnp.zeros_like(sr)
    return (jnp.concatenate([cr, cr, cc, cc], -1),
            jnp.concatenate([-sr, z, -sc, z], -1),
            jnp.concatenate([z, sr, z, sc], -1))


def _identity_rope(T):
    return (jnp.ones((T, HEAD_DIM), F32), jnp.zeros((T, HEAD_DIM), F32), jnp.zeros((T, HEAD_DIM), F32))


_NT = (((1,), (1,)), ((), ()))
_TN = (((0,), (0,)), ((), ()))


def _attn_kernel(sink_ref, q_ref, kp_ref, kc_ref, kn_ref, vp_ref, vc_ref, vn_ref, kx_ref, vx_ref, o_ref):
    n = pl.program_id(1)
    nb = pl.num_programs(1)
    blk = BAND_BLOCK
    rows = ATTN_GROUP * blk
    qi = lax.broadcasted_iota(I32, (rows, blk), 0) & (blk - 1)
    kj = lax.broadcasted_iota(I32, (rows, blk), 1)
    prev_ok = kj >= qi + jnp.where(n > 0, 0, blk)
    next_ok = kj <= qi - jnp.where(n < nb - 1, 0, blk)
    scale = HEAD_DIM ** -0.5
    for h in range(ATTN_KV_HEADS):
        cs = slice(h * HEAD_DIM, (h + 1) * HEAD_DIM)
        heads = [h * ATTN_GROUP + g for g in range(ATTN_GROUP)]
        q4 = jnp.concatenate([q_ref[0, :, hd * HEAD_DIM:(hd + 1) * HEAD_DIM] for hd in heads], axis=0)

        def scores(k_ref):
            return lax.dot_general(q4, k_ref[0, :, cs], _NT, preferred_element_type=F32) * scale

        s_p = jnp.where(prev_ok, scores(kp_ref), NEG_INF)
        s_c = scores(kc_ref)
        s_n = jnp.where(next_ok, scores(kn_ref), NEG_INF)
        s_x = scores(kx_ref)
        sink = jnp.concatenate([jnp.full((blk, 1), sink_ref[hd], F32) for hd in heads], axis=0)
        def lane_tiles(t):
            return [t[:, i:i + LANES] for i in range(0, t.shape[1], LANES)]

        tiles = [s_p, s_c, s_n] + lane_tiles(s_x)
        m = functools.reduce(jnp.maximum, tiles)
        m = jnp.maximum(jnp.max(m, -1, keepdims=True), sink)
        p_p, p_c, p_n, p_x = jnp.exp(s_p - m), jnp.exp(s_c - m), jnp.exp(s_n - m), jnp.exp(s_x - m)
        total = functools.reduce(jnp.add, [p_p, p_c, p_n] + lane_tiles(p_x))
        denom = jnp.sum(total, -1, keepdims=True) + jnp.exp(sink - m)

        def pv(p, v_ref):
            return jnp.dot(p.astype(BF16), v_ref[0, :, cs], preferred_element_type=F32)

        o = (pv(p_p, vp_ref) + pv(p_c, vc_ref) + pv(p_n, vn_ref) + pv(p_x, vx_ref)) / denom
        for g, hd in enumerate(heads):
            o_ref[0, :, hd * HEAD_DIM:(hd + 1) * HEAD_DIM] = o[g * blk:(g + 1) * blk].astype(BF16)


def _attention(q, k, v, k_ctx, v_ctx, sink):
    B, T, _ = q.shape
    L = k_ctx.shape[1]
    nb = T // BAND_BLOCK
    kv = lambda f: pl.BlockSpec((1, BAND_BLOCK, KV_WIDTH), lambda b, n: (b, f(n), 0))
    prev, cur, nxt = (lambda n: jnp.maximum(n - 1, 0)), (lambda n: n), (lambda n: jnp.minimum(n + 1, nb - 1))
    ctx = pl.BlockSpec((1, L, KV_WIDTH), lambda b, n: (b, 0, 0))
    return pl.pallas_call(
        _attn_kernel,
        grid=(B, nb),
        in_specs=[pl.BlockSpec(memory_space=pltpu.SMEM),
                  pl.BlockSpec((1, BAND_BLOCK, ATTN_WIDTH), lambda b, n: (b, n, 0)),
                  kv(prev), kv(cur), kv(nxt), kv(prev), kv(cur), kv(nxt), ctx, ctx],
        out_specs=pl.BlockSpec((1, BAND_BLOCK, ATTN_WIDTH), lambda b, n: (b, n, 0)),
        out_shape=jax.ShapeDtypeStruct((B, T, ATTN_WIDTH), BF16),
        compiler_params=_cparams("parallel", "parallel"),
        name="attn",
    )(sink, q, k, k, k, v, v, v, k_ctx, v_ctx)


_LOG2E = 1.4426950408889634
_DIAG = SUBLANES
_LEVELS = (64, 32, 16, 8)


def _hgrn_chunk(logit, v, q, lb, st_ref, reverse):
    C = HGRN_CHUNK
    f = lb + (1.0 - lb) * jax.nn.sigmoid(logit)
    lf = jnp.log(f)
    kk = 1.0 - f
    qf = q.astype(F32)
    r = lax.broadcasted_iota(I32, (C, C), 0)
    c = lax.broadcasted_iota(I32, (C, C), 1)
    incl = (r <= c) if reverse else (r >= c)
    a = jnp.dot(incl.astype(F32), lf, precision=lax.Precision.HIGHEST, preferred_element_type=F32) * _LOG2E
    a_end = a[0:1] if reverse else a[C - 1:C]

    st = st_ref[...]
    inter = lax.dot_general((qf * jnp.exp2(a)).astype(BF16), st.astype(BF16), _NT, preferred_element_type=F32)
    kd = (kk * jnp.exp2(a_end - a)).astype(BF16)
    st_ref[...] = st * jnp.exp2(a_end) + lax.dot_general(v, kd, _TN, preferred_element_type=F32)

    later = (r < c) if reverse else (r > c)
    att = jnp.zeros((C, C), F32)
    for m in _LEVELS:
        a3 = a.reshape(C // (2 * m), 2 * m, HGRN_D)
        edge = a3[:, m:m + 1, :] if reverse else a3[:, m - 1:m, :]
        e = jnp.exp2(-jnp.abs(a3 - edge)).reshape(C, HGRN_D)
        p = lax.dot_general((qf * e).astype(BF16), (kk * e).astype(BF16), _NT, preferred_element_type=F32)
        pair = ((r ^ c) >> (m.bit_length() - 1)) == 1
        att = jnp.where(pair & later, p, att)

    lane = lax.broadcasted_iota(I32, (_DIAG, C), 1)
    sub = lax.broadcasted_iota(I32, (_DIAG, C), 0)
    keep = [(lane == s) & ((sub <= s) if reverse else (sub >= s)) for s in range(_DIAG)]
    blocks = []
    for j in range(C // _DIAG):
        rows = slice(j * _DIAG, (j + 1) * _DIAG)
        a_j, q_j, k_j = a[rows], qf[rows], kk[rows]
        blk = jnp.zeros((_DIAG, C), F32)
        for s in range(_DIAG):
            e = jnp.exp2(a_j - a_j[s:s + 1])
            col = jnp.sum(q_j * e * k_j[s:s + 1], axis=-1, keepdims=True)
            blk = jnp.where(keep[s], col, blk)
        blocks.append(pltpu.roll(blk, j * _DIAG, 1) if j else blk)
    att = att + jnp.concatenate(blocks, axis=0)
    return inter + jnp.dot(att.astype(BF16), v, preferred_element_type=F32)


def _hgrn_kernel(ff_ref, vf_ref, qf_ref, fb_ref, vb_ref, qb_ref, lbf_ref, lbb_ref, s0f_ref, s0b_ref,
                 of_ref, ob_ref, sf_ref, sb_ref, stf, stb):
    step = pl.program_id(2)
    nsub = ff_ref.shape[1] // HGRN_CHUNK

    @pl.when(step == 0)
    def _():
        stf[...] = s0f_ref[0]
        stb[...] = s0b_ref[0]

    def body(j, carry):
        fo = pl.multiple_of(j * HGRN_CHUNK, HGRN_CHUNK)
        bo = pl.multiple_of((nsub - 1 - j) * HGRN_CHUNK, HGRN_CHUNK)
        for hh in range(HGRN_HEADS_PER_STEP):
            cs = slice(hh * HGRN_D, (hh + 1) * HGRN_D)
            rows = pl.ds(fo, HGRN_CHUNK)
            of_ref[0, rows, cs] = _hgrn_chunk(ff_ref[0, rows, cs], vf_ref[0, rows, cs], qf_ref[0, rows, cs],
                                              lbf_ref[:, cs], stf.at[hh], False)
            rows = pl.ds(bo, HGRN_CHUNK)
            ob_ref[0, rows, cs] = _hgrn_chunk(fb_ref[0, rows, cs], vb_ref[0, rows, cs], qb_ref[0, rows, cs],
                                              lbb_ref[:, cs], stb.at[hh], True)
        return carry

    lax.fori_loop(0, nsub, body, 0)

    @pl.when(step == pl.num_programs(2) - 1)
    def _():
        sf_ref[0] = stf[...]
        sb_ref[0] = stb[...]


def _hgrn(ff, fb, val, q, lb_f, lb_b, s0f, s0b):
    B, T, _ = ff.shape
    ts = min(HGRN_STEP, T)
    ns = T // ts
    hp = HGRN_HEADS_PER_STEP
    fwd = pl.BlockSpec((1, ts, hp * HGRN_D), lambda b, h, s: (b, s, h))
    bwd = pl.BlockSpec((1, ts, hp * HGRN_D), lambda b, h, s: (b, ns - 1 - s, h))
    lbs = pl.BlockSpec((1, hp * HGRN_D), lambda b, h, s: (0, h))
    st = pl.BlockSpec((1, hp, HGRN_D, HGRN_D), lambda b, h, s: (b, h, 0, 0))
    o_sds = jax.ShapeDtypeStruct((B, T, HGRN_WIDTH), F32)
    s_sds = jax.ShapeDtypeStruct((B, HGRN_HEADS, HGRN_D, HGRN_D), F32)
    return pl.pallas_call(
        _hgrn_kernel,
        grid=(B, HGRN_HEADS // hp, ns),
        in_specs=[fwd, fwd, fwd, bwd, bwd, bwd, lbs, lbs, st, st],
        out_specs=[fwd, bwd, st, st],
        out_shape=[o_sds, o_sds, s_sds, s_sds],
        scratch_shapes=[pltpu.VMEM((hp, HGRN_D, HGRN_D), F32), pltpu.VMEM((hp, HGRN_D, HGRN_D), F32)],
        compiler_params=_cparams("parallel", "parallel", "arbitrary"),
        name="hgrn",
    )(ff, val, q, fb, val, q, lb_f, lb_b, s0f, s0b)


def _outproj_kernel(attn_ref, of_ref, ob_ref, g_ref, x_ref, g1_ref, sh2_ref, sc2_ref, og_ref, nf_ref,
                    wo_ref, wq_ref, sk_ref, x1_ref, h2_ref, s_ref):
    o = of_ref[0] + ob_ref[0]
    og = og_ref[...]
    parts = []
    for hd in range(HGRN_HEADS):
        cs = slice(hd * HGRN_D, (hd + 1) * HGRN_D)
        gh = g_ref[0, :, cs].astype(F32)
        parts.append((_rms(o[:, cs], og) * (gh * jax.nn.sigmoid(gh))).astype(BF16))
    hg = jnp.concatenate(parts, axis=-1)
    mix = (jnp.dot(attn_ref[0], wo_ref[:ATTN_WIDTH], preferred_element_type=F32)
           + jnp.dot(hg, wo_ref[ATTN_WIDTH:], preferred_element_type=F32))
    x1 = x_ref[0] + g1_ref[0] * mix
    x1_ref[0] = x1
    h2 = _rms(x1, nf_ref[...]) * (1.0 + sc2_ref[0]) + sh2_ref[0]
    h2_ref[0] = _pack_bf16_pairs(h2)
    pq = jnp.dot(h2.astype(BF16), wq_ref[...], preferred_element_type=F32)
    for hp in range(2 * PEER_HEADS):
        cs = slice(hp * PEER_DHALF, (hp + 1) * PEER_DHALF)
        for jb in range(pq.shape[0] // LANES):
            rows = slice(jb * LANES, (jb + 1) * LANES)
            s_ref[hp, 0, jb * PEER_NKEYS:(jb + 1) * PEER_NKEYS, :] = lax.dot_general(
                sk_ref[hp], pq[rows, cs].astype(BF16), _NT, preferred_element_type=F32)


def _outproj(attn, of, ob, g, x, g1, sh2, sc2, o_gain, norm_ffn, wo_bf16, wq_bf16, sk_bf16):
    B, T, _ = x.shape
    tm = min(ROW_TILE, T)
    nt = T // tm
    per = TOPK_TOKENS // tm
    assert (B * T) % TOPK_TOKENS == 0 and TOPK_TOKENS % tm == 0 and tm % LANES == 0
    row = lambda w: pl.BlockSpec((1, tm, w), lambda b, t: (b, t, 0))
    vec = pl.BlockSpec((1, 1, D_MODEL), lambda b, t: (b, 0, 0))
    n_sk = 2 * PEER_HEADS
    sds = lambda w: jax.ShapeDtypeStruct((B, T, w), F32)
    s_rows = tm // LANES * PEER_NKEYS
    s_spec = pl.BlockSpec((n_sk, 1, s_rows, LANES), lambda b, t: (0, (b * nt + t) // per, (b * nt + t) % per, 0))
    s_sds = jax.ShapeDtypeStruct((n_sk, B * T // TOPK_TOKENS, TOPK_TOKENS // LANES * PEER_NKEYS, LANES), F32)
    return pl.pallas_call(
        _outproj_kernel,
        grid=(B, T // tm),
        in_specs=[row(ATTN_WIDTH), row(HGRN_WIDTH), row(HGRN_WIDTH), row(HGRN_WIDTH), row(D_MODEL),
                  vec, vec, vec, _resident((1, HGRN_D)), _resident((1, D_MODEL)),
                  _resident((ATTN_WIDTH + HGRN_WIDTH, D_MODEL)), _resident((D_MODEL, n_sk * PEER_DHALF)),
                  _resident((n_sk, PEER_NKEYS, PEER_DHALF))],
        out_specs=[row(D_MODEL), row(D_MODEL // 2), s_spec],
        out_shape=[sds(D_MODEL), jax.ShapeDtypeStruct((B, T, D_MODEL // 2), I32), s_sds],
        compiler_params=_cparams("parallel", "parallel"),
        name="outproj",
    )(attn, of, ob, g, x, g1, sh2, sc2, o_gain.reshape(1, HGRN_D), norm_ffn.reshape(1, D_MODEL),
      wo_bf16, wq_bf16, sk_bf16)


TOPK_TOKENS = SUBLANES * LANES
_CAND_PAIRS = [(a, b) for a in range(PEER_TOPK) for b in range(PEER_TOPK) if (a + 1) * (b + 1) <= PEER_TOPK]


def _first_argmax(values, ids, n_chains):
    per = -(-len(values) // n_chains)
    parts = []
    for lo in range(0, len(values), per):
        m, i = values[lo], ids[lo]
        if not isinstance(i, jax.Array):
            i = jnp.full(m.shape, i, F32)
        for v, vid in zip(values[lo + 1:lo + per], ids[lo + 1:lo + per]):
            c = v > m
            m = jnp.where(c, v, m)
            i = jnp.where(c, vid, i)
        parts.append((m, i))
    m, i = parts[0]
    for pm, pi in parts[1:]:
        c = pm > m
        m = jnp.where(c, pm, m)
        i = jnp.where(c, pi, i)
    return m, i


def _topk_kernel(s_ref, idx_ref, gate_ref, wk, tv, ti, cv, ci, bv):
    shape = (SUBLANES, LANES)
    none = jnp.full(shape, -1.0, F32)

    def head(h, carry):
        for p in range(2):
            hp = 2 * h + p
            for k in range(PEER_NKEYS):
                wk[k] = s_ref[hp, 0, pl.ds(k, SUBLANES, stride=PEER_NKEYS), :]

            def extract(r, prev, p=p):
                vals = []
                for k in range(PEER_NKEYS):
                    s = jnp.where(prev == float(k), NEG_INF, wk[k])
                    wk[k] = s
                    vals.append(s)
                m, i = _first_argmax(vals, [float(k) for k in range(PEER_NKEYS)], 4)
                tv[p, r] = m
                ti[p, r] = i
                return i

            lax.fori_loop(0, PEER_TOPK, extract, none)

        for c, (a, b) in enumerate(_CAND_PAIRS):
            cv[c] = tv[0, a] + tv[1, b]
            ci[c] = ti[0, a] * float(PEER_NKEYS) + ti[1, b]

        def pick(r, prev):
            vals, ids = [], []
            for c in range(len(_CAND_PAIRS)):
                cid = ci[c]
                s = jnp.where(cid == prev, NEG_INF, cv[c])
                cv[c] = s
                vals.append(s)
                ids.append(cid)
            m, i = _first_argmax(vals, ids, 2)
            bv[r] = m
            idx_ref[h * PEER_TOPK + r, 0] = i.astype(I32)
            return i

        lax.fori_loop(0, PEER_TOPK, pick, none)

        es = [jnp.exp(bv[r] - bv[0]) for r in range(PEER_TOPK)]
        tot = es[0]
        for e in es[1:]:
            tot = tot + e
        for r in range(PEER_TOPK):
            gate_ref[h * PEER_TOPK + r, 0] = es[r] / tot
        return carry

    lax.fori_loop(0, PEER_HEADS, head, 0)


def _topk(s):
    nt = s.shape[1]
    vreg = (SUBLANES, LANES)
    out_spec = pl.BlockSpec((PEER_SLOTS, 1) + vreg, lambda i: (0, i, 0, 0))
    return pl.pallas_call(
        _topk_kernel,
        grid=(nt,),
        in_specs=[pl.BlockSpec((s.shape[0], 1) + s.shape[2:], lambda i: (0, i, 0, 0))],
        out_specs=[out_spec, out_spec],
        out_shape=[jax.ShapeDtypeStruct((PEER_SLOTS, nt) + vreg, I32),
                   jax.ShapeDtypeStruct((PEER_SLOTS, nt) + vreg, F32)],
        scratch_shapes=[pltpu.VMEM((PEER_NKEYS,) + vreg, F32),
                        pltpu.VMEM((2, PEER_TOPK) + vreg, F32), pltpu.VMEM((2, PEER_TOPK) + vreg, F32),
                        pltpu.VMEM((len(_CAND_PAIRS),) + vreg, F32), pltpu.VMEM((len(_CAND_PAIRS),) + vreg, F32),
                        pltpu.VMEM((PEER_TOPK,) + vreg, F32)],
        compiler_params=_cparams("parallel"),
        name="topk",
    )(s)


SC_CORES = 2
SC_SUBCORES = 16
SC_LANES = 16
SC_WORKERS = SC_CORES * SC_SUBCORES
PEER_GROUP = 16
SC_BUFFERS = 4
PEER_ITEMS = PEER_GROUP * PEER_HEADS
SC_UNROLL = 8
PACKED_WORDS = D_MODEL // 2
EXPERT_SLAB = (PACKED_WORDS // LANES, LANES)


_SC_PARAMS = pltpu.CompilerParams(needs_layout_passes=False)


def _sc_mesh():
    return plsc.VectorSubcoreMesh(core_axis_name="c", subcore_axis_name="s")


def _sc_worker():
    return lax.axis_index("s") * SC_CORES + lax.axis_index("c")


def _pack_kernel(t_ref, o_ref):
    word = _pack_bf16_pairs(t_ref[...])
    for s in range(EXPERT_SLAB[0]):
        o_ref[:, s, :] = word[:, s * LANES:(s + 1) * LANES]


def _pack_table(table):
    e = table.shape[0]
    tr = 256
    return pl.pallas_call(
        _pack_kernel,
        grid=(e // tr,),
        in_specs=[pl.BlockSpec((tr, D_MODEL), lambda i: (i, 0))],
        out_specs=pl.BlockSpec((tr,) + EXPERT_SLAB, lambda i: (i, 0, 0)),
        out_shape=jax.ShapeDtypeStruct((e,) + EXPERT_SLAB, I32),
        compiler_params=_cparams("parallel"),
        name="pack_table",
    )(table)


def _row_words(rows, k, first=0, count=PEER_TOPK):
    per = LANES // SC_LANES
    sub, cols = k // per, pl.ds((k % per) * SC_LANES, SC_LANES)
    return tuple(rows[r, sub, cols] for r in range(first, first + count))


def _bf16_lanes(word):
    return plsc.bitcast(word, BF16)


def _f32_halves(pairs):
    return plsc.unpack(pairs, format=plsc.PackFormat.INTERLEAVED)


def _half_cols(k):
    return pl.ds(k * SC_LANES, SC_LANES), pl.ds(PACKED_WORDS + k * SC_LANES, SC_LANES)


def _sc_item_pipeline(table_hbm, idx_v, ring, sems, compute):
    nb = SC_BUFFERS

    def gather(j):
        b = j % nb
        return pltpu.make_async_copy(table_hbm.at[idx_v.at[j]], ring.at[b], sems.at[b])

    for j in range(nb - 1):
        gather(j).start()

    @pl.loop(0, PEER_ITEMS)
    def _(j):
        @pl.when(j + nb - 1 < PEER_ITEMS)
        def _():
            gather(j + nb - 1).start()

        gather(j).wait()
        compute(j, ring.at[j % nb])


def _peer_dot_kernel(x_hbm, idx_hbm, u_hbm, a_hbm, x_v, idx_v, a_v, ring, sems):
    tok_per_w = x_hbm.shape[0] // SC_WORKERS
    wid = _sc_worker()
    lane = lax.iota(I32, SC_LANES)

    def compute(j, rows):
        t = j // PEER_HEADS

        zero = jnp.zeros((SC_LANES,), F32)

        half = PEER_TOPK // 2
        steps = PACKED_WORDS // SC_LANES // 2

        def load(kp, first):
            out = ()
            for k in (2 * kp, 2 * kp + 1):
                out += (x_v[t, pl.ds(k * SC_LANES, SC_LANES)],) + _row_words(rows, k, first, half)
            return out

        def fma(acc, vals):
            x0, x1 = _bf16_lanes(vals[0]), _bf16_lanes(vals[1 + half])
            out = []
            for i in range(half):
                lo, hi = _f32_halves(_bf16_lanes(vals[1 + i]) * x0 + _bf16_lanes(vals[2 + half + i]) * x1)
                out.append(acc[i] + lo + hi)
            return tuple(out)

        @plsc.parallel_loop(0, steps, unroll=16, carry=((zero,) * half, (zero,) * half, load(0, half)))
        def state(kp, state):
            acc_a, acc_b, vals_b = state
            vals_a = load(kp, 0)
            acc_b = fma(acc_b, vals_b)
            vals_b = load(jnp.minimum(kp + 1, steps - 1), half)
            return fma(acc_a, vals_a), acc_b, vals_b

        acc = state[0] + state[1]
        out = zero
        for r in range(PEER_TOPK):
            out = jnp.where(lane == r, jnp.sum(acc[r]), out)
        a_v[t, pl.ds((j % PEER_HEADS) * PEER_TOPK, PEER_TOPK)] = out

    @pl.loop(0, tok_per_w // PEER_GROUP)
    def _(g):
        tok0 = wid * tok_per_w + g * PEER_GROUP
        item0 = tok0 * PEER_HEADS
        pltpu.sync_copy(x_hbm.at[pl.ds(tok0, PEER_GROUP)], x_v)
        pltpu.sync_copy(idx_hbm.at[pl.ds(item0, PEER_ITEMS)], idx_v)
        _sc_item_pipeline(u_hbm, idx_v, ring, sems, compute)
        pltpu.sync_copy(a_v, a_hbm.at[pl.ds(tok0, PEER_GROUP)])


def _peer_sum_kernel(w_hbm, idx_hbm, v_hbm, o_hbm, w_v, idx_v, o_v, ring, sems):
    tok_per_w = o_hbm.shape[0] // SC_WORKERS
    wid = _sc_worker()
    zero = jnp.zeros((SC_LANES,), F32)

    def compute(j, rows):
        t = j // PEER_HEADS
        tv = jnp.full((SC_LANES,), t, I32)
        slot0 = (j % PEER_HEADS) * PEER_TOPK
        ws = []
        for r in range(PEER_TOPK):
            wr = plsc.load_gather(w_v, [tv, jnp.full((SC_LANES,), slot0 + r, I32)])
            ws.append(plsc.pack(wr, wr, format=plsc.PackFormat.INTERLEAVED))

        def load(k):
            lo, hi = _half_cols(k)
            return (o_v[t, lo], o_v[t, hi]) + _row_words(rows, k)

        def finish(k, vals):
            acc_lo, acc_hi = vals[0], vals[1]
            for r in range(0, PEER_TOPK, 2):
                lo, hi = _f32_halves(_bf16_lanes(vals[2 + r]) * ws[r] + _bf16_lanes(vals[3 + r]) * ws[r + 1])
                acc_lo = acc_lo + lo
                acc_hi = acc_hi + hi
            lo, hi = _half_cols(k)
            o_v[t, lo] = acc_lo
            o_v[t, hi] = acc_hi

        @plsc.parallel_loop(1, PACKED_WORDS // SC_LANES, unroll=SC_UNROLL, carry=load(0))
        def vals(k, vals):
            nxt = load(k)
            finish(k - 1, vals)
            return nxt

        finish(PACKED_WORDS // SC_LANES - 1, vals)

    @pl.loop(0, tok_per_w // PEER_GROUP)
    def _(g):
        tok0 = wid * tok_per_w + g * PEER_GROUP
        item0 = tok0 * PEER_HEADS
        pltpu.sync_copy(w_hbm.at[pl.ds(tok0, PEER_GROUP)], w_v)
        pltpu.sync_copy(idx_hbm.at[pl.ds(item0, PEER_ITEMS)], idx_v)

        @pl.loop(0, PEER_GROUP)
        def _(t):
            @plsc.parallel_loop(0, D_MODEL // SC_LANES, unroll=16)
            def _(k):
                o_v[t, pl.ds(k * SC_LANES, SC_LANES)] = zero

        _sc_item_pipeline(v_hbm, idx_v, ring, sems, compute)
        pltpu.sync_copy(o_v, o_hbm.at[pl.ds(tok0, PEER_GROUP)])


def _sc_scratch(first, last):
    return ([first, pltpu.VMEM((PEER_ITEMS, PEER_TOPK), I32), last]
            + [pltpu.VMEM((SC_BUFFERS, PEER_TOPK) + EXPERT_SLAB, I32), pltpu.SemaphoreType.DMA((SC_BUFFERS,))])


def _peer_dot(h2, idx, u):
    n = h2.shape[0]
    assert n % (SC_WORKERS * PEER_GROUP) == 0
    call = pl.kernel(
        _peer_dot_kernel,
        out_type=jax.ShapeDtypeStruct((n, PEER_SLOTS), F32),
        mesh=_sc_mesh(),
        scratch_types=_sc_scratch(pltpu.VMEM((PEER_GROUP, PACKED_WORDS), I32),
                                  pltpu.VMEM((PEER_GROUP, PEER_SLOTS), F32)),
        compiler_params=_SC_PARAMS,
    )
    return call(h2, idx, u)


def _peer_sum(w, idx, v, n):
    assert n % (SC_WORKERS * PEER_GROUP) == 0
    call = pl.kernel(
        _peer_sum_kernel,
        out_type=jax.ShapeDtypeStruct((n, D_MODEL), F32),
        mesh=_sc_mesh(),
        scratch_types=_sc_scratch(pltpu.VMEM((PEER_GROUP, PEER_SLOTS), F32),
                                  pltpu.VMEM((PEER_GROUP, D_MODEL), F32)),
        compiler_params=_SC_PARAMS,
    )
    return call(w, idx, v)


def _gelu_gate_kernel(a_ref, g_ref, w_ref):
    a = a_ref[...]
    w_ref[...] = g_ref[...] * (0.5 * a * (1.0 + lax.erf(a * (2.0 ** -0.5))))


def _gelu_gate(a, gate):
    n = a.shape[0]
    tm = min(2048, n)
    spec = pl.BlockSpec((tm, PEER_SLOTS), lambda i: (i, 0))
    return pl.pallas_call(
        _gelu_gate_kernel, grid=(n // tm,), in_specs=[spec, spec], out_specs=spec,
        out_shape=jax.ShapeDtypeStruct(a.shape, F32), compiler_params=_cparams("parallel"), name="gelu_gate",
    )(a, gate)


def _residual_kernel(x_ref, g_ref, p_ref, o_ref):
    o_ref[0] = x_ref[0] + g_ref[0] * p_ref[0]


def _residual(x1, g2, peer):
    B, T, _ = x1.shape
    tm = min(2 * ROW_TILE, T)
    row = pl.BlockSpec((1, tm, D_MODEL), lambda b, t: (b, t, 0))
    vec = pl.BlockSpec((1, 1, D_MODEL), lambda b, t: (b, 0, 0))
    return pl.pallas_call(
        _residual_kernel, grid=(B, T // tm), in_specs=[row, vec, row], out_specs=row,
        out_shape=jax.ShapeDtypeStruct(x1.shape, F32), compiler_params=_cparams("parallel", "parallel"),
        name="residual",
    )(x1, g2, peer)


def kernel(x, c, ctx, c_ctx, w_ada, b_ada, norm_mix, norm_ffn, w_in, q_norm, k_norm, attn_sink, hgrn_lb_logits,
           hgrn_norm, w_out, peer_w_q, peer_sub_keys, peer_u, peer_v):
    assert w_ada.shape[0] == 1, "single-layer block"
    B, T, D = x.shape
    L = ctx.shape[1]
    n = B * T

    cvecs = jnp.zeros((SUBLANES, D), F32).at[:B].set(c).at[B].set(c_ctx)
    mod = _ada(cvecs, w_ada[0], b_ada[0])
    part = lambda rows, i: rows[:, None, i * D:(i + 1) * D]
    mod_x = mod[:B]
    mod_c = jnp.broadcast_to(mod[B:B + 1], (B, 6 * D))
    sh1, sc1, g1, sh2, sc2, g2 = (part(mod_x, i) for i in range(6))

    lbs = jnp.cumsum(jax.nn.softmax(hgrn_lb_logits.astype(F32), axis=1), axis=1)
    lb_f, lb_b = lbs[0, 0].reshape(1, HGRN_WIDTH), lbs[1, 0].reshape(1, HGRN_WIDTH)

    w_in_b = w_in[0].astype(BF16)
    rope = _rope_tables(T)
    kc, vc, ffc, fbc, ic, _, qhc, _ = _inproj(ctx, part(mod_c, 0), part(mod_c, 1), norm_mix[0], w_in_b,
                                              _identity_rope(L), q_norm[0], k_norm[0])
    s0 = jnp.zeros((B, HGRN_HEADS, HGRN_D, HGRN_D), F32)
    _, _, sfc, sbc = _hgrn(ffc, fbc, ic, qhc, lb_f, lb_b, s0, s0)

    sk = peer_sub_keys[0].reshape(2 * PEER_HEADS, PEER_NKEYS, PEER_DHALF).astype(BF16)
    w_out_b, w_q_b = w_out[0].astype(BF16), peer_w_q[0].astype(BF16)
    u_packed, v_packed = _pack_table(peer_u[0]), _pack_table(peer_v[0])

    kx, vx, ffx, fbx, ix, qx, qhx, gx = _inproj(x, sh1, sc1, norm_mix[0], w_in_b, rope, q_norm[0], k_norm[0])
    attn = _attention(qx, kx, vx, kc, vc, attn_sink[0])
    of, ob, _, _ = _hgrn(ffx, fbx, ix, qhx, lb_f, lb_b, sfc, sbc)
    x1, h2p, s = _outproj(attn, of, ob, gx, x, g1, sh2, sc2, hgrn_norm[0], norm_ffn[0], w_out_b, w_q_b, sk)
    idx_t, gate_t = _topk(s)
    idx16 = idx_t.reshape(PEER_SLOTS, n).T.reshape(n * PEER_HEADS, PEER_TOPK)
    gate = gate_t.reshape(PEER_SLOTS, n).T
    a = _peer_dot(h2p.reshape(n, PACKED_WORDS), idx16, u_packed)
    w = _gelu_gate(a, gate)
    peer = _peer_sum(w, idx16, v_packed, n)
    return _residual(x1, g2, peer.reshape(B, T, D))
```

```python
import functools

import jax
import jax.numpy as jnp
from jax import lax
from jax.experimental import pallas as pl
from jax.experimental.pallas import tpu as pltpu
from jax.experimental.pallas import tpu_sc as plsc

F32 = jnp.float32
BF16 = jnp.bfloat16
I32 = jnp.int32

D_MODEL = 2048
GRID_W = 64
EPS = 1e-6
HEAD_DIM = 128
ATTN_HEADS = 8
ATTN_KV_HEADS = 2
ATTN_GROUP = ATTN_HEADS // ATTN_KV_HEADS
BAND_BLOCK = 128
ROPE_THETA = 10000.0
HGRN_HEADS = 8
HGRN_D = 128
ATTN_WIDTH = ATTN_HEADS * HEAD_DIM
KV_WIDTH = ATTN_KV_HEADS * HEAD_DIM
HGRN_WIDTH = HGRN_HEADS * HGRN_D
COL_K = 0
COL_V = COL_K + KV_WIDTH
COL_FF = COL_V + KV_WIDTH
COL_FB = COL_FF + HGRN_WIDTH
COL_I = COL_FB + HGRN_WIDTH
COL_Q = COL_I + HGRN_WIDTH
COL_QH = COL_Q + ATTN_WIDTH
COL_G = COL_QH + HGRN_WIDTH
N_IN_COLS = COL_G + HGRN_WIDTH
PEER_HEADS = 8
PEER_NKEYS = 128
PEER_DHALF = 128
PEER_TOPK = 16
PEER_SLOTS = PEER_HEADS * PEER_TOPK

LANES = 128
SUBLANES = 8
VMEM_LIMIT_BYTES = 56 * 1024 * 1024

ROW_TILE = 256
HGRN_CHUNK = 128
HGRN_STEP = 512
HGRN_HEADS_PER_STEP = 8
NEG_INF = float("-inf")


def _cparams(*sem):
    return pltpu.CompilerParams(dimension_semantics=sem, vmem_limit_bytes=VMEM_LIMIT_BYTES)


def _resident(shape):
    nd = len(shape)
    return pl.BlockSpec(shape, lambda *_: (0,) * nd, pipeline_mode=pl.Buffered(1))


def _ada_kernel(c_ref, w_ref, b_ref, o_ref):
    c = c_ref[...]
    s = (c * jax.nn.sigmoid(c)).astype(BF16)
    o_ref[...] = jnp.dot(s, w_ref[...].astype(BF16), preferred_element_type=F32) + b_ref[...]


def _ada(cvecs, w, b):
    n = w.shape[1]
    tn = 1024
    return pl.pallas_call(
        _ada_kernel,
        grid=(n // tn,),
        in_specs=[pl.BlockSpec((SUBLANES, D_MODEL), lambda j: (0, 0)),
                  pl.BlockSpec((D_MODEL, tn), lambda j: (0, j)),
                  pl.BlockSpec((1, tn), lambda j: (0, j))],
        out_specs=pl.BlockSpec((SUBLANES, tn), lambda j: (0, j)),
        out_shape=jax.ShapeDtypeStruct((SUBLANES, n), F32),
        compiler_params=_cparams("arbitrary"),
        name="ada",
    )(cvecs, w, b.reshape(1, n))


def _pack_bf16_pairs(x):
    w = x.shape[1] // 2
    lo = pltpu.bitcast(x[:, :w].astype(BF16).astype(F32), jnp.uint32)
    hi = pltpu.bitcast(x[:, w:].astype(BF16).astype(F32), jnp.uint32)
    return pltpu.bitcast((hi & jnp.uint32(0xFFFF0000)) | (lo >> 16), I32)


def _rms(x, gain):
    return x * lax.rsqrt(jnp.mean(x * x, axis=-1, keepdims=True) + EPS) * gain


def _rope(x, cos, sin_a, sin_b):
    q = HEAD_DIM // 4
    return x * cos + pltpu.roll(x, HEAD_DIM - q, 1) * sin_a + pltpu.roll(x, q, 1) * sin_b


def _inproj_kernel(x_ref, sh_ref, sc_ref, gain_ref, w_ref, cos_ref, sa_ref, sb_ref, qg_ref, kg_ref,
                   k_ref, v_ref, ff_ref, fb_ref, i_ref, q_ref, qh_ref, g_ref):
    x = x_ref[0]
    h = _rms(x, gain_ref[...]) * (1.0 + sc_ref[0]) + sh_ref[0]
    hb = h.astype(BF16)

    def seg(lo, width):
        return jnp.dot(hb, w_ref[:, lo:lo + width], preferred_element_type=F32)

    cos, sa, sb = cos_ref[...], sa_ref[...], sb_ref[...]

    def normed_heads(p, gain, n_heads, out_ref):
        for hd in range(n_heads):
            ph = p[:, hd * HEAD_DIM:(hd + 1) * HEAD_DIM]
            out_ref[0, :, hd * HEAD_DIM:(hd + 1) * HEAD_DIM] = _rope(_rms(ph, gain), cos, sa, sb).astype(BF16)

    normed_heads(seg(COL_K, KV_WIDTH), kg_ref[...], ATTN_KV_HEADS, k_ref)
    v_ref[0] = seg(COL_V, KV_WIDTH).astype(BF16)
    ff_ref[0] = seg(COL_FF, HGRN_WIDTH)
    fb_ref[0] = seg(COL_FB, HGRN_WIDTH)
    i_ref[0] = seg(COL_I, HGRN_WIDTH).astype(BF16)
    normed_heads(seg(COL_Q, ATTN_WIDTH), qg_ref[...], ATTN_HEADS, q_ref)
    qh_ref[0] = seg(COL_QH, HGRN_WIDTH).astype(BF16)
    g_ref[0] = seg(COL_G, HGRN_WIDTH).astype(BF16)


def _inproj(x, shift, scale, gain, w_bf16, rope, q_gain, k_gain):
    B, T, _ = x.shape
    tm = min(ROW_TILE, T)
    row = lambda w: pl.BlockSpec((1, tm, w), lambda b, t: (b, t, 0))
    vec = pl.BlockSpec((1, 1, D_MODEL), lambda b, t: (b, 0, 0))
    tab = pl.BlockSpec((tm, HEAD_DIM), lambda b, t: (t, 0))
    out_w = [(KV_WIDTH, BF16), (KV_WIDTH, BF16), (HGRN_WIDTH, F32), (HGRN_WIDTH, F32), (HGRN_WIDTH, BF16),
             (ATTN_WIDTH, BF16), (HGRN_WIDTH, BF16), (HGRN_WIDTH, BF16)]
    return pl.pallas_call(
        _inproj_kernel,
        grid=(B, T // tm),
        in_specs=[row(D_MODEL), vec, vec, _resident((1, D_MODEL)), _resident((D_MODEL, N_IN_COLS)),
                  tab, tab, tab, _resident((1, HEAD_DIM)), _resident((1, HEAD_DIM))],
        out_specs=[row(w) for w, _ in out_w],
        out_shape=[jax.ShapeDtypeStruct((B, T, w), dt) for w, dt in out_w],
        compiler_params=_cparams("parallel", "parallel"),
        name="inproj",
    )(x, shift, scale, gain.reshape(1, D_MODEL), w_bf16, *rope,
      q_gain.reshape(1, HEAD_DIM), k_gain.reshape(1, HEAD_DIM))


def _rope_tables(T):
    rows = T // GRID_W
    row_pos = jnp.repeat(jnp.arange(rows, dtype=F32), GRID_W)
    col_pos = jnp.tile(jnp.arange(GRID_W, dtype=F32), rows)
    half = HEAD_DIM // 2
    inv_freq = jnp.power(ROPE_THETA, -jnp.arange(0, half, 2, dtype=F32) / half)
    ang_r = row_pos[:, None] * inv_freq
    ang_c = col_pos[:, None] * inv_freq
    cr, sr, cc, sc = jnp.cos(ang_r), jnp.sin(ang_r), jnp.cos(ang_c), jnp.sin(ang_c)
    z = jnp.zeros_like(sr)
    return (jnp.concatenate([cr, cr, cc, cc], -1),
            jnp.concatenate([-sr, z, -sc, z], -1),
            jnp.concatenate([z, sr, z, sc], -1))


def _identity_rope(T):
    return (jnp.ones((T, HEAD_DIM), F32), jnp.zeros((T, HEAD_DIM), F32), jnp.zeros((T, HEAD_DIM), F32))


_NT = (((1,), (1,)), ((), ()))
_TN = (((0,), (0,)), ((), ()))


def _attn_kernel(sink_ref, q_ref, kp_ref, kc_ref, kn_ref, vp_ref, vc_ref, vn_ref, kx_ref, vx_ref, o_ref):
    n = pl.program_id(1)
    nb = pl.num_programs(1)
    blk = BAND_BLOCK
    rows = ATTN_GROUP * blk
    qi = lax.broadcasted_iota(I32, (rows, blk), 0) & (blk - 1)
    kj = lax.broadcasted_iota(I32, (rows, blk), 1)
    prev_ok = kj >= qi + jnp.where(n > 0, 0, blk)
    next_ok = kj <= qi - jnp.where(n < nb - 1, 0, blk)
    scale = HEAD_DIM ** -0.5
    for h in range(ATTN_KV_HEADS):
        cs = slice(h * HEAD_DIM, (h + 1) * HEAD_DIM)
        heads = [h * ATTN_GROUP + g for g in range(ATTN_GROUP)]
        q4 = jnp.concatenate([q_ref[0, :, hd * HEAD_DIM:(hd + 1) * HEAD_DIM] for hd in heads], axis=0)

        def scores(k_ref):
            return lax.dot_general(q4, k_ref[0, :, cs], _NT, preferred_element_type=F32) * scale

        s_p = jnp.where(prev_ok, scores(kp_ref), NEG_INF)
        s_c = scores(kc_ref)
        s_n = jnp.where(next_ok, scores(kn_ref), NEG_INF)
        s_x = scores(kx_ref)
        sink = jnp.concatenate([jnp.full((blk, 1), sink_ref[hd], F32) for hd in heads], axis=0)
        def lane_tiles(t):
            return [t[:, i:i + LANES] for i in range(0, t.shape[1], LANES)]

        tiles = [s_p, s_c, s_n] + lane_tiles(s_x)
        m = functools.reduce(jnp.maximum, tiles)
        m = jnp.maximum(jnp.max(m, -1, keepdims=True), sink)
        p_p, p_c, p_n, p_x = jnp.exp(s_p - m), jnp.exp(s_c - m), jnp.exp(s_n - m), jnp.exp(s_x - m)
        total = functools.reduce(jnp.add, [p_p, p_c, p_n] + lane_tiles(p_x))
        denom = jnp.sum(total, -1, keepdims=True) + jnp.exp(sink - m)

        def pv(p, v_ref):
            return jnp.dot(p.astype(BF16), v_ref[0, :, cs], preferred_element_type=F32)

        o = (pv(p_p, vp_ref) + pv(p_c, vc_ref) + pv(p_n, vn_ref) + pv(p_x, vx_ref)) / denom
        for g, hd in enumerate(heads):
            o_ref[0, :, hd * HEAD_DIM:(hd + 1) * HEAD_DIM] = o[g * blk:(g + 1) * blk].astype(BF16)


def _attention(q, k, v, k_ctx, v_ctx, sink):
    B, T, _ = q.shape
    L = k_ctx.shape[1]
    nb = T // BAND_BLOCK
    kv = lambda f: pl.BlockSpec((1, BAND_BLOCK, KV_WIDTH), lambda b, n: (b, f(n), 0))
    prev, cur, nxt = (lambda n: jnp.maximum(n - 1, 0)), (lambda n: n), (lambda n: jnp.minimum(n + 1, nb - 1))
    ctx = pl.BlockSpec((1, L, KV_WIDTH), lambda b, n: (b, 0, 0))
    return pl.pallas_call(
        _attn_kernel,
        grid=(B, nb),
        in_specs=[pl.BlockSpec(memory_space=pltpu.SMEM),
                  pl.BlockSpec((1, BAND_BLOCK, ATTN_WIDTH), lambda b, n: (b, n, 0)),
                  kv(prev), kv(cur), kv(nxt), kv(prev), kv(cur), kv(nxt), ctx, ctx],
        out_specs=pl.BlockSpec((1, BAND_BLOCK, ATTN_WIDTH), lambda b, n: (b, n, 0)),
        out_shape=jax.ShapeDtypeStruct((B, T, ATTN_WIDTH), BF16),
        compiler_params=_cparams("parallel", "parallel"),
        name="attn",
    )(sink, q, k, k, k, v, v, v, k_ctx, v_ctx)


_LOG2E = 1.4426950408889634
_DIAG = SUBLANES
_LEVELS = (64, 32, 16, 8)


def _hgrn_chunk(logit, v, q, lb, st_ref, reverse):
    C = HGRN_CHUNK
    f = lb + (1.0 - lb) * jax.nn.sigmoid(logit)
    lf = jnp.log(f)
    kk = 1.0 - f
    qf = q.astype(F32)
    r = lax.broadcasted_iota(I32, (C, C), 0)
    c = lax.broadcasted_iota(I32, (C, C), 1)
    incl = (r <= c) if reverse else (r >= c)
    a = jnp.dot(incl.astype(F32), lf, precision=lax.Precision.HIGHEST, preferred_element_type=F32) * _LOG2E
    a_end = a[0:1] if reverse else a[C - 1:C]

    st = st_ref[...]
    inter = lax.dot_general((qf * jnp.exp2(a)).astype(BF16), st.astype(BF16), _NT, preferred_element_type=F32)
    kd = (kk * jnp.exp2(a_end - a)).astype(BF16)
    st_ref[...] = st * jnp.exp2(a_end) + lax.dot_general(v, kd, _TN, preferred_element_type=F32)

    later = (r < c) if reverse else (r > c)
    att = jnp.zeros((C, C), F32)
    for m in _LEVELS:
        a3 = a.reshape(C // (2 * m), 2 * m, HGRN_D)
        edge = a3[:, m:m + 1, :] if reverse else a3[:, m - 1:m, :]
        e = jnp.exp2(-jnp.abs(a3 - edge)).reshape(C, HGRN_D)
        p = lax.dot_general((qf * e).astype(BF16), (kk * e).astype(BF16), _NT, preferred_element_type=F32)
        pair = ((r ^ c) >> (m.bit_length() - 1)) == 1
        att = jnp.where(pair & later, p, att)

    lane = lax.broadcasted_iota(I32, (_DIAG, C), 1)
    sub = lax.broadcasted_iota(I32, (_DIAG, C), 0)
    keep = [(lane == s) & ((sub <= s) if reverse else (sub >= s)) for s in range(_DIAG)]
    blocks = []
    for j in range(C // _DIAG):
        rows = slice(j * _DIAG, (j + 1) * _DIAG)
        a_j, q_j, k_j = a[rows], qf[rows], kk[rows]
        blk = jnp.zeros((_DIAG, C), F32)
        for s in range(_DIAG):
            e = jnp.exp2(a_j - a_j[s:s + 1])
            col = jnp.sum(q_j * e * k_j[s:s + 1], axis=-1, keepdims=True)
            blk = jnp.where(keep[s], col, blk)
        blocks.append(pltpu.roll(blk, j * _DIAG, 1) if j else blk)
    att = att + jnp.concatenate(blocks, axis=0)
    return inter + jnp.dot(att.astype(BF16), v, preferred_element_type=F32)


def _hgrn_kernel(ff_ref, vf_ref, qf_ref, fb_ref, vb_ref, qb_ref, lbf_ref, lbb_ref, s0f_ref, s0b_ref,
                 of_ref, ob_ref, sf_ref, sb_ref, stf, stb):
    step = pl.program_id(2)
    nsub = ff_ref.shape[1] // HGRN_CHUNK

    @pl.when(step == 0)
    def _():
        stf[...] = s0f_ref[0]
        stb[...] = s0b_ref[0]

    def body(j, carry):
        fo = pl.multiple_of(j * HGRN_CHUNK, HGRN_CHUNK)
        bo = pl.multiple_of((nsub - 1 - j) * HGRN_CHUNK, HGRN_CHUNK)
        for hh in range(HGRN_HEADS_PER_STEP):
            cs = slice(hh * HGRN_D, (hh + 1) * HGRN_D)
            rows = pl.ds(fo, HGRN_CHUNK)
            of_ref[0, rows, cs] = _hgrn_chunk(ff_ref[0, rows, cs], vf_ref[0, rows, cs], qf_ref[0, rows, cs],
                                              lbf_ref[:, cs], stf.at[hh], False)
            rows = pl.ds(bo, HGRN_CHUNK)
            ob_ref[0, rows, cs] = _hgrn_chunk(fb_ref[0, rows, cs], vb_ref[0, rows, cs], qb_ref[0, rows, cs],
                                              lbb_ref[:, cs], stb.at[hh], True)
        return carry

    lax.fori_loop(0, nsub, body, 0)

    @pl.when(step == pl.num_programs(2) - 1)
    def _():
        sf_ref[0] = stf[...]
        sb_ref[0] = stb[...]


def _hgrn(ff, fb, val, q, lb_f, lb_b, s0f, s0b):
    B, T, _ = ff.shape
    ts = min(HGRN_STEP, T)
    ns = T // ts
    hp = HGRN_HEADS_PER_STEP
    fwd = pl.BlockSpec((1, ts, hp * HGRN_D), lambda b, h, s: (b, s, h))
    bwd = pl.BlockSpec((1, ts, hp * HGRN_D), lambda b, h, s: (b, ns - 1 - s, h))
    lbs = pl.BlockSpec((1, hp * HGRN_D), lambda b, h, s: (0, h))
    st = pl.BlockSpec((1, hp, HGRN_D, HGRN_D), lambda b, h, s: (b, h, 0, 0))
    o_sds = jax.ShapeDtypeStruct((B, T, HGRN_WIDTH), F32)
    s_sds = jax.ShapeDtypeStruct((B, HGRN_HEADS, HGRN_D, HGRN_D), F32)
    return pl.pallas_call(
        _hgrn_kernel,
        grid=(B, HGRN_HEADS // hp, ns),
        in_specs=[fwd, fwd, fwd, bwd, bwd, bwd, lbs, lbs, st, st],
        out_specs=[fwd, bwd, st, st],
        out_shape=[o_sds, o_sds, s_sds, s_sds],
        scratch_shapes=[pltpu.VMEM((hp, HGRN_D, HGRN_D), F32), pltpu.VMEM((hp, HGRN_D, HGRN_D), F32)],
        compiler_params=_cparams("parallel", "parallel", "arbitrary"),
        name="hgrn",
    )(ff, val, q, fb, val, q, lb_f, lb_b, s0f, s0b)


def _outproj_kernel(attn_ref, of_ref, ob_ref, g_ref, x_ref, g1_ref, sh2_ref, sc2_ref, og_ref, nf_ref,
                    wo_ref, wq_ref, sk_ref, x1_ref, h2_ref, s_ref):
    o = of_ref[0] + ob_ref[0]
    og = og_ref[...]
    parts = []
    for hd in range(HGRN_HEADS):
        cs = slice(hd * HGRN_D, (hd + 1) * HGRN_D)
        gh = g_ref[0, :, cs].astype(F32)
        parts.append((_rms(o[:, cs], og) * (gh * jax.nn.sigmoid(gh))).astype(BF16))
    hg = jnp.concatenate(parts, axis=-1)
    mix = (jnp.dot(attn_ref[0], wo_ref[:ATTN_WIDTH], preferred_element_type=F32)
           + jnp.dot(hg, wo_ref[ATTN_WIDTH:], preferred_element_type=F32))
    x1 = x_ref[0] + g1_ref[0] * mix
    x1_ref[0] = x1
    h2 = _rms(x1, nf_ref[...]) * (1.0 + sc2_ref[0]) + sh2_ref[0]
    h2_ref[0] = _pack_bf16_pairs(h2)
    pq = jnp.dot(h2.astype(BF16), wq_ref[...], preferred_element_type=F32)
    for hp in range(2 * PEER_HEADS):
        cs = slice(hp * PEER_DHALF, (hp + 1) * PEER_DHALF)
        for jb in range(pq.shape[0] // LANES):
            rows = slice(jb * LANES, (jb + 1) * LANES)
            s_ref[hp, 0, jb * PEER_NKEYS:(jb + 1) * PEER_NKEYS, :] = lax.dot_general(
                sk_ref[hp], pq[rows, cs].astype(BF16), _NT, preferred_element_type=F32)


def _outproj(attn, of, ob, g, x, g1, sh2, sc2, o_gain, norm_ffn, wo_bf16, wq_bf16, sk_bf16):
    B, T, _ = x.shape
    tm = min(ROW_TILE, T)
    nt = T // tm
    per = TOPK_TOKENS // tm
    assert (B * T) % TOPK_TOKENS == 0 and TOPK_TOKENS % tm == 0 and tm % LANES == 0
    row = lambda w: pl.BlockSpec((1, tm, w), lambda b, t: (b, t, 0))
    vec = pl.BlockSpec((1, 1, D_MODEL), lambda b, t: (b, 0, 0))
    n_sk = 2 * PEER_HEADS
    sds = lambda w: jax.ShapeDtypeStruct((B, T, w), F32)
    s_rows = tm // LANES * PEER_NKEYS
    s_spec = pl.BlockSpec((n_sk, 1, s_rows, LANES), lambda b, t: (0, (b * nt + t) // per, (b * nt + t) % per, 0))
    s_sds = jax.ShapeDtypeStruct((n_sk, B * T // TOPK_TOKENS, TOPK_TOKENS // LANES * PEER_NKEYS, LANES), F32)
    return pl.pallas_call(
        _outproj_kernel,
        grid=(B, T // tm),
        in_specs=[row(ATTN_WIDTH), row(HGRN_WIDTH), row(HGRN_WIDTH), row(HGRN_WIDTH), row(D_MODEL),
                  vec, vec, vec, _resident((1, HGRN_D)), _resident((1, D_MODEL)),
                  _resident((ATTN_WIDTH + HGRN_WIDTH, D_MODEL)), _resident((D_MODEL, n_sk * PEER_DHALF)),
                  _resident((n_sk, PEER_NKEYS, PEER_DHALF))],
        out_specs=[row(D_MODEL), row(D_MODEL // 2), s_spec],
        out_shape=[sds(D_MODEL), jax.ShapeDtypeStruct((B, T, D_MODEL // 2), I32), s_sds],
        compiler_params=_cparams("parallel", "parallel"),
        name="outproj",
    )(attn, of, ob, g, x, g1, sh2, sc2, o_gain.reshape(1, HGRN_D), norm_ffn.reshape(1, D_MODEL),
      wo_bf16, wq_bf16, sk_bf16)


TOPK_TOKENS = SUBLANES * LANES
_CAND_PAIRS = [(a, b) for a in range(PEER_TOPK) for b in range(PEER_TOPK) if (a + 1) * (b + 1) <= PEER_TOPK]


def _first_argmax(values, ids, n_chains):
    per = -(-len(values) // n_chains)
    parts = []
    for lo in range(0, len(values), per):
        m, i = values[lo], ids[lo]
        if not isinstance(i, jax.Array):
            i = jnp.full(m.shape, i, F32)
        for v, vid in zip(values[lo + 1:lo + per], ids[lo + 1:lo + per]):
            c = v > m
            m = jnp.where(c, v, m)
            i = jnp.where(c, vid, i)
        parts.append((m, i))
    m, i = parts[0]
    for pm, pi in parts[1:]:
        c = pm > m
        m = jnp.where(c, pm, m)
        i = jnp.where(c, pi, i)
    return m, i


def _topk_kernel(s_ref, idx_ref, gate_ref, wk, tv, ti, cv, ci, bv):
    shape = (SUBLANES, LANES)
    none = jnp.full(shape, -1.0, F32)

    def head(h, carry):
        for p in range(2):
            hp = 2 * h + p
            for k in range(PEER_NKEYS):
                wk[k] = s_ref[hp, 0, pl.ds(k, SUBLANES, stride=PEER_NKEYS), :]

            def extract(r, prev, p=p):
                vals = []
                for k in range(PEER_NKEYS):
                    s = jnp.where(prev == float(k), NEG_INF, wk[k])
                    wk[k] = s
                    vals.append(s)
                m, i = _first_argmax(vals, [float(k) for k in range(PEER_NKEYS)], 4)
                tv[p, r] = m
                ti[p, r] = i
                return i

            lax.fori_loop(0, PEER_TOPK, extract, none)

        for c, (a, b) in enumerate(_CAND_PAIRS):
            cv[c] = tv[0, a] + tv[1, b]
            ci[c] = ti[0, a] * float(PEER_NKEYS) + ti[1, b]

        def pick(r, prev):
            vals, ids = [], []
            for c in range(len(_CAND_PAIRS)):
                cid = ci[c]
                s = jnp.where(cid == prev, NEG_INF, cv[c])
                cv[c] = s
                vals.append(s)
                ids.append(cid)
            m, i = _first_argmax(vals, ids, 2)
            bv[r] = m
            idx_ref[h * PEER_TOPK + r, 0] = i.astype(I32)
            return i

        lax.fori_loop(0, PEER_TOPK, pick, none)

        es = [jnp.exp(bv[r] - bv[0]) for r in range(PEER_TOPK)]
        tot = es[0]
        for e in es[1:]:
            tot = tot + e
        for r in range(PEER_TOPK):
            gate_ref[h * PEER_TOPK + r, 0] = es[r] / tot
        return carry

    lax.fori_loop(0, PEER_HEADS, head, 0)


def _topk(s):
    nt = s.shape[1]
    vreg = (SUBLANES, LANES)
    out_spec = pl.BlockSpec((PEER_SLOTS, 1) + vreg, lambda i: (0, i, 0, 0))
    return pl.pallas_call(
        _topk_kernel,
        grid=(nt,),
        in_specs=[pl.BlockSpec((s.shape[0], 1) + s.shape[2:], lambda i: (0, i, 0, 0))],
        out_specs=[out_spec, out_spec],
        out_shape=[jax.ShapeDtypeStruct((PEER_SLOTS, nt) + vreg, I32),
                   jax.ShapeDtypeStruct((PEER_SLOTS, nt) + vreg, F32)],
        scratch_shapes=[pltpu.VMEM((PEER_NKEYS,) + vreg, F32),
                        pltpu.VMEM((2, PEER_TOPK) + vreg, F32), pltpu.VMEM((2, PEER_TOPK) + vreg, F32),
                        pltpu.VMEM((len(_CAND_PAIRS),) + vreg, F32), pltpu.VMEM((len(_CAND_PAIRS),) + vreg, F32),
                        pltpu.VMEM((PEER_TOPK,) + vreg, F32)],
        compiler_params=_cparams("parallel"),
        name="topk",
    )(s)


SC_CORES = 2
SC_SUBCORES = 16
SC_LANES = 16
SC_WORKERS = SC_CORES * SC_SUBCORES
PEER_GROUP = 16
SC_BUFFERS = 4
PEER_ITEMS = PEER_GROUP * PEER_HEADS
SC_UNROLL = 16
PACKED_WORDS = D_MODEL // 2
EXPERT_SLAB = (PACKED_WORDS // LANES, LANES)


_SC_PARAMS = pltpu.CompilerParams(needs_layout_passes=False)


def _sc_mesh():
    return plsc.VectorSubcoreMesh(core_axis_name="c", subcore_axis_name="s")


def _sc_worker():
    return lax.axis_index("s") * SC_CORES + lax.axis_index("c")


def _pack_kernel(t_ref, o_ref):
    word = _pack_bf16_pairs(t_ref[...])
    for s in range(EXPERT_SLAB[0]):
        o_ref[:, s, :] = word[:, s * LANES:(s + 1) * LANES]


def _pack_table(table):
    e = table.shape[0]
    tr = 256
    return pl.pallas_call(
        _pack_kernel,
        grid=(e // tr,),
        in_specs=[pl.BlockSpec((tr, D_MODEL), lambda i: (i, 0))],
        out_specs=pl.BlockSpec((tr,) + EXPERT_SLAB, lambda i: (i, 0, 0)),
        out_shape=jax.ShapeDtypeStruct((e,) + EXPERT_SLAB, I32),
        compiler_params=_cparams("parallel"),
        name="pack_table",
    )(table)


def _row_words(rows, k, first=0, count=PEER_TOPK):
    per = LANES // SC_LANES
    sub, cols = k // per, pl.ds((k % per) * SC_LANES, SC_LANES)
    return tuple(rows[r, sub, cols] for r in range(first, first + count))


def _bf16_lanes(word):
    return plsc.bitcast(word, BF16)


def _f32_halves(pairs):
    return plsc.unpack(pairs, format=plsc.PackFormat.INTERLEAVED)


def _half_cols(k):
    return pl.ds(k * SC_LANES, SC_LANES), pl.ds(PACKED_WORDS + k * SC_LANES, SC_LANES)


def _sc_item_pipeline(table_hbm, idx_v, ring, sems, compute):
    nb = SC_BUFFERS

    def gather(j):
        b = j % nb
        return pltpu.make_async_copy(table_hbm.at[idx_v.at[j]], ring.at[b], sems.at[b])

    for j in range(nb - 1):
        gather(j).start()

    @pl.loop(0, PEER_ITEMS)
    def _(j):
        @pl.when(j + nb - 1 < PEER_ITEMS)
        def _():
            gather(j + nb - 1).start()

        gather(j).wait()
        compute(j, ring.at[j % nb])


def _peer_dot_kernel(x_hbm, idx_hbm, u_hbm, a_hbm, x_v, idx_v, a_v, ring, sems):
    tok_per_w = x_hbm.shape[0] // SC_WORKERS
    wid = _sc_worker()
    lane = lax.iota(I32, SC_LANES)

    def compute(j, rows):
        t = j // PEER_HEADS

        zero = jnp.zeros((SC_LANES,), F32)

        half = PEER_TOPK // 2
        steps = PACKED_WORDS // SC_LANES // 2

        def load(kp, first):
            out = ()
            for k in (2 * kp, 2 * kp + 1):
                out += (x_v[t, pl.ds(k * SC_LANES, SC_LANES)],) + _row_words(rows, k, first, half)
            return out

        def fma(acc, vals):
            x0, x1 = _bf16_lanes(vals[0]), _bf16_lanes(vals[1 + half])
            out = []
            for i in range(half):
                lo, hi = _f32_halves(_bf16_lanes(vals[1 + i]) * x0 + _bf16_lanes(vals[2 + half + i]) * x1)
                out.append(acc[i] + lo + hi)
            return tuple(out)

        @plsc.parallel_loop(0, steps, unroll=32, carry=((zero,) * half, (zero,) * half, load(0, half)))
        def state(kp, state):
            acc_a, acc_b, vals_b = state
            vals_a = load(kp, 0)
            acc_b = fma(acc_b, vals_b)
            vals_b = load(jnp.minimum(kp + 1, steps - 1), half)
            return fma(acc_a, vals_a), acc_b, vals_b

        acc = state[0] + state[1]
        out = zero
        for r in range(PEER_TOPK):
            out = jnp.where(lane == r, jnp.sum(acc[r]), out)
        a_v[t, pl.ds((j % PEER_HEADS) * PEER_TOPK, PEER_TOPK)] = out

    @pl.loop(0, tok_per_w // PEER_GROUP)
    def _(g):
        tok0 = wid * tok_per_w + g * PEER_GROUP
        item0 = tok0 * PEER_HEADS
        pltpu.sync_copy(x_hbm.at[pl.ds(tok0, PEER_GROUP)], x_v)
        pltpu.sync_copy(idx_hbm.at[pl.ds(item0, PEER_ITEMS)], idx_v)
        _sc_item_pipeline(u_hbm, idx_v, ring, sems, compute)
        pltpu.sync_copy(a_v, a_hbm.at[pl.ds(tok0, PEER_GROUP)])


def _peer_sum_kernel(w_hbm, idx_hbm, v_hbm, o_hbm, w_v, idx_v, o_v, ring, sems):
    tok_per_w = o_hbm.shape[0] // SC_WORKERS
    wid = _sc_worker()
    zero = jnp.zeros((SC_LANES,), F32)

    def compute(j, rows):
        t = j // PEER_HEADS
        tv = jnp.full((SC_LANES,), t, I32)
        slot0 = (j % PEER_HEADS) * PEER_TOPK
        ws = []
        for r in range(PEER_TOPK):
            wr = plsc.load_gather(w_v, [tv, jnp.full((SC_LANES,), slot0 + r, I32)])
            ws.append(plsc.pack(wr, wr, format=plsc.PackFormat.INTERLEAVED))

        def load(k):
            lo, hi = _half_cols(k)
            return (o_v[t, lo], o_v[t, hi]) + _row_words(rows, k)

        def finish(k, vals):
            acc_lo, acc_hi = vals[0], vals[1]
            for r in range(0, PEER_TOPK, 2):
                lo, hi = _f32_halves(_bf16_lanes(vals[2 + r]) * ws[r] + _bf16_lanes(vals[3 + r]) * ws[r + 1])
                acc_lo = acc_lo + lo
                acc_hi = acc_hi + hi
            lo, hi = _half_cols(k)
            o_v[t, lo] = acc_lo
            o_v[t, hi] = acc_hi

        @plsc.parallel_loop(1, PACKED_WORDS // SC_LANES, unroll=SC_UNROLL, carry=load(0))
        def vals(k, vals):
            nxt = load(k)
            finish(k - 1, vals)
            return nxt

        finish(PACKED_WORDS // SC_LANES - 1, vals)

    @pl.loop(0, tok_per_w // PEER_GROUP)
    def _(g):
        tok0 = wid * tok_per_w + g * PEER_GROUP
        item0 = tok0 * PEER_HEADS
        pltpu.sync_copy(w_hbm.at[pl.ds(tok0, PEER_GROUP)], w_v)
        pltpu.sync_copy(idx_hbm.at[pl.ds(item0, PEER_ITEMS)], idx_v)

        @pl.loop(0, PEER_GROUP)
        def _(t):
            @plsc.parallel_loop(0, D_MODEL // SC_LANES, unroll=16)
            def _(k):
                o_v[t, pl.ds(k * SC_LANES, SC_LANES)] = zero

        _sc_item_pipeline(v_hbm, idx_v, ring, sems, compute)
        pltpu.sync_copy(o_v, o_hbm.at[pl.ds(tok0, PEER_GROUP)])


def _sc_scratch(first, last):
    return ([first, pltpu.VMEM((PEER_ITEMS, PEER_TOPK), I32), last]
            + [pltpu.VMEM((SC_BUFFERS, PEER_TOPK) + EXPERT_SLAB, I32), pltpu.SemaphoreType.DMA((SC_BUFFERS,))])


def _peer_dot(h2, idx, u):
    n = h2.shape[0]
    assert n % (SC_WORKERS * PEER_GROUP) == 0
    call = pl.kernel(
        _peer_dot_kernel,
        out_type=jax.ShapeDtypeStruct((n, PEER_SLOTS), F32),
        mesh=_sc_mesh(),
        scratch_types=_sc_scratch(pltpu.VMEM((PEER_GROUP, PACKED_WORDS), I32),
                                  pltpu.VMEM((PEER_GROUP, PEER_SLOTS), F32)),
        compiler_params=_SC_PARAMS,
    )
    return call(h2, idx, u)


def _peer_sum(w, idx, v, n):
    assert n % (SC_WORKERS * PEER_GROUP) == 0
    call = pl.kernel(
        _peer_sum_kernel,
        out_type=jax.ShapeDtypeStruct((n, D_MODEL), F32),
        mesh=_sc_mesh(),
        scratch_types=_sc_scratch(pltpu.VMEM((PEER_GROUP, PEER_SLOTS), F32),
                                  pltpu.VMEM((PEER_GROUP, D_MODEL), F32)),
        compiler_params=_SC_PARAMS,
    )
    return call(w, idx, v)


def _gelu_gate_kernel(a_ref, g_ref, w_ref):
    a = a_ref[...]
    w_ref[...] = g_ref[...] * (0.5 * a * (1.0 + lax.erf(a * (2.0 ** -0.5))))


def _gelu_gate(a, gate):
    n = a.shape[0]
    tm = min(2048, n)
    spec = pl.BlockSpec((tm, PEER_SLOTS), lambda i: (i, 0))
    return pl.pallas_call(
        _gelu_gate_kernel, grid=(n // tm,), in_specs=[spec, spec], out_specs=spec,
        out_shape=jax.ShapeDtypeStruct(a.shape, F32), compiler_params=_cparams("parallel"), name="gelu_gate",
    )(a, gate)


def _residual_kernel(x_ref, g_ref, p_ref, o_ref):
    o_ref[0] = x_ref[0] + g_ref[0] * p_ref[0]


def _residual(x1, g2, peer):
    B, T, _ = x1.shape
    tm = min(2 * ROW_TILE, T)
    row = pl.BlockSpec((1, tm, D_MODEL), lambda b, t: (b, t, 0))
    vec = pl.BlockSpec((1, 1, D_MODEL), lambda b, t: (b, 0, 0))
    return pl.pallas_call(
        _residual_kernel, grid=(B, T // tm), in_specs=[row, vec, row], out_specs=row,
        out_shape=jax.ShapeDtypeStruct(x1.shape, F32), compiler_params=_cparams("parallel", "parallel"),
        name="residual",
    )(x1, g2, peer)


def kernel(x, c, ctx, c_ctx, w_ada, b_ada, norm_mix, norm_ffn, w_in, q_norm, k_norm, attn_sink, hgrn_lb_logits,
           hgrn_norm, w_out, peer_w_q, peer_sub_keys, peer_u, peer_v):
    assert w_ada.shape[0] == 1, "single-layer block"
    B, T, D = x.shape
    L = ctx.shape[1]
    n = B * T

    cvecs = jnp.zeros((SUBLANES, D), F32).at[:B].set(c).at[B].set(c_ctx)
    mod = _ada(cvecs, w_ada[0], b_ada[0])
    part = lambda rows, i: rows[:, None, i * D:(i + 1) * D]
    mod_x = mod[:B]
    mod_c = jnp.broadcast_to(mod[B:B + 1], (B, 6 * D))
    sh1, sc1, g1, sh2, sc2, g2 = (part(mod_x, i) for i in range(6))

    lbs = jnp.cumsum(jax.nn.softmax(hgrn_lb_logits.astype(F32), axis=1), axis=1)
    lb_f, lb_b = lbs[0, 0].reshape(1, HGRN_WIDTH), lbs[1, 0].reshape(1, HGRN_WIDTH)

    w_in_b = w_in[0].astype(BF16)
    rope = _rope_tables(T)
    kc, vc, ffc, fbc, ic, _, qhc, _ = _inproj(ctx, part(mod_c, 0), part(mod_c, 1), norm_mix[0], w_in_b,
                                              _identity_rope(L), q_norm[0], k_norm[0])
    s0 = jnp.zeros((B, HGRN_HEADS, HGRN_D, HGRN_D), F32)
    _, _, sfc, sbc = _hgrn(ffc, fbc, ic, qhc, lb_f, lb_b, s0, s0)

    sk = peer_sub_keys[0].reshape(2 * PEER_HEADS, PEER_NKEYS, PEER_DHALF).astype(BF16)
    w_out_b, w_q_b = w_out[0].astype(BF16), peer_w_q[0].astype(BF16)
    u_packed, v_packed = _pack_table(peer_u[0]), _pack_table(peer_v[0])

    kx, vx, ffx, fbx, ix, qx, qhx, gx = _inproj(x, sh1, sc1, norm_mix[0], w_in_b, rope, q_norm[0], k_norm[0])
    attn = _attention(qx, kx, vx, kc, vc, attn_sink[0])
    of, ob, _, _ = _hgrn(ffx, fbx, ix, qhx, lb_f, lb_b, sfc, sbc)
    x1, h2p, s = _outproj(attn, of, ob, gx, x, g1, sh2, sc2, hgrn_norm[0], norm_ffn[0], w_out_b, w_q_b, sk)
    idx_t, gate_t = _topk(s)
    idx16 = idx_t.reshape(PEER_SLOTS, n).T.reshape(n * PEER_HEADS, PEER_TOPK)
    gate = gate_t.reshape(PEER_SLOTS, n).T
    a = _peer_dot(h2p.reshape(n, PACKED_WORDS), idx16, u_packed)
    w = _gelu_gate(a, gate)
    peer = _peer_sum(w, idx16, v_packed, n)
    return _residual(x1, g2, peer.reshape(B, T, D))
```
